```python
import jax
import jax.numpy as jnp
from jax import lax
import numpy as np

D_MODEL = 1024
BATCH = 8
SEQ = 2048
DEPTH = 1

ATT_HEADS = 16
ATT_HEAD_DIM = 64
ATT_W = ATT_HEADS * ATT_HEAD_DIM
DILATED_PATTERNS = ((128, 1), (512, 4), (2048, 16))
ROPE_THETA = 10000.0
RET_HEADS = 4
RET_QK_DIM = 256
RET_V_DIM = 512
RET_QK_W = RET_HEADS * RET_QK_DIM
RET_V_W = RET_HEADS * RET_V_DIM
RET_CHUNK = 128
N_GROUPS = 4
EXPERTS_PER_GROUP = 8
N_EXPERTS = N_GROUPS * EXPERTS_PER_GROUP
TOP_K_IN_GROUP = 2
EXPERT_FF = 256
EPS = 1e-6
IN_SPLITS = (ATT_W, ATT_W, ATT_W, RET_QK_W, RET_QK_W, RET_V_W, RET_V_W, D_MODEL, D_MODEL)
IN_WIDTH = sum(IN_SPLITS)

kernel_name = 'hybrid_dilated_retention_hmoe_block'


def _split_points():
    pts, acc = [], 0
    for s in IN_SPLITS[:-1]:
        acc += s
        pts.append(acc)
    return pts


def _rmsnorm(x, g):
    xf = x.astype(jnp.float32)
    y = xf * lax.rsqrt(jnp.mean(xf * xf, axis=-1, keepdims=True) + EPS)
    return (y * g.astype(jnp.float32)).astype(x.dtype)


def _rotate(x, pos, inv_freq):
    ang = pos[:, None] * inv_freq[None, :]
    cos, sin = jnp.cos(ang), jnp.sin(ang)
    x1, x2 = jnp.split(x, 2, axis=-1)
    return jnp.concatenate([x1 * cos - x2 * sin, x2 * cos + x1 * sin], axis=-1)


def _heads(t, n, d):
    B, S, _ = t.shape
    return t.reshape(B, S, n, d).transpose(0, 2, 1, 3)


def _merge_heads(t):
    B, H, S, d = t.shape
    return t.transpose(0, 2, 1, 3).reshape(B, S, H * d)


def _dilated_pattern(q, k, v, window, dilation):
    B, H, S, hd = q.shape
    L = window // dilation
    Sr = S // dilation
    nb = -(-Sr // L)
    pad = nb * L - Sr

    def to_blocks(t):
        t = t.reshape(B, H, Sr, dilation, hd).transpose(0, 1, 3, 2, 4)
        t = jnp.pad(t, ((0, 0), (0, 0), (0, 0), (0, pad), (0, 0)))
        return t.reshape(B, H, dilation, nb, L, hd)

    def with_prev(t):
        prev = jnp.pad(t[:, :, :, :-1], ((0, 0), (0, 0), (0, 0), (1, 0), (0, 0), (0, 0)))
        return jnp.concatenate([prev, t], axis=4)

    qb = to_blocks(q)
    k2 = with_prev(to_blocks(k))
    v2 = with_prev(to_blocks(v))
    s = jnp.einsum('bhrnqd,bhrnkd->bhrnqk', qb, k2) * (hd ** -0.5)
    qpos = jnp.arange(nb)[:, None] * L + jnp.arange(L)[None, :]
    kpos = jnp.arange(nb)[:, None] * L - L + jnp.arange(2 * L)[None, :]
    dist = qpos[:, :, None] - kpos[:, None, :]
    mask = (dist >= 0) & (dist <= L) & (kpos[:, None, :] >= 0)
    s = jnp.where(mask, s, -jnp.inf)
    m = jnp.max(s, axis=-1, keepdims=True)
    p = jnp.exp(s - m)
    den = jnp.sum(p, axis=-1, keepdims=True)
    o = jnp.einsum('bhrnqk,bhrnkd->bhrnqd', p, v2) / den

    def from_blocks(t):
        c = t.shape[-1]
        t = t.reshape(B, H, dilation, nb * L, c)[:, :, :, :Sr]
        return t.transpose(0, 1, 3, 2, 4).reshape(B, H, S, c)

    return from_blocks(o), from_blocks(m), from_blocks(den)


def _dilated_attention(q, k, v):
    outs, maxes, dens = [], [], []
    for window, dilation in DILATED_PATTERNS:
        o, m, d = _dilated_pattern(q, k, v, window, dilation)
        outs.append(o)
        maxes.append(m)
        dens.append(d)
    m_all = functools_max(maxes)
    weights = [d * jnp.exp(m - m_all) for d, m in zip(dens, maxes)]
    num = sum(w * o for w, o in zip(weights, outs))
    return num / sum(weights)


def functools_max(arrs):
    out = arrs[0]
    for a in arrs[1:]:
        out = jnp.maximum(out, a)
    return out


def _retention(q, k, v, log_gamma):
    B, H, S, dk = q.shape
    dv = v.shape[-1]
    C = min(RET_CHUNK, S)
    nc = S // C
    qc = q.reshape(B, H, nc, C, dk)
    kc = k.reshape(B, H, nc, C, dk)
    vc = v.reshape(B, H, nc, C, dv)
    idx = jnp.arange(C, dtype=jnp.float32)
    diff = idx[:, None] - idx[None, :]
    lg = log_gamma[:, None, None]
    decay = jnp.where(diff >= 0, jnp.exp(lg * jnp.maximum(diff, 0.0)), 0.0)
    zeta = jnp.exp(log_gamma[:, None] * (C - 1 - idx))
    xi = jnp.exp(log_gamma[:, None] * (idx + 1))
    gamma_c = jnp.exp(log_gamma * C)
    inner = jnp.einsum('bhnik,bhnjk->bhnij', qc, kc) * decay[None, :, None]
    y = jnp.einsum('bhnij,bhnjv->bhniv', inner, vc)
    kv = jnp.einsum('bhnjk,bhnjv->bhnkv', kc * zeta[None, :, None, :, None], vc)

    def step(state, kv_n):
        return state * gamma_c[None, :, None, None] + kv_n, state

    init = jnp.zeros((B, H, dk, dv), dtype=kv.dtype)
    _, prev = lax.scan(step, init, jnp.moveaxis(kv, 2, 0))
    prev = jnp.moveaxis(prev, 0, 2)
    y = y + jnp.einsum('bhnik,bhnkv->bhniv', qc * xi[None, :, None, :, None], prev)
    return y.reshape(B, H, S, dv)


def _mixer(xn, w_in, b_gate, g_q, g_k, w_att, g_ret, w_ret, w_out):
    dt = xn.dtype
    f32 = jnp.float32
    B, S, _ = xn.shape
    qa, ka, va, qr, kr, vr, gr, ga, gb = jnp.split(xn @ w_in, _split_points(), axis=-1)
    pos = jnp.arange(S, dtype=f32)
    inv_a = ROPE_THETA ** (-jnp.arange(0, ATT_HEAD_DIM, 2, dtype=f32) / ATT_HEAD_DIM)
    q = _rotate(_rmsnorm(_heads(qa, ATT_HEADS, ATT_HEAD_DIM).astype(f32), g_q), pos, inv_a)
    k = _rotate(_rmsnorm(_heads(ka, ATT_HEADS, ATT_HEAD_DIM).astype(f32), g_k), pos, inv_a)
    v = _heads(va, ATT_HEADS, ATT_HEAD_DIM).astype(f32)
    y_att = _merge_heads(_dilated_attention(q, k, v)).astype(dt) @ w_att
    inv_r = 1.0 / (ROPE_THETA ** jnp.linspace(0.0, 1.0, RET_QK_DIM // 2, dtype=f32))
    q = _rotate(_heads(qr, RET_HEADS, RET_QK_DIM).astype(f32), pos, inv_r)
    k = _rotate(_heads(kr, RET_HEADS, RET_QK_DIM).astype(f32), pos, inv_r) * (RET_QK_DIM ** -0.5)
    v = _heads(vr, RET_HEADS, RET_V_DIM).astype(f32)
    log_gamma = jnp.log(1.0 - jnp.exp2(-5.0 - jnp.arange(RET_HEADS, dtype=f32)))
    y = _rmsnorm(_retention(q, k, v, log_gamma), g_ret[:, None, :])
    y_ret = (_merge_heads(y).astype(dt) * jax.nn.silu(gr)) @ w_ret
    gate_a = jax.nn.sigmoid(ga + b_gate[:D_MODEL])
    gate_b = jax.nn.sigmoid(gb + b_gate[D_MODEL:])
    return (gate_a * y_att + gate_b * y_ret) @ w_out


def _hmoe(xn, w_rg, b_rg, w_re, b_re, w1, w3, w2):
    B, S, D = xn.shape
    t = xn.reshape(B * S, D)
    T = t.shape[0]
    group_logits = (t @ w_rg + b_rg).astype(jnp.float32)
    group_prob = jax.nn.softmax(group_logits, axis=-1)
    g_sel = jnp.argmax(group_logits, axis=-1)
    p_group = jnp.take_along_axis(group_prob, g_sel[:, None], axis=-1)
    exp_logits = (t @ w_re + b_re).astype(jnp.float32).reshape(T, N_GROUPS, EXPERTS_PER_GROUP)
    in_group = jnp.take_along_axis(exp_logits, g_sel[:, None, None], axis=1)[:, 0]
    top_v, top_i = lax.top_k(in_group, TOP_K_IN_GROUP)
    top_w = jax.nn.softmax(top_v, axis=-1) * p_group
    expert_id = g_sel[:, None] * EXPERTS_PER_GROUP + top_i
    gate = jnp.sum(jax.nn.one_hot(expert_id, N_EXPERTS, dtype=jnp.float32) * top_w[..., None], axis=1)
    gate = gate.astype(t.dtype)
    y = jnp.zeros_like(t)
    for e in range(N_EXPERTS):
        h = jax.nn.silu(t @ w1[e]) * (t @ w3[e])
        y = y + gate[:, e:e + 1] * (h @ w2[e])
    return y.reshape(B, S, D)


def setup_inputs(seed: int = 0) -> dict:
    key = jax.random.key(seed)
    ks = jax.random.split(key, 20)
    f32 = jnp.float32

    def nrm(k, shape, scale):
        return jax.random.normal(k, shape, f32) * scale

    return {
        'x': nrm(ks[0], (BATCH, SEQ, D_MODEL), 1.0),
        'g_norm_mix': 1.0 + nrm(ks[1], (DEPTH, D_MODEL), 0.02),
        'w_in': nrm(ks[2], (DEPTH, D_MODEL, IN_WIDTH), D_MODEL ** -0.5),
        'b_merge_gate': nrm(ks[3], (DEPTH, 2 * D_MODEL), 0.02),
        'g_q': 1.0 + nrm(ks[4], (DEPTH, ATT_HEAD_DIM), 0.02),
        'g_k': 1.0 + nrm(ks[5], (DEPTH, ATT_HEAD_DIM), 0.02),
        'w_branch_att': nrm(ks[6], (DEPTH, ATT_W, D_MODEL), ATT_W ** -0.5),
        'g_ret_norm': 1.0 + nrm(ks[7], (DEPTH, RET_HEADS, RET_V_DIM), 0.02),
        'w_branch_ret': nrm(ks[8], (DEPTH, RET_V_W, D_MODEL), RET_V_W ** -0.5),
        'w_out': nrm(ks[9], (DEPTH, D_MODEL, D_MODEL), D_MODEL ** -0.5),
        'g_norm_ffn': 1.0 + nrm(ks[10], (DEPTH, D_MODEL), 0.02),
        'w_router_group': nrm(ks[11], (DEPTH, D_MODEL, N_GROUPS), D_MODEL ** -0.5),
        'b_router_group': nrm(ks[12], (DEPTH, N_GROUPS), 0.01),
        'w_router_expert': nrm(ks[13], (DEPTH, D_MODEL, N_EXPERTS), D_MODEL ** -0.5),
        'b_router_expert': nrm(ks[14], (DEPTH, N_EXPERTS), 0.01),
        'w1': nrm(ks[15], (DEPTH, N_EXPERTS, D_MODEL, EXPERT_FF), D_MODEL ** -0.5),
        'w3': nrm(ks[16], (DEPTH, N_EXPERTS, D_MODEL, EXPERT_FF), D_MODEL ** -0.5),
        'w2': nrm(ks[17], (DEPTH, N_EXPERTS, EXPERT_FF, D_MODEL), EXPERT_FF ** -0.5),
    }


def reference(x, g_norm_mix, w_in, b_merge_gate, g_q, g_k, w_branch_att, g_ret_norm,
              w_branch_ret, w_out, g_norm_ffn, w_router_group, b_router_group,
              w_router_expert, b_router_expert, w1, w3, w2):
    for l in range(DEPTH):
        x = x + _mixer(_rmsnorm(x, g_norm_mix[l]), w_in[l], b_merge_gate[l], g_q[l], g_k[l],
                       w_branch_att[l], g_ret_norm[l], w_branch_ret[l], w_out[l])
        x = x + _hmoe(_rmsnorm(x, g_norm_ffn[l]), w_router_group[l], b_router_group[l],
                      w_router_expert[l], b_router_expert[l], w1[l], w3[l], w2[l])
    return x
```

```python
import functools
import math

import jax
import jax.numpy as jnp
from jax import lax
from jax.experimental import pallas as pl
from jax.experimental.pallas import tpu as pltpu

F32 = jnp.float32
BF16 = jnp.bfloat16

D_MODEL = 1024
SEQ = 2048
ATT_HEADS = 16
ATT_HEAD_DIM = 64
ATT_W = ATT_HEADS * ATT_HEAD_DIM
DILATED_PATTERNS = ((128, 1), (512, 4), (2048, 16))
ROPE_THETA = 10000.0
RET_HEADS = 4
RET_QK_DIM = 256
RET_V_DIM = 512
RET_QK_W = RET_HEADS * RET_QK_DIM
RET_V_W = RET_HEADS * RET_V_DIM
N_GROUPS = 4
EXPERTS_PER_GROUP = 8
N_EXPERTS = N_GROUPS * EXPERTS_PER_GROUP
EXPERT_FF = 256
EPS = 1e-6
IN_SPLITS = (ATT_W, ATT_W, ATT_W, RET_QK_W, RET_QK_W, RET_V_W, RET_V_W, D_MODEL, D_MODEL)
IN_WIDTH = sum(IN_SPLITS)

LANES = 128
NCB = IN_WIDTH // LANES
CB_QA, CB_KA, CB_VA = 0, 8, 16
CB_QR, CB_KR, CB_VR, CB_GR = 24, 32, 40, 56
CB_GA, CB_GB = 72, 80

ATT_BLOCK = 128
RET_CHUNK = 256
NEG_BIG = -1e30
VMEM_LIMIT = 48 * 1024 * 1024


def _cparams(sem):
    return pltpu.CompilerParams(dimension_semantics=sem, vmem_limit_bytes=VMEM_LIMIT)


IN_TM = 1024
IN_TN = 1024


def _inproj_kernel(x_ref, g_ref, w_ref, o_ref, xn_sc):
    @pl.when(pl.program_id(1) == 0)
    def _():
        x = x_ref[...]
        ms = jnp.mean(x * x, axis=-1, keepdims=True)
        xn_sc[...] = (x * lax.rsqrt(ms + EPS) * g_ref[...]).astype(BF16)

    xn = xn_sc[...]
    for c2 in range(IN_TN // 256):
        acc = jnp.dot(xn, w_ref[:, c2 * 256:(c2 + 1) * 256], preferred_element_type=F32)
        o_ref[2 * c2] = acc[:, :LANES].astype(BF16)
        o_ref[2 * c2 + 1] = acc[:, LANES:].astype(BF16)


def _inproj(x2d, g, w_bf16):
    t = x2d.shape[0]
    return pl.pallas_call(
        _inproj_kernel,
        grid=(t // IN_TM, IN_WIDTH // IN_TN),
        in_specs=[
            pl.BlockSpec((IN_TM, D_MODEL), lambda i, j: (i, 0)),
            pl.BlockSpec((1, D_MODEL), lambda i, j: (0, 0)),
            pl.BlockSpec((D_MODEL, IN_TN), lambda i, j: (0, j)),
        ],
        out_specs=pl.BlockSpec((IN_TN // LANES, IN_TM, LANES), lambda i, j: (j, i, 0)),
        out_shape=jax.ShapeDtypeStruct((NCB, t, LANES), BF16),
        scratch_shapes=[pltpu.VMEM((IN_TM, D_MODEL), BF16)],
        compiler_params=_cparams(("arbitrary", "arbitrary")),
        name="inproj",
    )(x2d, g, w_bf16)


PREP_ROWS = 256


def _att_kernel(q_ref, k_ref, v_ref, cos_ref, sin_ref, gq_ref, gk_ref, o_ref,
                qf, kf, vf, q0_sc, q1_sc, k_sc, v_sc, acc_sc, m_sc, l_sc):
    lane = lax.broadcasted_iota(jnp.int32, (1, LANES), 1)
    head0 = lane < ATT_HEAD_DIM
    first_half = (lane % ATT_HEAD_DIM) < (ATT_HEAD_DIM // 2)
    seg = (lax.broadcasted_iota(jnp.int32, (LANES, LANES), 0) // ATT_HEAD_DIM
           == lax.broadcasted_iota(jnp.int32, (LANES, LANES), 1) // ATT_HEAD_DIM).astype(BF16)

    def prep(ci, carry):
        r0 = pl.multiple_of(ci * PREP_ROWS, PREP_ROWS)
        cos = cos_ref[pl.ds(r0, PREP_ROWS), :]
        sin = sin_ref[pl.ds(r0, PREP_ROWS), :]
        for src, g_ref, dst in ((q_ref, gq_ref, qf), (k_ref, gk_ref, kf)):
            x = src[0, pl.ds(r0, PREP_ROWS), :].astype(F32)
            x2 = x * x
            hi = x2.astype(BF16)
            lo = (x2 - hi.astype(F32)).astype(BF16)
            ss = (jnp.dot(hi, seg, preferred_element_type=F32)
                  + jnp.dot(lo, seg, preferred_element_type=F32))
            xn = x * lax.rsqrt(ss * (1.0 / ATT_HEAD_DIM) + EPS) * g_ref[...]
            swapped = jnp.where(first_half,
                                pltpu.roll(xn, LANES - ATT_HEAD_DIM // 2, 1),
                                pltpu.roll(xn, ATT_HEAD_DIM // 2, 1))
            dst[pl.ds(r0, PREP_ROWS), :] = xn * cos + swapped * sin
        vf[pl.ds(r0, PREP_ROWS), :] = v_ref[0, pl.ds(r0, PREP_ROWS), :].astype(F32)
        return carry

    lax.fori_loop(0, SEQ // PREP_ROWS, prep, 0)

    scale = ATT_HEAD_DIM ** -0.5
    for p, (window, r) in enumerate(DILATED_PATTERNS):
        rows = SEQ // r
        n = min(rows, PREP_ROWS)
        for c in range(r):
            for ch in range(rows // n):
                if r == 1:
                    src = pl.ds(ch * n, n)
                else:
                    src = pl.ds(c + ch * n * r, n, stride=r)
                dst = pl.ds(c * rows + ch * n, n)
                xq = qf[src, :] * scale
                q0_sc[p, dst, :] = jnp.where(head0, xq, 0.0).astype(BF16)
                q1_sc[p, dst, :] = jnp.where(head0, 0.0, xq).astype(BF16)
                k_sc[p, dst, :] = kf[src, :].astype(BF16)
                v_sc[p, dst, :] = vf[src, :].astype(BF16)

    qi = lax.broadcasted_iota(jnp.int32, (ATT_BLOCK, 2 * ATT_BLOCK), 0)
    kj = lax.broadcasted_iota(jnp.int32, (ATT_BLOCK, 2 * ATT_BLOCK), 1)
    band = jnp.where(kj < ATT_BLOCK, kj - qi, qi - (kj - ATT_BLOCK)) >= 0
    tri = (lax.broadcasted_iota(jnp.int32, (ATT_BLOCK, ATT_BLOCK), 1)
           <= lax.broadcasted_iota(jnp.int32, (ATT_BLOCK, ATT_BLOCK), 0))
    nt_dims = (((1,), (1,)), ((), ()))

    def task(p, r, row0, tok0, first):
        if first:
            kk = k_sc[p, pl.ds(row0, ATT_BLOCK), :]
            vv = v_sc[p, pl.ds(row0, ATT_BLOCK), :]
            mask = tri
        else:
            kk = k_sc[p, pl.ds(row0 - ATT_BLOCK, 2 * ATT_BLOCK), :]
            vv = v_sc[p, pl.ds(row0 - ATT_BLOCK, 2 * ATT_BLOCK), :]
            mask = band
        res = []
        for q_sc in (q0_sc, q1_sc):
            q = q_sc[p, pl.ds(row0, ATT_BLOCK), :]
            s = lax.dot_general(q, kk, nt_dims, preferred_element_type=F32)
            s = jnp.where(mask, s, NEG_BIG)
            m = jnp.max(s, axis=-1, keepdims=True)
            e = jnp.exp(s - m)
            l = jnp.sum(e, axis=-1, keepdims=True)
            acc = jnp.dot(e.astype(BF16), vv, preferred_element_type=F32)
            res.append((acc, m, l))
        (a0, m0, l0), (a1, m1, l1) = res
        if r == 1:
            dst = pl.ds(tok0, ATT_BLOCK)
        else:
            dst = pl.ds(tok0, ATT_BLOCK, stride=r)
        acc_sc[p, dst, :] = jnp.where(head0, a0, a1)
        m_sc[p, dst, :] = jnp.where(head0, m0, m1)
        l_sc[p, dst, :] = jnp.where(head0, l0, l1)

    for p, (window, r) in enumerate(DILATED_PATTERNS):
        rows = SEQ // r
        nb = rows // ATT_BLOCK

        def class_body(c, carry, p=p, r=r, rows=rows, nb=nb):
            base = pl.multiple_of(c * rows, ATT_BLOCK)
            task(p, r, base, c, True)
            if nb > 1:
                def blk(n, carry2):
                    row0 = pl.multiple_of(base + n * ATT_BLOCK, ATT_BLOCK)
                    task(p, r, row0, n * ATT_BLOCK * r + c, False)
                    return carry2
                lax.fori_loop(1, nb, blk, 0)
            return carry

        if r == 1:
            class_body(0, 0)
        else:
            lax.fori_loop(0, r, class_body, 0)

    def merge(ci, carry):
        r0 = pl.multiple_of(ci * PREP_ROWS, PREP_ROWS)
        rs = pl.ds(r0, PREP_ROWS)
        ms = [m_sc[p, rs, :] for p in range(3)]
        m_all = jnp.maximum(jnp.maximum(ms[0], ms[1]), ms[2])
        num = jnp.zeros((PREP_ROWS, LANES), F32)
        den = jnp.zeros((PREP_ROWS, LANES), F32)
        for p in range(3):
            w = jnp.exp(ms[p] - m_all)
            num = num + w * acc_sc[p, rs, :]
            den = den + w * l_sc[p, rs, :]
        o_ref[rs, :] = (num / den).astype(BF16)
        return carry

    lax.fori_loop(0, SEQ // PREP_ROWS, merge, 0)


def _attention(proj, cos_a, sin_a, gq, gk, batch):
    t = batch * SEQ
    hp = ATT_W // LANES

    def col(cb):
        return pl.BlockSpec((1, SEQ, LANES), lambda b, h: (cb + h, b, 0))

    tab = pl.BlockSpec((SEQ, LANES), lambda b, h: (0, 0))
    gain = pl.BlockSpec((1, LANES), lambda b, h: (0, 0))
    return pl.pallas_call(
        _att_kernel,
        grid=(batch, hp),
        in_specs=[col(CB_QA), col(CB_KA), col(CB_VA), tab, tab, gain, gain],
        out_specs=pl.BlockSpec((SEQ, LANES), lambda b, h: (b, h)),
        out_shape=jax.ShapeDtypeStruct((t, ATT_W), BF16),
        scratch_shapes=[
            pltpu.VMEM((SEQ, LANES), F32), pltpu.VMEM((SEQ, LANES), F32), pltpu.VMEM((SEQ, LANES), F32),
            pltpu.VMEM((3, SEQ, LANES), BF16), pltpu.VMEM((3, SEQ, LANES), BF16),
            pltpu.VMEM((3, SEQ, LANES), BF16), pltpu.VMEM((3, SEQ, LANES), BF16),
            pltpu.VMEM((3, SEQ, LANES), F32), pltpu.VMEM((3, SEQ, LANES), F32),
            pltpu.VMEM((3, SEQ, LANES), F32),
        ],
        compiler_params=_cparams(("arbitrary", "arbitrary")),
        name="dilated_attention",
    )(proj, proj, proj, cos_a, sin_a, gq, gk)


def _ret_kernel(gam_ref, q_ref, k_ref, v_ref, g_ref, cos_ref, sin_ref, decay_ref, xi_ref, zeta_ref,
                gn_ref, o_ref, state_sc):
    h = pl.program_id(1)
    gamma_c = gam_ref[h]
    state_sc[...] = jnp.zeros_like(state_sc)
    c = RET_CHUNK
    nt_dims = (((1,), (1,)), ((), ()))
    tn_dims = (((0,), (0,)), ((), ()))
    kscale = RET_QK_DIM ** -0.5

    def chunk(n, carry):
        rs = pl.ds(pl.multiple_of(n * c, c), c)
        cos = cos_ref[rs, :]
        sin = sin_ref[rs, :]

        def rot(ref):
            x1 = ref[0, rs, :].astype(F32)
            x2 = ref[1, rs, :].astype(F32)
            return x1 * cos - x2 * sin, x2 * cos + x1 * sin

        q1, q2 = rot(q_ref)
        k1, k2 = rot(k_ref)
        q = jnp.concatenate([q1, q2], axis=-1)
        k = jnp.concatenate([k1, k2], axis=-1) * kscale
        v = jnp.concatenate([v_ref[i, rs, :] for i in range(RET_V_DIM // LANES)], axis=-1)
        inner = lax.dot_general(q.astype(BF16), k.astype(BF16), nt_dims,
                                preferred_element_type=F32) * decay_ref[0]
        y = jnp.dot(inner.astype(BF16), v, preferred_element_type=F32)
        state = state_sc[...]
        qx = (q * xi_ref[0]).astype(BF16)
        y = y + jnp.dot(qx, state.astype(BF16), preferred_element_type=F32)
        kz = (k * zeta_ref[0]).astype(BF16)
        state_sc[...] = state * gamma_c + lax.dot_general(kz, v, tn_dims, preferred_element_type=F32)
        yn = y * lax.rsqrt(jnp.mean(y * y, axis=-1, keepdims=True) + EPS) * gn_ref[0]
        g = jnp.concatenate([g_ref[i, rs, :] for i in range(RET_V_DIM // LANES)], axis=-1).astype(F32)
        o_ref[rs, :] = (yn * (g * jax.nn.sigmoid(g))).astype(BF16)
        return carry

    lax.fori_loop(0, SEQ // c, chunk, 0)


def _retention(proj, gamma_c, cos_r, sin_r, decay, xi, zeta, g_ret, batch):
    t = batch * SEQ
    nq = RET_QK_DIM // LANES
    nv = RET_V_DIM // LANES

    def cols(cb, n):
        return pl.BlockSpec((n, SEQ, LANES), lambda b, h: (cb // n + h, b, 0))

    tab = pl.BlockSpec((SEQ, LANES), lambda b, h: (0, 0))
    return pl.pallas_call(
        _ret_kernel,
        grid=(batch, RET_HEADS),
        in_specs=[
            pl.BlockSpec(memory_space=pltpu.SMEM),
            cols(CB_QR, nq), cols(CB_KR, nq), cols(CB_VR, nv), cols(CB_GR, nv),
            tab, tab,
            pl.BlockSpec((1, RET_CHUNK, RET_CHUNK), lambda b, h: (h, 0, 0)),
            pl.BlockSpec((1, RET_CHUNK, RET_QK_DIM), lambda b, h: (h, 0, 0)),
            pl.BlockSpec((1, RET_CHUNK, RET_QK_DIM), lambda b, h: (h, 0, 0)),
            pl.BlockSpec((1, 1, RET_V_DIM), lambda b, h: (h, 0, 0)),
        ],
        out_specs=pl.BlockSpec((SEQ, RET_V_DIM), lambda b, h: (b, h)),
        out_shape=jax.ShapeDtypeStruct((t, RET_V_W), BF16),
        scratch_shapes=[pltpu.VMEM((RET_QK_DIM, RET_V_DIM), F32)],
        compiler_params=_cparams(("arbitrary", "arbitrary")),
        name="retention",
    )(gamma_c, proj, proj, proj, proj, cos_r, sin_r, decay, xi, zeta, g_ret)


OUT_TM = 512
N_ROUTE = N_EXPERTS + N_GROUPS


def _split_bf16(x):
    hi = x.astype(BF16)
    lo = (x - hi.astype(F32)).astype(BF16)
    return hi, lo


def _out_kernel(ya_ref, yr_ref, ga_ref, gb_ref, bg_ref, x_ref, watt_ref, wret_ref, wout_ref,
                gffn_ref, wr_hi_ref, wr_lo_ref, br_ref, x1_ref, xn_ref, gate_ref):
    ya = jnp.dot(ya_ref[...], watt_ref[...], preferred_element_type=F32)
    yr = jnp.dot(yr_ref[...], wret_ref[...], preferred_element_type=F32)
    merged = []
    for cb in range(D_MODEL // LANES):
        cs = slice(cb * LANES, (cb + 1) * LANES)
        ga = jax.nn.sigmoid(ga_ref[cb].astype(F32) + bg_ref[:, cs])
        gb = jax.nn.sigmoid(gb_ref[cb].astype(F32) + bg_ref[:, D_MODEL + cb * LANES:D_MODEL + (cb + 1) * LANES])
        merged.append((ga * ya[:, cs] + gb * yr[:, cs]).astype(BF16))
    merged = jnp.concatenate(merged, axis=-1)
    x1 = x_ref[...] + jnp.dot(merged, wout_ref[...], preferred_element_type=F32)
    x1_ref[...] = x1
    xn = x1 * lax.rsqrt(jnp.mean(x1 * x1, axis=-1, keepdims=True) + EPS) * gffn_ref[...]
    xn_ref[...] = xn.astype(BF16)

    xh, xl = _split_bf16(xn)
    logits = (jnp.dot(xh, wr_hi_ref[...], preferred_element_type=F32)
              + jnp.dot(xl, wr_hi_ref[...], preferred_element_type=F32)
              + jnp.dot(xh, wr_lo_ref[...], preferred_element_type=F32)) + br_ref[...]
    lane = lax.broadcasted_iota(jnp.int32, logits.shape, 1)
    lane_f = lane.astype(F32)
    is_group = (lane >= N_EXPERTS) & (lane < N_ROUTE)
    gl = jnp.where(is_group, logits, NEG_BIG)
    gmax = jnp.max(gl, axis=-1, keepdims=True)
    gsel = jnp.min(jnp.where(gl == gmax, lane_f, 1e9), axis=-1, keepdims=True) - N_EXPERTS
    p_group = 1.0 / jnp.sum(jnp.where(is_group, jnp.exp(logits - gmax), 0.0), axis=-1, keepdims=True)
    lo_lane = gsel * EXPERTS_PER_GROUP
    in_group = (lane_f >= lo_lane) & (lane_f < lo_lane + EXPERTS_PER_GROUP)
    el = jnp.where(in_group, logits, NEG_BIG)
    v1 = jnp.max(el, axis=-1, keepdims=True)
    i1 = jnp.min(jnp.where(el == v1, lane_f, 1e9), axis=-1, keepdims=True)
    el2 = jnp.where(lane_f == i1, NEG_BIG, el)
    v2 = jnp.max(el2, axis=-1, keepdims=True)
    i2 = jnp.min(jnp.where(el2 == v2, lane_f, 1e9), axis=-1, keepdims=True)
    e21 = jnp.exp(v2 - v1)
    w1 = p_group / (1.0 + e21)
    w2 = w1 * e21
    gate_ref[...] = jnp.where(lane_f == i1, w1, 0.0) + jnp.where(lane_f == i2, w2, 0.0)


def _out_stage(y_att, y_ret, proj, b_gate, x2d, w_att, w_ret, w_out, g_ffn, wr_hi, wr_lo, b_route):
    t = x2d.shape[0]
    ncb = D_MODEL // LANES

    def full(shape):
        return pl.BlockSpec(shape, lambda i: tuple(0 for _ in shape))

    return pl.pallas_call(
        _out_kernel,
        grid=(t // OUT_TM,),
        in_specs=[
            pl.BlockSpec((OUT_TM, ATT_W), lambda i: (i, 0)),
            pl.BlockSpec((OUT_TM, RET_V_W), lambda i: (i, 0)),
            pl.BlockSpec((ncb, OUT_TM, LANES), lambda i: (CB_GA // ncb, i, 0)),
            pl.BlockSpec((ncb, OUT_TM, LANES), lambda i: (CB_GB // ncb, i, 0)),
            full((1, 2 * D_MODEL)),
            pl.BlockSpec((OUT_TM, D_MODEL), lambda i: (i, 0)),
            full((ATT_W, D_MODEL)), full((RET_V_W, D_MODEL)), full((D_MODEL, D_MODEL)),
            full((1, D_MODEL)), full((D_MODEL, LANES)), full((D_MODEL, LANES)), full((1, LANES)),
        ],
        out_specs=[
            pl.BlockSpec((OUT_TM, D_MODEL), lambda i: (i, 0)),
            pl.BlockSpec((OUT_TM, D_MODEL), lambda i: (i, 0)),
            pl.BlockSpec((OUT_TM, LANES), lambda i: (i, 0)),
        ],
        out_shape=[
            jax.ShapeDtypeStruct((t, D_MODEL), F32),
            jax.ShapeDtypeStruct((t, D_MODEL), BF16),
            jax.ShapeDtypeStruct((t, LANES), F32),
        ],
        compiler_params=_cparams(("arbitrary",)),
        name="out_stage",
    )(y_att, y_ret, proj, proj, b_gate, x2d, w_att, w_ret, w_out, g_ffn, wr_hi, wr_lo, b_route)


MOE_TM = 1024


def _moe_kernel(xn_ref, gate_ref, x1_ref, w1_ref, w3_ref, w2_ref, o_ref):
    e = pl.program_id(1)

    @pl.when(e == 0)
    def _():
        o_ref[...] = x1_ref[...]

    x = xn_ref[...]
    a = jnp.dot(x, w1_ref[0], preferred_element_type=F32)
    b = jnp.dot(x, w3_ref[0], preferred_element_type=F32)
    gate = gate_ref[...]
    lane = lax.broadcasted_iota(jnp.int32, gate.shape, 1)
    g = jnp.sum(jnp.where(lane == e, gate, 0.0), axis=-1, keepdims=True)
    hidden = (a * jax.nn.sigmoid(a) * b * g).astype(BF16)
    o_ref[...] += jnp.dot(hidden, w2_ref[0], preferred_element_type=F32)


def _moe(xn, gate, x1, w1, w3, w2):
    t = xn.shape[0]
    return pl.pallas_call(
        _moe_kernel,
        grid=(t // MOE_TM, N_EXPERTS),
        in_specs=[
            pl.BlockSpec((MOE_TM, D_MODEL), lambda i, e: (i, 0)),
            pl.BlockSpec((MOE_TM, LANES), lambda i, e: (i, 0)),
            pl.BlockSpec((MOE_TM, D_MODEL), lambda i, e: (i, 0)),
            pl.BlockSpec((1, D_MODEL, EXPERT_FF), lambda i, e: (e, 0, 0)),
            pl.BlockSpec((1, D_MODEL, EXPERT_FF), lambda i, e: (e, 0, 0)),
            pl.BlockSpec((1, EXPERT_FF, D_MODEL), lambda i, e: (e, 0, 0)),
        ],
        out_specs=pl.BlockSpec((MOE_TM, D_MODEL), lambda i, e: (i, 0)),
        out_shape=jax.ShapeDtypeStruct((t, D_MODEL), F32),
        compiler_params=_cparams(("arbitrary", "arbitrary")),
        name="moe",
    )(xn, gate, x1, w1, w3, w2)


def _rope_tables_att():
    pos = jnp.arange(SEQ, dtype=F32)
    inv = ROPE_THETA ** (-jnp.arange(0, ATT_HEAD_DIM, 2, dtype=F32) / ATT_HEAD_DIM)
    ang = pos[:, None] * inv[None, :]
    cos, sin = jnp.cos(ang), jnp.sin(ang)
    reps = LANES // ATT_HEAD_DIM
    cos_full = jnp.tile(jnp.concatenate([cos, cos], axis=-1), (1, reps))
    sin_full = jnp.tile(jnp.concatenate([-sin, sin], axis=-1), (1, reps))
    return cos_full, sin_full


def _rope_tables_ret():
    pos = jnp.arange(SEQ, dtype=F32)
    inv = 1.0 / (ROPE_THETA ** jnp.linspace(0.0, 1.0, RET_QK_DIM // 2, dtype=F32))
    ang = pos[:, None] * inv[None, :]
    return jnp.cos(ang), jnp.sin(ang)


def _decay_tables():
    c = RET_CHUNK
    log_gamma = jnp.log(1.0 - jnp.exp2(-5.0 - jnp.arange(RET_HEADS, dtype=F32)))
    idx = jnp.arange(c, dtype=F32)
    diff = idx[:, None] - idx[None, :]
    decay = jnp.where(diff >= 0, jnp.exp(log_gamma[:, None, None] * jnp.maximum(diff, 0.0)), 0.0)
    zeta = jnp.exp(log_gamma[:, None] * (c - 1 - idx))
    xi = jnp.exp(log_gamma[:, None] * (idx + 1))
    gamma_c = jnp.exp(log_gamma * c)
    bc = lambda a: jnp.broadcast_to(a[:, :, None], (RET_HEADS, c, RET_QK_DIM))
    return gamma_c, decay, bc(xi), bc(zeta)


def kernel(x, g_norm_mix, w_in, b_merge_gate, g_q, g_k, w_branch_att, g_ret_norm, w_branch_ret,
           w_out, g_norm_ffn, w_router_group, b_router_group, w_router_expert, b_router_expert,
           w1, w3, w2):
    batch = x.shape[0]
    t = batch * SEQ
    cos_a, sin_a = _rope_tables_att()
    cos_r, sin_r = _rope_tables_ret()
    gamma_c, decay, xi, zeta = _decay_tables()
    xf = x.reshape(t, D_MODEL)
    for l in range(g_norm_mix.shape[0]):
        proj = _inproj(xf, g_norm_mix[l][None, :], w_in[l].astype(BF16))
        reps = LANES // ATT_HEAD_DIM
        y_att = _attention(proj, cos_a, sin_a, jnp.tile(g_q[l], reps)[None, :],
                           jnp.tile(g_k[l], reps)[None, :], batch)
        y_ret = _retention(proj, gamma_c, cos_r, sin_r, decay, xi, zeta,
                           g_ret_norm[l][:, None, :], batch)
        w_route = jnp.concatenate(
            [w_router_expert[l], w_router_group[l],
             jnp.zeros((D_MODEL, LANES - N_ROUTE), F32)], axis=-1)
        wr_hi, wr_lo = _split_bf16(w_route)
        b_route = jnp.concatenate(
            [b_router_expert[l], b_router_group[l], jnp.zeros((LANES - N_ROUTE,), F32)])[None, :]
        x1, xn2, gate = _out_stage(
            y_att, y_ret, proj, b_merge_gate[l][None, :], xf,
            w_branch_att[l].astype(BF16), w_branch_ret[l].astype(BF16), w_out[l].astype(BF16),
            g_norm_ffn[l][None, :], wr_hi, wr_lo, b_route)
        xf = _moe(xn2, gate, x1, w1[l].astype(BF16), w3[l].astype(BF16), w2[l].astype(BF16))
    return xf.reshape(batch, SEQ, D_MODEL)
```

```python
import functools

import numpy as np

import jax
import jax.numpy as jnp
from jax import lax
from jax.experimental import pallas as pl
from jax.experimental.pallas import tpu as pltpu

F32 = jnp.float32
BF16 = jnp.bfloat16

D_MODEL = 1024
SEQ = 2048
ATT_HEADS = 16
ATT_HEAD_DIM = 64
ATT_W = ATT_HEADS * ATT_HEAD_DIM
ROPE_THETA = 10000.0
RET_HEADS = 4
RET_QK_DIM = 256
RET_V_DIM = 512
RET_QK_W = RET_HEADS * RET_QK_DIM
RET_V_W = RET_HEADS * RET_V_DIM
N_GROUPS = 4
EXPERTS_PER_GROUP = 8
N_EXPERTS = N_GROUPS * EXPERTS_PER_GROUP
EXPERT_FF = 256
EPS = 1e-6
ATT_IN_W = 3 * ATT_W
REST_IN_W = 2 * RET_QK_W + 2 * RET_V_W + 2 * D_MODEL

LANES = 128
CB_QA, CB_KA, CB_VA = 0, 8, 16
CB_QR, CB_KR, CB_VR, CB_GR, CB_GA, CB_GB = 0, 8, 16, 32, 48, 56

ATT_BLOCK = 128
ATT_CLASSES = 16
ROWS_PER_CLASS = SEQ // ATT_CLASSES
ATT_GROUP = 4
RET_CHUNK = 256
NEG_BIG = -1e30
VMEM_LIMIT = 48 * 1024 * 1024


def _cparams(sem):
    return pltpu.CompilerParams(dimension_semantics=sem, vmem_limit_bytes=VMEM_LIMIT)


def _aligned(x, k):
    return x if isinstance(x, int) else pl.multiple_of(x, k)


def _rows_from_lane_blocks(ref, lead, width, n):
    return jnp.concatenate([ref[lead + (slice(None), slice(jj * width, (jj + 1) * width))]
                            for jj in range(n)], axis=0)


IN_TM = 1024
IN_TN = 1024
IN_CLS = IN_TM // ROWS_PER_CLASS


def _inproj_kernel(x_ref, g_ref, w_ref, o_ref, xn_sc, *, class_major):
    @pl.when(pl.program_id(1) == 0)
    def _():
        def norm(x):
            ms = jnp.mean(x * x, axis=-1, keepdims=True)
            return (x * lax.rsqrt(ms + EPS) * g_ref[...]).astype(BF16)

        if class_major:
            for jj in range(IN_CLS):
                xn_sc[jj * ROWS_PER_CLASS:(jj + 1) * ROWS_PER_CLASS, :] = norm(
                    x_ref[0, :, jj * D_MODEL:(jj + 1) * D_MODEL])
        else:
            xn_sc[...] = norm(x_ref[...])

    xn = xn_sc[...]
    for c2 in range(IN_TN // 256):
        acc = jnp.dot(xn, w_ref[:, c2 * 256:(c2 + 1) * 256], preferred_element_type=F32)
        o_ref[2 * c2] = acc[:, :LANES].astype(BF16)
        o_ref[2 * c2 + 1] = acc[:, LANES:].astype(BF16)


def _inproj(x, g, w_bf16, class_major):
    batch = x.shape[0]
    t = batch * SEQ
    width = w_bf16.shape[1]
    if class_major:
        xv = x.reshape(batch, ROWS_PER_CLASS, ATT_CLASSES * D_MODEL)
        per_seq = ATT_CLASSES // IN_CLS
        x_spec = pl.BlockSpec((1, ROWS_PER_CLASS, IN_CLS * D_MODEL),
                              lambda i, j: (i // per_seq, 0, i % per_seq))
    else:
        xv = x.reshape(t, D_MODEL)
        x_spec = pl.BlockSpec((IN_TM, D_MODEL), lambda i, j: (i, 0))
    return pl.pallas_call(
        functools.partial(_inproj_kernel, class_major=class_major),
        grid=(t // IN_TM, width // IN_TN),
        in_specs=[
            x_spec,
            pl.BlockSpec((1, D_MODEL), lambda i, j: (0, 0)),
            pl.BlockSpec((D_MODEL, IN_TN), lambda i, j: (0, j)),
        ],
        out_specs=pl.BlockSpec((IN_TN // LANES, IN_TM, LANES), lambda i, j: (j, i, 0)),
        out_shape=jax.ShapeDtypeStruct((width // LANES, t, LANES), BF16),
        scratch_shapes=[pltpu.VMEM((IN_TM, D_MODEL), BF16)],
        compiler_params=_cparams(("arbitrary", "arbitrary")),
        name="inproj_att" if class_major else "inproj_rest",
    )(xv, g, w_bf16)


PREP_ROWS = 256
SEG4 = ATT_BLOCK // 4
SEG1 = ATT_BLOCK // ATT_CLASSES


def _att_bias_tables():
    def tile(qpos, kpos):
        d = qpos[:, None] - kpos[None, :]
        one = np.where((d >= 0) & (d <= ATT_BLOCK), 0.0, NEG_BIG).astype(np.float32)
        return np.concatenate([one, one], axis=0)

    u = np.arange(ATT_BLOCK)
    q4 = 4 * (u % SEG4) + u // SEG4
    q1 = ATT_CLASSES * (u % SEG1) + u // SEG1
    w = np.arange(2 * ATT_BLOCK)
    k1 = ATT_CLASSES * (w % SEG1) + w // (2 * SEG1) + ATT_BLOCK * ((w // SEG1) % 2 - 1)
    return (tile(q1, k1), tile(q1, q1), tile(q4, np.concatenate([q4 - ATT_BLOCK, q4])), tile(q4, q4),
            tile(u, u))


def _att_kernel(q_ref, k_ref, v_ref, cos_ref, sin_ref, gq_ref, gk_ref,
                b1_ref, b1f_ref, b4_ref, b4f_ref, b16_ref, o_ref,
                qj, kj, vj, qb, kb, vb, acc_sc, m_sc, l_sc):
    lane = lax.broadcasted_iota(jnp.int32, (1, LANES), 1)
    head0 = lane < ATT_HEAD_DIM
    first_half = (lane % ATT_HEAD_DIM) < (ATT_HEAD_DIM // 2)
    seg = (lax.broadcasted_iota(jnp.int32, (LANES, LANES), 0) // ATT_HEAD_DIM
           == lax.broadcasted_iota(jnp.int32, (LANES, LANES), 1) // ATT_HEAD_DIM).astype(BF16)
    scale = ATT_HEAD_DIM ** -0.5

    def prep(ci, carry):
        rows = pl.ds(pl.multiple_of(ci * PREP_ROWS, PREP_ROWS), PREP_ROWS)
        cos = cos_ref[rows, :]
        sin = sin_ref[rows, :]

        def norm_rope(src, g_ref):
            x = src[0, rows, :].astype(F32)
            x2 = x * x
            hi = x2.astype(BF16)
            lo = (x2 - hi.astype(F32)).astype(BF16)
            ss = (jnp.dot(hi, seg, preferred_element_type=F32)
                  + jnp.dot(lo, seg, preferred_element_type=F32))
            xn = x * lax.rsqrt(ss * (1.0 / ATT_HEAD_DIM) + EPS) * g_ref[...]
            swapped = jnp.where(first_half,
                                pltpu.roll(xn, LANES - ATT_HEAD_DIM // 2, 1),
                                pltpu.roll(xn, ATT_HEAD_DIM // 2, 1))
            return xn * cos + swapped * sin

        xq = norm_rope(q_ref, gq_ref) * scale
        qj[rows, :] = xq
        qb[0, rows, :] = jnp.where(head0, xq, 0.0).astype(BF16)
        qb[1, rows, :] = jnp.where(head0, 0.0, xq).astype(BF16)
        xk = norm_rope(k_ref, gk_ref)
        kj[rows, :] = xk
        kb[rows, :] = xk.astype(BF16)
        xv = v_ref[0, rows, :]
        vj[rows, :] = xv.astype(F32)
        vb[rows, :] = xv
        return carry

    lax.fori_loop(0, SEQ // PREP_ROWS, prep, 0)

    nt_dims = (((1,), (1,)), ((), ()))

    def attend(q2, kk, vv, bias):
        s = lax.dot_general(q2, kk, nt_dims, preferred_element_type=F32) + bias
        m = jnp.max(s, axis=-1, keepdims=True)
        e = jnp.exp(s - m).astype(BF16)
        v1 = jnp.concatenate([vv, jnp.ones(vv.shape, BF16)], axis=1)
        r = jnp.dot(e, v1, preferred_element_type=F32)
        top, bot = r[:ATT_BLOCK], r[ATT_BLOCK:]
        return (jnp.where(head0, top[:, :LANES], bot[:, :LANES]),
                jnp.where(head0, m[:ATT_BLOCK], m[ATT_BLOCK:]),
                jnp.where(head0, top[:, LANES:], bot[:, LANES:]))

    def gather(ref, pieces):
        return jnp.concatenate([ref[rows, :] for rows in pieces], axis=0)

    def gather_q(pieces):
        return jnp.concatenate([qb[0, rows, :] for rows in pieces] + [qb[1, rows, :] for rows in pieces], axis=0)

    def store(p, pieces, n, a, m, l):
        for idx, rows in enumerate(pieces):
            acc_sc[p, rows, :] = a[idx * n:(idx + 1) * n]
            m_sc[p, rows, :] = m[idx * n:(idx + 1) * n]
            l_sc[p, rows, :] = l[idx * n:(idx + 1) * n]

    def p16_body(g, carry):
        for u in range(ATT_GROUP):
            rows = [pl.ds(pl.multiple_of((g * ATT_GROUP + u) * ATT_BLOCK, ATT_BLOCK), ATT_BLOCK)]
            a, m, l = attend(gather_q(rows), kb[rows[0], :], vb[rows[0], :], b16_ref[...])
            store(2, rows, ATT_BLOCK, a, m, l)
        return carry

    lax.fori_loop(0, ATT_CLASSES // ATT_GROUP, p16_body, 0)

    def p4_body(c, carry):
        def segs(n):
            return [pl.ds(pl.multiple_of((4 * a + c) * ATT_BLOCK + SEG4 * n, SEG4), SEG4) for a in range(4)]

        for n in range(SEQ // 4 // ATT_BLOCK):
            cur = segs(n)
            keys = cur if n == 0 else segs(n - 1) + cur
            bias = b4f_ref[...] if n == 0 else b4_ref[...]
            a, m, l = attend(gather_q(cur), gather(kb, keys), gather(vb, keys), bias)
            store(1, cur, SEG4, a, m, l)
        return carry

    lax.fori_loop(0, 4, p4_body, 0)

    def p1_task(n, first):
        cur = [pl.ds(_aligned(j * ATT_BLOCK + SEG1 * n, SEG1), SEG1) for j in range(ATT_CLASSES)]
        if first:
            keys, bias = cur, b1f_ref[...]
        else:
            keys = [pl.ds(_aligned(j * ATT_BLOCK + SEG1 * n - SEG1, SEG1), 2 * SEG1) for j in range(ATT_CLASSES)]
            bias = b1_ref[...]
        xq = gather(qj, cur)
        q2 = jnp.concatenate([jnp.where(head0, xq, 0.0), jnp.where(head0, 0.0, xq)], axis=0).astype(BF16)
        a, m, l = attend(q2, gather(kj, keys).astype(BF16), gather(vj, keys).astype(BF16), bias)
        store(0, cur, SEG1, a, m, l)

    for n in range(ATT_GROUP):
        p1_task(n, n == 0)

    def p1_body(it, carry):
        for u in range(ATT_GROUP):
            p1_task(it * ATT_GROUP + u, False)
        return carry

    lax.fori_loop(1, SEQ // ATT_BLOCK // ATT_GROUP, p1_body, 0)

    def merge(ci, carry):
        rows = pl.ds(pl.multiple_of(ci * PREP_ROWS, PREP_ROWS), PREP_ROWS)
        ms = [m_sc[p, rows, :] for p in range(3)]
        m_all = jnp.maximum(jnp.maximum(ms[0], ms[1]), ms[2])
        num = jnp.zeros((PREP_ROWS, LANES), F32)
        den = jnp.zeros((PREP_ROWS, LANES), F32)
        for p in range(3):
            w = jnp.exp(ms[p] - m_all)
            num = num + w * acc_sc[p, rows, :]
            den = den + w * l_sc[p, rows, :]
        o_ref[rows, :] = (num / den).astype(BF16)
        return carry

    lax.fori_loop(0, SEQ // PREP_ROWS, merge, 0)


def _attention(proj_att, cos_a, sin_a, gq, gk, batch):
    t = batch * SEQ
    hp = ATT_W // LANES

    def col(cb):
        return pl.BlockSpec((1, SEQ, LANES), lambda b, h: (cb + h, b, 0))

    def const(shape):
        return pl.BlockSpec(shape, lambda b, h: tuple(0 for _ in shape))

    biases = [jnp.asarray(b) for b in _att_bias_tables()]
    row_f32 = pltpu.VMEM((SEQ, LANES), F32)
    row_bf16 = pltpu.VMEM((SEQ, LANES), BF16)
    stat = pltpu.VMEM((3, SEQ, LANES), F32)
    return pl.pallas_call(
        _att_kernel,
        grid=(batch, hp),
        in_specs=[col(CB_QA), col(CB_KA), col(CB_VA),
                  const((SEQ, LANES)), const((SEQ, LANES)), const((1, LANES)), const((1, LANES))]
                 + [const(b.shape) for b in biases],
        out_specs=pl.BlockSpec((SEQ, LANES), lambda b, h: (b, h)),
        out_shape=jax.ShapeDtypeStruct((t, ATT_W), BF16),
        scratch_shapes=[row_f32] * 3 + [pltpu.VMEM((2, SEQ, LANES), BF16)] + [row_bf16] * 2 + [stat] * 3,
        compiler_params=_cparams(("arbitrary", "arbitrary")),
        name="dilated_attention",
    )(proj_att, proj_att, proj_att, cos_a, sin_a, gq, gk, *biases)


def _ret_kernel(gam_ref, q_ref, k_ref, v_ref, g_ref, cos_ref, sin_ref, decay_ref, xi_ref, zeta_ref,
                gn_ref, o_ref, state_sc):
    h = pl.program_id(1)
    gamma_c = gam_ref[h]
    state_sc[...] = jnp.zeros_like(state_sc)
    c = RET_CHUNK
    nt_dims = (((1,), (1,)), ((), ()))
    tn_dims = (((0,), (0,)), ((), ()))
    kscale = RET_QK_DIM ** -0.5

    def chunk(n, carry):
        rs = pl.ds(pl.multiple_of(n * c, c), c)
        cos = cos_ref[rs, :]
        sin = sin_ref[rs, :]

        def rot(ref):
            x1 = ref[0, rs, :].astype(F32)
            x2 = ref[1, rs, :].astype(F32)
            return x1 * cos - x2 * sin, x2 * cos + x1 * sin

        q1, q2 = rot(q_ref)
        k1, k2 = rot(k_ref)
        q = jnp.concatenate([q1, q2], axis=-1)
        k = jnp.concatenate([k1, k2], axis=-1) * kscale
        v = jnp.concatenate([v_ref[i, rs, :] for i in range(RET_V_DIM // LANES)], axis=-1)
        inner = lax.dot_general(q.astype(BF16), k.astype(BF16), nt_dims,
                                preferred_element_type=F32) * decay_ref[0]
        y = jnp.dot(inner.astype(BF16), v, preferred_element_type=F32)
        state = state_sc[...]
        qx = (q * xi_ref[0]).astype(BF16)
        y = y + jnp.dot(qx, state.astype(BF16), preferred_element_type=F32)
        kz = (k * zeta_ref[0]).astype(BF16)
        state_sc[...] = state * gamma_c + lax.dot_general(kz, v, tn_dims, preferred_element_type=F32)
        yn = y * lax.rsqrt(jnp.mean(y * y, axis=-1, keepdims=True) + EPS) * gn_ref[0]
        g = jnp.concatenate([g_ref[i, rs, :] for i in range(RET_V_DIM // LANES)], axis=-1).astype(F32)
        o_ref[rs, :] = (yn * (g * jax.nn.sigmoid(g))).astype(BF16)
        return carry

    lax.fori_loop(0, SEQ // c, chunk, 0)


def _retention(proj, gamma_c, cos_r, sin_r, decay, xi, zeta, g_ret, batch):
    t = batch * SEQ
    nq = RET_QK_DIM // LANES
    nv = RET_V_DIM // LANES

    def cols(cb, n):
        return pl.BlockSpec((n, SEQ, LANES), lambda b, h: (cb // n + h, b, 0))

    tab = pl.BlockSpec((SEQ, LANES), lambda b, h: (0, 0))
    return pl.pallas_call(
        _ret_kernel,
        grid=(batch, RET_HEADS),
        in_specs=[
            pl.BlockSpec(memory_space=pltpu.SMEM),
            cols(CB_QR, nq), cols(CB_KR, nq), cols(CB_VR, nv), cols(CB_GR, nv),
            tab, tab,
            pl.BlockSpec((1, RET_CHUNK, RET_CHUNK), lambda b, h: (h, 0, 0)),
            pl.BlockSpec((1, RET_CHUNK, RET_QK_DIM), lambda b, h: (h, 0, 0)),
            pl.BlockSpec((1, RET_CHUNK, RET_QK_DIM), lambda b, h: (h, 0, 0)),
            pl.BlockSpec((1, 1, RET_V_DIM), lambda b, h: (h, 0, 0)),
        ],
        out_specs=pl.BlockSpec((SEQ, RET_V_DIM), lambda b, h: (b, h)),
        out_shape=jax.ShapeDtypeStruct((t, RET_V_W), BF16),
        scratch_shapes=[pltpu.VMEM((RET_QK_DIM, RET_V_DIM), F32)],
        compiler_params=_cparams(("arbitrary", "arbitrary")),
        name="retention",
    )(gamma_c, proj, proj, proj, proj, cos_r, sin_r, decay, xi, zeta, g_ret)


OUT_CLS = 4
OUT_TM = OUT_CLS * ROWS_PER_CLASS
N_ROUTE = N_EXPERTS + N_GROUPS


def _split_bf16(x):
    hi = x.astype(BF16)
    lo = (x - hi.astype(F32)).astype(BF16)
    return hi, lo


def _out_kernel(ya_ref, yr_ref, ga_ref, gb_ref, bg_ref, x_ref, watt_ref, wret_ref, wout_ref,
                gffn_ref, wr_hi_ref, wr_lo_ref, br_ref, x1_ref, xn_ref, gate_ref):
    ya = jnp.dot(ya_ref[...], watt_ref[...], preferred_element_type=F32)
    yr = jnp.dot(_rows_from_lane_blocks(yr_ref, (0,), RET_V_W, OUT_CLS), wret_ref[...],
                 preferred_element_type=F32)
    merged = []
    for cb in range(D_MODEL // LANES):
        cs = slice(cb * LANES, (cb + 1) * LANES)
        ga = jax.nn.sigmoid(_rows_from_lane_blocks(ga_ref, (cb, 0), LANES, OUT_CLS).astype(F32)
                            + bg_ref[:, cs])
        gb = jax.nn.sigmoid(_rows_from_lane_blocks(gb_ref, (cb, 0), LANES, OUT_CLS).astype(F32)
                            + bg_ref[:, D_MODEL + cb * LANES:D_MODEL + (cb + 1) * LANES])
        merged.append((ga * ya[:, cs] + gb * yr[:, cs]).astype(BF16))
    merged = jnp.concatenate(merged, axis=-1)
    x1 = (_rows_from_lane_blocks(x_ref, (0,), D_MODEL, OUT_CLS)
          + jnp.dot(merged, wout_ref[...], preferred_element_type=F32))
    xn = x1 * lax.rsqrt(jnp.mean(x1 * x1, axis=-1, keepdims=True) + EPS) * gffn_ref[...]

    xh, xl = _split_bf16(xn)
    logits = (jnp.dot(xh, wr_hi_ref[...], preferred_element_type=F32)
              + jnp.dot(xl, wr_hi_ref[...], preferred_element_type=F32)
              + jnp.dot(xh, wr_lo_ref[...], preferred_element_type=F32)) + br_ref[...]
    lane = lax.broadcasted_iota(jnp.int32, logits.shape, 1)
    lane_f = lane.astype(F32)
    is_group = (lane >= N_EXPERTS) & (lane < N_ROUTE)
    gl = jnp.where(is_group, logits, NEG_BIG)
    gmax = jnp.max(gl, axis=-1, keepdims=True)
    gsel = jnp.min(jnp.where(gl == gmax, lane_f, 1e9), axis=-1, keepdims=True) - N_EXPERTS
    p_group = 1.0 / jnp.sum(jnp.where(is_group, jnp.exp(logits - gmax), 0.0), axis=-1, keepdims=True)
    lo_lane = gsel * EXPERTS_PER_GROUP
    in_group = (lane_f >= lo_lane) & (lane_f < lo_lane + EXPERTS_PER_GROUP)
    el = jnp.where(in_group, logits, NEG_BIG)
    v1 = jnp.max(el, axis=-1, keepdims=True)
    i1 = jnp.min(jnp.where(el == v1, lane_f, 1e9), axis=-1, keepdims=True)
    el2 = jnp.where(lane_f == i1, NEG_BIG, el)
    v2 = jnp.max(el2, axis=-1, keepdims=True)
    i2 = jnp.min(jnp.where(el2 == v2, lane_f, 1e9), axis=-1, keepdims=True)
    e21 = jnp.exp(v2 - v1)
    w1 = p_group / (1.0 + e21)
    w2 = w1 * e21
    gate = jnp.where(lane_f == i1, w1, 0.0) + jnp.where(lane_f == i2, w2, 0.0)

    xn_bf = xn.astype(BF16)
    for jj in range(OUT_CLS):
        rs = slice(jj * ROWS_PER_CLASS, (jj + 1) * ROWS_PER_CLASS)
        x1_ref[0, :, jj * D_MODEL:(jj + 1) * D_MODEL] = x1[rs]
        xn_ref[0, :, jj * D_MODEL:(jj + 1) * D_MODEL] = xn_bf[rs]
        gate_ref[0, :, jj * LANES:(jj + 1) * LANES] = gate[rs]


def _out_stage(y_att, y_ret, proj, b_gate, x, w_att, w_ret, w_out, g_ffn, wr_hi, wr_lo, b_route):
    batch = x.shape[0]
    t = batch * SEQ
    ncb = D_MODEL // LANES
    per_seq = ATT_CLASSES // OUT_CLS

    def lanes_view(a, width):
        return a.reshape(batch, ROWS_PER_CLASS, ATT_CLASSES * width)

    def lanes_spec(width):
        return pl.BlockSpec((1, ROWS_PER_CLASS, OUT_CLS * width), lambda b, q: (b, 0, q))

    def full(shape):
        return pl.BlockSpec(shape, lambda b, q: tuple(0 for _ in shape))

    def gate_cols(cb):
        return pl.BlockSpec((ncb, 1, ROWS_PER_CLASS, OUT_CLS * LANES), lambda b, q: (cb // ncb, b, 0, q))

    proj_v = proj.reshape(proj.shape[0], batch, ROWS_PER_CLASS, ATT_CLASSES * LANES)
    x1, xn, gate = pl.pallas_call(
        _out_kernel,
        grid=(batch, per_seq),
        in_specs=[
            pl.BlockSpec((OUT_TM, ATT_W), lambda b, q: (b * per_seq + q, 0)),
            lanes_spec(RET_V_W),
            gate_cols(CB_GA), gate_cols(CB_GB),
            full((1, 2 * D_MODEL)),
            lanes_spec(D_MODEL),
            full((ATT_W, D_MODEL)), full((RET_V_W, D_MODEL)), full((D_MODEL, D_MODEL)),
            full((1, D_MODEL)), full((D_MODEL, LANES)), full((D_MODEL, LANES)), full((1, LANES)),
        ],
        out_specs=[lanes_spec(D_MODEL), lanes_spec(D_MODEL), lanes_spec(LANES)],
        out_shape=[
            jax.ShapeDtypeStruct((batch, ROWS_PER_CLASS, ATT_CLASSES * D_MODEL), F32),
            jax.ShapeDtypeStruct((batch, ROWS_PER_CLASS, ATT_CLASSES * D_MODEL), BF16),
            jax.ShapeDtypeStruct((batch, ROWS_PER_CLASS, ATT_CLASSES * LANES), F32),
        ],
        compiler_params=_cparams(("arbitrary", "arbitrary")),
        name="out_stage",
    )(y_att, lanes_view(y_ret, RET_V_W), proj_v, proj_v, b_gate, lanes_view(x, D_MODEL),
      w_att, w_ret, w_out, g_ffn, wr_hi, wr_lo, b_route)
    return x1.reshape(t, D_MODEL), xn.reshape(t, D_MODEL), gate.reshape(t, LANES)


MOE_TM = 1024


def _moe_kernel(xn_ref, gate_ref, x1_ref, w1_ref, w3_ref, w2_ref, o_ref):
    e = pl.program_id(1)

    @pl.when(e == 0)
    def _():
        o_ref[...] = x1_ref[...]

    x = xn_ref[...]
    a = jnp.dot(x, w1_ref[0], preferred_element_type=F32)
    b = jnp.dot(x, w3_ref[0], preferred_element_type=F32)
    gate = gate_ref[...]
    lane = lax.broadcasted_iota(jnp.int32, gate.shape, 1)
    g = jnp.sum(jnp.where(lane == e, gate, 0.0), axis=-1, keepdims=True)
    hidden = (a * jax.nn.sigmoid(a) * b * g).astype(BF16)
    o_ref[...] += jnp.dot(hidden, w2_ref[0], preferred_element_type=F32)


def _moe(xn, gate, x1, w1, w3, w2):
    t = xn.shape[0]
    return pl.pallas_call(
        _moe_kernel,
        grid=(t // MOE_TM, N_EXPERTS),
        in_specs=[
            pl.BlockSpec((MOE_TM, D_MODEL), lambda i, e: (i, 0)),
            pl.BlockSpec((MOE_TM, LANES), lambda i, e: (i, 0)),
            pl.BlockSpec((MOE_TM, D_MODEL), lambda i, e: (i, 0)),
            pl.BlockSpec((1, D_MODEL, EXPERT_FF), lambda i, e: (e, 0, 0)),
            pl.BlockSpec((1, D_MODEL, EXPERT_FF), lambda i, e: (e, 0, 0)),
            pl.BlockSpec((1, EXPERT_FF, D_MODEL), lambda i, e: (e, 0, 0)),
        ],
        out_specs=pl.BlockSpec((MOE_TM, D_MODEL), lambda i, e: (i, 0)),
        out_shape=jax.ShapeDtypeStruct((t, D_MODEL), F32),
        compiler_params=_cparams(("arbitrary", "arbitrary")),
        name="moe",
    )(xn, gate, x1, w1, w3, w2)


def _class_major(table):
    return table.reshape(ROWS_PER_CLASS, ATT_CLASSES, -1).transpose(1, 0, 2).reshape(SEQ, -1)


def _rope_tables_att():
    pos = jnp.arange(SEQ, dtype=F32)
    inv = ROPE_THETA ** (-jnp.arange(0, ATT_HEAD_DIM, 2, dtype=F32) / ATT_HEAD_DIM)
    ang = pos[:, None] * inv[None, :]
    cos, sin = jnp.cos(ang), jnp.sin(ang)
    reps = LANES // ATT_HEAD_DIM
    cos_full = jnp.tile(jnp.concatenate([cos, cos], axis=-1), (1, reps))
    sin_full = jnp.tile(jnp.concatenate([-sin, sin], axis=-1), (1, reps))
    return _class_major(cos_full), _class_major(sin_full)


def _rope_tables_ret():
    pos = jnp.arange(SEQ, dtype=F32)
    inv = 1.0 / (ROPE_THETA ** jnp.linspace(0.0, 1.0, RET_QK_DIM // 2, dtype=F32))
    ang = pos[:, None] * inv[None, :]
    return jnp.cos(ang), jnp.sin(ang)


def _decay_tables():
    c = RET_CHUNK
    log_gamma = jnp.log(1.0 - jnp.exp2(-5.0 - jnp.arange(RET_HEADS, dtype=F32)))
    idx = jnp.arange(c, dtype=F32)
    diff = idx[:, None] - idx[None, :]
    decay = jnp.where(diff >= 0, jnp.exp(log_gamma[:, None, None] * jnp.maximum(diff, 0.0)), 0.0)
    zeta = jnp.exp(log_gamma[:, None] * (c - 1 - idx))
    xi = jnp.exp(log_gamma[:, None] * (idx + 1))
    gamma_c = jnp.exp(log_gamma * c)
    bc = lambda a: jnp.broadcast_to(a[:, :, None], (RET_HEADS, c, RET_QK_DIM))
    return gamma_c, decay, bc(xi), bc(zeta)


def kernel(x, g_norm_mix, w_in, b_merge_gate, g_q, g_k, w_branch_att, g_ret_norm, w_branch_ret,
           w_out, g_norm_ffn, w_router_group, b_router_group, w_router_expert, b_router_expert,
           w1, w3, w2):
    batch = x.shape[0]
    t = batch * SEQ
    cos_a, sin_a = _rope_tables_att()
    cos_r, sin_r = _rope_tables_ret()
    gamma_c, decay, xi, zeta = _decay_tables()
    for l in range(g_norm_mix.shape[0]):
        g_mix = g_norm_mix[l][None, :]
        proj_att = _inproj(x, g_mix, w_in[l][:, :ATT_IN_W].astype(BF16), True)
        proj = _inproj(x, g_mix, w_in[l][:, ATT_IN_W:].astype(BF16), False)
        reps = LANES // ATT_HEAD_DIM
        y_att = _attention(proj_att, cos_a, sin_a, jnp.tile(g_q[l], reps)[None, :],
                           jnp.tile(g_k[l], reps)[None, :], batch)
        y_ret = _retention(proj, gamma_c, cos_r, sin_r, decay, xi, zeta,
                           g_ret_norm[l][:, None, :], batch)
        w_route = jnp.concatenate(
            [w_router_expert[l], w_router_group[l],
             jnp.zeros((D_MODEL, LANES - N_ROUTE), F32)], axis=-1)
        wr_hi, wr_lo = _split_bf16(w_route)
        b_route = jnp.concatenate(
            [b_router_expert[l], b_router_group[l], jnp.zeros((LANES - N_ROUTE,), F32)])[None, :]
        x1, xn2, gate = _out_stage(
            y_att, y_ret, proj, b_merge_gate[l][None, :], x,
            w_branch_att[l].astype(BF16), w_branch_ret[l].astype(BF16), w_out[l].astype(BF16),
            g_norm_ffn[l][None, :], wr_hi, wr_lo, b_route)
        x = _moe(xn2, gate, x1, w1[l].astype(BF16), w3[l].astype(BF16), w2[l].astype(BF16)
                 ).reshape(batch, SEQ, D_MODEL)
    return x
```

```python
import functools

import numpy as np

import jax
import jax.numpy as jnp
from jax import lax
from jax.experimental import pallas as pl
from jax.experimental.pallas import tpu as pltpu

F32 = jnp.float32
BF16 = jnp.bfloat16

D_MODEL = 1024
SEQ = 2048
ATT_HEADS = 16
ATT_HEAD_DIM = 64
ATT_W = ATT_HEADS * ATT_HEAD_DIM
ROPE_THETA = 10000.0
RET_HEADS = 4
RET_QK_DIM = 256
RET_V_DIM = 512
RET_QK_W = RET_HEADS * RET_QK_DIM
RET_V_W = RET_HEADS * RET_V_DIM
N_GROUPS = 4
EXPERTS_PER_GROUP = 8
N_EXPERTS = N_GROUPS * EXPERTS_PER_GROUP
EXPERT_FF = 256
EPS = 1e-6
ATT_IN_W = 3 * ATT_W
REST_IN_W = 2 * RET_QK_W + 2 * RET_V_W + 2 * D_MODEL

LANES = 128
CB_QA, CB_KA, CB_VA = 0, 8, 16
CB_QR, CB_KR, CB_VR, CB_GR, CB_GA, CB_GB = 0, 8, 16, 32, 48, 56

ATT_BLOCK = 128
ATT_CLASSES = 16
ROWS_PER_CLASS = SEQ // ATT_CLASSES
ATT_GROUP = 4
RET_CHUNK = 256
NEG_BIG = -1e30
VMEM_LIMIT = 48 * 1024 * 1024


def _cparams(sem):
    return pltpu.CompilerParams(dimension_semantics=sem, vmem_limit_bytes=VMEM_LIMIT)


def _aligned(x, k):
    return x if isinstance(x, int) else pl.multiple_of(x, k)


IN_TM = 1024
IN_TN = 1024
IN_NORM_ROWS = 256


def _inproj_kernel(x_ref, g_ref, w_ref, o_ref, xn_sc, *stage, class_major):
    @pl.when(pl.program_id(1) == 0)
    def _():
        def norm(x):
            ms = jnp.mean(x * x, axis=-1, keepdims=True)
            return x * lax.rsqrt(ms + EPS) * g_ref[...]

        if class_major:
            (xs,) = stage

            def norm_rows(ci, carry):
                rows = pl.ds(pl.multiple_of(ci * IN_NORM_ROWS, IN_NORM_ROWS), IN_NORM_ROWS)
                xn = norm(x_ref[rows, :])
                for c in range(D_MODEL // LANES):
                    xs[c, rows, :] = xn[:, c * LANES:(c + 1) * LANES]
                return carry

            lax.fori_loop(0, SEQ // IN_NORM_ROWS, norm_rows, 0)

            def gather_class(j, carry):
                dst = pl.ds(pl.multiple_of(j * ROWS_PER_CLASS, ROWS_PER_CLASS), ROWS_PER_CLASS)
                for c in range(D_MODEL // LANES):
                    xn_sc[dst, c * LANES:(c + 1) * LANES] = xs[
                        c, pl.ds(j, ROWS_PER_CLASS, stride=ATT_CLASSES), :].astype(BF16)
                return carry

            lax.fori_loop(0, ATT_CLASSES, gather_class, 0)
        else:
            xn_sc[...] = norm(x_ref[...]).astype(BF16)

    xn = xn_sc[...]
    for c2 in range(IN_TN // 256):
        acc = jnp.dot(xn, w_ref[:, c2 * 256:(c2 + 1) * 256], preferred_element_type=F32)
        o_ref[2 * c2] = acc[:, :LANES].astype(BF16)
        o_ref[2 * c2 + 1] = acc[:, LANES:].astype(BF16)


def _inproj(x2d, g, w_bf16, class_major):
    t = x2d.shape[0]
    width = w_bf16.shape[1]
    tm = SEQ if class_major else IN_TM
    return pl.pallas_call(
        functools.partial(_inproj_kernel, class_major=class_major),
        grid=(t // tm, width // IN_TN),
        in_specs=[
            pl.BlockSpec((tm, D_MODEL), lambda i, j: (i, 0)),
            pl.BlockSpec((1, D_MODEL), lambda i, j: (0, 0)),
            pl.BlockSpec((D_MODEL, IN_TN), lambda i, j: (0, j)),
        ],
        out_specs=pl.BlockSpec((IN_TN // LANES, tm, LANES), lambda i, j: (j, i, 0)),
        out_shape=jax.ShapeDtypeStruct((width // LANES, t, LANES), BF16),
        scratch_shapes=[pltpu.VMEM((tm, D_MODEL), BF16)]
                       + ([pltpu.VMEM((D_MODEL // LANES, tm, LANES), F32)] if class_major else []),
        compiler_params=_cparams(("arbitrary", "arbitrary")),
        name="inproj_att" if class_major else "inproj_rest",
    )(x2d, g, w_bf16)


PREP_ROWS = 256
SEG4 = ATT_BLOCK // 4
SEG1 = ATT_BLOCK // ATT_CLASSES


def _att_bias_tables():
    def tile(qpos, kpos):
        d = qpos[:, None] - kpos[None, :]
        one = np.where((d >= 0) & (d <= ATT_BLOCK), 0.0, NEG_BIG).astype(np.float32)
        return np.concatenate([one, one], axis=0)

    u = np.arange(ATT_BLOCK)
    q4 = 4 * (u % SEG4) + u // SEG4
    q1 = ATT_CLASSES * (u % SEG1) + u // SEG1
    w = np.arange(2 * ATT_BLOCK)
    k1 = ATT_CLASSES * (w % SEG1) + w // (2 * SEG1) + ATT_BLOCK * ((w // SEG1) % 2 - 1)
    return (tile(q1, k1), tile(q1, q1), tile(q4, np.concatenate([q4 - ATT_BLOCK, q4])), tile(q4, q4),
            tile(u, u))


def _att_kernel(q_ref, k_ref, v_ref, cos_ref, sin_ref, gq_ref, gk_ref,
                b1_ref, b1f_ref, b4_ref, b4f_ref, b16_ref, o_ref,
                qj, kj, vj, qb, kb, vb, acc_sc, m_sc, l_sc):
    lane = lax.broadcasted_iota(jnp.int32, (1, LANES), 1)
    head0 = lane < ATT_HEAD_DIM
    first_half = (lane % ATT_HEAD_DIM) < (ATT_HEAD_DIM // 2)
    seg = (lax.broadcasted_iota(jnp.int32, (LANES, LANES), 0) // ATT_HEAD_DIM
           == lax.broadcasted_iota(jnp.int32, (LANES, LANES), 1) // ATT_HEAD_DIM).astype(BF16)
    scale = ATT_HEAD_DIM ** -0.5

    def prep(ci, carry):
        rows = pl.ds(pl.multiple_of(ci * PREP_ROWS, PREP_ROWS), PREP_ROWS)
        cos = cos_ref[rows, :]
        sin = sin_ref[rows, :]

        def norm_rope(src, g_ref):
            x = src[0, rows, :].astype(F32)
            x2 = x * x
            hi = x2.astype(BF16)
            lo = (x2 - hi.astype(F32)).astype(BF16)
            ss = (jnp.dot(hi, seg, preferred_element_type=F32)
                  + jnp.dot(lo, seg, preferred_element_type=F32))
            xn = x * lax.rsqrt(ss * (1.0 / ATT_HEAD_DIM) + EPS) * g_ref[...]
            swapped = jnp.where(first_half,
                                pltpu.roll(xn, LANES - ATT_HEAD_DIM // 2, 1),
                                pltpu.roll(xn, ATT_HEAD_DIM // 2, 1))
            return xn * cos + swapped * sin

        xq = norm_rope(q_ref, gq_ref) * scale
        qj[rows, :] = xq
        qb[0, rows, :] = jnp.where(head0, xq, 0.0).astype(BF16)
        qb[1, rows, :] = jnp.where(head0, 0.0, xq).astype(BF16)
        xk = norm_rope(k_ref, gk_ref)
        kj[rows, :] = xk
        kb[rows, :] = xk.astype(BF16)
        xv = v_ref[0, rows, :]
        vj[rows, :] = xv.astype(F32)
        vb[rows, :] = xv
        return carry

    lax.fori_loop(0, SEQ // PREP_ROWS, prep, 0)

    nt_dims = (((1,), (1,)), ((), ()))

    def attend(q2, kk, vv, bias):
        s = lax.dot_general(q2, kk, nt_dims, preferred_element_type=F32) + bias
        m = jnp.max(s, axis=-1, keepdims=True)
        e = jnp.exp(s - m).astype(BF16)
        v1 = jnp.concatenate([vv, jnp.ones(vv.shape, BF16)], axis=1)
        r = jnp.dot(e, v1, preferred_element_type=F32)
        top, bot = r[:ATT_BLOCK], r[ATT_BLOCK:]
        return (jnp.where(head0, top[:, :LANES], bot[:, :LANES]),
                jnp.where(head0, m[:ATT_BLOCK], m[ATT_BLOCK:]),
                jnp.where(head0, top[:, LANES:], bot[:, LANES:]))

    def gather(ref, pieces):
        return jnp.concatenate([ref[rows, :] for rows in pieces], axis=0)

    def gather_q(pieces):
        return jnp.concatenate([qb[0, rows, :] for rows in pieces] + [qb[1, rows, :] for rows in pieces], axis=0)

    def store(p, pieces, n, a, m, l):
        for idx, rows in enumerate(pieces):
            acc_sc[p, rows, :] = a[idx * n:(idx + 1) * n]
            m_sc[p, rows, :] = m[idx * n:(idx + 1) * n]
            l_sc[p, rows, :] = l[idx * n:(idx + 1) * n]

    def p16_body(g, carry):
        for u in range(ATT_GROUP):
            rows = [pl.ds(pl.multiple_of((g * ATT_GROUP + u) * ATT_BLOCK, ATT_BLOCK), ATT_BLOCK)]
            a, m, l = attend(gather_q(rows), kb[rows[0], :], vb[rows[0], :], b16_ref[...])
            store(2, rows, ATT_BLOCK, a, m, l)
        return carry

    lax.fori_loop(0, ATT_CLASSES // ATT_GROUP, p16_body, 0)

    def p4_body(c, carry):
        def segs(n):
            return [pl.ds(pl.multiple_of((4 * a + c) * ATT_BLOCK + SEG4 * n, SEG4), SEG4) for a in range(4)]

        for n in range(SEQ // 4 // ATT_BLOCK):
            cur = segs(n)
            keys = cur if n == 0 else segs(n - 1) + cur
            bias = b4f_ref[...] if n == 0 else b4_ref[...]
            a, m, l = attend(gather_q(cur), gather(kb, keys), gather(vb, keys), bias)
            store(1, cur, SEG4, a, m, l)
        return carry

    lax.fori_loop(0, 4, p4_body, 0)

    def p1_task(n, first):
        cur = [pl.ds(_aligned(j * ATT_BLOCK + SEG1 * n, SEG1), SEG1) for j in range(ATT_CLASSES)]
        if first:
            keys, bias = cur, b1f_ref[...]
        else:
            keys = [pl.ds(_aligned(j * ATT_BLOCK + SEG1 * n - SEG1, SEG1), 2 * SEG1) for j in range(ATT_CLASSES)]
            bias = b1_ref[...]
        xq = gather(qj, cur)
        q2 = jnp.concatenate([jnp.where(head0, xq, 0.0), jnp.where(head0, 0.0, xq)], axis=0).astype(BF16)
        a, m, l = attend(q2, gather(kj, keys).astype(BF16), gather(vj, keys).astype(BF16), bias)
        store(0, cur, SEG1, a, m, l)

    for n in range(ATT_GROUP):
        p1_task(n, n == 0)

    def p1_body(it, carry):
        for u in range(ATT_GROUP):
            p1_task(it * ATT_GROUP + u, False)
        return carry

    lax.fori_loop(1, SEQ // ATT_BLOCK // ATT_GROUP, p1_body, 0)

    def merge(j, carry):
        rows = pl.ds(pl.multiple_of(j * ROWS_PER_CLASS, ROWS_PER_CLASS), ROWS_PER_CLASS)
        ms = [m_sc[p, rows, :] for p in range(3)]
        m_all = jnp.maximum(jnp.maximum(ms[0], ms[1]), ms[2])
        num = jnp.zeros((ROWS_PER_CLASS, LANES), F32)
        den = jnp.zeros((ROWS_PER_CLASS, LANES), F32)
        for p in range(3):
            w = jnp.exp(ms[p] - m_all)
            num = num + w * acc_sc[p, rows, :]
            den = den + w * l_sc[p, rows, :]
        qj[pl.ds(j, ROWS_PER_CLASS, stride=ATT_CLASSES), :] = num / den
        return carry

    lax.fori_loop(0, ATT_CLASSES, merge, 0)

    def emit(ci, carry):
        rows = pl.ds(pl.multiple_of(ci * PREP_ROWS, PREP_ROWS), PREP_ROWS)
        o_ref[rows, :] = qj[rows, :].astype(BF16)
        return carry

    lax.fori_loop(0, SEQ // PREP_ROWS, emit, 0)


def _attention(proj_att, cos_a, sin_a, gq, gk, batch):
    t = batch * SEQ
    hp = ATT_W // LANES

    def col(cb):
        return pl.BlockSpec((1, SEQ, LANES), lambda b, h: (cb + h, b, 0))

    def const(shape):
        return pl.BlockSpec(shape, lambda b, h: tuple(0 for _ in shape))

    biases = [jnp.asarray(b) for b in _att_bias_tables()]
    row_f32 = pltpu.VMEM((SEQ, LANES), F32)
    row_bf16 = pltpu.VMEM((SEQ, LANES), BF16)
    stat = pltpu.VMEM((3, SEQ, LANES), F32)
    return pl.pallas_call(
        _att_kernel,
        grid=(batch, hp),
        in_specs=[col(CB_QA), col(CB_KA), col(CB_VA),
                  const((SEQ, LANES)), const((SEQ, LANES)), const((1, LANES)), const((1, LANES))]
                 + [const(b.shape) for b in biases],
        out_specs=pl.BlockSpec((SEQ, LANES), lambda b, h: (b, h)),
        out_shape=jax.ShapeDtypeStruct((t, ATT_W), BF16),
        scratch_shapes=[row_f32] * 3 + [pltpu.VMEM((2, SEQ, LANES), BF16)] + [row_bf16] * 2 + [stat] * 3,
        compiler_params=_cparams(("arbitrary", "arbitrary")),
        name="dilated_attention",
    )(proj_att, proj_att, proj_att, cos_a, sin_a, gq, gk, *biases)


def _ret_kernel(gam_ref, q_ref, k_ref, v_ref, g_ref, cos_ref, sin_ref, decay_ref, xi_ref, zeta_ref,
                gn_ref, o_ref, state_sc):
    h = pl.program_id(1)
    gamma_c = gam_ref[h]
    state_sc[...] = jnp.zeros_like(state_sc)
    c = RET_CHUNK
    nt_dims = (((1,), (1,)), ((), ()))
    tn_dims = (((0,), (0,)), ((), ()))
    kscale = RET_QK_DIM ** -0.5

    def chunk(n, carry):
        rs = pl.ds(pl.multiple_of(n * c, c), c)
        cos = cos_ref[rs, :]
        sin = sin_ref[rs, :]

        def rot(ref):
            x1 = ref[0, rs, :].astype(F32)
            x2 = ref[1, rs, :].astype(F32)
            return x1 * cos - x2 * sin, x2 * cos + x1 * sin

        q1, q2 = rot(q_ref)
        k1, k2 = rot(k_ref)
        q = jnp.concatenate([q1, q2], axis=-1)
        k = jnp.concatenate([k1, k2], axis=-1) * kscale
        v = jnp.concatenate([v_ref[i, rs, :] for i in range(RET_V_DIM // LANES)], axis=-1)
        inner = lax.dot_general(q.astype(BF16), k.astype(BF16), nt_dims,
                                preferred_element_type=F32) * decay_ref[0]
        y = jnp.dot(inner.astype(BF16), v, preferred_element_type=F32)
        state = state_sc[...]
        qx = (q * xi_ref[0]).astype(BF16)
        y = y + jnp.dot(qx, state.astype(BF16), preferred_element_type=F32)
        kz = (k * zeta_ref[0]).astype(BF16)
        state_sc[...] = state * gamma_c + lax.dot_general(kz, v, tn_dims, preferred_element_type=F32)
        yn = y * lax.rsqrt(jnp.mean(y * y, axis=-1, keepdims=True) + EPS) * gn_ref[0]
        g = jnp.concatenate([g_ref[i, rs, :] for i in range(RET_V_DIM // LANES)], axis=-1).astype(F32)
        o_ref[rs, :] = (yn * (g * jax.nn.sigmoid(g))).astype(BF16)
        return carry

    lax.fori_loop(0, SEQ // c, chunk, 0)


def _retention(proj, gamma_c, cos_r, sin_r, decay, xi, zeta, g_ret, batch):
    t = batch * SEQ
    nq = RET_QK_DIM // LANES
    nv = RET_V_DIM // LANES

    def cols(cb, n):
        return pl.BlockSpec((n, SEQ, LANES), lambda b, h: (cb // n + h, b, 0))

    tab = pl.BlockSpec((SEQ, LANES), lambda b, h: (0, 0))
    return pl.pallas_call(
        _ret_kernel,
        grid=(batch, RET_HEADS),
        in_specs=[
            pl.BlockSpec(memory_space=pltpu.SMEM),
            cols(CB_QR, nq), cols(CB_KR, nq), cols(CB_VR, nv), cols(CB_GR, nv),
            tab, tab,
            pl.BlockSpec((1, RET_CHUNK, RET_CHUNK), lambda b, h: (h, 0, 0)),
            pl.BlockSpec((1, RET_CHUNK, RET_QK_DIM), lambda b, h: (h, 0, 0)),
            pl.BlockSpec((1, RET_CHUNK, RET_QK_DIM), lambda b, h: (h, 0, 0)),
            pl.BlockSpec((1, 1, RET_V_DIM), lambda b, h: (h, 0, 0)),
        ],
        out_specs=pl.BlockSpec((SEQ, RET_V_DIM), lambda b, h: (b, h)),
        out_shape=jax.ShapeDtypeStruct((t, RET_V_W), BF16),
        scratch_shapes=[pltpu.VMEM((RET_QK_DIM, RET_V_DIM), F32)],
        compiler_params=_cparams(("arbitrary", "arbitrary")),
        name="retention",
    )(gamma_c, proj, proj, proj, proj, cos_r, sin_r, decay, xi, zeta, g_ret)


OUT_TM = 512
N_ROUTE = N_EXPERTS + N_GROUPS


def _split_bf16(x):
    hi = x.astype(BF16)
    lo = (x - hi.astype(F32)).astype(BF16)
    return hi, lo


def _out_kernel(ya_ref, yr_ref, ga_ref, gb_ref, bg_ref, x_ref, watt_ref, wret_ref, wout_ref,
                gffn_ref, wr_hi_ref, wr_lo_ref, br_ref, x1_ref, xn_ref, gate_ref):
    ya = jnp.dot(ya_ref[...], watt_ref[...], preferred_element_type=F32)
    yr = jnp.dot(yr_ref[...], wret_ref[...], preferred_element_type=F32)
    merged = []
    for cb in range(D_MODEL // LANES):
        cs = slice(cb * LANES, (cb + 1) * LANES)
        ga = jax.nn.sigmoid(ga_ref[cb].astype(F32) + bg_ref[:, cs])
        gb = jax.nn.sigmoid(gb_ref[cb].astype(F32) + bg_ref[:, D_MODEL + cb * LANES:D_MODEL + (cb + 1) * LANES])
        merged.append((ga * ya[:, cs] + gb * yr[:, cs]).astype(BF16))
    merged = jnp.concatenate(merged, axis=-1)
    x1 = x_ref[...] + jnp.dot(merged, wout_ref[...], preferred_element_type=F32)
    x1_ref[...] = x1
    xn = x1 * lax.rsqrt(jnp.mean(x1 * x1, axis=-1, keepdims=True) + EPS) * gffn_ref[...]
    xn_ref[...] = xn.astype(BF16)

    xh, xl = _split_bf16(xn)
    logits = (jnp.dot(xh, wr_hi_ref[...], preferred_element_type=F32)
              + jnp.dot(xl, wr_hi_ref[...], preferred_element_type=F32)
              + jnp.dot(xh, wr_lo_ref[...], preferred_element_type=F32)) + br_ref[...]
    lane = lax.broadcasted_iota(jnp.int32, logits.shape, 1)
    lane_f = lane.astype(F32)
    is_group = (lane >= N_EXPERTS) & (lane < N_ROUTE)
    gl = jnp.where(is_group, logits, NEG_BIG)
    gmax = jnp.max(gl, axis=-1, keepdims=True)
    gsel = jnp.min(jnp.where(gl == gmax, lane_f, 1e9), axis=-1, keepdims=True) - N_EXPERTS
    p_group = 1.0 / jnp.sum(jnp.where(is_group, jnp.exp(logits - gmax), 0.0), axis=-1, keepdims=True)
    lo_lane = gsel * EXPERTS_PER_GROUP
    in_group = (lane_f >= lo_lane) & (lane_f < lo_lane + EXPERTS_PER_GROUP)
    el = jnp.where(in_group, logits, NEG_BIG)
    v1 = jnp.max(el, axis=-1, keepdims=True)
    i1 = jnp.min(jnp.where(el == v1, lane_f, 1e9), axis=-1, keepdims=True)
    el2 = jnp.where(lane_f == i1, NEG_BIG, el)
    v2 = jnp.max(el2, axis=-1, keepdims=True)
    i2 = jnp.min(jnp.where(el2 == v2, lane_f, 1e9), axis=-1, keepdims=True)
    e21 = jnp.exp(v2 - v1)
    w1 = p_group / (1.0 + e21)
    w2 = w1 * e21
    gate_ref[...] = jnp.where(lane_f == i1, w1, 0.0) + jnp.where(lane_f == i2, w2, 0.0)


def _out_stage(y_att, y_ret, proj, b_gate, x2d, w_att, w_ret, w_out, g_ffn, wr_hi, wr_lo, b_route):
    t = x2d.shape[0]
    ncb = D_MODEL // LANES

    def full(shape):
        return pl.BlockSpec(shape, lambda i: tuple(0 for _ in shape))

    return pl.pallas_call(
        _out_kernel,
        grid=(t // OUT_TM,),
        in_specs=[
            pl.BlockSpec((OUT_TM, ATT_W), lambda i: (i, 0)),
            pl.BlockSpec((OUT_TM, RET_V_W), lambda i: (i, 0)),
            pl.BlockSpec((ncb, OUT_TM, LANES), lambda i: (CB_GA // ncb, i, 0)),
            pl.BlockSpec((ncb, OUT_TM, LANES), lambda i: (CB_GB // ncb, i, 0)),
            full((1, 2 * D_MODEL)),
            pl.BlockSpec((OUT_TM, D_MODEL), lambda i: (i, 0)),
            full((ATT_W, D_MODEL)), full((RET_V_W, D_MODEL)), full((D_MODEL, D_MODEL)),
            full((1, D_MODEL)), full((D_MODEL, LANES)), full((D_MODEL, LANES)), full((1, LANES)),
        ],
        out_specs=[
            pl.BlockSpec((OUT_TM, D_MODEL), lambda i: (i, 0)),
            pl.BlockSpec((OUT_TM, D_MODEL), lambda i: (i, 0)),
            pl.BlockSpec((OUT_TM, LANES), lambda i: (i, 0)),
        ],
        out_shape=[
            jax.ShapeDtypeStruct((t, D_MODEL), F32),
            jax.ShapeDtypeStruct((t, D_MODEL), BF16),
            jax.ShapeDtypeStruct((t, LANES), F32),
        ],
        compiler_params=_cparams(("arbitrary",)),
        name="out_stage",
    )(y_att, y_ret, proj, proj, b_gate, x2d, w_att, w_ret, w_out, g_ffn, wr_hi, wr_lo, b_route)


MOE_TM = 1024


def _moe_kernel(xn_ref, gate_ref, x1_ref, w1_ref, w3_ref, w2_ref, o_ref):
    e = pl.program_id(1)

    @pl.when(e == 0)
    def _():
        o_ref[...] = x1_ref[...]

    x = xn_ref[...]
    a = jnp.dot(x, w1_ref[0], preferred_element_type=F32)
    b = jnp.dot(x, w3_ref[0], preferred_element_type=F32)
    gate = gate_ref[...]
    lane = lax.broadcasted_iota(jnp.int32, gate.shape, 1)
    g = jnp.sum(jnp.where(lane == e, gate, 0.0), axis=-1, keepdims=True)
    hidden = (a * jax.nn.sigmoid(a) * b * g).astype(BF16)
    o_ref[...] += jnp.dot(hidden, w2_ref[0], preferred_element_type=F32)


def _moe(xn, gate, x1, w1, w3, w2):
    t = xn.shape[0]
    return pl.pallas_call(
        _moe_kernel,
        grid=(t // MOE_TM, N_EXPERTS),
        in_specs=[
            pl.BlockSpec((MOE_TM, D_MODEL), lambda i, e: (i, 0)),
            pl.BlockSpec((MOE_TM, LANES), lambda i, e: (i, 0)),
            pl.BlockSpec((MOE_TM, D_MODEL), lambda i, e: (i, 0)),
            pl.BlockSpec((1, D_MODEL, EXPERT_FF), lambda i, e: (e, 0, 0)),
            pl.BlockSpec((1, D_MODEL, EXPERT_FF), lambda i, e: (e, 0, 0)),
            pl.BlockSpec((1, EXPERT_FF, D_MODEL), lambda i, e: (e, 0, 0)),
        ],
        out_specs=pl.BlockSpec((MOE_TM, D_MODEL), lambda i, e: (i, 0)),
        out_shape=jax.ShapeDtypeStruct((t, D_MODEL), F32),
        compiler_params=_cparams(("arbitrary", "arbitrary")),
        name="moe",
    )(xn, gate, x1, w1, w3, w2)


def _class_major(table):
    return table.reshape(ROWS_PER_CLASS, ATT_CLASSES, -1).transpose(1, 0, 2).reshape(SEQ, -1)


def _rope_tables_att():
    pos = jnp.arange(SEQ, dtype=F32)
    inv = ROPE_THETA ** (-jnp.arange(0, ATT_HEAD_DIM, 2, dtype=F32) / ATT_HEAD_DIM)
    ang = pos[:, None] * inv[None, :]
    cos, sin = jnp.cos(ang), jnp.sin(ang)
    reps = LANES // ATT_HEAD_DIM
    cos_full = jnp.tile(jnp.concatenate([cos, cos], axis=-1), (1, reps))
    sin_full = jnp.tile(jnp.concatenate([-sin, sin], axis=-1), (1, reps))
    return _class_major(cos_full), _class_major(sin_full)


def _rope_tables_ret():
    pos = jnp.arange(SEQ, dtype=F32)
    inv = 1.0 / (ROPE_THETA ** jnp.linspace(0.0, 1.0, RET_QK_DIM // 2, dtype=F32))
    ang = pos[:, None] * inv[None, :]
    return jnp.cos(ang), jnp.sin(ang)


def _decay_tables():
    c = RET_CHUNK
    log_gamma = jnp.log(1.0 - jnp.exp2(-5.0 - jnp.arange(RET_HEADS, dtype=F32)))
    idx = jnp.arange(c, dtype=F32)
    diff = idx[:, None] - idx[None, :]
    decay = jnp.where(diff >= 0, jnp.exp(log_gamma[:, None, None] * jnp.maximum(diff, 0.0)), 0.0)
    zeta = jnp.exp(log_gamma[:, None] * (c - 1 - idx))
    xi = jnp.exp(log_gamma[:, None] * (idx + 1))
    gamma_c = jnp.exp(log_gamma * c)
    bc = lambda a: jnp.broadcast_to(a[:, :, None], (RET_HEADS, c, RET_QK_DIM))
    return gamma_c, decay, bc(xi), bc(zeta)


def kernel(x, g_norm_mix, w_in, b_merge_gate, g_q, g_k, w_branch_att, g_ret_norm, w_branch_ret,
           w_out, g_norm_ffn, w_router_group, b_router_group, w_router_expert, b_router_expert,
           w1, w3, w2):
    batch = x.shape[0]
    t = batch * SEQ
    cos_a, sin_a = _rope_tables_att()
    cos_r, sin_r = _rope_tables_ret()
    gamma_c, decay, xi, zeta = _decay_tables()
    xf = x.reshape(t, D_MODEL)
    for l in range(g_norm_mix.shape[0]):
        g_mix = g_norm_mix[l][None, :]
        proj_att = _inproj(xf, g_mix, w_in[l][:, :ATT_IN_W].astype(BF16), True)
        proj = _inproj(xf, g_mix, w_in[l][:, ATT_IN_W:].astype(BF16), False)
        reps = LANES // ATT_HEAD_DIM
        y_att = _attention(proj_att, cos_a, sin_a, jnp.tile(g_q[l], reps)[None, :],
                           jnp.tile(g_k[l], reps)[None, :], batch)
        y_ret = _retention(proj, gamma_c, cos_r, sin_r, decay, xi, zeta,
                           g_ret_norm[l][:, None, :], batch)
        w_route = jnp.concatenate(
            [w_router_expert[l], w_router_group[l],
             jnp.zeros((D_MODEL, LANES - N_ROUTE), F32)], axis=-1)
        wr_hi, wr_lo = _split_bf16(w_route)
        b_route = jnp.concatenate(
            [b_router_expert[l], b_router_group[l], jnp.zeros((LANES - N_ROUTE,), F32)])[None, :]
        x1, xn2, gate = _out_stage(
            y_att, y_ret, proj, b_merge_gate[l][None, :], xf,
            w_branch_att[l].astype(BF16), w_branch_ret[l].astype(BF16), w_out[l].astype(BF16),
            g_norm_ffn[l][None, :], wr_hi, wr_lo, b_route)
        xf = _moe(xn2, gate, x1, w1[l].astype(BF16), w3[l].astype(BF16), w2[l].astype(BF16))
    return xf.reshape(batch, SEQ, D_MODEL)
```

```python
import functools

import numpy as np

import jax
import jax.numpy as jnp
from jax import lax
from jax.experimental import pallas as pl
from jax.experimental.pallas import tpu as pltpu

F32 = jnp.float32
BF16 = jnp.bfloat16

D_MODEL = 1024
SEQ = 2048
ATT_HEADS = 16
ATT_HEAD_DIM = 64
ATT_W = ATT_HEADS * ATT_HEAD_DIM
ROPE_THETA = 10000.0
RET_HEADS = 4
RET_QK_DIM = 256
RET_V_DIM = 512
RET_QK_W = RET_HEADS * RET_QK_DIM
RET_V_W = RET_HEADS * RET_V_DIM
N_GROUPS = 4
EXPERTS_PER_GROUP = 8
N_EXPERTS = N_GROUPS * EXPERTS_PER_GROUP
EXPERT_FF = 256
EPS = 1e-6
ATT_IN_W = 3 * ATT_W
REST_IN_W = 2 * RET_QK_W + 2 * RET_V_W + 2 * D_MODEL

LANES = 128
CB_QA, CB_KA, CB_VA = 0, 8, 16
CB_QR, CB_KR, CB_VR, CB_GR, CB_GA, CB_GB = 0, 8, 16, 32, 48, 56

ATT_BLOCK = 128
ATT_CLASSES = 16
ROWS_PER_CLASS = SEQ // ATT_CLASSES
ATT_GROUP = 4
RET_CHUNK = 256
NEG_BIG = -1e30
VMEM_LIMIT = 48 * 1024 * 1024


def _cparams(sem):
    return pltpu.CompilerParams(dimension_semantics=sem, vmem_limit_bytes=VMEM_LIMIT)


def _aligned(x, k):
    return x if isinstance(x, int) else pl.multiple_of(x, k)


IN_TM = 1024
IN_TN = 1024
IN_NORM_ROWS = 256


def _inproj_kernel(x_ref, g_ref, w_ref, o_ref, xn_sc, *stage, class_major):
    @pl.when(pl.program_id(1) == 0)
    def _():
        def norm(x):
            ms = jnp.mean(x * x, axis=-1, keepdims=True)
            return x * lax.rsqrt(ms + EPS) * g_ref[...]

        if class_major:
            (xs,) = stage

            def norm_rows(ci, carry):
                rows = pl.ds(pl.multiple_of(ci * IN_NORM_ROWS, IN_NORM_ROWS), IN_NORM_ROWS)
                xn = norm(x_ref[rows, :])
                for c in range(D_MODEL // LANES):
                    xs[c, rows, :] = xn[:, c * LANES:(c + 1) * LANES]
                return carry

            lax.fori_loop(0, SEQ // IN_NORM_ROWS, norm_rows, 0)

            def gather_class(j, carry):
                dst = pl.ds(pl.multiple_of(j * ROWS_PER_CLASS, ROWS_PER_CLASS), ROWS_PER_CLASS)
                for c in range(D_MODEL // LANES):
                    xn_sc[dst, c * LANES:(c + 1) * LANES] = xs[
                        c, pl.ds(j, ROWS_PER_CLASS, stride=ATT_CLASSES), :].astype(BF16)
                return carry

            lax.fori_loop(0, ATT_CLASSES, gather_class, 0)
        else:
            xn_sc[...] = norm(x_ref[...]).astype(BF16)

    xn = xn_sc[...]
    for c2 in range(IN_TN // 256):
        acc = jnp.dot(xn, w_ref[:, c2 * 256:(c2 + 1) * 256], preferred_element_type=F32)
        o_ref[2 * c2] = acc[:, :LANES].astype(BF16)
        o_ref[2 * c2 + 1] = acc[:, LANES:].astype(BF16)


def _inproj(x2d, g, w_bf16, class_major):
    t = x2d.shape[0]
    width = w_bf16.shape[1]
    tm = SEQ if class_major else IN_TM
    return pl.pallas_call(
        functools.partial(_inproj_kernel, class_major=class_major),
        grid=(t // tm, width // IN_TN),
        in_specs=[
            pl.BlockSpec((tm, D_MODEL), lambda i, j: (i, 0)),
            pl.BlockSpec((1, D_MODEL), lambda i, j: (0, 0)),
            pl.BlockSpec((D_MODEL, IN_TN), lambda i, j: (0, j)),
        ],
        out_specs=pl.BlockSpec((IN_TN // LANES, tm, LANES), lambda i, j: (j, i, 0)),
        out_shape=jax.ShapeDtypeStruct((width // LANES, t, LANES), BF16),
        scratch_shapes=[pltpu.VMEM((tm, D_MODEL), BF16)]
                       + ([pltpu.VMEM((D_MODEL // LANES, tm, LANES), F32)] if class_major else []),
        compiler_params=_cparams(("arbitrary", "arbitrary")),
        name="inproj_att" if class_major else "inproj_rest",
    )(x2d, g, w_bf16)


PREP_ROWS = 256
SEG4 = ATT_BLOCK // 4
SEG1 = ATT_BLOCK // ATT_CLASSES


def _att_bias_tables():
    def tile(qpos, kpos):
        d = qpos[:, None] - kpos[None, :]
        one = np.where((d >= 0) & (d <= ATT_BLOCK), 0.0, NEG_BIG).astype(np.float32)
        return np.concatenate([one, one], axis=0)

    u = np.arange(ATT_BLOCK)
    q4 = 4 * (u % SEG4) + u // SEG4
    q1 = ATT_CLASSES * (u % SEG1) + u // SEG1
    w = np.arange(2 * ATT_BLOCK)
    k1 = ATT_CLASSES * (w % SEG1) + w // (2 * SEG1) + ATT_BLOCK * ((w // SEG1) % 2 - 1)
    return (tile(q1, k1), tile(q1, q1), tile(q4, np.concatenate([q4 - ATT_BLOCK, q4])), tile(q4, q4),
            tile(u, u))


def _att_kernel(q_ref, k_ref, v_ref, cos_ref, sin_ref, gq_ref, gk_ref,
                b1_ref, b1f_ref, b4_ref, b4f_ref, b16_ref, o_ref,
                qj, kj, vj, qb, kb, vb, acc_sc, m_sc, l_sc):
    lane = lax.broadcasted_iota(jnp.int32, (1, LANES), 1)
    head0 = lane < ATT_HEAD_DIM
    first_half = (lane % ATT_HEAD_DIM) < (ATT_HEAD_DIM // 2)
    seg = (lax.broadcasted_iota(jnp.int32, (LANES, LANES), 0) // ATT_HEAD_DIM
           == lax.broadcasted_iota(jnp.int32, (LANES, LANES), 1) // ATT_HEAD_DIM).astype(BF16)
    scale = ATT_HEAD_DIM ** -0.5

    def prep(ci, carry):
        rows = pl.ds(pl.multiple_of(ci * PREP_ROWS, PREP_ROWS), PREP_ROWS)
        cos = cos_ref[rows, :]
        sin = sin_ref[rows, :]

        def norm_rope(src, g_ref):
            x = src[0, rows, :].astype(F32)
            x2 = x * x
            hi = x2.astype(BF16)
            lo = (x2 - hi.astype(F32)).astype(BF16)
            ss = (jnp.dot(hi, seg, preferred_element_type=F32)
                  + jnp.dot(lo, seg, preferred_element_type=F32))
            xn = x * lax.rsqrt(ss * (1.0 / ATT_HEAD_DIM) + EPS) * g_ref[...]
            swapped = jnp.where(first_half,
                                pltpu.roll(xn, LANES - ATT_HEAD_DIM // 2, 1),
                                pltpu.roll(xn, ATT_HEAD_DIM // 2, 1))
            return xn * cos + swapped * sin

        xq = norm_rope(q_ref, gq_ref) * scale
        qj[rows, :] = xq
        qb[0, rows, :] = jnp.where(head0, xq, 0.0).astype(BF16)
        qb[1, rows, :] = jnp.where(head0, 0.0, xq).astype(BF16)
        xk = norm_rope(k_ref, gk_ref)
        kj[rows, :] = xk
        kb[rows, :] = xk.astype(BF16)
        xv = v_ref[0, rows, :]
        vj[rows, :] = xv.astype(F32)
        vb[rows, :] = xv
        return carry

    lax.fori_loop(0, SEQ // PREP_ROWS, prep, 0)

    nt_dims = (((1,), (1,)), ((), ()))

    def attend(q2, kk, vv, bias):
        s = lax.dot_general(q2, kk, nt_dims, preferred_element_type=F32) + bias
        m = jnp.max(s, axis=-1, keepdims=True)
        e = jnp.exp(s - m).astype(BF16)
        v1 = jnp.concatenate([vv, jnp.ones(vv.shape, BF16)], axis=1)
        r = jnp.dot(e, v1, preferred_element_type=F32)
        top, bot = r[:ATT_BLOCK], r[ATT_BLOCK:]
        return (jnp.where(head0, top[:, :LANES], bot[:, :LANES]),
                jnp.where(head0, m[:ATT_BLOCK], m[ATT_BLOCK:]),
                jnp.where(head0, top[:, LANES:], bot[:, LANES:]))

    def gather(ref, pieces):
        return jnp.concatenate([ref[rows, :] for rows in pieces], axis=0)

    def gather_q(pieces):
        return jnp.concatenate([qb[0, rows, :] for rows in pieces] + [qb[1, rows, :] for rows in pieces], axis=0)

    def store(p, pieces, n, a, m, l):
        for idx, rows in enumerate(pieces):
            acc_sc[p, rows, :] = a[idx * n:(idx + 1) * n]
            m_sc[p, rows, :] = m[idx * n:(idx + 1) * n]
            l_sc[p, rows, :] = l[idx * n:(idx + 1) * n]

    def p16_body(g, carry):
        for u in range(ATT_GROUP):
            rows = [pl.ds(pl.multiple_of((g * ATT_GROUP + u) * ATT_BLOCK, ATT_BLOCK), ATT_BLOCK)]
            a, m, l = attend(gather_q(rows), kb[rows[0], :], vb[rows[0], :], b16_ref[...])
            store(2, rows, ATT_BLOCK, a, m, l)
        return carry

    lax.fori_loop(0, ATT_CLASSES // ATT_GROUP, p16_body, 0)

    def p4_body(c, carry):
        def segs(n):
            return [pl.ds(pl.multiple_of((4 * a + c) * ATT_BLOCK + SEG4 * n, SEG4), SEG4) for a in range(4)]

        for n in range(SEQ // 4 // ATT_BLOCK):
            cur = segs(n)
            keys = cur if n == 0 else segs(n - 1) + cur
            bias = b4f_ref[...] if n == 0 else b4_ref[...]
            a, m, l = attend(gather_q(cur), gather(kb, keys), gather(vb, keys), bias)
            store(1, cur, SEG4, a, m, l)
        return carry

    lax.fori_loop(0, 4, p4_body, 0)

    def p1_task(n, first):
        cur = [pl.ds(_aligned(j * ATT_BLOCK + SEG1 * n, SEG1), SEG1) for j in range(ATT_CLASSES)]
        if first:
            keys, bias = cur, b1f_ref[...]
        else:
            keys = [pl.ds(_aligned(j * ATT_BLOCK + SEG1 * n - SEG1, SEG1), 2 * SEG1) for j in range(ATT_CLASSES)]
            bias = b1_ref[...]
        xq = gather(qj, cur)
        q2 = jnp.concatenate([jnp.where(head0, xq, 0.0), jnp.where(head0, 0.0, xq)], axis=0).astype(BF16)
        a, m, l = attend(q2, gather(kj, keys).astype(BF16), gather(vj, keys).astype(BF16), bias)
        store(0, cur, SEG1, a, m, l)

    for n in range(ATT_GROUP):
        p1_task(n, n == 0)

    def p1_body(it, carry):
        for u in range(ATT_GROUP):
            p1_task(it * ATT_GROUP + u, False)
        return carry

    lax.fori_loop(1, SEQ // ATT_BLOCK // ATT_GROUP, p1_body, 0)

    def merge(j, carry):
        rows = pl.ds(pl.multiple_of(j * ROWS_PER_CLASS, ROWS_PER_CLASS), ROWS_PER_CLASS)
        ms = [m_sc[p, rows, :] for p in range(3)]
        m_all = jnp.maximum(jnp.maximum(ms[0], ms[1]), ms[2])
        num = jnp.zeros((ROWS_PER_CLASS, LANES), F32)
        den = jnp.zeros((ROWS_PER_CLASS, LANES), F32)
        for p in range(3):
            w = jnp.exp(ms[p] - m_all)
            num = num + w * acc_sc[p, rows, :]
            den = den + w * l_sc[p, rows, :]
        qj[pl.ds(j, ROWS_PER_CLASS, stride=ATT_CLASSES), :] = num / den
        return carry

    lax.fori_loop(0, ATT_CLASSES, merge, 0)

    def emit(ci, carry):
        rows = pl.ds(pl.multiple_of(ci * PREP_ROWS, PREP_ROWS), PREP_ROWS)
        o_ref[rows, :] = qj[rows, :].astype(BF16)
        return carry

    lax.fori_loop(0, SEQ // PREP_ROWS, emit, 0)


def _attention(proj_att, cos_a, sin_a, gq, gk, batch):
    t = batch * SEQ
    hp = ATT_W // LANES

    def col(cb):
        return pl.BlockSpec((1, SEQ, LANES), lambda b, h: (cb + h, b, 0))

    def const(shape):
        return pl.BlockSpec(shape, lambda b, h: tuple(0 for _ in shape))

    biases = [jnp.asarray(b) for b in _att_bias_tables()]
    row_f32 = pltpu.VMEM((SEQ, LANES), F32)
    row_bf16 = pltpu.VMEM((SEQ, LANES), BF16)
    stat = pltpu.VMEM((3, SEQ, LANES), F32)
    return pl.pallas_call(
        _att_kernel,
        grid=(batch, hp),
        in_specs=[col(CB_QA), col(CB_KA), col(CB_VA),
                  const((SEQ, LANES)), const((SEQ, LANES)), const((1, LANES)), const((1, LANES))]
                 + [const(b.shape) for b in biases],
        out_specs=pl.BlockSpec((SEQ, LANES), lambda b, h: (b, h)),
        out_shape=jax.ShapeDtypeStruct((t, ATT_W), BF16),
        scratch_shapes=[row_f32] * 3 + [pltpu.VMEM((2, SEQ, LANES), BF16)] + [row_bf16] * 2 + [stat] * 3,
        compiler_params=_cparams(("arbitrary", "arbitrary")),
        name="dilated_attention",
    )(proj_att, proj_att, proj_att, cos_a, sin_a, gq, gk, *biases)


def _ret_kernel(gam_ref, q_ref, k_ref, v_ref, g_ref, cos_ref, sin_ref, decay_ref, xi_ref, zeta_ref,
                gn_ref, o_ref, state_sc):
    h = pl.program_id(1)
    gamma_c = gam_ref[h]
    state_sc[...] = jnp.zeros_like(state_sc)
    c = RET_CHUNK
    nt_dims = (((1,), (1,)), ((), ()))
    tn_dims = (((0,), (0,)), ((), ()))
    kscale = RET_QK_DIM ** -0.5

    def chunk(n, carry):
        rs = pl.ds(pl.multiple_of(n * c, c), c)
        cos = cos_ref[rs, :]
        sin = sin_ref[rs, :]

        def rot(ref):
            x1 = ref[0, rs, :].astype(F32)
            x2 = ref[1, rs, :].astype(F32)
            return x1 * cos - x2 * sin, x2 * cos + x1 * sin

        q1, q2 = rot(q_ref)
        k1, k2 = rot(k_ref)
        q = jnp.concatenate([q1, q2], axis=-1)
        k = jnp.concatenate([k1, k2], axis=-1) * kscale
        v = jnp.concatenate([v_ref[i, rs, :] for i in range(RET_V_DIM // LANES)], axis=-1)
        inner = lax.dot_general(q.astype(BF16), k.astype(BF16), nt_dims,
                                preferred_element_type=F32) * decay_ref[0]
        y = jnp.dot(inner.astype(BF16), v, preferred_element_type=F32)
        state = state_sc[...]
        qx = (q * xi_ref[0]).astype(BF16)
        y = y + jnp.dot(qx, state.astype(BF16), preferred_element_type=F32)
        kz = (k * zeta_ref[0]).astype(BF16)
        state_sc[...] = state * gamma_c + lax.dot_general(kz, v, tn_dims, preferred_element_type=F32)
        yn = y * lax.rsqrt(jnp.mean(y * y, axis=-1, keepdims=True) + EPS) * gn_ref[0]
        g = jnp.concatenate([g_ref[i, rs, :] for i in range(RET_V_DIM // LANES)], axis=-1).astype(F32)
        o_ref[rs, :] = (yn * (g * jax.nn.sigmoid(g))).astype(BF16)
        return carry

    lax.fori_loop(0, SEQ // c, chunk, 0)


def _retention(proj, gamma_c, cos_r, sin_r, decay, xi, zeta, g_ret, batch):
    t = batch * SEQ
    nq = RET_QK_DIM // LANES
    nv = RET_V_DIM // LANES

    def cols(cb, n):
        return pl.BlockSpec((n, SEQ, LANES), lambda b, h: (cb // n + h, b, 0))

    tab = pl.BlockSpec((SEQ, LANES), lambda b, h: (0, 0))
    return pl.pallas_call(
        _ret_kernel,
        grid=(batch, RET_HEADS),
        in_specs=[
            pl.BlockSpec(memory_space=pltpu.SMEM),
            cols(CB_QR, nq), cols(CB_KR, nq), cols(CB_VR, nv), cols(CB_GR, nv),
            tab, tab,
            pl.BlockSpec((1, RET_CHUNK, RET_CHUNK), lambda b, h: (h, 0, 0)),
            pl.BlockSpec((1, RET_CHUNK, RET_QK_DIM), lambda b, h: (h, 0, 0)),
            pl.BlockSpec((1, RET_CHUNK, RET_QK_DIM), lambda b, h: (h, 0, 0)),
            pl.BlockSpec((1, 1, RET_V_DIM), lambda b, h: (h, 0, 0)),
        ],
        out_specs=pl.BlockSpec((SEQ, RET_V_DIM), lambda b, h: (b, h)),
        out_shape=jax.ShapeDtypeStruct((t, RET_V_W), BF16),
        scratch_shapes=[pltpu.VMEM((RET_QK_DIM, RET_V_DIM), F32)],
        compiler_params=_cparams(("arbitrary", "arbitrary")),
        name="retention",
    )(gamma_c, proj, proj, proj, proj, cos_r, sin_r, decay, xi, zeta, g_ret)


OUT_TM = 512
N_ROUTE = N_EXPERTS + N_GROUPS
GROUP_LANE = LANES - 1
XG_W = D_MODEL + LANES


def _split_bf16(x):
    hi = x.astype(BF16)
    lo = (x - hi.astype(F32)).astype(BF16)
    return hi, lo


def _out_kernel(ya_ref, yr_ref, ga_ref, gb_ref, bg_ref, x_ref, watt_ref, wret_ref, wout_ref,
                gffn_ref, wr_hi_ref, wr_lo_ref, br_ref, x1_ref, xg_ref):
    ya = jnp.dot(ya_ref[...], watt_ref[...], preferred_element_type=F32)
    yr = jnp.dot(yr_ref[...], wret_ref[...], preferred_element_type=F32)
    merged = []
    for cb in range(D_MODEL // LANES):
        cs = slice(cb * LANES, (cb + 1) * LANES)
        ga = jax.nn.sigmoid(ga_ref[cb].astype(F32) + bg_ref[:, cs])
        gb = jax.nn.sigmoid(gb_ref[cb].astype(F32) + bg_ref[:, D_MODEL + cb * LANES:D_MODEL + (cb + 1) * LANES])
        merged.append((ga * ya[:, cs] + gb * yr[:, cs]).astype(BF16))
    merged = jnp.concatenate(merged, axis=-1)
    x1 = x_ref[...] + jnp.dot(merged, wout_ref[...], preferred_element_type=F32)
    x1_ref[...] = x1
    xn = x1 * lax.rsqrt(jnp.mean(x1 * x1, axis=-1, keepdims=True) + EPS) * gffn_ref[...]
    xg_ref[:, :D_MODEL] = xn

    xh, xl = _split_bf16(xn)
    logits = (jnp.dot(xh, wr_hi_ref[...], preferred_element_type=F32)
              + jnp.dot(xl, wr_hi_ref[...], preferred_element_type=F32)
              + jnp.dot(xh, wr_lo_ref[...], preferred_element_type=F32)) + br_ref[...]
    lane = lax.broadcasted_iota(jnp.int32, logits.shape, 1)
    lane_f = lane.astype(F32)
    is_group = (lane >= N_EXPERTS) & (lane < N_ROUTE)
    gl = jnp.where(is_group, logits, NEG_BIG)
    gmax = jnp.max(gl, axis=-1, keepdims=True)
    gsel = jnp.min(jnp.where(gl == gmax, lane_f, 1e9), axis=-1, keepdims=True) - N_EXPERTS
    p_group = 1.0 / jnp.sum(jnp.where(is_group, jnp.exp(logits - gmax), 0.0), axis=-1, keepdims=True)
    lo_lane = gsel * EXPERTS_PER_GROUP
    in_group = (lane_f >= lo_lane) & (lane_f < lo_lane + EXPERTS_PER_GROUP)
    el = jnp.where(in_group, logits, NEG_BIG)
    v1 = jnp.max(el, axis=-1, keepdims=True)
    i1 = jnp.min(jnp.where(el == v1, lane_f, 1e9), axis=-1, keepdims=True)
    el2 = jnp.where(lane_f == i1, NEG_BIG, el)
    v2 = jnp.max(el2, axis=-1, keepdims=True)
    i2 = jnp.min(jnp.where(el2 == v2, lane_f, 1e9), axis=-1, keepdims=True)
    e21 = jnp.exp(v2 - v1)
    w1 = p_group / (1.0 + e21)
    w2 = w1 * e21
    gate = jnp.where(lane_f == i1, w1, 0.0) + jnp.where(lane_f == i2, w2, 0.0)
    xg_ref[:, D_MODEL:] = jnp.where(lane == GROUP_LANE, gsel, gate)


def _out_stage(y_att, y_ret, proj, b_gate, x2d, w_att, w_ret, w_out, g_ffn, wr_hi, wr_lo, b_route):
    t = x2d.shape[0]
    ncb = D_MODEL // LANES

    def full(shape):
        return pl.BlockSpec(shape, lambda i: tuple(0 for _ in shape))

    return pl.pallas_call(
        _out_kernel,
        grid=(t // OUT_TM,),
        in_specs=[
            pl.BlockSpec((OUT_TM, ATT_W), lambda i: (i, 0)),
            pl.BlockSpec((OUT_TM, RET_V_W), lambda i: (i, 0)),
            pl.BlockSpec((ncb, OUT_TM, LANES), lambda i: (CB_GA // ncb, i, 0)),
            pl.BlockSpec((ncb, OUT_TM, LANES), lambda i: (CB_GB // ncb, i, 0)),
            full((1, 2 * D_MODEL)),
            pl.BlockSpec((OUT_TM, D_MODEL), lambda i: (i, 0)),
            full((ATT_W, D_MODEL)), full((RET_V_W, D_MODEL)), full((D_MODEL, D_MODEL)),
            full((1, D_MODEL)), full((D_MODEL, LANES)), full((D_MODEL, LANES)), full((1, LANES)),
        ],
        out_specs=[
            pl.BlockSpec((OUT_TM, D_MODEL), lambda i: (i, 0)),
            pl.BlockSpec((OUT_TM, XG_W), lambda i: (i, 0)),
        ],
        out_shape=[
            jax.ShapeDtypeStruct((t, D_MODEL), F32),
            jax.ShapeDtypeStruct((t, XG_W), F32),
        ],
        compiler_params=_cparams(("arbitrary",)),
        name="out_stage",
    )(y_att, y_ret, proj, proj, b_gate, x2d, w_att, w_ret, w_out, g_ffn, wr_hi, wr_lo, b_route)


MOE_TILE = 512
MOVE_TM = 1024
GROUP_FF = EXPERTS_PER_GROUP * EXPERT_FF


def _moe_plan(xg, t):
    i32 = jnp.int32
    g = xg[:, D_MODEL + GROUP_LANE].astype(i32)
    onehot = (g[:, None] == jnp.arange(N_GROUPS, dtype=i32)[None, :]).astype(i32)
    csum = jnp.cumsum(onehot, axis=0)
    rank = jnp.sum(onehot * csum, axis=1) - 1
    padded = (csum[-1] + MOE_TILE - 1) // MOE_TILE * MOE_TILE
    ends = jnp.cumsum(padded)
    pos = rank + jnp.sum(onehot * (ends - padded)[None, :], axis=1)
    tile_start = jnp.arange(t // MOE_TILE + N_GROUPS, dtype=i32) * MOE_TILE
    tile_group = jnp.minimum(jnp.sum((tile_start[:, None] >= ends[None, :]).astype(i32), axis=1),
                             N_GROUPS - 1)
    return pos.astype(i32), tile_group.astype(i32), (ends[-1:] // MOE_TILE).astype(i32)


def _dispatch_kernel(pos_ref, xg_ref, xs_init_ref, xs_ref, sem):
    del xs_init_ref
    base = pl.program_id(0) * MOVE_TM

    def send(r, carry):
        pltpu.make_async_copy(xg_ref.at[pl.ds(r, 1)], xs_ref.at[pl.ds(pos_ref[base + r], 1)], sem).start()
        return carry

    lax.fori_loop(0, MOVE_TM, send, 0)
    pltpu.make_async_copy(xg_ref, xs_ref.at[pl.ds(0, MOVE_TM)], sem).wait()


def _dispatch(pos, xg, n_rows):
    t = xg.shape[0]
    return pl.pallas_call(
        _dispatch_kernel,
        grid_spec=pltpu.PrefetchScalarGridSpec(
            num_scalar_prefetch=1,
            grid=(t // MOVE_TM,),
            in_specs=[pl.BlockSpec((MOVE_TM, XG_W), lambda i, pos: (i, 0)),
                      pl.BlockSpec(memory_space=pl.ANY)],
            out_specs=pl.BlockSpec(memory_space=pl.ANY),
            scratch_shapes=[pltpu.SemaphoreType.DMA],
        ),
        out_shape=jax.ShapeDtypeStruct((n_rows, XG_W), F32),
        input_output_aliases={2: 0},
        compiler_params=_cparams(("arbitrary",)),
        name="moe_dispatch",
    )(pos, xg, jnp.zeros((n_rows, XG_W), F32))


def _experts_kernel(tg_ref, nused_ref, xs_ref, w13_ref, w2_ref, ys_ref):
    i = pl.program_id(0)

    @pl.when(i < nused_ref[0])
    def _():
        x = xs_ref[:, :D_MODEL].astype(BF16)
        gate = xs_ref[:, D_MODEL:]
        lane = lax.broadcasted_iota(jnp.int32, gate.shape, 1)
        first = tg_ref[i] * EXPERTS_PER_GROUP
        hidden = []
        for e in range(EXPERTS_PER_GROUP):
            ab = jnp.dot(x, w13_ref[0, :, 2 * e * EXPERT_FF:2 * (e + 1) * EXPERT_FF],
                         preferred_element_type=F32)
            a, b = ab[:, :EXPERT_FF], ab[:, EXPERT_FF:]
            g = jnp.sum(jnp.where(lane == first + e, gate, 0.0), axis=-1, keepdims=True)
            hidden.append((a * jax.nn.sigmoid(a) * b * g).astype(BF16))
        ys_ref[...] = jnp.dot(jnp.concatenate(hidden, axis=-1), w2_ref[0], preferred_element_type=F32)

    @pl.when(i >= nused_ref[0])
    def _():
        ys_ref[...] = jnp.zeros_like(ys_ref)


def _experts(tile_group, n_used, xs, w13g, w2g):
    n_rows = xs.shape[0]
    return pl.pallas_call(
        _experts_kernel,
        grid_spec=pltpu.PrefetchScalarGridSpec(
            num_scalar_prefetch=2,
            grid=(n_rows // MOE_TILE,),
            in_specs=[
                pl.BlockSpec((MOE_TILE, XG_W), lambda i, tg, nu: (i, 0)),
                pl.BlockSpec((1, D_MODEL, 2 * GROUP_FF), lambda i, tg, nu: (tg[i], 0, 0)),
                pl.BlockSpec((1, GROUP_FF, D_MODEL), lambda i, tg, nu: (tg[i], 0, 0)),
            ],
            out_specs=pl.BlockSpec((MOE_TILE, D_MODEL), lambda i, tg, nu: (i, 0)),
        ),
        out_shape=jax.ShapeDtypeStruct((n_rows, D_MODEL), F32),
        compiler_params=_cparams(("arbitrary",)),
        name="moe_experts",
    )(tile_group, n_used, xs, w13g, w2g)


def _combine_kernel(pos_ref, x1_ref, ys_ref, o_ref, buf, sem):
    base = pl.program_id(0) * MOVE_TM

    def fetch(r, carry):
        pltpu.make_async_copy(ys_ref.at[pl.ds(pos_ref[base + r], 1)], buf.at[pl.ds(r, 1)], sem).start()
        return carry

    lax.fori_loop(0, MOVE_TM, fetch, 0)
    pltpu.make_async_copy(ys_ref.at[pl.ds(0, MOVE_TM)], buf, sem).wait()
    o_ref[...] = x1_ref[...] + buf[...]


def _combine(pos, x1, ys):
    t = x1.shape[0]
    return pl.pallas_call(
        _combine_kernel,
        grid_spec=pltpu.PrefetchScalarGridSpec(
            num_scalar_prefetch=1,
            grid=(t // MOVE_TM,),
            in_specs=[pl.BlockSpec((MOVE_TM, D_MODEL), lambda i, pos: (i, 0)),
                      pl.BlockSpec(memory_space=pl.ANY)],
            out_specs=pl.BlockSpec((MOVE_TM, D_MODEL), lambda i, pos: (i, 0)),
            scratch_shapes=[pltpu.VMEM((MOVE_TM, D_MODEL), F32), pltpu.SemaphoreType.DMA],
        ),
        out_shape=jax.ShapeDtypeStruct((t, D_MODEL), F32),
        compiler_params=_cparams(("arbitrary",)),
        name="moe_combine",
    )(pos, x1, ys)


def _moe(xg, x1, w1, w3, w2):
    t = xg.shape[0]
    w13g = (jnp.concatenate([w1, w3], axis=-1)
            .reshape(N_GROUPS, EXPERTS_PER_GROUP, D_MODEL, 2 * EXPERT_FF)
            .transpose(0, 2, 1, 3).reshape(N_GROUPS, D_MODEL, 2 * GROUP_FF).astype(BF16))
    w2g = w2.reshape(N_GROUPS, GROUP_FF, D_MODEL).astype(BF16)
    pos, tile_group, n_used = _moe_plan(xg, t)
    xs = _dispatch(pos, xg, t + N_GROUPS * MOE_TILE)
    ys = _experts(tile_group, n_used, xs, w13g, w2g)
    return _combine(pos, x1, ys)


def _class_major(table):
    return table.reshape(ROWS_PER_CLASS, ATT_CLASSES, -1).transpose(1, 0, 2).reshape(SEQ, -1)


def _rope_tables_att():
    pos = jnp.arange(SEQ, dtype=F32)
    inv = ROPE_THETA ** (-jnp.arange(0, ATT_HEAD_DIM, 2, dtype=F32) / ATT_HEAD_DIM)
    ang = pos[:, None] * inv[None, :]
    cos, sin = jnp.cos(ang), jnp.sin(ang)
    reps = LANES // ATT_HEAD_DIM
    cos_full = jnp.tile(jnp.concatenate([cos, cos], axis=-1), (1, reps))
    sin_full = jnp.tile(jnp.concatenate([-sin, sin], axis=-1), (1, reps))
    return _class_major(cos_full), _class_major(sin_full)


def _rope_tables_ret():
    pos = jnp.arange(SEQ, dtype=F32)
    inv = 1.0 / (ROPE_THETA ** jnp.linspace(0.0, 1.0, RET_QK_DIM // 2, dtype=F32))
    ang = pos[:, None] * inv[None, :]
    return jnp.cos(ang), jnp.sin(ang)


def _decay_tables():
    c = RET_CHUNK
    log_gamma = jnp.log(1.0 - jnp.exp2(-5.0 - jnp.arange(RET_HEADS, dtype=F32)))
    idx = jnp.arange(c, dtype=F32)
    diff = idx[:, None] - idx[None, :]
    decay = jnp.where(diff >= 0, jnp.exp(log_gamma[:, None, None] * jnp.maximum(diff, 0.0)), 0.0)
    zeta = jnp.exp(log_gamma[:, None] * (c - 1 - idx))
    xi = jnp.exp(log_gamma[:, None] * (idx + 1))
    gamma_c = jnp.exp(log_gamma * c)
    bc = lambda a: jnp.broadcast_to(a[:, :, None], (RET_HEADS, c, RET_QK_DIM))
    return gamma_c, decay, bc(xi), bc(zeta)


def kernel(x, g_norm_mix, w_in, b_merge_gate, g_q, g_k, w_branch_att, g_ret_norm, w_branch_ret,
           w_out, g_norm_ffn, w_router_group, b_router_group, w_router_expert, b_router_expert,
           w1, w3, w2):
    batch = x.shape[0]
    t = batch * SEQ
    cos_a, sin_a = _rope_tables_att()
    cos_r, sin_r = _rope_tables_ret()
    gamma_c, decay, xi, zeta = _decay_tables()
    xf = x.reshape(t, D_MODEL)
    for l in range(g_norm_mix.shape[0]):
        g_mix = g_norm_mix[l][None, :]
        proj_att = _inproj(xf, g_mix, w_in[l][:, :ATT_IN_W].astype(BF16), True)
        proj = _inproj(xf, g_mix, w_in[l][:, ATT_IN_W:].astype(BF16), False)
        reps = LANES // ATT_HEAD_DIM
        y_att = _attention(proj_att, cos_a, sin_a, jnp.tile(g_q[l], reps)[None, :],
                           jnp.tile(g_k[l], reps)[None, :], batch)
        y_ret = _retention(proj, gamma_c, cos_r, sin_r, decay, xi, zeta,
                           g_ret_norm[l][:, None, :], batch)
        w_route = jnp.concatenate(
            [w_router_expert[l], w_router_group[l],
             jnp.zeros((D_MODEL, LANES - N_ROUTE), F32)], axis=-1)
        wr_hi, wr_lo = _split_bf16(w_route)
        b_route = jnp.concatenate(
            [b_router_expert[l], b_router_group[l], jnp.zeros((LANES - N_ROUTE,), F32)])[None, :]
        x1, xg = _out_stage(
            y_att, y_ret, proj, b_merge_gate[l][None, :], xf,
            w_branch_att[l].astype(BF16), w_branch_ret[l].astype(BF16), w_out[l].astype(BF16),
            g_norm_ffn[l][None, :], wr_hi, wr_lo, b_route)
        xf = _moe(xg, x1, w1[l], w3[l], w2[l])
    return xf.reshape(batch, SEQ, D_MODEL)
```

```python
import functools

import numpy as np

import jax
import jax.numpy as jnp
from jax import lax
from jax.experimental import pallas as pl
from jax.experimental.pallas import tpu as pltpu

F32 = jnp.float32
BF16 = jnp.bfloat16

D_MODEL = 1024
SEQ = 2048
ATT_HEADS = 16
ATT_HEAD_DIM = 64
ATT_W = ATT_HEADS * ATT_HEAD_DIM
ROPE_THETA = 10000.0
RET_HEADS = 4
RET_QK_DIM = 256
RET_V_DIM = 512
RET_QK_W = RET_HEADS * RET_QK_DIM
RET_V_W = RET_HEADS * RET_V_DIM
N_GROUPS = 4
EXPERTS_PER_GROUP = 8
N_EXPERTS = N_GROUPS * EXPERTS_PER_GROUP
EXPERT_FF = 256
EPS = 1e-6
ATT_IN_W = 3 * ATT_W
REST_IN_W = 2 * RET_QK_W + 2 * RET_V_W + 2 * D_MODEL

LANES = 128
CB_QA, CB_KA, CB_VA = 0, 8, 16
CB_QR, CB_KR, CB_VR, CB_GR, CB_GA, CB_GB = 0, 8, 16, 32, 48, 56

ATT_BLOCK = 128
ATT_CLASSES = 16
ROWS_PER_CLASS = SEQ // ATT_CLASSES
ATT_GROUP = 16
RET_CHUNK = 256
NEG_BIG = -1e30
VMEM_LIMIT = 48 * 1024 * 1024


def _cparams(sem):
    return pltpu.CompilerParams(dimension_semantics=sem, vmem_limit_bytes=VMEM_LIMIT)


def _aligned(x, k):
    return x if isinstance(x, int) else pl.multiple_of(x, k)


IN_TM = 1024
IN_TN = 1024
IN_NORM_ROWS = 256


def _inproj_kernel(x_ref, g_ref, w_ref, o_ref, xn_sc, *stage, class_major):
    @pl.when(pl.program_id(1) == 0)
    def _():
        def norm(x):
            ms = jnp.mean(x * x, axis=-1, keepdims=True)
            return x * lax.rsqrt(ms + EPS) * g_ref[...]

        if class_major:
            (xs,) = stage

            def norm_rows(ci, carry):
                rows = pl.ds(pl.multiple_of(ci * IN_NORM_ROWS, IN_NORM_ROWS), IN_NORM_ROWS)
                xn = norm(x_ref[rows, :])
                for c in range(D_MODEL // LANES):
                    xs[c, rows, :] = xn[:, c * LANES:(c + 1) * LANES]
                return carry

            lax.fori_loop(0, SEQ // IN_NORM_ROWS, norm_rows, 0)

            def gather_class(j, carry):
                dst = pl.ds(pl.multiple_of(j * ROWS_PER_CLASS, ROWS_PER_CLASS), ROWS_PER_CLASS)
                for c in range(D_MODEL // LANES):
                    xn_sc[dst, c * LANES:(c + 1) * LANES] = xs[
                        c, pl.ds(j, ROWS_PER_CLASS, stride=ATT_CLASSES), :].astype(BF16)
                return carry

            lax.fori_loop(0, ATT_CLASSES, gather_class, 0)
        else:
            xn_sc[...] = norm(x_ref[...]).astype(BF16)

    xn = xn_sc[...]
    for c2 in range(IN_TN // 256):
        acc = jnp.dot(xn, w_ref[:, c2 * 256:(c2 + 1) * 256], preferred_element_type=F32)
        o_ref[2 * c2] = acc[:, :LANES].astype(BF16)
        o_ref[2 * c2 + 1] = acc[:, LANES:].astype(BF16)


def _inproj(x2d, g, w_bf16, class_major):
    t = x2d.shape[0]
    width = w_bf16.shape[1]
    tm = SEQ if class_major else IN_TM
    return pl.pallas_call(
        functools.partial(_inproj_kernel, class_major=class_major),
        grid=(t // tm, width // IN_TN),
        in_specs=[
            pl.BlockSpec((tm, D_MODEL), lambda i, j: (i, 0)),
            pl.BlockSpec((1, D_MODEL), lambda i, j: (0, 0)),
            pl.BlockSpec((D_MODEL, IN_TN), lambda i, j: (0, j)),
        ],
        out_specs=pl.BlockSpec((IN_TN // LANES, tm, LANES), lambda i, j: (j, i, 0)),
        out_shape=jax.ShapeDtypeStruct((width // LANES, t, LANES), BF16),
        scratch_shapes=[pltpu.VMEM((tm, D_MODEL), BF16)]
                       + ([pltpu.VMEM((D_MODEL // LANES, tm, LANES), F32)] if class_major else []),
        compiler_params=_cparams(("arbitrary", "arbitrary")),
        name="inproj_att" if class_major else "inproj_rest",
    )(x2d, g, w_bf16)


PREP_ROWS = 256
SEG4 = ATT_BLOCK // 4
SEG1 = ATT_BLOCK // ATT_CLASSES


def _att_bias_tables():
    def tile(qpos, kpos):
        d = qpos[:, None] - kpos[None, :]
        one = np.where((d >= 0) & (d <= ATT_BLOCK), 0.0, NEG_BIG).astype(np.float32)
        return np.concatenate([one, one], axis=0)

    u = np.arange(ATT_BLOCK)
    q4 = 4 * (u % SEG4) + u // SEG4
    q1 = ATT_CLASSES * (u % SEG1) + u // SEG1
    w = np.arange(2 * ATT_BLOCK)
    k1 = ATT_CLASSES * (w % SEG1) + w // (2 * SEG1) + ATT_BLOCK * ((w // SEG1) % 2 - 1)
    return (tile(q1, k1), tile(q1, q1), tile(q4, np.concatenate([q4 - ATT_BLOCK, q4])), tile(q4, q4),
            tile(u, u))


def _att_kernel(q_ref, k_ref, v_ref, cos_ref, sin_ref, gq_ref, gk_ref,
                b1_ref, b1f_ref, b4_ref, b4f_ref, b16_ref, o_ref,
                qj, kj, vj, qb, kb, vb, acc_sc, m_sc, l_sc):
    lane = lax.broadcasted_iota(jnp.int32, (1, LANES), 1)
    head0 = lane < ATT_HEAD_DIM
    first_half = (lane % ATT_HEAD_DIM) < (ATT_HEAD_DIM // 2)
    seg = (lax.broadcasted_iota(jnp.int32, (LANES, LANES), 0) // ATT_HEAD_DIM
           == lax.broadcasted_iota(jnp.int32, (LANES, LANES), 1) // ATT_HEAD_DIM).astype(BF16)
    scale = ATT_HEAD_DIM ** -0.5

    def prep(ci, carry):
        rows = pl.ds(pl.multiple_of(ci * PREP_ROWS, PREP_ROWS), PREP_ROWS)
        cos = cos_ref[rows, :]
        sin = sin_ref[rows, :]

        def norm_rope(src, g_ref):
            x = src[0, rows, :].astype(F32)
            x2 = x * x
            hi = x2.astype(BF16)
            lo = (x2 - hi.astype(F32)).astype(BF16)
            ss = (jnp.dot(hi, seg, preferred_element_type=F32)
                  + jnp.dot(lo, seg, preferred_element_type=F32))
            xn = x * lax.rsqrt(ss * (1.0 / ATT_HEAD_DIM) + EPS) * g_ref[...]
            swapped = jnp.where(first_half,
                                pltpu.roll(xn, LANES - ATT_HEAD_DIM // 2, 1),
                                pltpu.roll(xn, ATT_HEAD_DIM // 2, 1))
            return xn * cos + swapped * sin

        xq = norm_rope(q_ref, gq_ref) * scale
        qj[rows, :] = xq
        qb[0, rows, :] = jnp.where(head0, xq, 0.0).astype(BF16)
        qb[1, rows, :] = jnp.where(head0, 0.0, xq).astype(BF16)
        xk = norm_rope(k_ref, gk_ref)
        kj[rows, :] = xk
        kb[rows, :] = xk.astype(BF16)
        xv = v_ref[0, rows, :]
        vj[rows, :] = xv.astype(F32)
        vb[rows, :] = xv
        return carry

    lax.fori_loop(0, SEQ // PREP_ROWS, prep, 0)

    nt_dims = (((1,), (1,)), ((), ()))

    def attend(q2, kk, vv, bias):
        s = lax.dot_general(q2, kk, nt_dims, preferred_element_type=F32) + bias
        m = jnp.max(s, axis=-1, keepdims=True)
        e = jnp.exp(s - m).astype(BF16)
        v1 = jnp.concatenate([vv, jnp.ones(vv.shape, BF16)], axis=1)
        r = jnp.dot(e, v1, preferred_element_type=F32)
        top, bot = r[:ATT_BLOCK], r[ATT_BLOCK:]
        return (jnp.where(head0, top[:, :LANES], bot[:, :LANES]),
                jnp.where(head0, m[:ATT_BLOCK], m[ATT_BLOCK:]),
                jnp.where(head0, top[:, LANES:], bot[:, LANES:]))

    def gather(ref, pieces):
        return jnp.concatenate([ref[rows, :] for rows in pieces], axis=0)

    def gather_q(pieces):
        return jnp.concatenate([qb[0, rows, :] for rows in pieces] + [qb[1, rows, :] for rows in pieces], axis=0)

    def store(p, pieces, n, a, m, l):
        for idx, rows in enumerate(pieces):
            acc_sc[p, rows, :] = a[idx * n:(idx + 1) * n]
            m_sc[p, rows, :] = m[idx * n:(idx + 1) * n]
            l_sc[p, rows, :] = l[idx * n:(idx + 1) * n]

    def p16_body(g, carry):
        for u in range(ATT_GROUP):
            rows = [pl.ds(_aligned((g * ATT_GROUP + u) * ATT_BLOCK, ATT_BLOCK), ATT_BLOCK)]
            a, m, l = attend(gather_q(rows), kb[rows[0], :], vb[rows[0], :], b16_ref[...])
            store(2, rows, ATT_BLOCK, a, m, l)
        return carry

    if ATT_GROUP == ATT_CLASSES:
        p16_body(0, 0)
    else:
        lax.fori_loop(0, ATT_CLASSES // ATT_GROUP, p16_body, 0)

    def p4_class(c):
        def segs(n):
            return [pl.ds(_aligned((4 * a + c) * ATT_BLOCK + SEG4 * n, SEG4), SEG4) for a in range(4)]

        for n in range(SEQ // 4 // ATT_BLOCK):
            cur = segs(n)
            keys = cur if n == 0 else segs(n - 1) + cur
            bias = b4f_ref[...] if n == 0 else b4_ref[...]
            a, m, l = attend(gather_q(cur), gather(kb, keys), gather(vb, keys), bias)
            store(1, cur, SEG4, a, m, l)

    classes_per_body = ATT_GROUP // 4

    def p4_body(g, carry):
        for u in range(classes_per_body):
            p4_class(g * classes_per_body + u)
        return carry

    if classes_per_body == 4:
        p4_body(0, 0)
    else:
        lax.fori_loop(0, 4 // classes_per_body, p4_body, 0)

    def p1_task(n, first):
        cur = [pl.ds(_aligned(j * ATT_BLOCK + SEG1 * n, SEG1), SEG1) for j in range(ATT_CLASSES)]
        if first:
            keys, bias = cur, b1f_ref[...]
        else:
            keys = [pl.ds(_aligned(j * ATT_BLOCK + SEG1 * n - SEG1, SEG1), 2 * SEG1) for j in range(ATT_CLASSES)]
            bias = b1_ref[...]
        xq = gather(qj, cur)
        q2 = jnp.concatenate([jnp.where(head0, xq, 0.0), jnp.where(head0, 0.0, xq)], axis=0).astype(BF16)
        a, m, l = attend(q2, gather(kj, keys).astype(BF16), gather(vj, keys).astype(BF16), bias)
        store(0, cur, SEG1, a, m, l)

    for n in range(ATT_GROUP):
        p1_task(n, n == 0)

    def p1_body(it, carry):
        for u in range(ATT_GROUP):
            p1_task(it * ATT_GROUP + u, False)
        return carry

    if SEQ // ATT_BLOCK > ATT_GROUP:
        lax.fori_loop(1, SEQ // ATT_BLOCK // ATT_GROUP, p1_body, 0)

    def merge(j, carry):
        rows = pl.ds(pl.multiple_of(j * ROWS_PER_CLASS, ROWS_PER_CLASS), ROWS_PER_CLASS)
        ms = [m_sc[p, rows, :] for p in range(3)]
        m_all = jnp.maximum(jnp.maximum(ms[0], ms[1]), ms[2])
        num = jnp.zeros((ROWS_PER_CLASS, LANES), F32)
        den = jnp.zeros((ROWS_PER_CLASS, LANES), F32)
        for p in range(3):
            w = jnp.exp(ms[p] - m_all)
            num = num + w * acc_sc[p, rows, :]
            den = den + w * l_sc[p, rows, :]
        qj[pl.ds(j, ROWS_PER_CLASS, stride=ATT_CLASSES), :] = num / den
        return carry

    lax.fori_loop(0, ATT_CLASSES, merge, 0)

    def emit(ci, carry):
        rows = pl.ds(pl.multiple_of(ci * PREP_ROWS, PREP_ROWS), PREP_ROWS)
        o_ref[rows, :] = qj[rows, :].astype(BF16)
        return carry

    lax.fori_loop(0, SEQ // PREP_ROWS, emit, 0)


def _attention(proj_att, cos_a, sin_a, gq, gk, batch):
    t = batch * SEQ
    hp = ATT_W // LANES

    def col(cb):
        return pl.BlockSpec((1, SEQ, LANES), lambda b, h: (cb + h, b, 0))

    def const(shape):
        return pl.BlockSpec(shape, lambda b, h: tuple(0 for _ in shape))

    biases = [jnp.asarray(b) for b in _att_bias_tables()]
    row_f32 = pltpu.VMEM((SEQ, LANES), F32)
    row_bf16 = pltpu.VMEM((SEQ, LANES), BF16)
    stat = pltpu.VMEM((3, SEQ, LANES), F32)
    return pl.pallas_call(
        _att_kernel,
        grid=(batch, hp),
        in_specs=[col(CB_QA), col(CB_KA), col(CB_VA),
                  const((SEQ, LANES)), const((SEQ, LANES)), const((1, LANES)), const((1, LANES))]
                 + [const(b.shape) for b in biases],
        out_specs=pl.BlockSpec((SEQ, LANES), lambda b, h: (b, h)),
        out_shape=jax.ShapeDtypeStruct((t, ATT_W), BF16),
        scratch_shapes=[row_f32] * 3 + [pltpu.VMEM((2, SEQ, LANES), BF16)] + [row_bf16] * 2 + [stat] * 3,
        compiler_params=_cparams(("arbitrary", "arbitrary")),
        name="dilated_attention",
    )(proj_att, proj_att, proj_att, cos_a, sin_a, gq, gk, *biases)


def _ret_kernel(gam_ref, q_ref, k_ref, v_ref, g_ref, cos_ref, sin_ref, decay_ref, xi_ref, zeta_ref,
                gn_ref, o_ref, state_sc):
    h = pl.program_id(1)
    gamma_c = gam_ref[h]
    state_sc[...] = jnp.zeros_like(state_sc)
    c = RET_CHUNK
    nt_dims = (((1,), (1,)), ((), ()))
    tn_dims = (((0,), (0,)), ((), ()))
    kscale = RET_QK_DIM ** -0.5

    def chunk(n, carry):
        rs = pl.ds(pl.multiple_of(n * c, c), c)
        cos = cos_ref[rs, :]
        sin = sin_ref[rs, :]

        def rot(ref):
            x1 = ref[0, rs, :].astype(F32)
            x2 = ref[1, rs, :].astype(F32)
            return x1 * cos - x2 * sin, x2 * cos + x1 * sin

        q1, q2 = rot(q_ref)
        k1, k2 = rot(k_ref)
        q = jnp.concatenate([q1, q2], axis=-1)
        k = jnp.concatenate([k1, k2], axis=-1) * kscale
        v = jnp.concatenate([v_ref[i, rs, :] for i in range(RET_V_DIM // LANES)], axis=-1)
        inner = lax.dot_general(q.astype(BF16), k.astype(BF16), nt_dims,
                                preferred_element_type=F32) * decay_ref[0]
        y = jnp.dot(inner.astype(BF16), v, preferred_element_type=F32)
        state = state_sc[...]
        qx = (q * xi_ref[0]).astype(BF16)
        y = y + jnp.dot(qx, state.astype(BF16), preferred_element_type=F32)
        kz = (k * zeta_ref[0]).astype(BF16)
        state_sc[...] = state * gamma_c + lax.dot_general(kz, v, tn_dims, preferred_element_type=F32)
        yn = y * lax.rsqrt(jnp.mean(y * y, axis=-1, keepdims=True) + EPS) * gn_ref[0]
        g = jnp.concatenate([g_ref[i, rs, :] for i in range(RET_V_DIM // LANES)], axis=-1).astype(F32)
        o_ref[rs, :] = (yn * (g * jax.nn.sigmoid(g))).astype(BF16)
        return carry

    lax.fori_loop(0, SEQ // c, chunk, 0)


def _retention(proj, gamma_c, cos_r, sin_r, decay, xi, zeta, g_ret, batch):
    t = batch * SEQ
    nq = RET_QK_DIM // LANES
    nv = RET_V_DIM // LANES

    def cols(cb, n):
        return pl.BlockSpec((n, SEQ, LANES), lambda b, h: (cb // n + h, b, 0))

    tab = pl.BlockSpec((SEQ, LANES), lambda b, h: (0, 0))
    return pl.pallas_call(
        _ret_kernel,
        grid=(batch, RET_HEADS),
        in_specs=[
            pl.BlockSpec(memory_space=pltpu.SMEM),
            cols(CB_QR, nq), cols(CB_KR, nq), cols(CB_VR, nv), cols(CB_GR, nv),
            tab, tab,
            pl.BlockSpec((1, RET_CHUNK, RET_CHUNK), lambda b, h: (h, 0, 0)),
            pl.BlockSpec((1, RET_CHUNK, RET_QK_DIM), lambda b, h: (h, 0, 0)),
            pl.BlockSpec((1, RET_CHUNK, RET_QK_DIM), lambda b, h: (h, 0, 0)),
            pl.BlockSpec((1, 1, RET_V_DIM), lambda b, h: (h, 0, 0)),
        ],
        out_specs=pl.BlockSpec((SEQ, RET_V_DIM), lambda b, h: (b, h)),
        out_shape=jax.ShapeDtypeStruct((t, RET_V_W), BF16),
        scratch_shapes=[pltpu.VMEM((RET_QK_DIM, RET_V_DIM), F32)],
        compiler_params=_cparams(("arbitrary", "arbitrary")),
        name="retention",
    )(gamma_c, proj, proj, proj, proj, cos_r, sin_r, decay, xi, zeta, g_ret)


OUT_TM = 512
N_ROUTE = N_EXPERTS + N_GROUPS
GROUP_LANE = LANES - 1
XG_W = D_MODEL + LANES


def _split_bf16(x):
    hi = x.astype(BF16)
    lo = (x - hi.astype(F32)).astype(BF16)
    return hi, lo


def _out_kernel(ya_ref, yr_ref, ga_ref, gb_ref, bg_ref, x_ref, watt_ref, wret_ref, wout_ref,
                gffn_ref, wr_hi_ref, wr_lo_ref, br_ref, x1_ref, xg_ref):
    ya = jnp.dot(ya_ref[...], watt_ref[...], preferred_element_type=F32)
    yr = jnp.dot(yr_ref[...], wret_ref[...], preferred_element_type=F32)
    merged = []
    for cb in range(D_MODEL // LANES):
        cs = slice(cb * LANES, (cb + 1) * LANES)
        ga = jax.nn.sigmoid(ga_ref[cb].astype(F32) + bg_ref[:, cs])
        gb = jax.nn.sigmoid(gb_ref[cb].astype(F32) + bg_ref[:, D_MODEL + cb * LANES:D_MODEL + (cb + 1) * LANES])
        merged.append((ga * ya[:, cs] + gb * yr[:, cs]).astype(BF16))
    merged = jnp.concatenate(merged, axis=-1)
    x1 = x_ref[...] + jnp.dot(merged, wout_ref[...], preferred_element_type=F32)
    x1_ref[...] = x1
    xn = x1 * lax.rsqrt(jnp.mean(x1 * x1, axis=-1, keepdims=True) + EPS) * gffn_ref[...]
    xg_ref[:, :D_MODEL] = xn

    xh, xl = _split_bf16(xn)
    logits = (jnp.dot(xh, wr_hi_ref[...], preferred_element_type=F32)
              + jnp.dot(xl, wr_hi_ref[...], preferred_element_type=F32)
              + jnp.dot(xh, wr_lo_ref[...], preferred_element_type=F32)) + br_ref[...]
    lane = lax.broadcasted_iota(jnp.int32, logits.shape, 1)
    lane_f = lane.astype(F32)
    is_group = (lane >= N_EXPERTS) & (lane < N_ROUTE)
    gl = jnp.where(is_group, logits, NEG_BIG)
    gmax = jnp.max(gl, axis=-1, keepdims=True)
    gsel = jnp.min(jnp.where(gl == gmax, lane_f, 1e9), axis=-1, keepdims=True) - N_EXPERTS
    p_group = 1.0 / jnp.sum(jnp.where(is_group, jnp.exp(logits - gmax), 0.0), axis=-1, keepdims=True)
    lo_lane = gsel * EXPERTS_PER_GROUP
    in_group = (lane_f >= lo_lane) & (lane_f < lo_lane + EXPERTS_PER_GROUP)
    el = jnp.where(in_group, logits, NEG_BIG)
    v1 = jnp.max(el, axis=-1, keepdims=True)
    i1 = jnp.min(jnp.where(el == v1, lane_f, 1e9), axis=-1, keepdims=True)
    el2 = jnp.where(lane_f == i1, NEG_BIG, el)
    v2 = jnp.max(el2, axis=-1, keepdims=True)
    i2 = jnp.min(jnp.where(el2 == v2, lane_f, 1e9), axis=-1, keepdims=True)
    e21 = jnp.exp(v2 - v1)
    w1 = p_group / (1.0 + e21)
    w2 = w1 * e21
    gate = jnp.where(lane_f == i1, w1, 0.0) + jnp.where(lane_f == i2, w2, 0.0)
    xg_ref[:, D_MODEL:] = jnp.where(lane == GROUP_LANE, gsel, gate)


def _out_stage(y_att, y_ret, proj, b_gate, x2d, w_att, w_ret, w_out, g_ffn, wr_hi, wr_lo, b_route):
    t = x2d.shape[0]
    ncb = D_MODEL // LANES

    def full(shape):
        return pl.BlockSpec(shape, lambda i: tuple(0 for _ in shape))

    return pl.pallas_call(
        _out_kernel,
        grid=(t // OUT_TM,),
        in_specs=[
            pl.BlockSpec((OUT_TM, ATT_W), lambda i: (i, 0)),
            pl.BlockSpec((OUT_TM, RET_V_W), lambda i: (i, 0)),
            pl.BlockSpec((ncb, OUT_TM, LANES), lambda i: (CB_GA // ncb, i, 0)),
            pl.BlockSpec((ncb, OUT_TM, LANES), lambda i: (CB_GB // ncb, i, 0)),
            full((1, 2 * D_MODEL)),
            pl.BlockSpec((OUT_TM, D_MODEL), lambda i: (i, 0)),
            full((ATT_W, D_MODEL)), full((RET_V_W, D_MODEL)), full((D_MODEL, D_MODEL)),
            full((1, D_MODEL)), full((D_MODEL, LANES)), full((D_MODEL, LANES)), full((1, LANES)),
        ],
        out_specs=[
            pl.BlockSpec((OUT_TM, D_MODEL), lambda i: (i, 0)),
            pl.BlockSpec((OUT_TM, XG_W), lambda i: (i, 0)),
        ],
        out_shape=[
            jax.ShapeDtypeStruct((t, D_MODEL), F32),
            jax.ShapeDtypeStruct((t, XG_W), F32),
        ],
        compiler_params=_cparams(("arbitrary",)),
        name="out_stage",
    )(y_att, y_ret, proj, proj, b_gate, x2d, w_att, w_ret, w_out, g_ffn, wr_hi, wr_lo, b_route)


MOE_TILE = 512
MOVE_TM = 1024
MOVE_UNROLL = 8
GROUP_FF = EXPERTS_PER_GROUP * EXPERT_FF


def _moe_plan(xg, t):
    i32 = jnp.int32
    g = xg[:, D_MODEL + GROUP_LANE].astype(i32)
    onehot = (g[:, None] == jnp.arange(N_GROUPS, dtype=i32)[None, :]).astype(i32)
    csum = jnp.cumsum(onehot, axis=0)
    rank = jnp.sum(onehot * csum, axis=1) - 1
    padded = (csum[-1] + MOE_TILE - 1) // MOE_TILE * MOE_TILE
    ends = jnp.cumsum(padded)
    pos = rank + jnp.sum(onehot * (ends - padded)[None, :], axis=1)
    tile_start = jnp.arange(t // MOE_TILE + N_GROUPS, dtype=i32) * MOE_TILE
    tile_group = jnp.minimum(jnp.sum((tile_start[:, None] >= ends[None, :]).astype(i32), axis=1),
                             N_GROUPS - 1)
    return pos.astype(i32), tile_group.astype(i32), (ends[-1:] // MOE_TILE).astype(i32)


def _dispatch_kernel(pos_ref, xg_ref, xs_init_ref, xs_ref, sem):
    del xs_init_ref
    base = pl.program_id(0) * MOVE_TM

    def send(r, carry):
        pltpu.make_async_copy(xg_ref.at[pl.ds(r, 1)], xs_ref.at[pl.ds(pos_ref[base + r], 1)], sem).start()
        return carry

    lax.fori_loop(0, MOVE_TM, send, 0, unroll=MOVE_UNROLL)
    pltpu.make_async_copy(xg_ref, xs_ref.at[pl.ds(0, MOVE_TM)], sem).wait()


def _dispatch(pos, xg, n_rows):
    t = xg.shape[0]
    return pl.pallas_call(
        _dispatch_kernel,
        grid_spec=pltpu.PrefetchScalarGridSpec(
            num_scalar_prefetch=1,
            grid=(t // MOVE_TM,),
            in_specs=[pl.BlockSpec((MOVE_TM, XG_W), lambda i, pos: (i, 0)),
                      pl.BlockSpec(memory_space=pl.ANY)],
            out_specs=pl.BlockSpec(memory_space=pl.ANY),
            scratch_shapes=[pltpu.SemaphoreType.DMA],
        ),
        out_shape=jax.ShapeDtypeStruct((n_rows, XG_W), F32),
        input_output_aliases={2: 0},
        compiler_params=_cparams(("arbitrary",)),
        name="moe_dispatch",
    )(pos, xg, jnp.zeros((n_rows, XG_W), F32))


def _experts_kernel(tg_ref, nused_ref, xs_ref, w1_ref, w3_ref, w2_ref, ys_ref):
    i = pl.program_id(0)

    @pl.when(i < nused_ref[0])
    def _():
        x = xs_ref[:, :D_MODEL].astype(BF16)
        gate = xs_ref[:, D_MODEL:]
        lane = lax.broadcasted_iota(jnp.int32, gate.shape, 1)
        first = tg_ref[i] * EXPERTS_PER_GROUP
        hidden = []
        for e in range(EXPERTS_PER_GROUP):
            a = jnp.dot(x, w1_ref[0, e], preferred_element_type=F32)
            b = jnp.dot(x, w3_ref[0, e], preferred_element_type=F32)
            g = jnp.sum(jnp.where(lane == first + e, gate, 0.0), axis=-1, keepdims=True)
            hidden.append((a * jax.nn.sigmoid(a) * b * g).astype(BF16))
        ys_ref[...] = jnp.dot(jnp.concatenate(hidden, axis=-1), w2_ref[0], preferred_element_type=F32)

    @pl.when(i >= nused_ref[0])
    def _():
        ys_ref[...] = jnp.zeros_like(ys_ref)


def _experts(tile_group, n_used, xs, w1g, w3g, w2g):
    n_rows = xs.shape[0]
    up = pl.BlockSpec((1, EXPERTS_PER_GROUP, D_MODEL, EXPERT_FF), lambda i, tg, nu: (tg[i], 0, 0, 0))
    return pl.pallas_call(
        _experts_kernel,
        grid_spec=pltpu.PrefetchScalarGridSpec(
            num_scalar_prefetch=2,
            grid=(n_rows // MOE_TILE,),
            in_specs=[
                pl.BlockSpec((MOE_TILE, XG_W), lambda i, tg, nu: (i, 0)),
                up, up,
                pl.BlockSpec((1, GROUP_FF, D_MODEL), lambda i, tg, nu: (tg[i], 0, 0)),
            ],
            out_specs=pl.BlockSpec((MOE_TILE, D_MODEL), lambda i, tg, nu: (i, 0)),
        ),
        out_shape=jax.ShapeDtypeStruct((n_rows, D_MODEL), F32),
        compiler_params=_cparams(("arbitrary",)),
        name="moe_experts",
    )(tile_group, n_used, xs, w1g, w3g, w2g)


def _combine_kernel(pos_ref, x1_ref, ys_ref, o_ref, buf, sem):
    base = pl.program_id(0) * MOVE_TM

    def fetch(r, carry):
        pltpu.make_async_copy(ys_ref.at[pl.ds(pos_ref[base + r], 1)], buf.at[pl.ds(r, 1)], sem).start()
        return carry

    lax.fori_loop(0, MOVE_TM, fetch, 0, unroll=MOVE_UNROLL)
    pltpu.make_async_copy(ys_ref.at[pl.ds(0, MOVE_TM)], buf, sem).wait()
    o_ref[...] = x1_ref[...] + buf[...]


def _combine(pos, x1, ys):
    t = x1.shape[0]
    return pl.pallas_call(
        _combine_kernel,
        grid_spec=pltpu.PrefetchScalarGridSpec(
            num_scalar_prefetch=1,
            grid=(t // MOVE_TM,),
            in_specs=[pl.BlockSpec((MOVE_TM, D_MODEL), lambda i, pos: (i, 0)),
                      pl.BlockSpec(memory_space=pl.ANY)],
            out_specs=pl.BlockSpec((MOVE_TM, D_MODEL), lambda i, pos: (i, 0)),
            scratch_shapes=[pltpu.VMEM((MOVE_TM, D_MODEL), F32), pltpu.SemaphoreType.DMA],
        ),
        out_shape=jax.ShapeDtypeStruct((t, D_MODEL), F32),
        compiler_params=_cparams(("arbitrary",)),
        name="moe_combine",
    )(pos, x1, ys)


def _moe(xg, x1, w1, w3, w2):
    t = xg.shape[0]
    w1g = w1.reshape(N_GROUPS, EXPERTS_PER_GROUP, D_MODEL, EXPERT_FF).astype(BF16)
    w3g = w3.reshape(N_GROUPS, EXPERTS_PER_GROUP, D_MODEL, EXPERT_FF).astype(BF16)
    w2g = w2.reshape(N_GROUPS, GROUP_FF, D_MODEL).astype(BF16)
    pos, tile_group, n_used = _moe_plan(xg, t)
    xs = _dispatch(pos, xg, t + N_GROUPS * MOE_TILE)
    ys = _experts(tile_group, n_used, xs, w1g, w3g, w2g)
    return _combine(pos, x1, ys)


def _class_major(table):
    return table.reshape(ROWS_PER_CLASS, ATT_CLASSES, -1).transpose(1, 0, 2).reshape(SEQ, -1)


def _rope_tables_att():
    pos = jnp.arange(SEQ, dtype=F32)
    inv = ROPE_THETA ** (-jnp.arange(0, ATT_HEAD_DIM, 2, dtype=F32) / ATT_HEAD_DIM)
    ang = pos[:, None] * inv[None, :]
    cos, sin = jnp.cos(ang), jnp.sin(ang)
    reps = LANES // ATT_HEAD_DIM
    cos_full = jnp.tile(jnp.concatenate([cos, cos], axis=-1), (1, reps))
    sin_full = jnp.tile(jnp.concatenate([-sin, sin], axis=-1), (1, reps))
    return _class_major(cos_full), _class_major(sin_full)


def _rope_tables_ret():
    pos = jnp.arange(SEQ, dtype=F32)
    inv = 1.0 / (ROPE_THETA ** jnp.linspace(0.0, 1.0, RET_QK_DIM // 2, dtype=F32))
    ang = pos[:, None] * inv[None, :]
    return jnp.cos(ang), jnp.sin(ang)


def _decay_tables():
    c = RET_CHUNK
    log_gamma = jnp.log(1.0 - jnp.exp2(-5.0 - jnp.arange(RET_HEADS, dtype=F32)))
    idx = jnp.arange(c, dtype=F32)
    diff = idx[:, None] - idx[None, :]
    decay = jnp.where(diff >= 0, jnp.exp(log_gamma[:, None, None] * jnp.maximum(diff, 0.0)), 0.0)
    zeta = jnp.exp(log_gamma[:, None] * (c - 1 - idx))
    xi = jnp.exp(log_gamma[:, None] * (idx + 1))
    gamma_c = jnp.exp(log_gamma * c)
    bc = lambda a: jnp.broadcast_to(a[:, :, None], (RET_HEADS, c, RET_QK_DIM))
    return gamma_c, decay, bc(xi), bc(zeta)


def kernel(x, g_norm_mix, w_in, b_merge_gate, g_q, g_k, w_branch_att, g_ret_norm, w_branch_ret,
           w_out, g_norm_ffn, w_router_group, b_router_group, w_router_expert, b_router_expert,
           w1, w3, w2):
    batch = x.shape[0]
    t = batch * SEQ
    cos_a, sin_a = _rope_tables_att()
    cos_r, sin_r = _rope_tables_ret()
    gamma_c, decay, xi, zeta = _decay_tables()
    xf = x.reshape(t, D_MODEL)
    for l in range(g_norm_mix.shape[0]):
        g_mix = g_norm_mix[l][None, :]
        proj_att = _inproj(xf, g_mix, w_in[l][:, :ATT_IN_W].astype(BF16), True)
        proj = _inproj(xf, g_mix, w_in[l][:, ATT_IN_W:].astype(BF16), False)
        reps = LANES // ATT_HEAD_DIM
        y_att = _attention(proj_att, cos_a, sin_a, jnp.tile(g_q[l], reps)[None, :],
                           jnp.tile(g_k[l], reps)[None, :], batch)
        y_ret = _retention(proj, gamma_c, cos_r, sin_r, decay, xi, zeta,
                           g_ret_norm[l][:, None, :], batch)
        w_route = jnp.concatenate(
            [w_router_expert[l], w_router_group[l],
             jnp.zeros((D_MODEL, LANES - N_ROUTE), F32)], axis=-1)
        wr_hi, wr_lo = _split_bf16(w_route)
        b_route = jnp.concatenate(
            [b_router_expert[l], b_router_group[l], jnp.zeros((LANES - N_ROUTE,), F32)])[None, :]
        x1, xg = _out_stage(
            y_att, y_ret, proj, b_merge_gate[l][None, :], xf,
            w_branch_att[l].astype(BF16), w_branch_ret[l].astype(BF16), w_out[l].astype(BF16),
            g_norm_ffn[l][None, :], wr_hi, wr_lo, b_route)
        xf = _moe(xg, x1, w1[l], w3[l], w2[l])
    return xf.reshape(batch, SEQ, D_MODEL)
```

```python
import functools

import numpy as np

import jax
import jax.numpy as jnp
from jax import lax
from jax.experimental import pallas as pl
from jax.experimental.pallas import tpu as pltpu

F32 = jnp.float32
BF16 = jnp.bfloat16

D_MODEL = 1024
SEQ = 2048
ATT_HEADS = 16
ATT_HEAD_DIM = 64
ATT_W = ATT_HEADS * ATT_HEAD_DIM
ROPE_THETA = 10000.0
RET_HEADS = 4
RET_QK_DIM = 256
RET_V_DIM = 512
RET_QK_W = RET_HEADS * RET_QK_DIM
RET_V_W = RET_HEADS * RET_V_DIM
N_GROUPS = 4
EXPERTS_PER_GROUP = 8
N_EXPERTS = N_GROUPS * EXPERTS_PER_GROUP
EXPERT_FF = 256
EPS = 1e-6
ATT_IN_W = 3 * ATT_W
REST_IN_W = 2 * RET_QK_W + 2 * RET_V_W + 2 * D_MODEL

LANES = 128
CB_QA, CB_KA, CB_VA = 0, 8, 16
CB_QR, CB_KR, CB_VR, CB_GR, CB_GA, CB_GB = 0, 8, 16, 32, 48, 56

ATT_BLOCK = 128
ATT_CLASSES = 16
ROWS_PER_CLASS = SEQ // ATT_CLASSES
RET_CHUNK = 256
NEG_BIG = -1e30
VMEM_LIMIT = 48 * 1024 * 1024


def _cparams(sem):
    return pltpu.CompilerParams(dimension_semantics=sem, vmem_limit_bytes=VMEM_LIMIT)


IN_TM = 1024
IN_TN = 1024
IN_NORM_ROWS = 256


def _inproj_kernel(x_ref, g_ref, w_ref, o_ref, xn_sc, *stage, class_major):
    @pl.when(pl.program_id(1) == 0)
    def _():
        def norm(x):
            ms = jnp.mean(x * x, axis=-1, keepdims=True)
            return x * lax.rsqrt(ms + EPS) * g_ref[...]

        if class_major:
            (xs,) = stage

            def norm_rows(ci, carry):
                rows = pl.ds(pl.multiple_of(ci * IN_NORM_ROWS, IN_NORM_ROWS), IN_NORM_ROWS)
                xn = norm(x_ref[rows, :])
                for c in range(D_MODEL // LANES):
                    xs[c, rows, :] = xn[:, c * LANES:(c + 1) * LANES]
                return carry

            lax.fori_loop(0, SEQ // IN_NORM_ROWS, norm_rows, 0)

            def gather_class(j, carry):
                dst = pl.ds(pl.multiple_of(j * ROWS_PER_CLASS, ROWS_PER_CLASS), ROWS_PER_CLASS)
                for c in range(D_MODEL // LANES):
                    xn_sc[dst, c * LANES:(c + 1) * LANES] = xs[
                        c, pl.ds(j, ROWS_PER_CLASS, stride=ATT_CLASSES), :].astype(BF16)
                return carry

            lax.fori_loop(0, ATT_CLASSES, gather_class, 0)
        else:
            xn_sc[...] = norm(x_ref[...]).astype(BF16)

    xn = xn_sc[...]
    for c2 in range(IN_TN // 256):
        acc = jnp.dot(xn, w_ref[:, c2 * 256:(c2 + 1) * 256], preferred_element_type=F32)
        o_ref[2 * c2] = acc[:, :LANES].astype(BF16)
        o_ref[2 * c2 + 1] = acc[:, LANES:].astype(BF16)


def _inproj(x2d, g, w_bf16, class_major):
    t = x2d.shape[0]
    width = w_bf16.shape[1]
    tm = SEQ if class_major else IN_TM
    return pl.pallas_call(
        functools.partial(_inproj_kernel, class_major=class_major),
        grid=(t // tm, width // IN_TN),
        in_specs=[
            pl.BlockSpec((tm, D_MODEL), lambda i, j: (i, 0)),
            pl.BlockSpec((1, D_MODEL), lambda i, j: (0, 0)),
            pl.BlockSpec((D_MODEL, IN_TN), lambda i, j: (0, j)),
        ],
        out_specs=pl.BlockSpec((IN_TN // LANES, tm, LANES), lambda i, j: (j, i, 0)),
        out_shape=jax.ShapeDtypeStruct((width // LANES, t, LANES), BF16),
        scratch_shapes=[pltpu.VMEM((tm, D_MODEL), BF16)]
                       + ([pltpu.VMEM((D_MODEL // LANES, tm, LANES), F32)] if class_major else []),
        compiler_params=_cparams(("arbitrary", "arbitrary")),
        name="inproj_att" if class_major else "inproj_rest",
    )(x2d, g, w_bf16)


PREP_ROWS = 256
PREP_UNROLL = 4
SEG4 = ATT_BLOCK // 4
SEG1 = ATT_BLOCK // ATT_CLASSES


def _att_bias_tables():
    def tile(qpos, kpos):
        d = qpos[:, None] - kpos[None, :]
        one = np.where((d >= 0) & (d <= ATT_BLOCK), 0.0, NEG_BIG).astype(np.float32)
        return np.concatenate([one, one], axis=1)

    u = np.arange(ATT_BLOCK)
    q4 = 4 * (u % SEG4) + u // SEG4
    q1 = ATT_CLASSES * (u % SEG1) + u // SEG1
    w = np.arange(2 * ATT_BLOCK)
    k1 = ATT_CLASSES * (w % SEG1) + w // (2 * SEG1) + ATT_BLOCK * ((w // SEG1) % 2 - 1)
    return (tile(q1, k1), tile(q1, q1), tile(q4, np.concatenate([q4 - ATT_BLOCK, q4])), tile(q4, q4),
            tile(u, u))


def _att_kernel(q_ref, k_ref, v_ref, cos_ref, sin_ref, gq_ref, gk_ref,
                b1_ref, b1f_ref, b4_ref, b4f_ref, b16_ref, o_ref,
                qj, kj, vj, qb, kb, vb, acc_sc, m_sc, l_sc):
    lane = lax.broadcasted_iota(jnp.int32, (1, LANES), 1)
    half = ATT_HEAD_DIM // 2
    qk_head0 = (lane // half) % 2 == 0
    v_head0 = lane < ATT_HEAD_DIM
    seg = ((lax.broadcasted_iota(jnp.int32, (LANES, LANES), 0) // half) % 2
           == (lax.broadcasted_iota(jnp.int32, (LANES, LANES), 1) // half) % 2).astype(BF16)
    scale = ATT_HEAD_DIM ** -0.5

    def prep(ci, carry):
        rows = pl.ds(pl.multiple_of(ci * PREP_ROWS, PREP_ROWS), PREP_ROWS)
        cos = cos_ref[rows, :]
        sin = sin_ref[rows, :]

        def norm_rope(src, g_ref):
            x = src[0, rows, :].astype(F32)
            x2 = x * x
            hi = x2.astype(BF16)
            lo = (x2 - hi.astype(F32)).astype(BF16)
            ss = (jnp.dot(hi, seg, preferred_element_type=F32)
                  + jnp.dot(lo, seg, preferred_element_type=F32))
            xn = x * lax.rsqrt(ss * (1.0 / ATT_HEAD_DIM) + EPS) * g_ref[...]
            return xn * cos + pltpu.roll(xn, ATT_HEAD_DIM, 1) * sin

        xq = norm_rope(q_ref, gq_ref) * scale
        qj[rows, :] = xq
        qb[rows, :] = xq.astype(BF16)
        xk = norm_rope(k_ref, gk_ref)
        kj[rows, :] = xk
        kb[0, rows, :] = jnp.where(qk_head0, xk, 0.0).astype(BF16)
        kb[1, rows, :] = jnp.where(qk_head0, 0.0, xk).astype(BF16)
        xv = v_ref[0, rows, :].astype(F32)
        vj[rows, :] = xv
        vb[0, rows, :] = jnp.where(v_head0, xv, 0.0).astype(BF16)
        vb[1, rows, :] = jnp.where(v_head0, 0.0, xv).astype(BF16)
        return carry

    lax.fori_loop(0, SEQ // PREP_ROWS, prep, 0, unroll=PREP_UNROLL)

    nt_dims = (((1,), (1,)), ((), ()))

    def attend(q, k0, k1, v0, v1, bias):
        w = k0.shape[0]
        s = lax.dot_general(q, jnp.concatenate([k0, k1], axis=0), nt_dims,
                            preferred_element_type=F32) + bias
        m0 = jnp.max(s[:, :w], axis=-1, keepdims=True)
        m1 = jnp.max(s[:, w:], axis=-1, keepdims=True)
        e = jnp.concatenate([jnp.exp(s[:, :w] - m0), jnp.exp(s[:, w:] - m1)], axis=1).astype(BF16)
        ones0 = jnp.broadcast_to(jnp.where(v_head0, 1.0, 0.0).astype(BF16), (w, LANES))
        ones1 = jnp.broadcast_to(jnp.where(v_head0, 0.0, 1.0).astype(BF16), (w, LANES))
        v2 = jnp.concatenate([jnp.concatenate([v0, ones0], axis=1),
                              jnp.concatenate([v1, ones1], axis=1)], axis=0)
        r = jnp.dot(e, v2, preferred_element_type=F32)
        return r[:, :LANES], jnp.where(v_head0, m0, m1), r[:, LANES:]

    def gather(ref, pieces, lead=()):
        return jnp.concatenate([ref[lead + (rows, slice(None))] for rows in pieces], axis=0)

    def masked_pair(x, head0_mask):
        return jnp.where(head0_mask, x, 0.0).astype(BF16), jnp.where(head0_mask, 0.0, x).astype(BF16)

    def store(p, pieces, n, a, m, l):
        for idx, rows in enumerate(pieces):
            acc_sc[p, rows, :] = a[idx * n:(idx + 1) * n]
            m_sc[p, rows, :] = m[idx * n:(idx + 1) * n]
            l_sc[p, rows, :] = l[idx * n:(idx + 1) * n]

    for j in range(ATT_CLASSES):
        rows = [pl.ds(j * ATT_BLOCK, ATT_BLOCK)]
        a, m, l = attend(qb[rows[0], :], kb[0, rows[0], :], kb[1, rows[0], :],
                         vb[0, rows[0], :], vb[1, rows[0], :], b16_ref[...])
        store(2, rows, ATT_BLOCK, a, m, l)

    for c in range(4):
        def segs(n):
            return [pl.ds((4 * a + c) * ATT_BLOCK + SEG4 * n, SEG4) for a in range(4)]

        for n in range(SEQ // 4 // ATT_BLOCK):
            cur = segs(n)
            keys = cur if n == 0 else segs(n - 1) + cur
            bias = b4f_ref[...] if n == 0 else b4_ref[...]
            a, m, l = attend(gather(qb, cur), gather(kb, keys, (0,)), gather(kb, keys, (1,)),
                             gather(vb, keys, (0,)), gather(vb, keys, (1,)), bias)
            store(1, cur, SEG4, a, m, l)

    for n in range(SEQ // ATT_BLOCK):
        cur = [pl.ds(j * ATT_BLOCK + SEG1 * n, SEG1) for j in range(ATT_CLASSES)]
        if n == 0:
            keys, bias = cur, b1f_ref[...]
        else:
            keys = [pl.ds(j * ATT_BLOCK + SEG1 * n - SEG1, 2 * SEG1) for j in range(ATT_CLASSES)]
            bias = b1_ref[...]
        k0, k1 = masked_pair(gather(kj, keys), qk_head0)
        v0, v1 = masked_pair(gather(vj, keys), v_head0)
        a, m, l = attend(gather(qj, cur).astype(BF16), k0, k1, v0, v1, bias)
        store(0, cur, SEG1, a, m, l)

    def merge(j, carry):
        rows = pl.ds(pl.multiple_of(j * ROWS_PER_CLASS, ROWS_PER_CLASS), ROWS_PER_CLASS)
        ms = [m_sc[p, rows, :] for p in range(3)]
        m_all = jnp.maximum(jnp.maximum(ms[0], ms[1]), ms[2])
        num = jnp.zeros((ROWS_PER_CLASS, LANES), F32)
        den = jnp.zeros((ROWS_PER_CLASS, LANES), F32)
        for p in range(3):
            w = jnp.exp(ms[p] - m_all)
            num = num + w * acc_sc[p, rows, :]
            den = den + w * l_sc[p, rows, :]
        qj[pl.ds(j, ROWS_PER_CLASS, stride=ATT_CLASSES), :] = num / den
        return carry

    lax.fori_loop(0, ATT_CLASSES, merge, 0)

    def emit(ci, carry):
        rows = pl.ds(pl.multiple_of(ci * PREP_ROWS, PREP_ROWS), PREP_ROWS)
        o_ref[rows, :] = qj[rows, :].astype(BF16)
        return carry

    lax.fori_loop(0, SEQ // PREP_ROWS, emit, 0)


def _attention(proj_att, cos_a, sin_a, gq, gk, batch):
    t = batch * SEQ
    hp = ATT_W // LANES

    def col(cb):
        return pl.BlockSpec((1, SEQ, LANES), lambda b, h: (cb + h, b, 0))

    def const(shape):
        return pl.BlockSpec(shape, lambda b, h: tuple(0 for _ in shape))

    biases = [jnp.asarray(b) for b in _att_bias_tables()]
    row_f32 = pltpu.VMEM((SEQ, LANES), F32)
    row_bf16 = pltpu.VMEM((SEQ, LANES), BF16)
    stat = pltpu.VMEM((3, SEQ, LANES), F32)
    return pl.pallas_call(
        _att_kernel,
        grid=(batch, hp),
        in_specs=[col(CB_QA), col(CB_KA), col(CB_VA),
                  const((SEQ, LANES)), const((SEQ, LANES)), const((1, LANES)), const((1, LANES))]
                 + [const(b.shape) for b in biases],
        out_specs=pl.BlockSpec((SEQ, LANES), lambda b, h: (b, h)),
        out_shape=jax.ShapeDtypeStruct((t, ATT_W), BF16),
        scratch_shapes=[row_f32] * 3 + [row_bf16] + [pltpu.VMEM((2, SEQ, LANES), BF16)] * 2 + [stat] * 3,
        compiler_params=_cparams(("arbitrary", "arbitrary")),
        name="dilated_attention",
    )(proj_att, proj_att, proj_att, cos_a, sin_a, gq, gk, *biases)


def _ret_kernel(gam_ref, q_ref, k_ref, v_ref, g_ref, cos_ref, sin_ref, decay_ref, xi_ref, zeta_ref,
                gn_ref, o_ref, state_sc):
    h = pl.program_id(1)
    gamma_c = gam_ref[h]
    state_sc[...] = jnp.zeros_like(state_sc)
    c = RET_CHUNK
    nt_dims = (((1,), (1,)), ((), ()))
    tn_dims = (((0,), (0,)), ((), ()))
    kscale = RET_QK_DIM ** -0.5

    def chunk(n, carry):
        rs = pl.ds(pl.multiple_of(n * c, c), c)
        cos = cos_ref[rs, :]
        sin = sin_ref[rs, :]

        def rot(ref):
            x1 = ref[0, rs, :].astype(F32)
            x2 = ref[1, rs, :].astype(F32)
            return x1 * cos - x2 * sin, x2 * cos + x1 * sin

        q1, q2 = rot(q_ref)
        k1, k2 = rot(k_ref)
        q = jnp.concatenate([q1, q2], axis=-1)
        k = jnp.concatenate([k1, k2], axis=-1) * kscale
        v = jnp.concatenate([v_ref[i, rs, :] for i in range(RET_V_DIM // LANES)], axis=-1)
        inner = lax.dot_general(q.astype(BF16), k.astype(BF16), nt_dims,
                                preferred_element_type=F32) * decay_ref[0]
        y = jnp.dot(inner.astype(BF16), v, preferred_element_type=F32)
        state = state_sc[...]
        qx = (q * xi_ref[0]).astype(BF16)
        y = y + jnp.dot(qx, state.astype(BF16), preferred_element_type=F32)
        kz = (k * zeta_ref[0]).astype(BF16)
        state_sc[...] = state * gamma_c + lax.dot_general(kz, v, tn_dims, preferred_element_type=F32)
        yn = y * lax.rsqrt(jnp.mean(y * y, axis=-1, keepdims=True) + EPS) * gn_ref[0]
        g = jnp.concatenate([g_ref[i, rs, :] for i in range(RET_V_DIM // LANES)], axis=-1).astype(F32)
        o_ref[rs, :] = (yn * (g * jax.nn.sigmoid(g))).astype(BF16)
        return carry

    lax.fori_loop(0, SEQ // c, chunk, 0)


def _retention(proj, gamma_c, cos_r, sin_r, decay, xi, zeta, g_ret, batch):
    t = batch * SEQ
    nq = RET_QK_DIM // LANES
    nv = RET_V_DIM // LANES

    def cols(cb, n):
        return pl.BlockSpec((n, SEQ, LANES), lambda b, h: (cb // n + h, b, 0))

    tab = pl.BlockSpec((SEQ, LANES), lambda b, h: (0, 0))
    return pl.pallas_call(
        _ret_kernel,
        grid=(batch, RET_HEADS),
        in_specs=[
            pl.BlockSpec(memory_space=pltpu.SMEM),
            cols(CB_QR, nq), cols(CB_KR, nq), cols(CB_VR, nv), cols(CB_GR, nv),
            tab, tab,
            pl.BlockSpec((1, RET_CHUNK, RET_CHUNK), lambda b, h: (h, 0, 0)),
            pl.BlockSpec((1, RET_CHUNK, RET_QK_DIM), lambda b, h: (h, 0, 0)),
            pl.BlockSpec((1, RET_CHUNK, RET_QK_DIM), lambda b, h: (h, 0, 0)),
            pl.BlockSpec((1, 1, RET_V_DIM), lambda b, h: (h, 0, 0)),
        ],
        out_specs=pl.BlockSpec((SEQ, RET_V_DIM), lambda b, h: (b, h)),
        out_shape=jax.ShapeDtypeStruct((t, RET_V_W), BF16),
        scratch_shapes=[pltpu.VMEM((RET_QK_DIM, RET_V_DIM), F32)],
        compiler_params=_cparams(("arbitrary", "arbitrary")),
        name="retention",
    )(gamma_c, proj, proj, proj, proj, cos_r, sin_r, decay, xi, zeta, g_ret)


OUT_TM = 512
N_ROUTE = N_EXPERTS + N_GROUPS
GROUP_LANE = LANES - 1
XG_W = D_MODEL + LANES


def _split_bf16(x):
    hi = x.astype(BF16)
    lo = (x - hi.astype(F32)).astype(BF16)
    return hi, lo


def _out_kernel(ya_ref, yr_ref, ga_ref, gb_ref, bg_ref, x_ref, watt_ref, wret_ref, wout_ref,
                gffn_ref, wr_hi_ref, wr_lo_ref, br_ref, x1_ref, xg_ref):
    ya = jnp.dot(ya_ref[...], watt_ref[...], preferred_element_type=F32)
    yr = jnp.dot(yr_ref[...], wret_ref[...], preferred_element_type=F32)
    merged = []
    for cb in range(D_MODEL // LANES):
        cs = slice(cb * LANES, (cb + 1) * LANES)
        ga = jax.nn.sigmoid(ga_ref[cb].astype(F32) + bg_ref[:, cs])
        gb = jax.nn.sigmoid(gb_ref[cb].astype(F32) + bg_ref[:, D_MODEL + cb * LANES:D_MODEL + (cb + 1) * LANES])
        merged.append((ga * ya[:, cs] + gb * yr[:, cs]).astype(BF16))
    merged = jnp.concatenate(merged, axis=-1)
    x1 = x_ref[...] + jnp.dot(merged, wout_ref[...], preferred_element_type=F32)
    x1_ref[...] = x1
    xn = x1 * lax.rsqrt(jnp.mean(x1 * x1, axis=-1, keepdims=True) + EPS) * gffn_ref[...]
    xg_ref[:, :D_MODEL] = xn

    xh, xl = _split_bf16(xn)
    logits = (jnp.dot(xh, wr_hi_ref[...], preferred_element_type=F32)
              + jnp.dot(xl, wr_hi_ref[...], preferred_element_type=F32)
              + jnp.dot(xh, wr_lo_ref[...], preferred_element_type=F32)) + br_ref[...]
    lane = lax.broadcasted_iota(jnp.int32, logits.shape, 1)
    lane_f = lane.astype(F32)
    is_group = (lane >= N_EXPERTS) & (lane < N_ROUTE)
    gl = jnp.where(is_group, logits, NEG_BIG)
    gmax = jnp.max(gl, axis=-1, keepdims=True)
    gsel = jnp.min(jnp.where(gl == gmax, lane_f, 1e9), axis=-1, keepdims=True) - N_EXPERTS
    p_group = 1.0 / jnp.sum(jnp.where(is_group, jnp.exp(logits - gmax), 0.0), axis=-1, keepdims=True)
    lo_lane = gsel * EXPERTS_PER_GROUP
    in_group = (lane_f >= lo_lane) & (lane_f < lo_lane + EXPERTS_PER_GROUP)
    el = jnp.where(in_group, logits, NEG_BIG)
    v1 = jnp.max(el, axis=-1, keepdims=True)
    i1 = jnp.min(jnp.where(el == v1, lane_f, 1e9), axis=-1, keepdims=True)
    el2 = jnp.where(lane_f == i1, NEG_BIG, el)
    v2 = jnp.max(el2, axis=-1, keepdims=True)
    i2 = jnp.min(jnp.where(el2 == v2, lane_f, 1e9), axis=-1, keepdims=True)
    e21 = jnp.exp(v2 - v1)
    w1 = p_group / (1.0 + e21)
    w2 = w1 * e21
    gate = jnp.where(lane_f == i1, w1, 0.0) + jnp.where(lane_f == i2, w2, 0.0)
    xg_ref[:, D_MODEL:] = jnp.where(lane == GROUP_LANE, gsel, gate)


def _out_stage(y_att, y_ret, proj, b_gate, x2d, w_att, w_ret, w_out, g_ffn, wr_hi, wr_lo, b_route):
    t = x2d.shape[0]
    ncb = D_MODEL // LANES

    def full(shape):
        return pl.BlockSpec(shape, lambda i: tuple(0 for _ in shape))

    return pl.pallas_call(
        _out_kernel,
        grid=(t // OUT_TM,),
        in_specs=[
            pl.BlockSpec((OUT_TM, ATT_W), lambda i: (i, 0)),
            pl.BlockSpec((OUT_TM, RET_V_W), lambda i: (i, 0)),
            pl.BlockSpec((ncb, OUT_TM, LANES), lambda i: (CB_GA // ncb, i, 0)),
            pl.BlockSpec((ncb, OUT_TM, LANES), lambda i: (CB_GB // ncb, i, 0)),
            full((1, 2 * D_MODEL)),
            pl.BlockSpec((OUT_TM, D_MODEL), lambda i: (i, 0)),
            full((ATT_W, D_MODEL)), full((RET_V_W, D_MODEL)), full((D_MODEL, D_MODEL)),
            full((1, D_MODEL)), full((D_MODEL, LANES)), full((D_MODEL, LANES)), full((1, LANES)),
        ],
        out_specs=[
            pl.BlockSpec((OUT_TM, D_MODEL), lambda i: (i, 0)),
            pl.BlockSpec((OUT_TM, XG_W), lambda i: (i, 0)),
        ],
        out_shape=[
            jax.ShapeDtypeStruct((t, D_MODEL), F32),
            jax.ShapeDtypeStruct((t, XG_W), F32),
        ],
        compiler_params=_cparams(("arbitrary",)),
        name="out_stage",
    )(y_att, y_ret, proj, proj, b_gate, x2d, w_att, w_ret, w_out, g_ffn, wr_hi, wr_lo, b_route)


MOE_TILE = 512
MOVE_TM = 1024
MOVE_UNROLL = 8
GROUP_FF = EXPERTS_PER_GROUP * EXPERT_FF


def _moe_plan(xg, t):
    i32 = jnp.int32
    g = xg[:, D_MODEL + GROUP_LANE].astype(i32)
    onehot = (g[:, None] == jnp.arange(N_GROUPS, dtype=i32)[None, :]).astype(i32)
    csum = jnp.cumsum(onehot, axis=0)
    rank = jnp.sum(onehot * csum, axis=1) - 1
    padded = (csum[-1] + MOE_TILE - 1) // MOE_TILE * MOE_TILE
    ends = jnp.cumsum(padded)
    pos = rank + jnp.sum(onehot * (ends - padded)[None, :], axis=1)
    tile_start = jnp.arange(t // MOE_TILE + N_GROUPS, dtype=i32) * MOE_TILE
    tile_group = jnp.minimum(jnp.sum((tile_start[:, None] >= ends[None, :]).astype(i32), axis=1),
                             N_GROUPS - 1)
    return pos.astype(i32), tile_group.astype(i32), (ends[-1:] // MOE_TILE).astype(i32)


def _dispatch_kernel(pos_ref, xg_ref, xs_init_ref, xs_ref, sem):
    del xs_init_ref
    base = pl.program_id(0) * MOVE_TM

    def send(r, carry):
        pltpu.make_async_copy(xg_ref.at[pl.ds(r, 1)], xs_ref.at[pl.ds(pos_ref[base + r], 1)], sem).start()
        return carry

    lax.fori_loop(0, MOVE_TM, send, 0, unroll=MOVE_UNROLL)
    pltpu.make_async_copy(xg_ref, xs_ref.at[pl.ds(0, MOVE_TM)], sem).wait()


def _dispatch(pos, xg, n_rows):
    t = xg.shape[0]
    return pl.pallas_call(
        _dispatch_kernel,
        grid_spec=pltpu.PrefetchScalarGridSpec(
            num_scalar_prefetch=1,
            grid=(t // MOVE_TM,),
            in_specs=[pl.BlockSpec((MOVE_TM, XG_W), lambda i, pos: (i, 0)),
                      pl.BlockSpec(memory_space=pl.ANY)],
            out_specs=pl.BlockSpec(memory_space=pl.ANY),
            scratch_shapes=[pltpu.SemaphoreType.DMA],
        ),
        out_shape=jax.ShapeDtypeStruct((n_rows, XG_W), F32),
        input_output_aliases={2: 0},
        compiler_params=_cparams(("arbitrary",)),
        name="moe_dispatch",
    )(pos, xg, jnp.zeros((n_rows, XG_W), F32))


def _experts_kernel(tg_ref, nused_ref, xs_ref, w1_ref, w3_ref, w2_ref, ys_ref):
    i = pl.program_id(0)

    @pl.when(i < nused_ref[0])
    def _():
        x = xs_ref[:, :D_MODEL].astype(BF16)
        gate = xs_ref[:, D_MODEL:]
        lane = lax.broadcasted_iota(jnp.int32, gate.shape, 1)
        first = tg_ref[i] * EXPERTS_PER_GROUP
        hidden = []
        for e in range(EXPERTS_PER_GROUP):
            a = jnp.dot(x, w1_ref[0, e], preferred_element_type=F32)
            b = jnp.dot(x, w3_ref[0, e], preferred_element_type=F32)
            g = jnp.sum(jnp.where(lane == first + e, gate, 0.0), axis=-1, keepdims=True)
            hidden.append((a * jax.nn.sigmoid(a) * b * g).astype(BF16))
        ys_ref[...] = jnp.dot(jnp.concatenate(hidden, axis=-1), w2_ref[0], preferred_element_type=F32)

    @pl.when(i >= nused_ref[0])
    def _():
        ys_ref[...] = jnp.zeros_like(ys_ref)


def _experts(tile_group, n_used, xs, w1g, w3g, w2g):
    n_rows = xs.shape[0]
    up = pl.BlockSpec((1, EXPERTS_PER_GROUP, D_MODEL, EXPERT_FF), lambda i, tg, nu: (tg[i], 0, 0, 0))
    return pl.pallas_call(
        _experts_kernel,
        grid_spec=pltpu.PrefetchScalarGridSpec(
            num_scalar_prefetch=2,
            grid=(n_rows // MOE_TILE,),
            in_specs=[
                pl.BlockSpec((MOE_TILE, XG_W), lambda i, tg, nu: (i, 0)),
                up, up,
                pl.BlockSpec((1, GROUP_FF, D_MODEL), lambda i, tg, nu: (tg[i], 0, 0)),
            ],
            out_specs=pl.BlockSpec((MOE_TILE, D_MODEL), lambda i, tg, nu: (i, 0)),
        ),
        out_shape=jax.ShapeDtypeStruct((n_rows, D_MODEL), F32),
        compiler_params=_cparams(("arbitrary",)),
        name="moe_experts",
    )(tile_group, n_used, xs, w1g, w3g, w2g)


def _combine_kernel(pos_ref, x1_ref, ys_ref, o_ref, buf, sem):
    base = pl.program_id(0) * MOVE_TM

    def fetch(r, carry):
        pltpu.make_async_copy(ys_ref.at[pl.ds(pos_ref[base + r], 1)], buf.at[pl.ds(r, 1)], sem).start()
        return carry

    lax.fori_loop(0, MOVE_TM, fetch, 0, unroll=MOVE_UNROLL)
    pltpu.make_async_copy(ys_ref.at[pl.ds(0, MOVE_TM)], buf, sem).wait()
    o_ref[...] = x1_ref[...] + buf[...]


def _combine(pos, x1, ys):
    t = x1.shape[0]
    return pl.pallas_call(
        _combine_kernel,
        grid_spec=pltpu.PrefetchScalarGridSpec(
            num_scalar_prefetch=1,
            grid=(t // MOVE_TM,),
            in_specs=[pl.BlockSpec((MOVE_TM, D_MODEL), lambda i, pos: (i, 0)),
                      pl.BlockSpec(memory_space=pl.ANY)],
            out_specs=pl.BlockSpec((MOVE_TM, D_MODEL), lambda i, pos: (i, 0)),
            scratch_shapes=[pltpu.VMEM((MOVE_TM, D_MODEL), F32), pltpu.SemaphoreType.DMA],
        ),
        out_shape=jax.ShapeDtypeStruct((t, D_MODEL), F32),
        compiler_params=_cparams(("arbitrary",)),
        name="moe_combine",
    )(pos, x1, ys)


def _moe(xg, x1, w1, w3, w2):
    t = xg.shape[0]
    w1g = w1.reshape(N_GROUPS, EXPERTS_PER_GROUP, D_MODEL, EXPERT_FF).astype(BF16)
    w3g = w3.reshape(N_GROUPS, EXPERTS_PER_GROUP, D_MODEL, EXPERT_FF).astype(BF16)
    w2g = w2.reshape(N_GROUPS, GROUP_FF, D_MODEL).astype(BF16)
    pos, tile_group, n_used = _moe_plan(xg, t)
    xs = _dispatch(pos, xg, t + N_GROUPS * MOE_TILE)
    ys = _experts(tile_group, n_used, xs, w1g, w3g, w2g)
    return _combine(pos, x1, ys)


def _class_major(table):
    return table.reshape(ROWS_PER_CLASS, ATT_CLASSES, -1).transpose(1, 0, 2).reshape(SEQ, -1)


def _rope_tables_att():
    pos = jnp.arange(SEQ, dtype=F32)
    inv = ROPE_THETA ** (-jnp.arange(0, ATT_HEAD_DIM, 2, dtype=F32) / ATT_HEAD_DIM)
    ang = pos[:, None] * inv[None, :]
    cos, sin = jnp.cos(ang), jnp.sin(ang)
    cos_full = jnp.concatenate([cos, cos, cos, cos], axis=-1)
    sin_full = jnp.concatenate([-sin, -sin, sin, sin], axis=-1)
    return _class_major(cos_full), _class_major(sin_full)


def _pair_lanes(a):
    half = ATT_HEAD_DIM // 2
    lead = a.shape[:-1]
    a = a.reshape(lead + (-1, 2, 2, half))
    return jnp.swapaxes(a, -3, -2).reshape(lead + (-1,))


def _rope_tables_ret():
    pos = jnp.arange(SEQ, dtype=F32)
    inv = 1.0 / (ROPE_THETA ** jnp.linspace(0.0, 1.0, RET_QK_DIM // 2, dtype=F32))
    ang = pos[:, None] * inv[None, :]
    return jnp.cos(ang), jnp.sin(ang)


def _decay_tables():
    c = RET_CHUNK
    log_gamma = jnp.log(1.0 - jnp.exp2(-5.0 - jnp.arange(RET_HEADS, dtype=F32)))
    idx = jnp.arange(c, dtype=F32)
    diff = idx[:, None] - idx[None, :]
    decay = jnp.where(diff >= 0, jnp.exp(log_gamma[:, None, None] * jnp.maximum(diff, 0.0)), 0.0)
    zeta = jnp.exp(log_gamma[:, None] * (c - 1 - idx))
    xi = jnp.exp(log_gamma[:, None] * (idx + 1))
    gamma_c = jnp.exp(log_gamma * c)
    bc = lambda a: jnp.broadcast_to(a[:, :, None], (RET_HEADS, c, RET_QK_DIM))
    return gamma_c, decay, bc(xi), bc(zeta)


def kernel(x, g_norm_mix, w_in, b_merge_gate, g_q, g_k, w_branch_att, g_ret_norm, w_branch_ret,
           w_out, g_norm_ffn, w_router_group, b_router_group, w_router_expert, b_router_expert,
           w1, w3, w2):
    batch = x.shape[0]
    t = batch * SEQ
    cos_a, sin_a = _rope_tables_att()
    cos_r, sin_r = _rope_tables_ret()
    gamma_c, decay, xi, zeta = _decay_tables()
    xf = x.reshape(t, D_MODEL)
    for l in range(g_norm_mix.shape[0]):
        g_mix = g_norm_mix[l][None, :]
        w_att_in = jnp.concatenate(
            [_pair_lanes(w_in[l][:, :2 * ATT_W]), w_in[l][:, 2 * ATT_W:ATT_IN_W]], axis=-1)
        proj_att = _inproj(xf, g_mix, w_att_in.astype(BF16), True)
        proj = _inproj(xf, g_mix, w_in[l][:, ATT_IN_W:].astype(BF16), False)
        reps = LANES // ATT_HEAD_DIM
        y_att = _attention(proj_att, cos_a, sin_a, _pair_lanes(jnp.tile(g_q[l], reps))[None, :],
                           _pair_lanes(jnp.tile(g_k[l], reps))[None, :], batch)
        y_ret = _retention(proj, gamma_c, cos_r, sin_r, decay, xi, zeta,
                           g_ret_norm[l][:, None, :], batch)
        w_route = jnp.concatenate(
            [w_router_expert[l], w_router_group[l],
             jnp.zeros((D_MODEL, LANES - N_ROUTE), F32)], axis=-1)
        wr_hi, wr_lo = _split_bf16(w_route)
        b_route = jnp.concatenate(
            [b_router_expert[l], b_router_group[l], jnp.zeros((LANES - N_ROUTE,), F32)])[None, :]
        x1, xg = _out_stage(
            y_att, y_ret, proj, b_merge_gate[l][None, :], xf,
            w_branch_att[l].astype(BF16), w_branch_ret[l].astype(BF16), w_out[l].astype(BF16),
            g_norm_ffn[l][None, :], wr_hi, wr_lo, b_route)
        xf = _moe(xg, x1, w1[l], w3[l], w2[l])
    return xf.reshape(batch, SEQ, D_MODEL)
```

```python
import functools

import numpy as np

import jax
import jax.numpy as jnp
from jax import lax
from jax.experimental import pallas as pl
from jax.experimental.pallas import tpu as pltpu

F32 = jnp.float32
BF16 = jnp.bfloat16

D_MODEL = 1024
SEQ = 2048
ATT_HEADS = 16
ATT_HEAD_DIM = 64
ATT_W = ATT_HEADS * ATT_HEAD_DIM
ROPE_THETA = 10000.0
RET_HEADS = 4
RET_QK_DIM = 256
RET_V_DIM = 512
RET_QK_W = RET_HEADS * RET_QK_DIM
RET_V_W = RET_HEADS * RET_V_DIM
N_GROUPS = 4
EXPERTS_PER_GROUP = 8
N_EXPERTS = N_GROUPS * EXPERTS_PER_GROUP
EXPERT_FF = 256
EPS = 1e-6
ATT_IN_W = 3 * ATT_W
REST_IN_W = 2 * RET_QK_W + 2 * RET_V_W + 2 * D_MODEL

LANES = 128
CB_QA, CB_KA, CB_VA = 0, 8, 16
CB_QR, CB_KR, CB_VR, CB_GR, CB_GA, CB_GB = 0, 8, 16, 32, 48, 56

ATT_BLOCK = 128
ATT_CLASSES = 16
ROWS_PER_CLASS = SEQ // ATT_CLASSES
RET_CHUNK = 256
RET_UNROLL = 2
NEG_BIG = -1e30
VMEM_LIMIT = 48 * 1024 * 1024


def _cparams(sem):
    return pltpu.CompilerParams(dimension_semantics=sem, vmem_limit_bytes=VMEM_LIMIT)


IN_TM = 1024
IN_TN = 1024
IN_NORM_ROWS = 256
REST_TILE_QR, REST_TILE_KR = CB_QR * LANES // IN_TN, CB_KR * LANES // IN_TN
REST_TILE_GR, REST_TILE_GA = CB_GR * LANES // IN_TN, CB_GA * LANES // IN_TN


def _rmsnorm_rows(x, g_ref):
    ms = jnp.mean(x * x, axis=-1, keepdims=True)
    return x * lax.rsqrt(ms + EPS) * g_ref[...]


def _project(xn_sc, w_ref, o_ref, epilogue):
    xn = xn_sc[...]
    for c2 in range(IN_TN // 256):
        acc = jnp.dot(xn, w_ref[:, c2 * 256:(c2 + 1) * 256], preferred_element_type=F32)
        lo, hi = epilogue(acc[:, :LANES], acc[:, LANES:])
        o_ref[2 * c2] = lo.astype(BF16)
        o_ref[2 * c2 + 1] = hi.astype(BF16)


def _inproj_att_kernel(x_ref, g_ref, w_ref, o_ref, xn_sc, xs):
    @pl.when(pl.program_id(1) == 0)
    def _():
        def norm_rows(ci, carry):
            rows = pl.ds(pl.multiple_of(ci * IN_NORM_ROWS, IN_NORM_ROWS), IN_NORM_ROWS)
            xn = _rmsnorm_rows(x_ref[rows, :], g_ref)
            for c in range(D_MODEL // LANES):
                xs[c, rows, :] = xn[:, c * LANES:(c + 1) * LANES]
            return carry

        lax.fori_loop(0, SEQ // IN_NORM_ROWS, norm_rows, 0)

        def gather_class(j, carry):
            dst = pl.ds(pl.multiple_of(j * ROWS_PER_CLASS, ROWS_PER_CLASS), ROWS_PER_CLASS)
            for c in range(D_MODEL // LANES):
                xn_sc[dst, c * LANES:(c + 1) * LANES] = xs[
                    c, pl.ds(j, ROWS_PER_CLASS, stride=ATT_CLASSES), :].astype(BF16)
            return carry

        lax.fori_loop(0, ATT_CLASSES, gather_class, 0)

    _project(xn_sc, w_ref, o_ref, lambda lo, hi: (lo, hi))


def _inproj_rest_kernel(x_ref, g_ref, w_ref, cos_ref, sin_ref, o_ref, xn_sc):
    j = pl.program_id(1)

    @pl.when(j == 0)
    def _():
        xn_sc[...] = _rmsnorm_rows(x_ref[...], g_ref).astype(BF16)

    def rotate(scale):
        def epilogue(x1, x2):
            cos, sin = cos_ref[...], sin_ref[...]
            return (x1 * cos - x2 * sin) * scale, (x2 * cos + x1 * sin) * scale
        return epilogue

    def swish(lo, hi):
        return lo * jax.nn.sigmoid(lo), hi * jax.nn.sigmoid(hi)

    is_gate = (j >= REST_TILE_GR) & (j < REST_TILE_GA)

    @pl.when(j == REST_TILE_QR)
    def _():
        _project(xn_sc, w_ref, o_ref, rotate(1.0))

    @pl.when(j == REST_TILE_KR)
    def _():
        _project(xn_sc, w_ref, o_ref, rotate(RET_QK_DIM ** -0.5))

    @pl.when(is_gate)
    def _():
        _project(xn_sc, w_ref, o_ref, swish)

    @pl.when((j > REST_TILE_KR) & jnp.logical_not(is_gate))
    def _():
        _project(xn_sc, w_ref, o_ref, lambda lo, hi: (lo, hi))


def _inproj_att(x2d, g, w_bf16):
    t = x2d.shape[0]
    width = w_bf16.shape[1]
    return pl.pallas_call(
        _inproj_att_kernel,
        grid=(t // SEQ, width // IN_TN),
        in_specs=[
            pl.BlockSpec((SEQ, D_MODEL), lambda i, j: (i, 0)),
            pl.BlockSpec((1, D_MODEL), lambda i, j: (0, 0)),
            pl.BlockSpec((D_MODEL, IN_TN), lambda i, j: (0, j)),
        ],
        out_specs=pl.BlockSpec((IN_TN // LANES, SEQ, LANES), lambda i, j: (j, i, 0)),
        out_shape=jax.ShapeDtypeStruct((width // LANES, t, LANES), BF16),
        scratch_shapes=[pltpu.VMEM((SEQ, D_MODEL), BF16),
                        pltpu.VMEM((D_MODEL // LANES, SEQ, LANES), F32)],
        compiler_params=_cparams(("arbitrary", "arbitrary")),
        name="inproj_att",
    )(x2d, g, w_bf16)


def _inproj_rest(x2d, g, w_bf16, cos_r, sin_r):
    t = x2d.shape[0]
    width = w_bf16.shape[1]
    tiles_per_seq = SEQ // IN_TM
    table = pl.BlockSpec((IN_TM, LANES), lambda i, j: (i % tiles_per_seq, 0))
    return pl.pallas_call(
        _inproj_rest_kernel,
        grid=(t // IN_TM, width // IN_TN),
        in_specs=[
            pl.BlockSpec((IN_TM, D_MODEL), lambda i, j: (i, 0)),
            pl.BlockSpec((1, D_MODEL), lambda i, j: (0, 0)),
            pl.BlockSpec((D_MODEL, IN_TN), lambda i, j: (0, j)),
            table, table,
        ],
        out_specs=pl.BlockSpec((IN_TN // LANES, IN_TM, LANES), lambda i, j: (j, i, 0)),
        out_shape=jax.ShapeDtypeStruct((width // LANES, t, LANES), BF16),
        scratch_shapes=[pltpu.VMEM((IN_TM, D_MODEL), BF16)],
        compiler_params=_cparams(("arbitrary", "arbitrary")),
        name="inproj_rest",
    )(x2d, g, w_bf16, cos_r, sin_r)


PREP_ROWS = 256
PREP_UNROLL = 4
SEG4 = ATT_BLOCK // 4
SEG1 = ATT_BLOCK // ATT_CLASSES


def _att_bias_tables():
    def tile(qpos, kpos):
        d = qpos[:, None] - kpos[None, :]
        one = np.where((d >= 0) & (d <= ATT_BLOCK), 0.0, NEG_BIG).astype(np.float32)
        return np.concatenate([one, one], axis=1)

    u = np.arange(ATT_BLOCK)
    q4 = 4 * (u % SEG4) + u // SEG4
    q1 = ATT_CLASSES * (u % SEG1) + u // SEG1
    w = np.arange(2 * ATT_BLOCK)
    k1 = ATT_CLASSES * (w % SEG1) + w // (2 * SEG1) + ATT_BLOCK * ((w // SEG1) % 2 - 1)
    return (tile(q1, k1), tile(q1, q1), tile(q4, np.concatenate([q4 - ATT_BLOCK, q4])), tile(q4, q4),
            tile(u, u))


def _att_kernel(q_ref, k_ref, v_ref, cos_ref, sin_ref, gq_ref, gk_ref,
                b1_ref, b1f_ref, b4_ref, b4f_ref, b16_ref, o_ref,
                qj, kj, vj, qb, kb, vb, acc_sc, m_sc, l_sc):
    lane = lax.broadcasted_iota(jnp.int32, (1, LANES), 1)
    half = ATT_HEAD_DIM // 2
    qk_head0 = (lane // half) % 2 == 0
    v_head0 = lane < ATT_HEAD_DIM
    seg = ((lax.broadcasted_iota(jnp.int32, (LANES, LANES), 0) // half) % 2
           == (lax.broadcasted_iota(jnp.int32, (LANES, LANES), 1) // half) % 2).astype(BF16)
    scale = ATT_HEAD_DIM ** -0.5

    def prep(ci, carry):
        rows = pl.ds(pl.multiple_of(ci * PREP_ROWS, PREP_ROWS), PREP_ROWS)
        cos = cos_ref[rows, :]
        sin = sin_ref[rows, :]

        def norm_rope(src, g_ref):
            x = src[0, rows, :].astype(F32)
            x2 = x * x
            hi = x2.astype(BF16)
            lo = (x2 - hi.astype(F32)).astype(BF16)
            ss = (jnp.dot(hi, seg, preferred_element_type=F32)
                  + jnp.dot(lo, seg, preferred_element_type=F32))
            xn = x * lax.rsqrt(ss * (1.0 / ATT_HEAD_DIM) + EPS) * g_ref[...]
            return xn * cos + pltpu.roll(xn, ATT_HEAD_DIM, 1) * sin

        xq = norm_rope(q_ref, gq_ref) * scale
        qj[rows, :] = xq
        qb[rows, :] = xq.astype(BF16)
        xk = norm_rope(k_ref, gk_ref)
        kj[rows, :] = xk
        kb[0, rows, :] = jnp.where(qk_head0, xk, 0.0).astype(BF16)
        kb[1, rows, :] = jnp.where(qk_head0, 0.0, xk).astype(BF16)
        xv = v_ref[0, rows, :].astype(F32)
        vj[rows, :] = xv
        vb[0, rows, :] = jnp.where(v_head0, xv, 0.0).astype(BF16)
        vb[1, rows, :] = jnp.where(v_head0, 0.0, xv).astype(BF16)
        return carry

    lax.fori_loop(0, SEQ // PREP_ROWS, prep, 0, unroll=PREP_UNROLL)

    nt_dims = (((1,), (1,)), ((), ()))

    def attend(q, k0, k1, v0, v1, bias):
        w = k0.shape[0]
        s = lax.dot_general(q, jnp.concatenate([k0, k1], axis=0), nt_dims,
                            preferred_element_type=F32) + bias
        m0 = jnp.max(s[:, :w], axis=-1, keepdims=True)
        m1 = jnp.max(s[:, w:], axis=-1, keepdims=True)
        e = jnp.concatenate([jnp.exp(s[:, :w] - m0), jnp.exp(s[:, w:] - m1)], axis=1).astype(BF16)
        ones0 = jnp.broadcast_to(jnp.where(v_head0, 1.0, 0.0).astype(BF16), (w, LANES))
        ones1 = jnp.broadcast_to(jnp.where(v_head0, 0.0, 1.0).astype(BF16), (w, LANES))
        v2 = jnp.concatenate([jnp.concatenate([v0, ones0], axis=1),
                              jnp.concatenate([v1, ones1], axis=1)], axis=0)
        r = jnp.dot(e, v2, preferred_element_type=F32)
        return r[:, :LANES], jnp.where(v_head0, m0, m1), r[:, LANES:]

    def gather(ref, pieces, lead=()):
        return jnp.concatenate([ref[lead + (rows, slice(None))] for rows in pieces], axis=0)

    def masked_pair(x, head0_mask):
        return jnp.where(head0_mask, x, 0.0).astype(BF16), jnp.where(head0_mask, 0.0, x).astype(BF16)

    def store(p, pieces, n, a, m, l):
        for idx, rows in enumerate(pieces):
            acc_sc[p, rows, :] = a[idx * n:(idx + 1) * n]
            m_sc[p, rows, :] = m[idx * n:(idx + 1) * n]
            l_sc[p, rows, :] = l[idx * n:(idx + 1) * n]

    for j in range(ATT_CLASSES):
        rows = [pl.ds(j * ATT_BLOCK, ATT_BLOCK)]
        a, m, l = attend(qb[rows[0], :], kb[0, rows[0], :], kb[1, rows[0], :],
                         vb[0, rows[0], :], vb[1, rows[0], :], b16_ref[...])
        store(2, rows, ATT_BLOCK, a, m, l)

    for c in range(4):
        def segs(n):
            return [pl.ds((4 * a + c) * ATT_BLOCK + SEG4 * n, SEG4) for a in range(4)]

        for n in range(SEQ // 4 // ATT_BLOCK):
            cur = segs(n)
            keys = cur if n == 0 else segs(n - 1) + cur
            bias = b4f_ref[...] if n == 0 else b4_ref[...]
            a, m, l = attend(gather(qb, cur), gather(kb, keys, (0,)), gather(kb, keys, (1,)),
                             gather(vb, keys, (0,)), gather(vb, keys, (1,)), bias)
            store(1, cur, SEG4, a, m, l)

    for n in range(SEQ // ATT_BLOCK):
        cur = [pl.ds(j * ATT_BLOCK + SEG1 * n, SEG1) for j in range(ATT_CLASSES)]
        if n == 0:
            keys, bias = cur, b1f_ref[...]
        else:
            keys = [pl.ds(j * ATT_BLOCK + SEG1 * n - SEG1, 2 * SEG1) for j in range(ATT_CLASSES)]
            bias = b1_ref[...]
        k0, k1 = masked_pair(gather(kj, keys), qk_head0)
        v0, v1 = masked_pair(gather(vj, keys), v_head0)
        a, m, l = attend(gather(qj, cur).astype(BF16), k0, k1, v0, v1, bias)
        store(0, cur, SEG1, a, m, l)

    def merge(j, carry):
        rows = pl.ds(pl.multiple_of(j * ROWS_PER_CLASS, ROWS_PER_CLASS), ROWS_PER_CLASS)
        ms = [m_sc[p, rows, :] for p in range(3)]
        m_all = jnp.maximum(jnp.maximum(ms[0], ms[1]), ms[2])
        num = jnp.zeros((ROWS_PER_CLASS, LANES), F32)
        den = jnp.zeros((ROWS_PER_CLASS, LANES), F32)
        for p in range(3):
            w = jnp.exp(ms[p] - m_all)
            num = num + w * acc_sc[p, rows, :]
            den = den + w * l_sc[p, rows, :]
        qj[pl.ds(j, ROWS_PER_CLASS, stride=ATT_CLASSES), :] = num / den
        return carry

    lax.fori_loop(0, ATT_CLASSES, merge, 0)

    def emit(ci, carry):
        rows = pl.ds(pl.multiple_of(ci * PREP_ROWS, PREP_ROWS), PREP_ROWS)
        o_ref[rows, :] = qj[rows, :].astype(BF16)
        return carry

    lax.fori_loop(0, SEQ // PREP_ROWS, emit, 0)


def _attention(proj_att, cos_a, sin_a, gq, gk, batch):
    t = batch * SEQ
    hp = ATT_W // LANES

    def col(cb):
        return pl.BlockSpec((1, SEQ, LANES), lambda b, h: (cb + h, b, 0))

    def const(shape):
        return pl.BlockSpec(shape, lambda b, h: tuple(0 for _ in shape))

    biases = [jnp.asarray(b) for b in _att_bias_tables()]
    row_f32 = pltpu.VMEM((SEQ, LANES), F32)
    row_bf16 = pltpu.VMEM((SEQ, LANES), BF16)
    stat = pltpu.VMEM((3, SEQ, LANES), F32)
    return pl.pallas_call(
        _att_kernel,
        grid=(batch, hp),
        in_specs=[col(CB_QA), col(CB_KA), col(CB_VA),
                  const((SEQ, LANES)), const((SEQ, LANES)), const((1, LANES)), const((1, LANES))]
                 + [const(b.shape) for b in biases],
        out_specs=pl.BlockSpec((SEQ, LANES), lambda b, h: (b, h)),
        out_shape=jax.ShapeDtypeStruct((t, ATT_W), BF16),
        scratch_shapes=[row_f32] * 3 + [row_bf16] + [pltpu.VMEM((2, SEQ, LANES), BF16)] * 2 + [stat] * 3,
        compiler_params=_cparams(("arbitrary", "arbitrary")),
        name="dilated_attention",
    )(proj_att, proj_att, proj_att, cos_a, sin_a, gq, gk, *biases)


def _ret_kernel(gam_ref, q_ref, k_ref, v_ref, g_ref, decay_ref, xi_ref, zeta_ref,
                gn_ref, o_ref, state_sc):
    h = pl.program_id(1)
    gamma_c = gam_ref[h]
    state_sc[...] = jnp.zeros_like(state_sc)
    c = RET_CHUNK
    nt_dims = (((1,), (1,)), ((), ()))
    tn_dims = (((0,), (0,)), ((), ()))

    def cols(ref, rs, n):
        return jnp.concatenate([ref[i, rs, :] for i in range(n)], axis=-1)

    def chunk(n, carry):
        rs = pl.ds(pl.multiple_of(n * c, c), c)
        q = cols(q_ref, rs, RET_QK_DIM // LANES)
        k = cols(k_ref, rs, RET_QK_DIM // LANES)
        v = cols(v_ref, rs, RET_V_DIM // LANES)
        inner = lax.dot_general(q, k, nt_dims, preferred_element_type=F32) * decay_ref[0]
        y = jnp.dot(inner.astype(BF16), v, preferred_element_type=F32)
        state = state_sc[...]
        y = y + jnp.dot(q * xi_ref[0], state.astype(BF16), preferred_element_type=F32)
        state_sc[...] = state * gamma_c + lax.dot_general(k * zeta_ref[0], v, tn_dims,
                                                          preferred_element_type=F32)
        yn = y * lax.rsqrt(jnp.mean(y * y, axis=-1, keepdims=True) + EPS) * gn_ref[0]
        o_ref[rs, :] = (yn * cols(g_ref, rs, RET_V_DIM // LANES).astype(F32)).astype(BF16)
        return carry

    lax.fori_loop(0, SEQ // c, chunk, 0, unroll=RET_UNROLL)


def _retention(proj, gamma_c, decay, xi, zeta, g_ret, batch):
    t = batch * SEQ
    nq = RET_QK_DIM // LANES
    nv = RET_V_DIM // LANES

    def cols(cb, n):
        return pl.BlockSpec((n, SEQ, LANES), lambda b, h: (cb // n + h, b, 0))

    return pl.pallas_call(
        _ret_kernel,
        grid=(batch, RET_HEADS),
        in_specs=[
            pl.BlockSpec(memory_space=pltpu.SMEM),
            cols(CB_QR, nq), cols(CB_KR, nq), cols(CB_VR, nv), cols(CB_GR, nv),
            pl.BlockSpec((1, RET_CHUNK, RET_CHUNK), lambda b, h: (h, 0, 0)),
            pl.BlockSpec((1, RET_CHUNK, RET_QK_DIM), lambda b, h: (h, 0, 0)),
            pl.BlockSpec((1, RET_CHUNK, RET_QK_DIM), lambda b, h: (h, 0, 0)),
            pl.BlockSpec((1, 1, RET_V_DIM), lambda b, h: (h, 0, 0)),
        ],
        out_specs=pl.BlockSpec((SEQ, RET_V_DIM), lambda b, h: (b, h)),
        out_shape=jax.ShapeDtypeStruct((t, RET_V_W), BF16),
        scratch_shapes=[pltpu.VMEM((RET_QK_DIM, RET_V_DIM), F32)],
        compiler_params=_cparams(("arbitrary", "arbitrary")),
        name="retention",
    )(gamma_c, proj, proj, proj, proj, decay, xi, zeta, g_ret)


OUT_TM = 512
N_ROUTE = N_EXPERTS + N_GROUPS
GROUP_LANE = LANES - 1
XG_W = D_MODEL + LANES


def _split_bf16(x):
    hi = x.astype(BF16)
    lo = (x - hi.astype(F32)).astype(BF16)
    return hi, lo


def _out_kernel(ya_ref, yr_ref, ga_ref, gb_ref, bg_ref, x_ref, watt_ref, wret_ref, wout_ref,
                gffn_ref, wr_hi_ref, wr_lo_ref, br_ref, x1_ref, xg_ref):
    ya = jnp.dot(ya_ref[...], watt_ref[...], preferred_element_type=F32)
    yr = jnp.dot(yr_ref[...], wret_ref[...], preferred_element_type=F32)
    merged = []
    for cb in range(D_MODEL // LANES):
        cs = slice(cb * LANES, (cb + 1) * LANES)
        ga = jax.nn.sigmoid(ga_ref[cb].astype(F32) + bg_ref[:, cs])
        gb = jax.nn.sigmoid(gb_ref[cb].astype(F32) + bg_ref[:, D_MODEL + cb * LANES:D_MODEL + (cb + 1) * LANES])
        merged.append((ga * ya[:, cs] + gb * yr[:, cs]).astype(BF16))
    merged = jnp.concatenate(merged, axis=-1)
    x1 = x_ref[...] + jnp.dot(merged, wout_ref[...], preferred_element_type=F32)
    x1_ref[...] = x1
    xn = x1 * lax.rsqrt(jnp.mean(x1 * x1, axis=-1, keepdims=True) + EPS) * gffn_ref[...]
    xg_ref[:, :D_MODEL] = xn

    xh, xl = _split_bf16(xn)
    logits = (jnp.dot(xh, wr_hi_ref[...], preferred_element_type=F32)
              + jnp.dot(xl, wr_hi_ref[...], preferred_element_type=F32)
              + jnp.dot(xh, wr_lo_ref[...], preferred_element_type=F32)) + br_ref[...]
    lane = lax.broadcasted_iota(jnp.int32, logits.shape, 1)
    lane_f = lane.astype(F32)
    is_group = (lane >= N_EXPERTS) & (lane < N_ROUTE)
    gl = jnp.where(is_group, logits, NEG_BIG)
    gmax = jnp.max(gl, axis=-1, keepdims=True)
    gsel = jnp.min(jnp.where(gl == gmax, lane_f, 1e9), axis=-1, keepdims=True) - N_EXPERTS
    p_group = 1.0 / jnp.sum(jnp.where(is_group, jnp.exp(logits - gmax), 0.0), axis=-1, keepdims=True)
    lo_lane = gsel * EXPERTS_PER_GROUP
    in_group = (lane_f >= lo_lane) & (lane_f < lo_lane + EXPERTS_PER_GROUP)
    el = jnp.where(in_group, logits, NEG_BIG)
    v1 = jnp.max(el, axis=-1, keepdims=True)
    i1 = jnp.min(jnp.where(el == v1, lane_f, 1e9), axis=-1, keepdims=True)
    el2 = jnp.where(lane_f == i1, NEG_BIG, el)
    v2 = jnp.max(el2, axis=-1, keepdims=True)
    i2 = jnp.min(jnp.where(el2 == v2, lane_f, 1e9), axis=-1, keepdims=True)
    e21 = jnp.exp(v2 - v1)
    w1 = p_group / (1.0 + e21)
    w2 = w1 * e21
    gate = jnp.where(lane_f == i1, w1, 0.0) + jnp.where(lane_f == i2, w2, 0.0)
    xg_ref[:, D_MODEL:] = jnp.where(lane == GROUP_LANE, gsel, gate)


def _out_stage(y_att, y_ret, proj, b_gate, x2d, w_att, w_ret, w_out, g_ffn, wr_hi, wr_lo, b_route):
    t = x2d.shape[0]
    ncb = D_MODEL // LANES

    def full(shape):
        return pl.BlockSpec(shape, lambda i: tuple(0 for _ in shape))

    return pl.pallas_call(
        _out_kernel,
        grid=(t // OUT_TM,),
        in_specs=[
            pl.BlockSpec((OUT_TM, ATT_W), lambda i: (i, 0)),
            pl.BlockSpec((OUT_TM, RET_V_W), lambda i: (i, 0)),
            pl.BlockSpec((ncb, OUT_TM, LANES), lambda i: (CB_GA // ncb, i, 0)),
            pl.BlockSpec((ncb, OUT_TM, LANES), lambda i: (CB_GB // ncb, i, 0)),
            full((1, 2 * D_MODEL)),
            pl.BlockSpec((OUT_TM, D_MODEL), lambda i: (i, 0)),
            full((ATT_W, D_MODEL)), full((RET_V_W, D_MODEL)), full((D_MODEL, D_MODEL)),
            full((1, D_MODEL)), full((D_MODEL, LANES)), full((D_MODEL, LANES)), full((1, LANES)),
        ],
        out_specs=[
            pl.BlockSpec((OUT_TM, D_MODEL), lambda i: (i, 0)),
            pl.BlockSpec((OUT_TM, XG_W), lambda i: (i, 0)),
        ],
        out_shape=[
            jax.ShapeDtypeStruct((t, D_MODEL), F32),
            jax.ShapeDtypeStruct((t, XG_W), F32),
        ],
        compiler_params=_cparams(("arbitrary",)),
        name="out_stage",
    )(y_att, y_ret, proj, proj, b_gate, x2d, w_att, w_ret, w_out, g_ffn, wr_hi, wr_lo, b_route)


MOE_TILE = 512
MOVE_TM = 1024
MOVE_UNROLL = 8
GROUP_FF = EXPERTS_PER_GROUP * EXPERT_FF


def _moe_plan(xg, t):
    i32 = jnp.int32
    g = xg[:, D_MODEL + GROUP_LANE].astype(i32)
    onehot = (g[:, None] == jnp.arange(N_GROUPS, dtype=i32)[None, :]).astype(i32)
    csum = jnp.cumsum(onehot, axis=0)
    rank = jnp.sum(onehot * csum, axis=1) - 1
    padded = (csum[-1] + MOE_TILE - 1) // MOE_TILE * MOE_TILE
    ends = jnp.cumsum(padded)
    pos = rank + jnp.sum(onehot * (ends - padded)[None, :], axis=1)
    tile_start = jnp.arange(t // MOE_TILE + N_GROUPS, dtype=i32) * MOE_TILE
    tile_group = jnp.minimum(jnp.sum((tile_start[:, None] >= ends[None, :]).astype(i32), axis=1),
                             N_GROUPS - 1)
    return pos.astype(i32), tile_group.astype(i32), (ends[-1:] // MOE_TILE).astype(i32)


def _dispatch_kernel(pos_ref, xg_ref, xs_init_ref, xs_ref, sem):
    del xs_init_ref
    base = pl.program_id(0) * MOVE_TM

    def send(r, carry):
        pltpu.make_async_copy(xg_ref.at[pl.ds(r, 1)], xs_ref.at[pl.ds(pos_ref[base + r], 1)], sem).start()
        return carry

    lax.fori_loop(0, MOVE_TM, send, 0, unroll=MOVE_UNROLL)
    pltpu.make_async_copy(xg_ref, xs_ref.at[pl.ds(0, MOVE_TM)], sem).wait()


def _dispatch(pos, xg, n_rows):
    t = xg.shape[0]
    return pl.pallas_call(
        _dispatch_kernel,
        grid_spec=pltpu.PrefetchScalarGridSpec(
            num_scalar_prefetch=1,
            grid=(t // MOVE_TM,),
            in_specs=[pl.BlockSpec((MOVE_TM, XG_W), lambda i, pos: (i, 0)),
                      pl.BlockSpec(memory_space=pl.ANY)],
            out_specs=pl.BlockSpec(memory_space=pl.ANY),
            scratch_shapes=[pltpu.SemaphoreType.DMA],
        ),
        out_shape=jax.ShapeDtypeStruct((n_rows, XG_W), F32),
        input_output_aliases={2: 0},
        compiler_params=_cparams(("arbitrary",)),
        name="moe_dispatch",
    )(pos, xg, jnp.zeros((n_rows, XG_W), F32))


def _experts_kernel(tg_ref, nused_ref, xs_ref, w1_ref, w3_ref, w2_ref, ys_ref):
    i = pl.program_id(0)

    @pl.when(i < nused_ref[0])
    def _():
        x = xs_ref[:, :D_MODEL].astype(BF16)
        gate = xs_ref[:, D_MODEL:]
        lane = lax.broadcasted_iota(jnp.int32, gate.shape, 1)
        first = tg_ref[i] * EXPERTS_PER_GROUP
        hidden = []
        for e in range(EXPERTS_PER_GROUP):
            a = jnp.dot(x, w1_ref[0, e], preferred_element_type=F32)
            b = jnp.dot(x, w3_ref[0, e], preferred_element_type=F32)
            g = jnp.sum(jnp.where(lane == first + e, gate, 0.0), axis=-1, keepdims=True)
            hidden.append((a * jax.nn.sigmoid(a) * b * g).astype(BF16))
        ys_ref[...] = jnp.dot(jnp.concatenate(hidden, axis=-1), w2_ref[0], preferred_element_type=F32)

    @pl.when(i >= nused_ref[0])
    def _():
        ys_ref[...] = jnp.zeros_like(ys_ref)


def _experts(tile_group, n_used, xs, w1g, w3g, w2g):
    n_rows = xs.shape[0]
    up = pl.BlockSpec((1, EXPERTS_PER_GROUP, D_MODEL, EXPERT_FF), lambda i, tg, nu: (tg[i], 0, 0, 0))
    return pl.pallas_call(
        _experts_kernel,
        grid_spec=pltpu.PrefetchScalarGridSpec(
            num_scalar_prefetch=2,
            grid=(n_rows // MOE_TILE,),
            in_specs=[
                pl.BlockSpec((MOE_TILE, XG_W), lambda i, tg, nu: (i, 0)),
                up, up,
                pl.BlockSpec((1, GROUP_FF, D_MODEL), lambda i, tg, nu: (tg[i], 0, 0)),
            ],
            out_specs=pl.BlockSpec((MOE_TILE, D_MODEL), lambda i, tg, nu: (i, 0)),
        ),
        out_shape=jax.ShapeDtypeStruct((n_rows, D_MODEL), F32),
        compiler_params=_cparams(("arbitrary",)),
        name="moe_experts",
    )(tile_group, n_used, xs, w1g, w3g, w2g)


def _combine_kernel(pos_ref, x1_ref, ys_ref, o_ref, buf, sem):
    base = pl.program_id(0) * MOVE_TM

    def fetch(r, carry):
        pltpu.make_async_copy(ys_ref.at[pl.ds(pos_ref[base + r], 1)], buf.at[pl.ds(r, 1)], sem).start()
        return carry

    lax.fori_loop(0, MOVE_TM, fetch, 0, unroll=MOVE_UNROLL)
    pltpu.make_async_copy(ys_ref.at[pl.ds(0, MOVE_TM)], buf, sem).wait()
    o_ref[...] = x1_ref[...] + buf[...]


def _combine(pos, x1, ys):
    t = x1.shape[0]
    return pl.pallas_call(
        _combine_kernel,
        grid_spec=pltpu.PrefetchScalarGridSpec(
            num_scalar_prefetch=1,
            grid=(t // MOVE_TM,),
            in_specs=[pl.BlockSpec((MOVE_TM, D_MODEL), lambda i, pos: (i, 0)),
                      pl.BlockSpec(memory_space=pl.ANY)],
            out_specs=pl.BlockSpec((MOVE_TM, D_MODEL), lambda i, pos: (i, 0)),
            scratch_shapes=[pltpu.VMEM((MOVE_TM, D_MODEL), F32), pltpu.SemaphoreType.DMA],
        ),
        out_shape=jax.ShapeDtypeStruct((t, D_MODEL), F32),
        compiler_params=_cparams(("arbitrary",)),
        name="moe_combine",
    )(pos, x1, ys)


def _moe(xg, x1, w1, w3, w2):
    t = xg.shape[0]
    w1g = w1.reshape(N_GROUPS, EXPERTS_PER_GROUP, D_MODEL, EXPERT_FF).astype(BF16)
    w3g = w3.reshape(N_GROUPS, EXPERTS_PER_GROUP, D_MODEL, EXPERT_FF).astype(BF16)
    w2g = w2.reshape(N_GROUPS, GROUP_FF, D_MODEL).astype(BF16)
    pos, tile_group, n_used = _moe_plan(xg, t)
    xs = _dispatch(pos, xg, t + N_GROUPS * MOE_TILE)
    ys = _experts(tile_group, n_used, xs, w1g, w3g, w2g)
    return _combine(pos, x1, ys)


def _class_major(table):
    return table.reshape(ROWS_PER_CLASS, ATT_CLASSES, -1).transpose(1, 0, 2).reshape(SEQ, -1)


def _rope_tables_att():
    pos = jnp.arange(SEQ, dtype=F32)
    inv = ROPE_THETA ** (-jnp.arange(0, ATT_HEAD_DIM, 2, dtype=F32) / ATT_HEAD_DIM)
    ang = pos[:, None] * inv[None, :]
    cos, sin = jnp.cos(ang), jnp.sin(ang)
    cos_full = jnp.concatenate([cos, cos, cos, cos], axis=-1)
    sin_full = jnp.concatenate([-sin, -sin, sin, sin], axis=-1)
    return _class_major(cos_full), _class_major(sin_full)


def _pair_lanes(a):
    half = ATT_HEAD_DIM // 2
    lead = a.shape[:-1]
    a = a.reshape(lead + (-1, 2, 2, half))
    return jnp.swapaxes(a, -3, -2).reshape(lead + (-1,))


def _rope_tables_ret():
    pos = jnp.arange(SEQ, dtype=F32)
    inv = 1.0 / (ROPE_THETA ** jnp.linspace(0.0, 1.0, RET_QK_DIM // 2, dtype=F32))
    ang = pos[:, None] * inv[None, :]
    return jnp.cos(ang), jnp.sin(ang)


def _decay_tables():
    c = RET_CHUNK
    log_gamma = jnp.log(1.0 - jnp.exp2(-5.0 - jnp.arange(RET_HEADS, dtype=F32)))
    idx = jnp.arange(c, dtype=F32)
    diff = idx[:, None] - idx[None, :]
    decay = jnp.where(diff >= 0, jnp.exp(log_gamma[:, None, None] * jnp.maximum(diff, 0.0)), 0.0)
    zeta = jnp.exp(log_gamma[:, None] * (c - 1 - idx))
    xi = jnp.exp(log_gamma[:, None] * (idx + 1))
    gamma_c = jnp.exp(log_gamma * c)
    bc = lambda a: jnp.broadcast_to(a[:, :, None], (RET_HEADS, c, RET_QK_DIM)).astype(BF16)
    return gamma_c, decay, bc(xi), bc(zeta)


def kernel(x, g_norm_mix, w_in, b_merge_gate, g_q, g_k, w_branch_att, g_ret_norm, w_branch_ret,
           w_out, g_norm_ffn, w_router_group, b_router_group, w_router_expert, b_router_expert,
           w1, w3, w2):
    batch = x.shape[0]
    t = batch * SEQ
    cos_a, sin_a = _rope_tables_att()
    cos_r, sin_r = _rope_tables_ret()
    gamma_c, decay, xi, zeta = _decay_tables()
    xf = x.reshape(t, D_MODEL)
    for l in range(g_norm_mix.shape[0]):
        g_mix = g_norm_mix[l][None, :]
        w_att_in = jnp.concatenate(
            [_pair_lanes(w_in[l][:, :2 * ATT_W]), w_in[l][:, 2 * ATT_W:ATT_IN_W]], axis=-1)
        proj_att = _inproj_att(xf, g_mix, w_att_in.astype(BF16))
        proj = _inproj_rest(xf, g_mix, w_in[l][:, ATT_IN_W:].astype(BF16), cos_r, sin_r)
        reps = LANES // ATT_HEAD_DIM
        y_att = _attention(proj_att, cos_a, sin_a, _pair_lanes(jnp.tile(g_q[l], reps))[None, :],
                           _pair_lanes(jnp.tile(g_k[l], reps))[None, :], batch)
        y_ret = _retention(proj, gamma_c, decay, xi, zeta, g_ret_norm[l][:, None, :], batch)
        w_route = jnp.concatenate(
            [w_router_expert[l], w_router_group[l],
             jnp.zeros((D_MODEL, LANES - N_ROUTE), F32)], axis=-1)
        wr_hi, wr_lo = _split_bf16(w_route)
        b_route = jnp.concatenate(
            [b_router_expert[l], b_router_group[l], jnp.zeros((LANES - N_ROUTE,), F32)])[None, :]
        x1, xg = _out_stage(
            y_att, y_ret, proj, b_merge_gate[l][None, :], xf,
            w_branch_att[l].astype(BF16), w_branch_ret[l].astype(BF16), w_out[l].astype(BF16),
            g_norm_ffn[l][None, :], wr_hi, wr_lo, b_route)
        xf = _moe(xg, x1, w1[l], w3[l], w2[l])
    return xf.reshape(batch, SEQ, D_MODEL)
```

```python
import functools

import numpy as np

import jax
import jax.numpy as jnp
from jax import lax
from jax.experimental import pallas as pl
from jax.experimental.pallas import tpu as pltpu

F32 = jnp.float32
BF16 = jnp.bfloat16

D_MODEL = 1024
SEQ = 2048
ATT_HEADS = 16
ATT_HEAD_DIM = 64
ATT_W = ATT_HEADS * ATT_HEAD_DIM
ROPE_THETA = 10000.0
RET_HEADS = 4
RET_QK_DIM = 256
RET_V_DIM = 512
RET_QK_W = RET_HEADS * RET_QK_DIM
RET_V_W = RET_HEADS * RET_V_DIM
N_GROUPS = 4
EXPERTS_PER_GROUP = 8
N_EXPERTS = N_GROUPS * EXPERTS_PER_GROUP
EXPERT_FF = 256
EPS = 1e-6
ATT_IN_W = 3 * ATT_W
REST_IN_W = 2 * RET_QK_W + 2 * RET_V_W + 2 * D_MODEL

LANES = 128
SUBLANES = 8
CB_QA, CB_KA, CB_VA = 0, 8, 16
CB_QR, CB_KR, CB_VR, CB_GR, CB_GA, CB_GB = 0, 8, 16, 32, 48, 56

ATT_BLOCK = 128
ATT_CLASSES = 16
ROWS_PER_CLASS = SEQ // ATT_CLASSES
RET_CHUNK = 256
RET_UNROLL = 2
NEG_BIG = -1e30
VMEM_LIMIT = 48 * 1024 * 1024


def _cparams(sem):
    return pltpu.CompilerParams(dimension_semantics=sem, vmem_limit_bytes=VMEM_LIMIT)


IN_TM = 1024
IN_TN = 1024
IN_NORM_ROWS = 256
REST_STEP_QK = CB_QR * LANES // (2 * IN_TN)
REST_STEP_GATE = CB_GR * LANES // (2 * IN_TN)
assert CB_KR * LANES == CB_QR * LANES + IN_TN and (CB_GA - CB_GR) * LANES == 2 * IN_TN


def _rmsnorm_rows(x, g_ref):
    ms = jnp.mean(x * x, axis=-1, keepdims=True)
    return x * lax.rsqrt(ms + EPS) * g_ref[...]


def _project(xn_sc, w_ref, o_ref, epilogue, first_block=0):
    xn = xn_sc[...]
    for c2 in range(IN_TN // 256):
        w = w_ref[:, c2 * 256:(c2 + 1) * 256].astype(BF16)
        acc = jnp.dot(xn, w, preferred_element_type=F32)
        lo, hi = epilogue(acc[:, :LANES], acc[:, LANES:])
        o_ref[first_block + 2 * c2] = lo.astype(BF16)
        o_ref[first_block + 2 * c2 + 1] = hi.astype(BF16)


def _inproj_att_kernel(x_ref, g_ref, w_ref, o_ref, xn_sc, xs):
    @pl.when(pl.program_id(1) == 0)
    def _():
        def norm_rows(ci, carry):
            rows = pl.ds(pl.multiple_of(ci * IN_NORM_ROWS, IN_NORM_ROWS), IN_NORM_ROWS)
            xn = _rmsnorm_rows(x_ref[rows, :], g_ref)
            for c in range(D_MODEL // LANES):
                xs[c, rows, :] = xn[:, c * LANES:(c + 1) * LANES]
            return carry

        lax.fori_loop(0, SEQ // IN_NORM_ROWS, norm_rows, 0)

        def gather_class(j, carry):
            dst = pl.ds(pl.multiple_of(j * ROWS_PER_CLASS, ROWS_PER_CLASS), ROWS_PER_CLASS)
            for c in range(D_MODEL // LANES):
                xn_sc[dst, c * LANES:(c + 1) * LANES] = xs[
                    c, pl.ds(j, ROWS_PER_CLASS, stride=ATT_CLASSES), :].astype(BF16)
            return carry

        lax.fori_loop(0, ATT_CLASSES, gather_class, 0)

    _project(xn_sc, w_ref, o_ref, lambda lo, hi: (lo, hi))


def _inproj_rest_kernel(x_ref, g_ref, wa_ref, wb_ref, cos_ref, sin_ref, o_ref, xn_sc):
    j = pl.program_id(1)
    second = IN_TN // LANES

    @pl.when(j == 0)
    def _():
        xn_sc[...] = _rmsnorm_rows(x_ref[...], g_ref).astype(BF16)

    def rotate(scale):
        def epilogue(x1, x2):
            cos, sin = cos_ref[...], sin_ref[...]
            return (x1 * cos - x2 * sin) * scale, (x2 * cos + x1 * sin) * scale
        return epilogue

    def swish(lo, hi):
        return lo * jax.nn.sigmoid(lo), hi * jax.nn.sigmoid(hi)

    def both(first, last):
        _project(xn_sc, wa_ref, o_ref, first)
        _project(xn_sc, wb_ref, o_ref, last, second)

    @pl.when(j == REST_STEP_QK)
    def _():
        both(rotate(1.0), rotate(RET_QK_DIM ** -0.5))

    @pl.when(j == REST_STEP_GATE)
    def _():
        both(swish, swish)

    @pl.when((j != REST_STEP_QK) & (j != REST_STEP_GATE))
    def _():
        both(lambda lo, hi: (lo, hi), lambda lo, hi: (lo, hi))


def _inproj_att(x2d, g, w_bf16):
    t = x2d.shape[0]
    width = w_bf16.shape[1]
    return pl.pallas_call(
        _inproj_att_kernel,
        grid=(t // SEQ, width // IN_TN),
        in_specs=[
            pl.BlockSpec((SEQ, D_MODEL), lambda i, j: (i, 0)),
            pl.BlockSpec((1, D_MODEL), lambda i, j: (0, 0)),
            pl.BlockSpec((D_MODEL, IN_TN), lambda i, j: (0, j)),
        ],
        out_specs=pl.BlockSpec((IN_TN // LANES, SEQ, LANES), lambda i, j: (j, i, 0)),
        out_shape=jax.ShapeDtypeStruct((width // LANES, t, LANES), BF16),
        scratch_shapes=[pltpu.VMEM((SEQ, D_MODEL), BF16),
                        pltpu.VMEM((D_MODEL // LANES, SEQ, LANES), F32)],
        compiler_params=_cparams(("arbitrary", "arbitrary")),
        name="inproj_att",
    )(x2d, g, w_bf16)


def _inproj_rest(x2d, g, w_in_f32, cos_r, sin_r):
    t = x2d.shape[0]
    tiles_per_seq = SEQ // IN_TM
    first_tile = ATT_IN_W // IN_TN
    table = pl.BlockSpec((IN_TM, LANES), lambda i, j: (i % tiles_per_seq, 0))
    return pl.pallas_call(
        _inproj_rest_kernel,
        grid=(t // IN_TM, REST_IN_W // (2 * IN_TN)),
        in_specs=[
            pl.BlockSpec((IN_TM, D_MODEL), lambda i, j: (i, 0)),
            pl.BlockSpec((1, D_MODEL), lambda i, j: (0, 0)),
            pl.BlockSpec((D_MODEL, IN_TN), lambda i, j: (0, first_tile + 2 * j)),
            pl.BlockSpec((D_MODEL, IN_TN), lambda i, j: (0, first_tile + 2 * j + 1)),
            table, table,
        ],
        out_specs=pl.BlockSpec((2 * IN_TN // LANES, IN_TM, LANES), lambda i, j: (j, i, 0)),
        out_shape=jax.ShapeDtypeStruct((REST_IN_W // LANES, t, LANES), BF16),
        scratch_shapes=[pltpu.VMEM((IN_TM, D_MODEL), BF16)],
        compiler_params=_cparams(("arbitrary", "arbitrary")),
        name="inproj_rest",
    )(x2d, g, w_in_f32, w_in_f32, cos_r, sin_r)


PREP_ROWS = 256
PREP_UNROLL = 4
SEG4 = ATT_BLOCK // 4
SEG1 = ATT_BLOCK // ATT_CLASSES


def _att_bias_tables():
    def tile(qpos, kpos):
        d = qpos[:, None] - kpos[None, :]
        one = np.where((d >= 0) & (d <= ATT_BLOCK), 0.0, NEG_BIG).astype(np.float32)
        return np.concatenate([one, one], axis=1)

    u = np.arange(ATT_BLOCK)
    q4 = 4 * (u % SEG4) + u // SEG4
    q1 = ATT_CLASSES * (u % SEG1) + u // SEG1
    w = np.arange(2 * ATT_BLOCK)
    k1 = ATT_CLASSES * (w % SEG1) + w // (2 * SEG1) + ATT_BLOCK * ((w // SEG1) % 2 - 1)
    return (tile(q1, k1), tile(q1, q1), tile(q4, np.concatenate([q4 - ATT_BLOCK, q4])), tile(q4, q4),
            tile(u, u))


def _att_kernel(q_ref, k_ref, v_ref, cos_ref, sin_ref, gq_ref, gk_ref,
                b1_ref, b1f_ref, b4_ref, b4f_ref, b16_ref, o_ref,
                qj, kj, vj, qb, kb, vb, acc_sc, m_sc, l_sc):
    lane = lax.broadcasted_iota(jnp.int32, (1, LANES), 1)
    half = ATT_HEAD_DIM // 2
    qk_head0 = (lane // half) % 2 == 0
    v_head0 = lane < ATT_HEAD_DIM
    seg = ((lax.broadcasted_iota(jnp.int32, (LANES, LANES), 0) // half) % 2
           == (lax.broadcasted_iota(jnp.int32, (LANES, LANES), 1) // half) % 2).astype(BF16)
    scale = ATT_HEAD_DIM ** -0.5

    def prep(ci, carry):
        rows = pl.ds(pl.multiple_of(ci * PREP_ROWS, PREP_ROWS), PREP_ROWS)
        cos = cos_ref[rows, :]
        sin = sin_ref[rows, :]

        def norm_rope(src, g_ref):
            x = src[0, rows, :].astype(F32)
            x2 = x * x
            hi = x2.astype(BF16)
            lo = (x2 - hi.astype(F32)).astype(BF16)
            ss = (jnp.dot(hi, seg, preferred_element_type=F32)
                  + jnp.dot(lo, seg, preferred_element_type=F32))
            xn = x * lax.rsqrt(ss * (1.0 / ATT_HEAD_DIM) + EPS) * g_ref[...]
            return xn * cos + pltpu.roll(xn, ATT_HEAD_DIM, 1) * sin

        xq = norm_rope(q_ref, gq_ref) * scale
        qj[rows, :] = xq
        qb[rows, :] = xq.astype(BF16)
        xk = norm_rope(k_ref, gk_ref)
        kj[rows, :] = xk
        kb[0, rows, :] = jnp.where(qk_head0, xk, 0.0).astype(BF16)
        kb[1, rows, :] = jnp.where(qk_head0, 0.0, xk).astype(BF16)
        xv = v_ref[0, rows, :].astype(F32)
        vj[rows, :] = xv
        vb[0, rows, :] = jnp.where(v_head0, xv, 0.0).astype(BF16)
        vb[1, rows, :] = jnp.where(v_head0, 0.0, xv).astype(BF16)
        return carry

    lax.fori_loop(0, SEQ // PREP_ROWS, prep, 0, unroll=PREP_UNROLL)

    nt_dims = (((1,), (1,)), ((), ()))

    def attend(q, k0, k1, v0, v1, bias):
        w = k0.shape[0]
        s = lax.dot_general(q, jnp.concatenate([k0, k1], axis=0), nt_dims,
                            preferred_element_type=F32) + bias
        m0 = jnp.max(s[:, :w], axis=-1, keepdims=True)
        m1 = jnp.max(s[:, w:], axis=-1, keepdims=True)
        e = jnp.concatenate([jnp.exp(s[:, :w] - m0), jnp.exp(s[:, w:] - m1)], axis=1).astype(BF16)
        ones0 = jnp.broadcast_to(jnp.where(v_head0, 1.0, 0.0).astype(BF16), (w, LANES))
        ones1 = jnp.broadcast_to(jnp.where(v_head0, 0.0, 1.0).astype(BF16), (w, LANES))
        v2 = jnp.concatenate([jnp.concatenate([v0, ones0], axis=1),
                              jnp.concatenate([v1, ones1], axis=1)], axis=0)
        r = jnp.dot(e, v2, preferred_element_type=F32)
        return r[:, :LANES], jnp.where(v_head0, m0, m1), r[:, LANES:]

    def gather(ref, pieces, lead=()):
        return jnp.concatenate([ref[lead + (rows, slice(None))] for rows in pieces], axis=0)

    def masked_pair(x, head0_mask):
        return jnp.where(head0_mask, x, 0.0).astype(BF16), jnp.where(head0_mask, 0.0, x).astype(BF16)

    def store(p, pieces, n, a, m, l):
        for idx, rows in enumerate(pieces):
            acc_sc[p, rows, :] = a[idx * n:(idx + 1) * n]
            m_sc[p, rows, :] = m[idx * n:(idx + 1) * n]
            l_sc[p, rows, :] = l[idx * n:(idx + 1) * n]

    for j in range(ATT_CLASSES):
        rows = [pl.ds(j * ATT_BLOCK, ATT_BLOCK)]
        a, m, l = attend(qb[rows[0], :], kb[0, rows[0], :], kb[1, rows[0], :],
                         vb[0, rows[0], :], vb[1, rows[0], :], b16_ref[...])
        store(2, rows, ATT_BLOCK, a, m, l)

    for c in range(4):
        def segs(n):
            return [pl.ds((4 * a + c) * ATT_BLOCK + SEG4 * n, SEG4) for a in range(4)]

        for n in range(SEQ // 4 // ATT_BLOCK):
            cur = segs(n)
            keys = cur if n == 0 else segs(n - 1) + cur
            bias = b4f_ref[...] if n == 0 else b4_ref[...]
            a, m, l = attend(gather(qb, cur), gather(kb, keys, (0,)), gather(kb, keys, (1,)),
                             gather(vb, keys, (0,)), gather(vb, keys, (1,)), bias)
            store(1, cur, SEG4, a, m, l)

    for n in range(SEQ // ATT_BLOCK):
        cur = [pl.ds(j * ATT_BLOCK + SEG1 * n, SEG1) for j in range(ATT_CLASSES)]
        if n == 0:
            keys, bias = cur, b1f_ref[...]
        else:
            keys = [pl.ds(j * ATT_BLOCK + SEG1 * n - SEG1, 2 * SEG1) for j in range(ATT_CLASSES)]
            bias = b1_ref[...]
        k0, k1 = masked_pair(gather(kj, keys), qk_head0)
        v0, v1 = masked_pair(gather(vj, keys), v_head0)
        a, m, l = attend(gather(qj, cur).astype(BF16), k0, k1, v0, v1, bias)
        store(0, cur, SEG1, a, m, l)

    def merge(j, carry):
        rows = pl.ds(pl.multiple_of(j * ROWS_PER_CLASS, ROWS_PER_CLASS), ROWS_PER_CLASS)
        ms = [m_sc[p, rows, :] for p in range(3)]
        m_all = jnp.maximum(jnp.maximum(ms[0], ms[1]), ms[2])
        num = jnp.zeros((ROWS_PER_CLASS, LANES), F32)
        den = jnp.zeros((ROWS_PER_CLASS, LANES), F32)
        for p in range(3):
            w = jnp.exp(ms[p] - m_all)
            num = num + w * acc_sc[p, rows, :]
            den = den + w * l_sc[p, rows, :]
        qj[pl.ds(j, ROWS_PER_CLASS, stride=ATT_CLASSES), :] = num / den
        return carry

    lax.fori_loop(0, ATT_CLASSES, merge, 0)

    def emit(ci, carry):
        rows = pl.ds(pl.multiple_of(ci * PREP_ROWS, PREP_ROWS), PREP_ROWS)
        o_ref[rows, :] = qj[rows, :].astype(BF16)
        return carry

    lax.fori_loop(0, SEQ // PREP_ROWS, emit, 0)


def _attention(proj_att, cos_a, sin_a, gq, gk, batch):
    t = batch * SEQ
    hp = ATT_W // LANES

    def col(cb):
        return pl.BlockSpec((1, SEQ, LANES), lambda b, h: (cb + h, b, 0))

    def const(shape):
        return pl.BlockSpec(shape, lambda b, h: tuple(0 for _ in shape))

    biases = [jnp.asarray(b) for b in _att_bias_tables()]
    row_f32 = pltpu.VMEM((SEQ, LANES), F32)
    row_bf16 = pltpu.VMEM((SEQ, LANES), BF16)
    stat = pltpu.VMEM((3, SEQ, LANES), F32)
    return pl.pallas_call(
        _att_kernel,
        grid=(batch, hp),
        in_specs=[col(CB_QA), col(CB_KA), col(CB_VA),
                  const((SEQ, LANES)), const((SEQ, LANES)), const((1, LANES)), const((1, LANES))]
                 + [const(b.shape) for b in biases],
        out_specs=pl.BlockSpec((SEQ, LANES), lambda b, h: (b, h)),
        out_shape=jax.ShapeDtypeStruct((t, ATT_W), BF16),
        scratch_shapes=[row_f32] * 3 + [row_bf16] + [pltpu.VMEM((2, SEQ, LANES), BF16)] * 2 + [stat] * 3,
        compiler_params=_cparams(("arbitrary", "arbitrary")),
        name="dilated_attention",
    )(proj_att, proj_att, proj_att, cos_a, sin_a, gq, gk, *biases)


def _ret_kernel(gam_ref, q_ref, k_ref, v_ref, g_ref, decay_ref, xi_ref, zeta_ref,
                gn_ref, o_ref, state_sc):
    h = pl.program_id(1)
    gamma_c = gam_ref[h]
    state_sc[...] = jnp.zeros_like(state_sc)
    c = RET_CHUNK
    nt_dims = (((1,), (1,)), ((), ()))
    tn_dims = (((0,), (0,)), ((), ()))

    def cols(ref, rs, n):
        return jnp.concatenate([ref[i, rs, :] for i in range(n)], axis=-1)

    def chunk(n, carry):
        rs = pl.ds(pl.multiple_of(n * c, c), c)
        q = cols(q_ref, rs, RET_QK_DIM // LANES)
        k = cols(k_ref, rs, RET_QK_DIM // LANES)
        v = cols(v_ref, rs, RET_V_DIM // LANES)
        inner = lax.dot_general(q, k, nt_dims, preferred_element_type=F32) * decay_ref[0]
        y = jnp.dot(inner.astype(BF16), v, preferred_element_type=F32)
        state = state_sc[...]
        y = y + jnp.dot(q * xi_ref[0], state.astype(BF16), preferred_element_type=F32)
        state_sc[...] = state * gamma_c + lax.dot_general(k * zeta_ref[0], v, tn_dims,
                                                          preferred_element_type=F32)
        yn = y * lax.rsqrt(jnp.mean(y * y, axis=-1, keepdims=True) + EPS) * gn_ref[0]
        o_ref[rs, :] = (yn * cols(g_ref, rs, RET_V_DIM // LANES).astype(F32)).astype(BF16)
        return carry

    lax.fori_loop(0, SEQ // c, chunk, 0, unroll=RET_UNROLL)


def _retention(proj, gamma_c, decay, xi, zeta, g_ret, batch):
    t = batch * SEQ
    nq = RET_QK_DIM // LANES
    nv = RET_V_DIM // LANES

    def cols(cb, n):
        return pl.BlockSpec((n, SEQ, LANES), lambda b, h: (cb // n + h, b, 0))

    return pl.pallas_call(
        _ret_kernel,
        grid=(batch, RET_HEADS),
        in_specs=[
            pl.BlockSpec(memory_space=pltpu.SMEM),
            cols(CB_QR, nq), cols(CB_KR, nq), cols(CB_VR, nv), cols(CB_GR, nv),
            pl.BlockSpec((1, RET_CHUNK, RET_CHUNK), lambda b, h: (h, 0, 0)),
            pl.BlockSpec((1, RET_CHUNK, RET_QK_DIM), lambda b, h: (h, 0, 0)),
            pl.BlockSpec((1, RET_CHUNK, RET_QK_DIM), lambda b, h: (h, 0, 0)),
            pl.BlockSpec((1, 1, RET_V_DIM), lambda b, h: (h, 0, 0)),
        ],
        out_specs=pl.BlockSpec((SEQ, RET_V_DIM), lambda b, h: (b, h)),
        out_shape=jax.ShapeDtypeStruct((t, RET_V_W), BF16),
        scratch_shapes=[pltpu.VMEM((RET_QK_DIM, RET_V_DIM), F32)],
        compiler_params=_cparams(("arbitrary", "arbitrary")),
        name="retention",
    )(gamma_c, proj, proj, proj, proj, decay, xi, zeta, g_ret)


OUT_TM = 512
N_ROUTE = N_EXPERTS + N_GROUPS
GROUP_LANE = LANES - 1
XG_W = D_MODEL + LANES


def _split_bf16(x):
    hi = x.astype(BF16)
    lo = (x - hi.astype(F32)).astype(BF16)
    return hi, lo


def _out_kernel(ya_ref, yr_ref, ga_ref, gb_ref, bg_ref, x_ref, watt_ref, wret_ref, wout_ref,
                gffn_ref, wr_hi_ref, wr_lo_ref, br_ref, x1_ref, xg_ref):
    ya = jnp.dot(ya_ref[...], watt_ref[...], preferred_element_type=F32)
    yr = jnp.dot(yr_ref[...], wret_ref[...], preferred_element_type=F32)
    merged = []
    for cb in range(D_MODEL // LANES):
        cs = slice(cb * LANES, (cb + 1) * LANES)
        ga = jax.nn.sigmoid(ga_ref[cb].astype(F32) + bg_ref[:, cs])
        gb = jax.nn.sigmoid(gb_ref[cb].astype(F32) + bg_ref[:, D_MODEL + cb * LANES:D_MODEL + (cb + 1) * LANES])
        merged.append((ga * ya[:, cs] + gb * yr[:, cs]).astype(BF16))
    merged = jnp.concatenate(merged, axis=-1)
    x1 = x_ref[...] + jnp.dot(merged, wout_ref[...], preferred_element_type=F32)
    x1_ref[...] = x1
    xn = x1 * lax.rsqrt(jnp.mean(x1 * x1, axis=-1, keepdims=True) + EPS) * gffn_ref[...]
    xg_ref[:, :D_MODEL] = xn

    xh, xl = _split_bf16(xn)
    logits = (jnp.dot(xh, wr_hi_ref[...], preferred_element_type=F32)
              + jnp.dot(xl, wr_hi_ref[...], preferred_element_type=F32)
              + jnp.dot(xh, wr_lo_ref[...], preferred_element_type=F32)) + br_ref[...]
    lane = lax.broadcasted_iota(jnp.int32, logits.shape, 1)
    lane_f = lane.astype(F32)
    is_group = (lane >= N_EXPERTS) & (lane < N_ROUTE)
    gl = jnp.where(is_group, logits, NEG_BIG)
    gmax = jnp.max(gl, axis=-1, keepdims=True)
    gsel = jnp.min(jnp.where(gl == gmax, lane_f, 1e9), axis=-1, keepdims=True) - N_EXPERTS
    p_group = 1.0 / jnp.sum(jnp.where(is_group, jnp.exp(logits - gmax), 0.0), axis=-1, keepdims=True)
    lo_lane = gsel * EXPERTS_PER_GROUP
    in_group = (lane_f >= lo_lane) & (lane_f < lo_lane + EXPERTS_PER_GROUP)
    el = jnp.where(in_group, logits, NEG_BIG)
    v1 = jnp.max(el, axis=-1, keepdims=True)
    i1 = jnp.min(jnp.where(el == v1, lane_f, 1e9), axis=-1, keepdims=True)
    el2 = jnp.where(lane_f == i1, NEG_BIG, el)
    v2 = jnp.max(el2, axis=-1, keepdims=True)
    i2 = jnp.min(jnp.where(el2 == v2, lane_f, 1e9), axis=-1, keepdims=True)
    e21 = jnp.exp(v2 - v1)
    w1 = p_group / (1.0 + e21)
    w2 = w1 * e21
    gate = jnp.where(lane_f == i1, w1, 0.0) + jnp.where(lane_f == i2, w2, 0.0)
    xg_ref[:, D_MODEL:] = jnp.where(lane == GROUP_LANE, gsel, gate)


def _out_stage(y_att, y_ret, proj, b_gate, x2d, w_att, w_ret, w_out, g_ffn, wr_hi, wr_lo, b_route):
    t = x2d.shape[0]
    ncb = D_MODEL // LANES

    def full(shape):
        return pl.BlockSpec(shape, lambda i: tuple(0 for _ in shape))

    return pl.pallas_call(
        _out_kernel,
        grid=(t // OUT_TM,),
        in_specs=[
            pl.BlockSpec((OUT_TM, ATT_W), lambda i: (i, 0)),
            pl.BlockSpec((OUT_TM, RET_V_W), lambda i: (i, 0)),
            pl.BlockSpec((ncb, OUT_TM, LANES), lambda i: (CB_GA // ncb, i, 0)),
            pl.BlockSpec((ncb, OUT_TM, LANES), lambda i: (CB_GB // ncb, i, 0)),
            full((1, 2 * D_MODEL)),
            pl.BlockSpec((OUT_TM, D_MODEL), lambda i: (i, 0)),
            full((ATT_W, D_MODEL)), full((RET_V_W, D_MODEL)), full((D_MODEL, D_MODEL)),
            full((1, D_MODEL)), full((D_MODEL, LANES)), full((D_MODEL, LANES)), full((1, LANES)),
        ],
        out_specs=[
            pl.BlockSpec((OUT_TM, D_MODEL), lambda i: (i, 0)),
            pl.BlockSpec((OUT_TM, XG_W), lambda i: (i, 0)),
        ],
        out_shape=[
            jax.ShapeDtypeStruct((t, D_MODEL), F32),
            jax.ShapeDtypeStruct((t, XG_W), F32),
        ],
        compiler_params=_cparams(("arbitrary",)),
        name="out_stage",
    )(y_att, y_ret, proj, proj, b_gate, x2d, w_att, w_ret, w_out, g_ffn, wr_hi, wr_lo, b_route)


MOE_TILE = 512
MOVE_TM = 1024
GROUP_FF = EXPERTS_PER_GROUP * EXPERT_FF


def _moe_plan(xg, t):
    i32 = jnp.int32
    g = xg[:, D_MODEL + GROUP_LANE].astype(i32)
    onehot = (g[:, None] == jnp.arange(N_GROUPS, dtype=i32)[None, :]).astype(i32)
    csum = jnp.cumsum(onehot, axis=0)
    rank = jnp.sum(onehot * csum, axis=1) - 1
    padded = (csum[-1] + MOE_TILE - 1) // MOE_TILE * MOE_TILE
    ends = jnp.cumsum(padded)
    pos = rank + jnp.sum(onehot * (ends - padded)[None, :], axis=1)
    tile_start = jnp.arange(t // MOE_TILE + N_GROUPS, dtype=i32) * MOE_TILE
    tile_group = jnp.minimum(jnp.sum((tile_start[:, None] >= ends[None, :]).astype(i32), axis=1),
                             N_GROUPS - 1)
    return pos.astype(i32), tile_group.astype(i32), (ends[-1:] // MOE_TILE).astype(i32)


def _dispatch_kernel(hi_ref, lo_ref, xg_ref, xs_init_ref, xs_ref, sem):
    del xs_init_ref
    base = pl.program_id(0) * MOVE_TM

    def send(i, carry):
        for u in range(SUBLANES):
            r = base + i * SUBLANES + u
            pltpu.make_async_copy(xg_ref.at[i, pl.ds(u, 1)],
                                  xs_ref.at[hi_ref[r], pl.ds(lo_ref[r], 1)], sem).start()
        return carry

    lax.fori_loop(0, MOVE_TM // SUBLANES, send, 0)
    pltpu.make_async_copy(xg_ref, xs_ref.at[pl.ds(0, MOVE_TM // SUBLANES)], sem).wait()


def _dispatch(pos_hi, pos_lo, xg, n_rows):
    t = xg.shape[0]
    xs = pl.pallas_call(
        _dispatch_kernel,
        grid_spec=pltpu.PrefetchScalarGridSpec(
            num_scalar_prefetch=2,
            grid=(t // MOVE_TM,),
            in_specs=[pl.BlockSpec((MOVE_TM // SUBLANES, SUBLANES, XG_W), lambda i, hi, lo: (i, 0, 0)),
                      pl.BlockSpec(memory_space=pl.ANY)],
            out_specs=pl.BlockSpec(memory_space=pl.ANY),
            scratch_shapes=[pltpu.SemaphoreType.DMA],
        ),
        out_shape=jax.ShapeDtypeStruct((n_rows // SUBLANES, SUBLANES, XG_W), F32),
        input_output_aliases={3: 0},
        compiler_params=_cparams(("arbitrary",)),
        name="moe_dispatch",
    )(pos_hi, pos_lo, xg.reshape(t // SUBLANES, SUBLANES, XG_W),
      jnp.zeros((n_rows // SUBLANES, SUBLANES, XG_W), F32))
    return xs.reshape(n_rows, XG_W)


def _experts_kernel(tg_ref, nused_ref, xs_ref, w1_ref, w3_ref, w2_ref, ys_ref):
    i = pl.program_id(0)

    @pl.when(i < nused_ref[0])
    def _():
        x = xs_ref[:, :D_MODEL].astype(BF16)
        gate = xs_ref[:, D_MODEL:]
        lane = lax.broadcasted_iota(jnp.int32, gate.shape, 1)
        first = tg_ref[i] * EXPERTS_PER_GROUP
        hidden = []
        for e in range(EXPERTS_PER_GROUP):
            a = jnp.dot(x, w1_ref[0, e], preferred_element_type=F32)
            b = jnp.dot(x, w3_ref[0, e], preferred_element_type=F32)
            g = jnp.sum(jnp.where(lane == first + e, gate, 0.0), axis=-1, keepdims=True)
            hidden.append((a * jax.nn.sigmoid(a) * b * g).astype(BF16))
        ys_ref[...] = jnp.dot(jnp.concatenate(hidden, axis=-1), w2_ref[0], preferred_element_type=F32)

    @pl.when(i >= nused_ref[0])
    def _():
        ys_ref[...] = jnp.zeros_like(ys_ref)


def _experts(tile_group, n_used, xs, w1g, w3g, w2g):
    n_rows = xs.shape[0]
    up = pl.BlockSpec((1, EXPERTS_PER_GROUP, D_MODEL, EXPERT_FF), lambda i, tg, nu: (tg[i], 0, 0, 0))
    return pl.pallas_call(
        _experts_kernel,
        grid_spec=pltpu.PrefetchScalarGridSpec(
            num_scalar_prefetch=2,
            grid=(n_rows // MOE_TILE,),
            in_specs=[
                pl.BlockSpec((MOE_TILE, XG_W), lambda i, tg, nu: (i, 0)),
                up, up,
                pl.BlockSpec((1, GROUP_FF, D_MODEL), lambda i, tg, nu: (tg[i], 0, 0)),
            ],
            out_specs=pl.BlockSpec((MOE_TILE, D_MODEL), lambda i, tg, nu: (i, 0)),
        ),
        out_shape=jax.ShapeDtypeStruct((n_rows, D_MODEL), F32),
        compiler_params=_cparams(("arbitrary",)),
        name="moe_experts",
    )(tile_group, n_used, xs, w1g, w3g, w2g)


def _combine_kernel(hi_ref, lo_ref, x1_ref, ys_ref, o_ref, buf, sem):
    base = pl.program_id(0) * MOVE_TM

    def fetch(i, carry):
        for u in range(SUBLANES):
            r = base + i * SUBLANES + u
            pltpu.make_async_copy(ys_ref.at[hi_ref[r], pl.ds(lo_ref[r], 1)],
                                  buf.at[i, pl.ds(u, 1)], sem).start()
        return carry

    lax.fori_loop(0, MOVE_TM // SUBLANES, fetch, 0)
    pltpu.make_async_copy(ys_ref.at[pl.ds(0, MOVE_TM // SUBLANES)], buf, sem).wait()
    o_ref[...] = x1_ref[...] + buf[...]


def _combine(pos_hi, pos_lo, x1, ys):
    t = x1.shape[0]
    tile = (MOVE_TM // SUBLANES, SUBLANES, D_MODEL)
    out = pl.pallas_call(
        _combine_kernel,
        grid_spec=pltpu.PrefetchScalarGridSpec(
            num_scalar_prefetch=2,
            grid=(t // MOVE_TM,),
            in_specs=[pl.BlockSpec(tile, lambda i, hi, lo: (i, 0, 0)),
                      pl.BlockSpec(memory_space=pl.ANY)],
            out_specs=pl.BlockSpec(tile, lambda i, hi, lo: (i, 0, 0)),
            scratch_shapes=[pltpu.VMEM(tile, F32), pltpu.SemaphoreType.DMA],
        ),
        out_shape=jax.ShapeDtypeStruct((t // SUBLANES, SUBLANES, D_MODEL), F32),
        compiler_params=_cparams(("arbitrary",)),
        name="moe_combine",
    )(pos_hi, pos_lo, x1.reshape(t // SUBLANES, SUBLANES, D_MODEL),
      ys.reshape(ys.shape[0] // SUBLANES, SUBLANES, D_MODEL))
    return out.reshape(t, D_MODEL)


def _moe(xg, x1, w1, w3, w2):
    t = xg.shape[0]
    w1g = w1.reshape(N_GROUPS, EXPERTS_PER_GROUP, D_MODEL, EXPERT_FF).astype(BF16)
    w3g = w3.reshape(N_GROUPS, EXPERTS_PER_GROUP, D_MODEL, EXPERT_FF).astype(BF16)
    w2g = w2.reshape(N_GROUPS, GROUP_FF, D_MODEL).astype(BF16)
    pos, tile_group, n_used = _moe_plan(xg, t)
    pos_hi, pos_lo = pos // SUBLANES, pos % SUBLANES
    xs = _dispatch(pos_hi, pos_lo, xg, t + N_GROUPS * MOE_TILE)
    ys = _experts(tile_group, n_used, xs, w1g, w3g, w2g)
    return _combine(pos_hi, pos_lo, x1, ys)


def _class_major(table):
    return table.reshape(ROWS_PER_CLASS, ATT_CLASSES, -1).transpose(1, 0, 2).reshape(SEQ, -1)


def _rope_tables_att():
    pos = jnp.arange(SEQ, dtype=F32)
    inv = ROPE_THETA ** (-jnp.arange(0, ATT_HEAD_DIM, 2, dtype=F32) / ATT_HEAD_DIM)
    ang = pos[:, None] * inv[None, :]
    cos, sin = jnp.cos(ang), jnp.sin(ang)
    cos_full = jnp.concatenate([cos, cos, cos, cos], axis=-1)
    sin_full = jnp.concatenate([-sin, -sin, sin, sin], axis=-1)
    return _class_major(cos_full), _class_major(sin_full)


def _pair_lanes(a):
    half = ATT_HEAD_DIM // 2
    lead = a.shape[:-1]
    a = a.reshape(lead + (-1, 2, 2, half))
    return jnp.swapaxes(a, -3, -2).reshape(lead + (-1,))


def _rope_tables_ret():
    pos = jnp.arange(SEQ, dtype=F32)
    inv = 1.0 / (ROPE_THETA ** jnp.linspace(0.0, 1.0, RET_QK_DIM // 2, dtype=F32))
    ang = pos[:, None] * inv[None, :]
    return jnp.cos(ang), jnp.sin(ang)


def _decay_tables():
    c = RET_CHUNK
    log_gamma = jnp.log(1.0 - jnp.exp2(-5.0 - jnp.arange(RET_HEADS, dtype=F32)))
    idx = jnp.arange(c, dtype=F32)
    diff = idx[:, None] - idx[None, :]
    decay = jnp.where(diff >= 0, jnp.exp(log_gamma[:, None, None] * jnp.maximum(diff, 0.0)), 0.0)
    zeta = jnp.exp(log_gamma[:, None] * (c - 1 - idx))
    xi = jnp.exp(log_gamma[:, None] * (idx + 1))
    gamma_c = jnp.exp(log_gamma * c)
    bc = lambda a: jnp.broadcast_to(a[:, :, None], (RET_HEADS, c, RET_QK_DIM)).astype(BF16)
    return gamma_c, decay, bc(xi), bc(zeta)


def kernel(x, g_norm_mix, w_in, b_merge_gate, g_q, g_k, w_branch_att, g_ret_norm, w_branch_ret,
           w_out, g_norm_ffn, w_router_group, b_router_group, w_router_expert, b_router_expert,
           w1, w3, w2):
    batch = x.shape[0]
    t = batch * SEQ
    cos_a, sin_a = _rope_tables_att()
    cos_r, sin_r = _rope_tables_ret()
    gamma_c, decay, xi, zeta = _decay_tables()
    xf = x.reshape(t, D_MODEL)
    for l in range(g_norm_mix.shape[0]):
        g_mix = g_norm_mix[l][None, :]
        w_att_in = jnp.concatenate(
            [_pair_lanes(w_in[l][:, :2 * ATT_W]), w_in[l][:, 2 * ATT_W:ATT_IN_W]], axis=-1)
        proj_att = _inproj_att(xf, g_mix, w_att_in.astype(BF16))
        proj = _inproj_rest(xf, g_mix, w_in[l], cos_r, sin_r)
        reps = LANES // ATT_HEAD_DIM
        y_att = _attention(proj_att, cos_a, sin_a, _pair_lanes(jnp.tile(g_q[l], reps))[None, :],
                           _pair_lanes(jnp.tile(g_k[l], reps))[None, :], batch)
        y_ret = _retention(proj, gamma_c, decay, xi, zeta, g_ret_norm[l][:, None, :], batch)
        w_route = jnp.concatenate(
            [w_router_expert[l], w_router_group[l],
             jnp.zeros((D_MODEL, LANES - N_ROUTE), F32)], axis=-1)
        wr_hi, wr_lo = _split_bf16(w_route)
        b_route = jnp.concatenate(
            [b_router_expert[l], b_router_group[l], jnp.zeros((LANES - N_ROUTE,), F32)])[None, :]
        x1, xg = _out_stage(
            y_att, y_ret, proj, b_merge_gate[l][None, :], xf,
            w_branch_att[l].astype(BF16), w_branch_ret[l].astype(BF16), w_out[l].astype(BF16),
            g_norm_ffn[l][None, :], wr_hi, wr_lo, b_route)
        xf = _moe(xg, x1, w1[l], w3[l], w2[l])
    return xf.reshape(batch, SEQ, D_MODEL)
```

```python
import functools

import numpy as np

import jax
import jax.numpy as jnp
from jax import lax
from jax.experimental import pallas as pl
from jax.experimental.pallas import tpu as pltpu

F32 = jnp.float32
BF16 = jnp.bfloat16

D_MODEL = 1024
SEQ = 2048
ATT_HEADS = 16
ATT_HEAD_DIM = 64
ATT_W = ATT_HEADS * ATT_HEAD_DIM
ROPE_THETA = 10000.0
RET_HEADS = 4
RET_QK_DIM = 256
RET_V_DIM = 512
RET_QK_W = RET_HEADS * RET_QK_DIM
RET_V_W = RET_HEADS * RET_V_DIM
N_GROUPS = 4
EXPERTS_PER_GROUP = 8
N_EXPERTS = N_GROUPS * EXPERTS_PER_GROUP
EXPERT_FF = 256
EPS = 1e-6
ATT_IN_W = 3 * ATT_W
REST_IN_W = 2 * RET_QK_W + 2 * RET_V_W + 2 * D_MODEL

LANES = 128
SUBLANES = 8
CB_QA, CB_KA, CB_VA = 0, 8, 16
CB_QR, CB_KR, CB_VR, CB_GR, CB_GA, CB_GB = 0, 8, 16, 32, 48, 56

ATT_BLOCK = 128
ATT_CLASSES = 16
ROWS_PER_CLASS = SEQ // ATT_CLASSES
RET_CHUNK = 256
RET_UNROLL = 2
NEG_BIG = -1e30
VMEM_LIMIT = 48 * 1024 * 1024


def _cparams(sem):
    return pltpu.CompilerParams(dimension_semantics=sem, vmem_limit_bytes=VMEM_LIMIT)


IN_TM = 1024
IN_TN = 1024
IN_NORM_ROWS = 256
REST_STEP_QK = CB_QR * LANES // (2 * IN_TN)
REST_STEP_GATE = CB_GR * LANES // (2 * IN_TN)
assert CB_KR * LANES == CB_QR * LANES + IN_TN and (CB_GA - CB_GR) * LANES == 2 * IN_TN


def _rmsnorm_rows(x, g_ref):
    ms = jnp.mean(x * x, axis=-1, keepdims=True)
    return x * lax.rsqrt(ms + EPS) * g_ref[...]


def _project(xn_sc, w_ref, o_ref, epilogue, first_block=0):
    xn = xn_sc[...]
    for c2 in range(IN_TN // 256):
        w = w_ref[:, c2 * 256:(c2 + 1) * 256].astype(BF16)
        acc = jnp.dot(xn, w, preferred_element_type=F32)
        lo, hi = epilogue(acc[:, :LANES], acc[:, LANES:])
        o_ref[first_block + 2 * c2] = lo.astype(BF16)
        o_ref[first_block + 2 * c2 + 1] = hi.astype(BF16)


def _inproj_att_kernel(x_ref, g_ref, w_ref, o_ref, xn_sc, xs):
    @pl.when(pl.program_id(1) == 0)
    def _():
        def norm_rows(ci, carry):
            rows = pl.ds(pl.multiple_of(ci * IN_NORM_ROWS, IN_NORM_ROWS), IN_NORM_ROWS)
            xn = _rmsnorm_rows(x_ref[rows, :], g_ref)
            for c in range(D_MODEL // LANES):
                xs[c, rows, :] = xn[:, c * LANES:(c + 1) * LANES]
            return carry

        lax.fori_loop(0, SEQ // IN_NORM_ROWS, norm_rows, 0)

        def gather_class(j, carry):
            dst = pl.ds(pl.multiple_of(j * ROWS_PER_CLASS, ROWS_PER_CLASS), ROWS_PER_CLASS)
            for c in range(D_MODEL // LANES):
                xn_sc[dst, c * LANES:(c + 1) * LANES] = xs[
                    c, pl.ds(j, ROWS_PER_CLASS, stride=ATT_CLASSES), :].astype(BF16)
            return carry

        lax.fori_loop(0, ATT_CLASSES, gather_class, 0, unroll=2)

    _project(xn_sc, w_ref, o_ref, lambda lo, hi: (lo, hi))


def _inproj_rest_kernel(x_ref, g_ref, wa_ref, wb_ref, cos_ref, sin_ref, o_ref, xn_sc):
    j = pl.program_id(1)
    second = IN_TN // LANES

    @pl.when(j == 0)
    def _():
        xn_sc[...] = _rmsnorm_rows(x_ref[...], g_ref).astype(BF16)

    def rotate(scale):
        def epilogue(x1, x2):
            cos, sin = cos_ref[...], sin_ref[...]
            return (x1 * cos - x2 * sin) * scale, (x2 * cos + x1 * sin) * scale
        return epilogue

    def swish(lo, hi):
        return lo * jax.nn.sigmoid(lo), hi * jax.nn.sigmoid(hi)

    def both(first, last):
        _project(xn_sc, wa_ref, o_ref, first)
        _project(xn_sc, wb_ref, o_ref, last, second)

    @pl.when(j == REST_STEP_QK)
    def _():
        both(rotate(1.0), rotate(RET_QK_DIM ** -0.5))

    @pl.when(j == REST_STEP_GATE)
    def _():
        both(swish, swish)

    @pl.when((j != REST_STEP_QK) & (j != REST_STEP_GATE))
    def _():
        both(lambda lo, hi: (lo, hi), lambda lo, hi: (lo, hi))


def _inproj_att(x2d, g, w_bf16):
    t = x2d.shape[0]
    width = w_bf16.shape[1]
    return pl.pallas_call(
        _inproj_att_kernel,
        grid=(t // SEQ, width // IN_TN),
        in_specs=[
            pl.BlockSpec((SEQ, D_MODEL), lambda i, j: (i, 0)),
            pl.BlockSpec((1, D_MODEL), lambda i, j: (0, 0)),
            pl.BlockSpec((D_MODEL, IN_TN), lambda i, j: (0, j)),
        ],
        out_specs=pl.BlockSpec((IN_TN // LANES, SEQ, LANES), lambda i, j: (j, i, 0)),
        out_shape=jax.ShapeDtypeStruct((width // LANES, t, LANES), BF16),
        scratch_shapes=[pltpu.VMEM((SEQ, D_MODEL), BF16),
                        pltpu.VMEM((D_MODEL // LANES, SEQ, LANES), F32)],
        compiler_params=_cparams(("arbitrary", "arbitrary")),
        name="inproj_att",
    )(x2d, g, w_bf16)


def _inproj_rest(x2d, g, w_in_f32, cos_r, sin_r):
    t = x2d.shape[0]
    tiles_per_seq = SEQ // IN_TM
    first_tile = ATT_IN_W // IN_TN
    table = pl.BlockSpec((IN_TM, LANES), lambda i, j: (i % tiles_per_seq, 0))
    return pl.pallas_call(
        _inproj_rest_kernel,
        grid=(t // IN_TM, REST_IN_W // (2 * IN_TN)),
        in_specs=[
            pl.BlockSpec((IN_TM, D_MODEL), lambda i, j: (i, 0)),
            pl.BlockSpec((1, D_MODEL), lambda i, j: (0, 0)),
            pl.BlockSpec((D_MODEL, IN_TN), lambda i, j: (0, first_tile + 2 * j)),
            pl.BlockSpec((D_MODEL, IN_TN), lambda i, j: (0, first_tile + 2 * j + 1)),
            table, table,
        ],
        out_specs=pl.BlockSpec((2 * IN_TN // LANES, IN_TM, LANES), lambda i, j: (j, i, 0)),
        out_shape=jax.ShapeDtypeStruct((REST_IN_W // LANES, t, LANES), BF16),
        scratch_shapes=[pltpu.VMEM((IN_TM, D_MODEL), BF16)],
        compiler_params=_cparams(("arbitrary", "arbitrary")),
        name="inproj_rest",
    )(x2d, g, w_in_f32, w_in_f32, cos_r, sin_r)


PREP_ROWS = 256
PREP_UNROLL = 4
SEG4 = ATT_BLOCK // 4
SEG1 = ATT_BLOCK // ATT_CLASSES


def _att_bias_tables():
    def tile(qpos, kpos):
        d = qpos[:, None] - kpos[None, :]
        one = np.where((d >= 0) & (d <= ATT_BLOCK), 0.0, NEG_BIG).astype(np.float32)
        return np.concatenate([one, one], axis=1)

    u = np.arange(ATT_BLOCK)
    q4 = 4 * (u % SEG4) + u // SEG4
    q1 = ATT_CLASSES * (u % SEG1) + u // SEG1
    w = np.arange(2 * ATT_BLOCK)
    k1 = ATT_CLASSES * (w % SEG1) + w // (2 * SEG1) + ATT_BLOCK * ((w // SEG1) % 2 - 1)
    return (tile(q1, k1), tile(q1, q1), tile(q4, np.concatenate([q4 - ATT_BLOCK, q4])), tile(q4, q4),
            tile(u, u))


def masked_pair(x, head0_mask):
    return jnp.where(head0_mask, x, 0.0).astype(BF16), jnp.where(head0_mask, 0.0, x).astype(BF16)


def _att_kernel(q_ref, k_ref, v_ref, cos_ref, sin_ref, gq_ref, gk_ref,
                b1_ref, b1f_ref, b4_ref, b4f_ref, b16_ref, o_ref,
                qj, kj, vj, qb, kb, vb, acc_sc, m_sc, l_sc):
    lane = lax.broadcasted_iota(jnp.int32, (1, LANES), 1)
    half = ATT_HEAD_DIM // 2
    qk_head0 = (lane // half) % 2 == 0
    v_head0 = lane < ATT_HEAD_DIM
    qk_keep0, qk_keep1 = masked_pair(jnp.ones((1, LANES), F32), qk_head0)
    v_keep0, v_keep1 = masked_pair(jnp.ones((1, LANES), F32), v_head0)
    seg = jnp.where((lax.broadcasted_iota(jnp.int32, (LANES, LANES), 0) // half) % 2
                    == (lax.broadcasted_iota(jnp.int32, (LANES, LANES), 1) // half) % 2,
                    1.0 / ATT_HEAD_DIM, 0.0).astype(BF16)

    def prep(ci, carry):
        rows = pl.ds(pl.multiple_of(ci * PREP_ROWS, PREP_ROWS), PREP_ROWS)
        cos = cos_ref[rows, :]
        sin = sin_ref[rows, :]

        def norm_rope(src, g_ref):
            x = src[0, rows, :].astype(F32)
            ms = jnp.dot((x * x).astype(BF16), seg, preferred_element_type=F32)
            xn = x * lax.rsqrt(ms + EPS) * g_ref[...]
            return xn * cos + pltpu.roll(xn, ATT_HEAD_DIM, 1) * sin

        xq = norm_rope(q_ref, gq_ref)
        qj[rows, :] = xq
        qb[rows, :] = xq.astype(BF16)
        xk = norm_rope(k_ref, gk_ref)
        kj[rows, :] = xk
        xk = xk.astype(BF16)
        kb[0, rows, :] = xk * qk_keep0
        kb[1, rows, :] = xk * qk_keep1
        xv = v_ref[0, rows, :]
        vj[rows, :] = xv.astype(F32)
        vb[0, rows, :] = xv * v_keep0
        vb[1, rows, :] = xv * v_keep1
        return carry

    lax.fori_loop(0, SEQ // PREP_ROWS, prep, 0, unroll=PREP_UNROLL)

    nt_dims = (((1,), (1,)), ((), ()))

    def attend(q, k0, k1, v0, v1, bias):
        w = k0.shape[0]
        s = lax.dot_general(q, jnp.concatenate([k0, k1], axis=0), nt_dims,
                            preferred_element_type=F32) + bias
        m0 = jnp.max(s[:, :w], axis=-1, keepdims=True)
        m1 = jnp.max(s[:, w:], axis=-1, keepdims=True)
        e = jnp.concatenate([jnp.exp(s[:, :w] - m0), jnp.exp(s[:, w:] - m1)], axis=1).astype(BF16)
        ones0 = jnp.broadcast_to(jnp.where(v_head0, 1.0, 0.0).astype(BF16), (w, LANES))
        ones1 = jnp.broadcast_to(jnp.where(v_head0, 0.0, 1.0).astype(BF16), (w, LANES))
        v2 = jnp.concatenate([jnp.concatenate([v0, ones0], axis=1),
                              jnp.concatenate([v1, ones1], axis=1)], axis=0)
        r = jnp.dot(e, v2, preferred_element_type=F32)
        return r[:, :LANES], jnp.where(v_head0, m0, m1), r[:, LANES:]

    def gather(ref, pieces, lead=()):
        return jnp.concatenate([ref[lead + (rows, slice(None))] for rows in pieces], axis=0)


    def store(p, pieces, n, a, m, l):
        for idx, rows in enumerate(pieces):
            acc_sc[p, rows, :] = a[idx * n:(idx + 1) * n]
            m_sc[p, rows, :] = m[idx * n:(idx + 1) * n]
            l_sc[p, rows, :] = l[idx * n:(idx + 1) * n]

    for j in range(ATT_CLASSES):
        rows = [pl.ds(j * ATT_BLOCK, ATT_BLOCK)]
        a, m, l = attend(qb[rows[0], :], kb[0, rows[0], :], kb[1, rows[0], :],
                         vb[0, rows[0], :], vb[1, rows[0], :], b16_ref[...])
        store(2, rows, ATT_BLOCK, a, m, l)

    for c in range(4):
        def segs(n):
            return [pl.ds((4 * a + c) * ATT_BLOCK + SEG4 * n, SEG4) for a in range(4)]

        for n in range(SEQ // 4 // ATT_BLOCK):
            cur = segs(n)
            keys = cur if n == 0 else segs(n - 1) + cur
            bias = b4f_ref[...] if n == 0 else b4_ref[...]
            a, m, l = attend(gather(qb, cur), gather(kb, keys, (0,)), gather(kb, keys, (1,)),
                             gather(vb, keys, (0,)), gather(vb, keys, (1,)), bias)
            store(1, cur, SEG4, a, m, l)

    for n in range(SEQ // ATT_BLOCK):
        cur = [pl.ds(j * ATT_BLOCK + SEG1 * n, SEG1) for j in range(ATT_CLASSES)]
        if n == 0:
            keys, bias = cur, b1f_ref[...]
        else:
            keys = [pl.ds(j * ATT_BLOCK + SEG1 * n - SEG1, 2 * SEG1) for j in range(ATT_CLASSES)]
            bias = b1_ref[...]
        k0, k1 = masked_pair(gather(kj, keys), qk_head0)
        v0, v1 = masked_pair(gather(vj, keys), v_head0)
        a, m, l = attend(gather(qj, cur).astype(BF16), k0, k1, v0, v1, bias)
        store(0, cur, SEG1, a, m, l)

    def merge(j, carry):
        rows = pl.ds(pl.multiple_of(j * ROWS_PER_CLASS, ROWS_PER_CLASS), ROWS_PER_CLASS)
        ms = [m_sc[p, rows, :] for p in range(3)]
        m_all = jnp.maximum(jnp.maximum(ms[0], ms[1]), ms[2])
        num = jnp.zeros((ROWS_PER_CLASS, LANES), F32)
        den = jnp.zeros((ROWS_PER_CLASS, LANES), F32)
        for p in range(3):
            w = jnp.exp(ms[p] - m_all)
            num = num + w * acc_sc[p, rows, :]
            den = den + w * l_sc[p, rows, :]
        qj[pl.ds(j, ROWS_PER_CLASS, stride=ATT_CLASSES), :] = num / den
        return carry

    lax.fori_loop(0, ATT_CLASSES, merge, 0)

    def emit(ci, carry):
        rows = pl.ds(pl.multiple_of(ci * PREP_ROWS, PREP_ROWS), PREP_ROWS)
        o_ref[rows, :] = qj[rows, :].astype(BF16)
        return carry

    lax.fori_loop(0, SEQ // PREP_ROWS, emit, 0)


def _attention(proj_att, cos_a, sin_a, gq, gk, batch):
    t = batch * SEQ
    hp = ATT_W // LANES

    def col(cb):
        return pl.BlockSpec((1, SEQ, LANES), lambda b, h: (cb + h, b, 0))

    def const(shape):
        return pl.BlockSpec(shape, lambda b, h: tuple(0 for _ in shape))

    biases = [jnp.asarray(b) for b in _att_bias_tables()]
    row_f32 = pltpu.VMEM((SEQ, LANES), F32)
    row_bf16 = pltpu.VMEM((SEQ, LANES), BF16)
    stat = pltpu.VMEM((3, SEQ, LANES), F32)
    return pl.pallas_call(
        _att_kernel,
        grid=(batch, hp),
        in_specs=[col(CB_QA), col(CB_KA), col(CB_VA),
                  const((SEQ, LANES)), const((SEQ, LANES)), const((1, LANES)), const((1, LANES))]
                 + [const(b.shape) for b in biases],
        out_specs=pl.BlockSpec((SEQ, LANES), lambda b, h: (b, h)),
        out_shape=jax.ShapeDtypeStruct((t, ATT_W), BF16),
        scratch_shapes=[row_f32] * 3 + [row_bf16] + [pltpu.VMEM((2, SEQ, LANES), BF16)] * 2 + [stat] * 3,
        compiler_params=_cparams(("arbitrary", "arbitrary")),
        name="dilated_attention",
    )(proj_att, proj_att, proj_att, cos_a, sin_a, gq, gk, *biases)


def _ret_kernel(gam_ref, q_ref, k_ref, v_ref, g_ref, decay_ref, xi_ref, zeta_ref,
                gn_ref, o_ref, state_sc):
    h = pl.program_id(1)
    gamma_c = gam_ref[h]
    state_sc[...] = jnp.zeros_like(state_sc)
    c = RET_CHUNK
    nt_dims = (((1,), (1,)), ((), ()))
    tn_dims = (((0,), (0,)), ((), ()))

    def cols(ref, rs, n):
        return jnp.concatenate([ref[i, rs, :] for i in range(n)], axis=-1)

    def chunk(n, carry):
        rs = pl.ds(pl.multiple_of(n * c, c), c)
        q = cols(q_ref, rs, RET_QK_DIM // LANES)
        k = cols(k_ref, rs, RET_QK_DIM // LANES)
        v = cols(v_ref, rs, RET_V_DIM // LANES)
        inner = lax.dot_general(q, k, nt_dims, preferred_element_type=F32) * decay_ref[0]
        y = jnp.dot(inner.astype(BF16), v, preferred_element_type=F32)
        state = state_sc[...]
        y = y + jnp.dot(q * xi_ref[0], state.astype(BF16), preferred_element_type=F32)
        state_sc[...] = state * gamma_c + lax.dot_general(k * zeta_ref[0], v, tn_dims,
                                                          preferred_element_type=F32)
        yn = y * lax.rsqrt(jnp.mean(y * y, axis=-1, keepdims=True) + EPS) * gn_ref[0]
        o_ref[rs, :] = (yn * cols(g_ref, rs, RET_V_DIM // LANES).astype(F32)).astype(BF16)
        return carry

    lax.fori_loop(0, SEQ // c, chunk, 0, unroll=RET_UNROLL)


def _retention(proj, gamma_c, decay, xi, zeta, g_ret, batch):
    t = batch * SEQ
    nq = RET_QK_DIM // LANES
    nv = RET_V_DIM // LANES

    def cols(cb, n):
        return pl.BlockSpec((n, SEQ, LANES), lambda b, h: (cb // n + h, b, 0))

    return pl.pallas_call(
        _ret_kernel,
        grid=(batch, RET_HEADS),
        in_specs=[
            pl.BlockSpec(memory_space=pltpu.SMEM),
            cols(CB_QR, nq), cols(CB_KR, nq), cols(CB_VR, nv), cols(CB_GR, nv),
            pl.BlockSpec((1, RET_CHUNK, RET_CHUNK), lambda b, h: (h, 0, 0)),
            pl.BlockSpec((1, RET_CHUNK, RET_QK_DIM), lambda b, h: (h, 0, 0)),
            pl.BlockSpec((1, RET_CHUNK, RET_QK_DIM), lambda b, h: (h, 0, 0)),
            pl.BlockSpec((1, 1, RET_V_DIM), lambda b, h: (h, 0, 0)),
        ],
        out_specs=pl.BlockSpec((SEQ, RET_V_DIM), lambda b, h: (b, h)),
        out_shape=jax.ShapeDtypeStruct((t, RET_V_W), BF16),
        scratch_shapes=[pltpu.VMEM((RET_QK_DIM, RET_V_DIM), F32)],
        compiler_params=_cparams(("arbitrary", "arbitrary")),
        name="retention",
    )(gamma_c, proj, proj, proj, proj, decay, xi, zeta, g_ret)


OUT_TM = 512
N_ROUTE = N_EXPERTS + N_GROUPS
GROUP_LANE = LANES - 1
RANK_LANE = LANES - 2
XG_W = D_MODEL + LANES


def _split_bf16(x):
    hi = x.astype(BF16)
    lo = (x - hi.astype(F32)).astype(BF16)
    return hi, lo


def _out_kernel(ya_ref, yr_ref, ga_ref, gb_ref, bg_ref, x_ref, watt_ref, wret_ref, wout_ref,
                gffn_ref, wr_hi_ref, wr_lo_ref, br_ref, x1_ref, xg_ref, cnt_ref):
    ya = jnp.dot(ya_ref[...], watt_ref[...], preferred_element_type=F32)
    yr = jnp.dot(yr_ref[...], wret_ref[...], preferred_element_type=F32)
    merged = []
    for cb in range(D_MODEL // LANES):
        cs = slice(cb * LANES, (cb + 1) * LANES)
        ga = jax.nn.sigmoid(ga_ref[cb].astype(F32) + bg_ref[:, cs])
        gb = jax.nn.sigmoid(gb_ref[cb].astype(F32) + bg_ref[:, D_MODEL + cb * LANES:D_MODEL + (cb + 1) * LANES])
        merged.append((ga * ya[:, cs] + gb * yr[:, cs]).astype(BF16))
    merged = jnp.concatenate(merged, axis=-1)
    x1 = x_ref[...] + jnp.dot(merged, wout_ref[...], preferred_element_type=F32)
    x1_ref[...] = x1
    xn = x1 * lax.rsqrt(jnp.mean(x1 * x1, axis=-1, keepdims=True) + EPS) * gffn_ref[...]
    xg_ref[:, :D_MODEL] = xn

    xh, xl = _split_bf16(xn)
    logits = (jnp.dot(xh, wr_hi_ref[...], preferred_element_type=F32)
              + jnp.dot(xl, wr_hi_ref[...], preferred_element_type=F32)
              + jnp.dot(xh, wr_lo_ref[...], preferred_element_type=F32)) + br_ref[...]
    lane = lax.broadcasted_iota(jnp.int32, logits.shape, 1)
    lane_f = lane.astype(F32)
    is_group = (lane >= N_EXPERTS) & (lane < N_ROUTE)
    gl = jnp.where(is_group, logits, NEG_BIG)
    gmax = jnp.max(gl, axis=-1, keepdims=True)
    gsel = jnp.min(jnp.where(gl == gmax, lane_f, 1e9), axis=-1, keepdims=True) - N_EXPERTS
    p_group = 1.0 / jnp.sum(jnp.where(is_group, jnp.exp(logits - gmax), 0.0), axis=-1, keepdims=True)
    lo_lane = gsel * EXPERTS_PER_GROUP
    in_group = (lane_f >= lo_lane) & (lane_f < lo_lane + EXPERTS_PER_GROUP)
    el = jnp.where(in_group, logits, NEG_BIG)
    v1 = jnp.max(el, axis=-1, keepdims=True)
    i1 = jnp.min(jnp.where(el == v1, lane_f, 1e9), axis=-1, keepdims=True)
    el2 = jnp.where(lane_f == i1, NEG_BIG, el)
    v2 = jnp.max(el2, axis=-1, keepdims=True)
    i2 = jnp.min(jnp.where(el2 == v2, lane_f, 1e9), axis=-1, keepdims=True)
    e21 = jnp.exp(v2 - v1)
    w1 = p_group / (1.0 + e21)
    w2 = w1 * e21
    gate = jnp.where(lane_f == i1, w1, 0.0) + jnp.where(lane_f == i2, w2, 0.0)

    @pl.when(pl.program_id(0) == 0)
    def _():
        cnt_ref[...] = jnp.zeros_like(cnt_ref)

    in_lane = lane_f == gsel
    onehot = jnp.where(in_lane, 1.0, 0.0)
    rows = logits.shape[0]
    earlier = (lax.broadcasted_iota(jnp.int32, (rows, rows), 1)
               < lax.broadcasted_iota(jnp.int32, (rows, rows), 0)).astype(BF16)
    before = jnp.dot(earlier, onehot.astype(BF16), preferred_element_type=F32) + cnt_ref[...]
    rank = jnp.sum(jnp.where(in_lane, before, 0.0), axis=-1, keepdims=True)
    cnt_ref[...] += jnp.sum(onehot, axis=0, keepdims=True)
    xg_ref[:, D_MODEL:] = jnp.where(lane == GROUP_LANE, gsel, jnp.where(lane == RANK_LANE, rank, gate))


def _out_stage(y_att, y_ret, proj, b_gate, x2d, w_att, w_ret, w_out, g_ffn, wr_hi, wr_lo, b_route):
    t = x2d.shape[0]
    ncb = D_MODEL // LANES

    def full(shape):
        return pl.BlockSpec(shape, lambda i: tuple(0 for _ in shape))

    return pl.pallas_call(
        _out_kernel,
        grid=(t // OUT_TM,),
        in_specs=[
            pl.BlockSpec((OUT_TM, ATT_W), lambda i: (i, 0)),
            pl.BlockSpec((OUT_TM, RET_V_W), lambda i: (i, 0)),
            pl.BlockSpec((ncb, OUT_TM, LANES), lambda i: (CB_GA // ncb, i, 0)),
            pl.BlockSpec((ncb, OUT_TM, LANES), lambda i: (CB_GB // ncb, i, 0)),
            full((1, 2 * D_MODEL)),
            pl.BlockSpec((OUT_TM, D_MODEL), lambda i: (i, 0)),
            full((ATT_W, D_MODEL)), full((RET_V_W, D_MODEL)), full((D_MODEL, D_MODEL)),
            full((1, D_MODEL)), full((D_MODEL, LANES)), full((D_MODEL, LANES)), full((1, LANES)),
        ],
        out_specs=[
            pl.BlockSpec((OUT_TM, D_MODEL), lambda i: (i, 0)),
            pl.BlockSpec((OUT_TM, XG_W), lambda i: (i, 0)),
            full((1, LANES)),
        ],
        out_shape=[
            jax.ShapeDtypeStruct((t, D_MODEL), F32),
            jax.ShapeDtypeStruct((t, XG_W), F32),
            jax.ShapeDtypeStruct((1, LANES), F32),
        ],
        compiler_params=_cparams(("arbitrary",)),
        name="out_stage",
    )(y_att, y_ret, proj, proj, b_gate, x2d, w_att, w_ret, w_out, g_ffn, wr_hi, wr_lo, b_route)


MOE_TILE = 512
MOVE_TM = 1024
GROUP_FF = EXPERTS_PER_GROUP * EXPERT_FF


def _moe_plan(xg, counts, t):
    i32 = jnp.int32
    g = xg[:, D_MODEL + GROUP_LANE].astype(i32)
    rank = xg[:, D_MODEL + RANK_LANE].astype(i32)
    padded = (counts[0, :N_GROUPS].astype(i32) + MOE_TILE - 1) // MOE_TILE * MOE_TILE
    ends = jnp.cumsum(padded)
    starts = ends - padded
    pos = rank + sum(jnp.where(g == k, starts[k], 0) for k in range(N_GROUPS))
    tile_start = jnp.arange(t // MOE_TILE + N_GROUPS, dtype=i32) * MOE_TILE
    tile_group = jnp.minimum(jnp.sum((tile_start[:, None] >= ends[None, :]).astype(i32), axis=1),
                             N_GROUPS - 1)
    return pos.astype(i32), tile_group.astype(i32), (ends[-1:] // MOE_TILE).astype(i32)


def _dispatch_kernel(hi_ref, lo_ref, xg_ref, xs_init_ref, xs_ref, sem):
    del xs_init_ref
    base = pl.program_id(0) * MOVE_TM

    def send(i, carry):
        for u in range(SUBLANES):
            r = base + i * SUBLANES + u
            pltpu.make_async_copy(xg_ref.at[i, pl.ds(u, 1)],
                                  xs_ref.at[hi_ref[r], pl.ds(lo_ref[r], 1)], sem).start()
        return carry

    lax.fori_loop(0, MOVE_TM // SUBLANES, send, 0)
    pltpu.make_async_copy(xg_ref, xs_ref.at[pl.ds(0, MOVE_TM // SUBLANES)], sem).wait()


def _dispatch(pos_hi, pos_lo, xg, n_rows):
    t = xg.shape[0]
    xs = pl.pallas_call(
        _dispatch_kernel,
        grid_spec=pltpu.PrefetchScalarGridSpec(
            num_scalar_prefetch=2,
            grid=(t // MOVE_TM,),
            in_specs=[pl.BlockSpec((MOVE_TM // SUBLANES, SUBLANES, XG_W), lambda i, hi, lo: (i, 0, 0)),
                      pl.BlockSpec(memory_space=pl.ANY)],
            out_specs=pl.BlockSpec(memory_space=pl.ANY),
            scratch_shapes=[pltpu.SemaphoreType.DMA],
        ),
        out_shape=jax.ShapeDtypeStruct((n_rows // SUBLANES, SUBLANES, XG_W), F32),
        input_output_aliases={3: 0},
        compiler_params=_cparams(("arbitrary",)),
        name="moe_dispatch",
    )(pos_hi, pos_lo, xg.reshape(t // SUBLANES, SUBLANES, XG_W),
      jnp.zeros((n_rows // SUBLANES, SUBLANES, XG_W), F32))
    return xs.reshape(n_rows, XG_W)


def _experts_kernel(tg_ref, nused_ref, xs_ref, w1_ref, w3_ref, w2_ref, ys_ref):
    i = pl.program_id(0)

    @pl.when(i < nused_ref[0])
    def _():
        x = xs_ref[:, :D_MODEL].astype(BF16)
        gate = xs_ref[:, D_MODEL:]
        lane = lax.broadcasted_iota(jnp.int32, gate.shape, 1)
        first = tg_ref[i] * EXPERTS_PER_GROUP
        hidden = []
        for e in range(EXPERTS_PER_GROUP):
            a = jnp.dot(x, w1_ref[0, e], preferred_element_type=F32)
            b = jnp.dot(x, w3_ref[0, e], preferred_element_type=F32)
            g = jnp.sum(jnp.where(lane == first + e, gate, 0.0), axis=-1, keepdims=True)
            hidden.append((a * jax.nn.sigmoid(a) * b * g).astype(BF16))
        ys_ref[...] = jnp.dot(jnp.concatenate(hidden, axis=-1), w2_ref[0], preferred_element_type=F32)

    @pl.when(i >= nused_ref[0])
    def _():
        ys_ref[...] = jnp.zeros_like(ys_ref)


def _experts(tile_group, n_used, xs, w1g, w3g, w2g):
    n_rows = xs.shape[0]
    up = pl.BlockSpec((1, EXPERTS_PER_GROUP, D_MODEL, EXPERT_FF), lambda i, tg, nu: (tg[i], 0, 0, 0))
    return pl.pallas_call(
        _experts_kernel,
        grid_spec=pltpu.PrefetchScalarGridSpec(
            num_scalar_prefetch=2,
            grid=(n_rows // MOE_TILE,),
            in_specs=[
                pl.BlockSpec((MOE_TILE, XG_W), lambda i, tg, nu: (i, 0)),
                up, up,
                pl.BlockSpec((1, GROUP_FF, D_MODEL), lambda i, tg, nu: (tg[i], 0, 0)),
            ],
            out_specs=pl.BlockSpec((MOE_TILE, D_MODEL), lambda i, tg, nu: (i, 0)),
        ),
        out_shape=jax.ShapeDtypeStruct((n_rows, D_MODEL), F32),
        compiler_params=_cparams(("arbitrary",)),
        name="moe_experts",
    )(tile_group, n_used, xs, w1g, w3g, w2g)


def _combine_kernel(hi_ref, lo_ref, x1_ref, ys_ref, o_ref, buf, sem):
    base = pl.program_id(0) * MOVE_TM

    def fetch(i, carry):
        for u in range(SUBLANES):
            r = base + i * SUBLANES + u
            pltpu.make_async_copy(ys_ref.at[hi_ref[r], pl.ds(lo_ref[r], 1)],
                                  buf.at[i, pl.ds(u, 1)], sem).start()
        return carry

    lax.fori_loop(0, MOVE_TM // SUBLANES, fetch, 0)
    pltpu.make_async_copy(ys_ref.at[pl.ds(0, MOVE_TM // SUBLANES)], buf, sem).wait()
    o_ref[...] = x1_ref[...] + buf[...]


def _combine(pos_hi, pos_lo, x1, ys):
    t = x1.shape[0]
    tile = (MOVE_TM // SUBLANES, SUBLANES, D_MODEL)
    out = pl.pallas_call(
        _combine_kernel,
        grid_spec=pltpu.PrefetchScalarGridSpec(
            num_scalar_prefetch=2,
            grid=(t // MOVE_TM,),
            in_specs=[pl.BlockSpec(tile, lambda i, hi, lo: (i, 0, 0)),
                      pl.BlockSpec(memory_space=pl.ANY)],
            out_specs=pl.BlockSpec(tile, lambda i, hi, lo: (i, 0, 0)),
            scratch_shapes=[pltpu.VMEM(tile, F32), pltpu.SemaphoreType.DMA],
        ),
        out_shape=jax.ShapeDtypeStruct((t // SUBLANES, SUBLANES, D_MODEL), F32),
        compiler_params=_cparams(("arbitrary",)),
        name="moe_combine",
    )(pos_hi, pos_lo, x1.reshape(t // SUBLANES, SUBLANES, D_MODEL),
      ys.reshape(ys.shape[0] // SUBLANES, SUBLANES, D_MODEL))
    return out.reshape(t, D_MODEL)


def _moe(xg, counts, x1, w1, w3, w2):
    t = xg.shape[0]
    w1g = w1.reshape(N_GROUPS, EXPERTS_PER_GROUP, D_MODEL, EXPERT_FF).astype(BF16)
    w3g = w3.reshape(N_GROUPS, EXPERTS_PER_GROUP, D_MODEL, EXPERT_FF).astype(BF16)
    w2g = w2.reshape(N_GROUPS, GROUP_FF, D_MODEL).astype(BF16)
    pos, tile_group, n_used = _moe_plan(xg, counts, t)
    pos_hi, pos_lo = pos // SUBLANES, pos % SUBLANES
    xs = _dispatch(pos_hi, pos_lo, xg, t + N_GROUPS * MOE_TILE)
    ys = _experts(tile_group, n_used, xs, w1g, w3g, w2g)
    return _combine(pos_hi, pos_lo, x1, ys)


def _class_major(table):
    return table.reshape(ROWS_PER_CLASS, ATT_CLASSES, -1).transpose(1, 0, 2).reshape(SEQ, -1)


def _rope_tables_att():
    pos = jnp.arange(SEQ, dtype=F32)
    inv = ROPE_THETA ** (-jnp.arange(0, ATT_HEAD_DIM, 2, dtype=F32) / ATT_HEAD_DIM)
    ang = pos[:, None] * inv[None, :]
    cos, sin = jnp.cos(ang), jnp.sin(ang)
    cos_full = jnp.concatenate([cos, cos, cos, cos], axis=-1)
    sin_full = jnp.concatenate([-sin, -sin, sin, sin], axis=-1)
    return _class_major(cos_full), _class_major(sin_full)


def _pair_lanes(a):
    half = ATT_HEAD_DIM // 2
    lead = a.shape[:-1]
    a = a.reshape(lead + (-1, 2, 2, half))
    return jnp.swapaxes(a, -3, -2).reshape(lead + (-1,))


def _rope_tables_ret():
    pos = jnp.arange(SEQ, dtype=F32)
    inv = 1.0 / (ROPE_THETA ** jnp.linspace(0.0, 1.0, RET_QK_DIM // 2, dtype=F32))
    ang = pos[:, None] * inv[None, :]
    return jnp.cos(ang), jnp.sin(ang)


def _decay_tables():
    c = RET_CHUNK
    log_gamma = jnp.log(1.0 - jnp.exp2(-5.0 - jnp.arange(RET_HEADS, dtype=F32)))
    idx = jnp.arange(c, dtype=F32)
    diff = idx[:, None] - idx[None, :]
    decay = jnp.where(diff >= 0, jnp.exp(log_gamma[:, None, None] * jnp.maximum(diff, 0.0)), 0.0)
    zeta = jnp.exp(log_gamma[:, None] * (c - 1 - idx))
    xi = jnp.exp(log_gamma[:, None] * (idx + 1))
    gamma_c = jnp.exp(log_gamma * c)
    bc = lambda a: jnp.broadcast_to(a[:, :, None], (RET_HEADS, c, RET_QK_DIM)).astype(BF16)
    return gamma_c, decay, bc(xi), bc(zeta)


def kernel(x, g_norm_mix, w_in, b_merge_gate, g_q, g_k, w_branch_att, g_ret_norm, w_branch_ret,
           w_out, g_norm_ffn, w_router_group, b_router_group, w_router_expert, b_router_expert,
           w1, w3, w2):
    batch = x.shape[0]
    t = batch * SEQ
    cos_a, sin_a = _rope_tables_att()
    cos_r, sin_r = _rope_tables_ret()
    gamma_c, decay, xi, zeta = _decay_tables()
    xf = x.reshape(t, D_MODEL)
    for l in range(g_norm_mix.shape[0]):
        g_mix = g_norm_mix[l][None, :]
        w_att_in = jnp.concatenate(
            [_pair_lanes(w_in[l][:, :2 * ATT_W]), w_in[l][:, 2 * ATT_W:ATT_IN_W]], axis=-1)
        proj_att = _inproj_att(xf, g_mix, w_att_in.astype(BF16))
        proj = _inproj_rest(xf, g_mix, w_in[l], cos_r, sin_r)
        reps = LANES // ATT_HEAD_DIM
        gq = _pair_lanes(jnp.tile(g_q[l], reps))[None, :] * ATT_HEAD_DIM ** -0.5
        y_att = _attention(proj_att, cos_a, sin_a, gq, _pair_lanes(jnp.tile(g_k[l], reps))[None, :],
                           batch)
        y_ret = _retention(proj, gamma_c, decay, xi, zeta, g_ret_norm[l][:, None, :], batch)
        w_route = jnp.concatenate(
            [w_router_expert[l], w_router_group[l],
             jnp.zeros((D_MODEL, LANES - N_ROUTE), F32)], axis=-1)
        wr_hi, wr_lo = _split_bf16(w_route)
        b_route = jnp.concatenate(
            [b_router_expert[l], b_router_group[l], jnp.zeros((LANES - N_ROUTE,), F32)])[None, :]
        x1, xg, counts = _out_stage(
            y_att, y_ret, proj, b_merge_gate[l][None, :], xf,
            w_branch_att[l].astype(BF16), w_branch_ret[l].astype(BF16), w_out[l].astype(BF16),
            g_norm_ffn[l][None, :], wr_hi, wr_lo, b_route)
        xf = _moe(xg, counts, x1, w1[l], w3[l], w2[l])
    return xf.reshape(batch, SEQ, D_MODEL)
```

```python
import functools

import numpy as np

import jax
import jax.numpy as jnp
from jax import lax
from jax.experimental import pallas as pl
from jax.experimental.pallas import tpu as pltpu

F32 = jnp.float32
BF16 = jnp.bfloat16

D_MODEL = 1024
SEQ = 2048
ATT_HEADS = 16
ATT_HEAD_DIM = 64
ATT_W = ATT_HEADS * ATT_HEAD_DIM
ROPE_THETA = 10000.0
RET_HEADS = 4
RET_QK_DIM = 256
RET_V_DIM = 512
RET_QK_W = RET_HEADS * RET_QK_DIM
RET_V_W = RET_HEADS * RET_V_DIM
N_GROUPS = 4
EXPERTS_PER_GROUP = 8
N_EXPERTS = N_GROUPS * EXPERTS_PER_GROUP
EXPERT_FF = 256
EPS = 1e-6
ATT_IN_W = 3 * ATT_W
REST_IN_W = 2 * RET_QK_W + 2 * RET_V_W + 2 * D_MODEL

LANES = 128
SUBLANES = 8
CB_QA, CB_KA, CB_VA = 0, 8, 16
CB_QR, CB_KR, CB_VR, CB_GR, CB_GA, CB_GB = 0, 8, 16, 32, 48, 56

ATT_BLOCK = 128
ATT_CLASSES = 16
ROWS_PER_CLASS = SEQ // ATT_CLASSES
RET_CHUNK = 256
RET_UNROLL = 2
NEG_BIG = -1e30
VMEM_LIMIT = 48 * 1024 * 1024
IN_ATT_VMEM_LIMIT = 56 * 1024 * 1024


def _cparams(sem, vmem_limit=VMEM_LIMIT):
    return pltpu.CompilerParams(dimension_semantics=sem, vmem_limit_bytes=vmem_limit)


IN_TM = 1024
IN_TN = 1024
IN_NORM_ROWS = 256
REST_STEP_QK = CB_QR * LANES // (2 * IN_TN)
REST_STEP_GATE = CB_GR * LANES // (2 * IN_TN)
assert CB_KR * LANES == CB_QR * LANES + IN_TN and (CB_GA - CB_GR) * LANES == 2 * IN_TN


def _rmsnorm_rows(x, g_ref):
    ms = jnp.mean(x * x, axis=-1, keepdims=True)
    return x * lax.rsqrt(ms + EPS) * g_ref[...]


def _project(xn_sc, w_ref, o_ref, epilogue, first_block=0):
    xn = xn_sc[...]
    for c2 in range(IN_TN // 256):
        w = w_ref[:, c2 * 256:(c2 + 1) * 256].astype(BF16)
        acc = jnp.dot(xn, w, preferred_element_type=F32)
        lo, hi = epilogue(acc[:, :LANES], acc[:, LANES:])
        o_ref[first_block + 2 * c2] = lo.astype(BF16)
        o_ref[first_block + 2 * c2 + 1] = hi.astype(BF16)


def _inproj_att_kernel(x_ref, g_ref, wqk_ref, wv_ref, o_ref, xn_sc, xs):
    @pl.when(pl.program_id(1) == 0)
    def _():
        def norm_rows(ci, carry):
            rows = pl.ds(pl.multiple_of(ci * IN_NORM_ROWS, IN_NORM_ROWS), IN_NORM_ROWS)
            xn = _rmsnorm_rows(x_ref[rows, :], g_ref)
            for c in range(D_MODEL // LANES):
                xs[c, rows, :] = xn[:, c * LANES:(c + 1) * LANES]
            return carry

        lax.fori_loop(0, SEQ // IN_NORM_ROWS, norm_rows, 0)

        def gather_class(j, carry):
            dst = pl.ds(pl.multiple_of(j * ROWS_PER_CLASS, ROWS_PER_CLASS), ROWS_PER_CLASS)
            for c in range(D_MODEL // LANES):
                xn_sc[dst, c * LANES:(c + 1) * LANES] = xs[
                    c, pl.ds(j, ROWS_PER_CLASS, stride=ATT_CLASSES), :].astype(BF16)
            return carry

        lax.fori_loop(0, ATT_CLASSES, gather_class, 0)

    qk_tiles = 2 * ATT_W // IN_TN

    @pl.when(pl.program_id(1) < qk_tiles)
    def _():
        _project(xn_sc, wqk_ref, o_ref, lambda lo, hi: (lo, hi))

    @pl.when(pl.program_id(1) >= qk_tiles)
    def _():
        _project(xn_sc, wv_ref, o_ref, lambda lo, hi: (lo, hi))


def _inproj_rest_kernel(x_ref, g_ref, wa_ref, wb_ref, cos_ref, sin_ref, o_ref, xn_sc):
    j = pl.program_id(1)
    second = IN_TN // LANES

    @pl.when(j == 0)
    def _():
        xn_sc[...] = _rmsnorm_rows(x_ref[...], g_ref).astype(BF16)

    def rotate(scale):
        def epilogue(x1, x2):
            cos, sin = cos_ref[...], sin_ref[...]
            return (x1 * cos - x2 * sin) * scale, (x2 * cos + x1 * sin) * scale
        return epilogue

    def swish(lo, hi):
        return lo * jax.nn.sigmoid(lo), hi * jax.nn.sigmoid(hi)

    def both(first, last):
        _project(xn_sc, wa_ref, o_ref, first)
        _project(xn_sc, wb_ref, o_ref, last, second)

    @pl.when(j == REST_STEP_QK)
    def _():
        both(rotate(1.0), rotate(RET_QK_DIM ** -0.5))

    @pl.when(j == REST_STEP_GATE)
    def _():
        both(swish, swish)

    @pl.when((j != REST_STEP_QK) & (j != REST_STEP_GATE))
    def _():
        both(lambda lo, hi: (lo, hi), lambda lo, hi: (lo, hi))


def _inproj_att(x2d, g, w_qk_f32, w_in_f32):
    t = x2d.shape[0]
    width = ATT_IN_W
    qk_tiles = 2 * ATT_W // IN_TN
    return pl.pallas_call(
        _inproj_att_kernel,
        grid=(t // SEQ, width // IN_TN),
        in_specs=[
            pl.BlockSpec((SEQ, D_MODEL), lambda i, j: (i, 0)),
            pl.BlockSpec((1, D_MODEL), lambda i, j: (0, 0)),
            pl.BlockSpec((D_MODEL, IN_TN), lambda i, j: (0, jnp.minimum(j, qk_tiles - 1))),
            pl.BlockSpec((D_MODEL, IN_TN), lambda i, j: (0, qk_tiles), pipeline_mode=pl.Buffered(1)),
        ],
        out_specs=pl.BlockSpec((IN_TN // LANES, SEQ, LANES), lambda i, j: (j, i, 0)),
        out_shape=jax.ShapeDtypeStruct((width // LANES, t, LANES), BF16),
        scratch_shapes=[pltpu.VMEM((SEQ, D_MODEL), BF16),
                        pltpu.VMEM((D_MODEL // LANES, SEQ, LANES), F32)],
        compiler_params=_cparams(("arbitrary", "arbitrary"), vmem_limit=IN_ATT_VMEM_LIMIT),
        name="inproj_att",
    )(x2d, g, w_qk_f32, w_in_f32)


def _inproj_rest(x2d, g, w_in_f32, cos_r, sin_r):
    t = x2d.shape[0]
    tiles_per_seq = SEQ // IN_TM
    first_tile = ATT_IN_W // IN_TN
    table = pl.BlockSpec((IN_TM, LANES), lambda i, j: (i % tiles_per_seq, 0))
    return pl.pallas_call(
        _inproj_rest_kernel,
        grid=(t // IN_TM, REST_IN_W // (2 * IN_TN)),
        in_specs=[
            pl.BlockSpec((IN_TM, D_MODEL), lambda i, j: (i, 0)),
            pl.BlockSpec((1, D_MODEL), lambda i, j: (0, 0)),
            pl.BlockSpec((D_MODEL, IN_TN), lambda i, j: (0, first_tile + 2 * j)),
            pl.BlockSpec((D_MODEL, IN_TN), lambda i, j: (0, first_tile + 2 * j + 1)),
            table, table,
        ],
        out_specs=pl.BlockSpec((2 * IN_TN // LANES, IN_TM, LANES), lambda i, j: (j, i, 0)),
        out_shape=jax.ShapeDtypeStruct((REST_IN_W // LANES, t, LANES), BF16),
        scratch_shapes=[pltpu.VMEM((IN_TM, D_MODEL), BF16)],
        compiler_params=_cparams(("arbitrary", "arbitrary")),
        name="inproj_rest",
    )(x2d, g, w_in_f32, w_in_f32, cos_r, sin_r)


PREP_ROWS = 256
PREP_UNROLL = 4
SEG4 = ATT_BLOCK // 4
SEG1 = ATT_BLOCK // ATT_CLASSES


def _att_bias_tables():
    def tile(qpos, kpos):
        d = qpos[:, None] - kpos[None, :]
        one = np.where((d >= 0) & (d <= ATT_BLOCK), 0.0, NEG_BIG).astype(np.float32)
        return np.concatenate([one, one], axis=1)

    u = np.arange(ATT_BLOCK)
    q4 = 4 * (u % SEG4) + u // SEG4
    q1 = ATT_CLASSES * (u % SEG1) + u // SEG1
    w = np.arange(2 * ATT_BLOCK)
    k1 = ATT_CLASSES * (w % SEG1) + w // (2 * SEG1) + ATT_BLOCK * ((w // SEG1) % 2 - 1)
    return (tile(q1, k1), tile(q1, q1), tile(q4, np.concatenate([q4 - ATT_BLOCK, q4])), tile(q4, q4),
            tile(u, u))


def masked_pair(x, head0_mask):
    return jnp.where(head0_mask, x, 0.0).astype(BF16), jnp.where(head0_mask, 0.0, x).astype(BF16)


def _att_kernel(q_ref, k_ref, v_ref, cos_ref, sin_ref, gq_ref, gk_ref,
                b1_ref, b1f_ref, b4_ref, b4f_ref, b16_ref, o_ref,
                qj, kj, vj, qb, kb, vb, acc_sc, m_sc, l_sc):
    lane = lax.broadcasted_iota(jnp.int32, (1, LANES), 1)
    half = ATT_HEAD_DIM // 2
    qk_head0 = (lane // half) % 2 == 0
    v_head0 = lane < ATT_HEAD_DIM
    qk_keep0, qk_keep1 = masked_pair(jnp.ones((1, LANES), F32), qk_head0)
    v_keep0, v_keep1 = masked_pair(jnp.ones((1, LANES), F32), v_head0)
    seg = jnp.where((lax.broadcasted_iota(jnp.int32, (LANES, LANES), 0) // half) % 2
                    == (lax.broadcasted_iota(jnp.int32, (LANES, LANES), 1) // half) % 2,
                    1.0 / ATT_HEAD_DIM, 0.0).astype(BF16)

    def prep(ci, carry):
        rows = pl.ds(pl.multiple_of(ci * PREP_ROWS, PREP_ROWS), PREP_ROWS)
        cos = cos_ref[rows, :]
        sin = sin_ref[rows, :]

        def norm_rope(src, g_ref):
            x = src[0, rows, :].astype(F32)
            ms = jnp.dot((x * x).astype(BF16), seg, preferred_element_type=F32)
            xn = x * lax.rsqrt(ms + EPS) * g_ref[...]
            return xn * cos + pltpu.roll(xn, ATT_HEAD_DIM, 1) * sin

        xq = norm_rope(q_ref, gq_ref)
        qj[rows, :] = xq
        qb[rows, :] = xq.astype(BF16)
        xk = norm_rope(k_ref, gk_ref)
        kj[rows, :] = xk
        xk = xk.astype(BF16)
        kb[0, rows, :] = xk * qk_keep0
        kb[1, rows, :] = xk * qk_keep1
        xv = v_ref[0, rows, :]
        vj[rows, :] = xv.astype(F32)
        vb[0, rows, :] = xv * v_keep0
        vb[1, rows, :] = xv * v_keep1
        return carry

    lax.fori_loop(0, SEQ // PREP_ROWS, prep, 0, unroll=PREP_UNROLL)

    nt_dims = (((1,), (1,)), ((), ()))

    def attend(q, k0, k1, v0, v1, bias):
        w = k0.shape[0]
        s = lax.dot_general(q, jnp.concatenate([k0, k1], axis=0), nt_dims,
                            preferred_element_type=F32) + bias
        m0 = jnp.max(s[:, :w], axis=-1, keepdims=True)
        m1 = jnp.max(s[:, w:], axis=-1, keepdims=True)
        e = jnp.concatenate([jnp.exp(s[:, :w] - m0), jnp.exp(s[:, w:] - m1)], axis=1).astype(BF16)
        v2 = jnp.concatenate([jnp.concatenate([v0, jnp.broadcast_to(v_keep0, (w, LANES))], axis=1),
                              jnp.concatenate([v1, jnp.broadcast_to(v_keep1, (w, LANES))], axis=1)], axis=0)
        r = jnp.dot(e, v2, preferred_element_type=F32)
        return r[:, :LANES], jnp.where(v_head0, m0, m1), r[:, LANES:]

    def gather(ref, pieces, lead=()):
        return jnp.concatenate([ref[lead + (rows, slice(None))] for rows in pieces], axis=0)


    def store(p, pieces, n, a, m, l):
        for idx, rows in enumerate(pieces):
            acc_sc[p, rows, :] = a[idx * n:(idx + 1) * n]
            m_sc[p, rows, :] = m[idx * n:(idx + 1) * n]
            l_sc[p, rows, :] = l[idx * n:(idx + 1) * n]

    for j in range(ATT_CLASSES):
        rows = [pl.ds(j * ATT_BLOCK, ATT_BLOCK)]
        a, m, l = attend(qb[rows[0], :], kb[0, rows[0], :], kb[1, rows[0], :],
                         vb[0, rows[0], :], vb[1, rows[0], :], b16_ref[...])
        store(2, rows, ATT_BLOCK, a, m, l)

    for c in range(4):
        def segs(n):
            return [pl.ds((4 * a + c) * ATT_BLOCK + SEG4 * n, SEG4) for a in range(4)]

        for n in range(SEQ // 4 // ATT_BLOCK):
            cur = segs(n)
            keys = cur if n == 0 else segs(n - 1) + cur
            bias = b4f_ref[...] if n == 0 else b4_ref[...]
            a, m, l = attend(gather(qb, cur), gather(kb, keys, (0,)), gather(kb, keys, (1,)),
                             gather(vb, keys, (0,)), gather(vb, keys, (1,)), bias)
            store(1, cur, SEG4, a, m, l)

    for n in range(SEQ // ATT_BLOCK):
        cur = [pl.ds(j * ATT_BLOCK + SEG1 * n, SEG1) for j in range(ATT_CLASSES)]
        if n == 0:
            keys, bias = cur, b1f_ref[...]
        else:
            keys = [pl.ds(j * ATT_BLOCK + SEG1 * n - SEG1, 2 * SEG1) for j in range(ATT_CLASSES)]
            bias = b1_ref[...]
        k0, k1 = masked_pair(gather(kj, keys), qk_head0)
        v0, v1 = masked_pair(gather(vj, keys), v_head0)
        a, m, l = attend(gather(qj, cur).astype(BF16), k0, k1, v0, v1, bias)
        store(0, cur, SEG1, a, m, l)

    def merge(j, carry):
        rows = pl.ds(pl.multiple_of(j * ROWS_PER_CLASS, ROWS_PER_CLASS), ROWS_PER_CLASS)
        ms = [m_sc[p, rows, :] for p in range(3)]
        m_all = jnp.maximum(jnp.maximum(ms[0], ms[1]), ms[2])
        num = jnp.zeros((ROWS_PER_CLASS, LANES), F32)
        den = jnp.zeros((ROWS_PER_CLASS, LANES), F32)
        for p in range(3):
            w = jnp.exp(ms[p] - m_all)
            num = num + w * acc_sc[p, rows, :]
            den = den + w * l_sc[p, rows, :]
        qj[pl.ds(j, ROWS_PER_CLASS, stride=ATT_CLASSES), :] = num / den
        return carry

    lax.fori_loop(0, ATT_CLASSES, merge, 0)

    def emit(ci, carry):
        rows = pl.ds(pl.multiple_of(ci * PREP_ROWS, PREP_ROWS), PREP_ROWS)
        o_ref[rows, :] = qj[rows, :].astype(BF16)
        return carry

    lax.fori_loop(0, SEQ // PREP_ROWS, emit, 0)


def _attention(proj_att, cos_a, sin_a, gq, gk, batch):
    t = batch * SEQ
    hp = ATT_W // LANES

    def col(cb):
        return pl.BlockSpec((1, SEQ, LANES), lambda b, h: (cb + h, b, 0))

    def const(shape):
        return pl.BlockSpec(shape, lambda b, h: tuple(0 for _ in shape))

    biases = [jnp.asarray(b) for b in _att_bias_tables()]
    row_f32 = pltpu.VMEM((SEQ, LANES), F32)
    row_bf16 = pltpu.VMEM((SEQ, LANES), BF16)
    stat = pltpu.VMEM((3, SEQ, LANES), F32)
    return pl.pallas_call(
        _att_kernel,
        grid=(batch, hp),
        in_specs=[col(CB_QA), col(CB_KA), col(CB_VA),
                  const((SEQ, LANES)), const((SEQ, LANES)), const((1, LANES)), const((1, LANES))]
                 + [const(b.shape) for b in biases],
        out_specs=pl.BlockSpec((SEQ, LANES), lambda b, h: (b, h)),
        out_shape=jax.ShapeDtypeStruct((t, ATT_W), BF16),
        scratch_shapes=[row_f32] * 3 + [row_bf16] + [pltpu.VMEM((2, SEQ, LANES), BF16)] * 2 + [stat] * 3,
        compiler_params=_cparams(("arbitrary", "arbitrary")),
        name="dilated_attention",
    )(proj_att, proj_att, proj_att, cos_a, sin_a, gq, gk, *biases)


def _ret_kernel(gam_ref, q_ref, k_ref, v_ref, g_ref, decay_ref, xi_ref, zeta_ref,
                gn_ref, o_ref, state_sc):
    h = pl.program_id(1)
    gamma_c = gam_ref[h]
    state_sc[...] = jnp.zeros_like(state_sc)
    c = RET_CHUNK
    nt_dims = (((1,), (1,)), ((), ()))
    tn_dims = (((0,), (0,)), ((), ()))

    def cols(ref, rs, n):
        return jnp.concatenate([ref[i, rs, :] for i in range(n)], axis=-1)

    def chunk(n, carry):
        rs = pl.ds(pl.multiple_of(n * c, c), c)
        q = cols(q_ref, rs, RET_QK_DIM // LANES)
        k = cols(k_ref, rs, RET_QK_DIM // LANES)
        v = cols(v_ref, rs, RET_V_DIM // LANES)
        inner = lax.dot_general(q, k, nt_dims, preferred_element_type=F32) * decay_ref[0]
        y = jnp.dot(inner.astype(BF16), v, preferred_element_type=F32)
        state = state_sc[...]
        y = y + jnp.dot(q * xi_ref[0], state.astype(BF16), preferred_element_type=F32)
        state_sc[...] = state * gamma_c + lax.dot_general(k * zeta_ref[0], v, tn_dims,
                                                          preferred_element_type=F32)
        yn = y * lax.rsqrt(jnp.mean(y * y, axis=-1, keepdims=True) + EPS) * gn_ref[0]
        o_ref[rs, :] = (yn * cols(g_ref, rs, RET_V_DIM // LANES).astype(F32)).astype(BF16)
        return carry

    lax.fori_loop(0, SEQ // c, chunk, 0, unroll=RET_UNROLL)


def _retention(proj, gamma_c, decay, xi, zeta, g_ret, batch):
    t = batch * SEQ
    nq = RET_QK_DIM // LANES
    nv = RET_V_DIM // LANES

    def cols(cb, n):
        return pl.BlockSpec((n, SEQ, LANES), lambda b, h: (cb // n + h, b, 0))

    return pl.pallas_call(
        _ret_kernel,
        grid=(batch, RET_HEADS),
        in_specs=[
            pl.BlockSpec(memory_space=pltpu.SMEM),
            cols(CB_QR, nq), cols(CB_KR, nq), cols(CB_VR, nv), cols(CB_GR, nv),
            pl.BlockSpec((1, RET_CHUNK, RET_CHUNK), lambda b, h: (h, 0, 0)),
            pl.BlockSpec((1, RET_CHUNK, RET_QK_DIM), lambda b, h: (h, 0, 0)),
            pl.BlockSpec((1, RET_CHUNK, RET_QK_DIM), lambda b, h: (h, 0, 0)),
            pl.BlockSpec((1, 1, RET_V_DIM), lambda b, h: (h, 0, 0)),
        ],
        out_specs=pl.BlockSpec((SEQ, RET_V_DIM), lambda b, h: (b, h)),
        out_shape=jax.ShapeDtypeStruct((t, RET_V_W), BF16),
        scratch_shapes=[pltpu.VMEM((RET_QK_DIM, RET_V_DIM), F32)],
        compiler_params=_cparams(("arbitrary", "arbitrary")),
        name="retention",
    )(gamma_c, proj, proj, proj, proj, decay, xi, zeta, g_ret)


OUT_TM = 512
N_ROUTE = N_EXPERTS + N_GROUPS
GROUP_LANE = LANES - 1
XG_W = D_MODEL + LANES


def _split_bf16(x):
    hi = x.astype(BF16)
    lo = (x - hi.astype(F32)).astype(BF16)
    return hi, lo


def _out_kernel(ya_ref, yr_ref, ga_ref, gb_ref, bg_ref, x_ref, watt_ref, wret_ref, wout_ref,
                gffn_ref, wr_hi_ref, wr_lo_ref, br_ref, x1_ref, xg_ref):
    ya = jnp.dot(ya_ref[...], watt_ref[...], preferred_element_type=F32)
    yr = jnp.dot(yr_ref[...], wret_ref[...], preferred_element_type=F32)
    merged = []
    for cb in range(D_MODEL // LANES):
        cs = slice(cb * LANES, (cb + 1) * LANES)
        ga = jax.nn.sigmoid(ga_ref[cb].astype(F32) + bg_ref[:, cs])
        gb = jax.nn.sigmoid(gb_ref[cb].astype(F32) + bg_ref[:, D_MODEL + cb * LANES:D_MODEL + (cb + 1) * LANES])
        merged.append((ga * ya[:, cs] + gb * yr[:, cs]).astype(BF16))
    merged = jnp.concatenate(merged, axis=-1)
    x1 = x_ref[...] + jnp.dot(merged, wout_ref[...], preferred_element_type=F32)
    x1_ref[...] = x1
    xn = x1 * lax.rsqrt(jnp.mean(x1 * x1, axis=-1, keepdims=True) + EPS) * gffn_ref[...]
    xg_ref[:, :D_MODEL] = xn

    xh, xl = _split_bf16(xn)
    logits = (jnp.dot(xh, wr_hi_ref[...], preferred_element_type=F32)
              + jnp.dot(xl, wr_hi_ref[...], preferred_element_type=F32)
              + jnp.dot(xh, wr_lo_ref[...], preferred_element_type=F32)) + br_ref[...]
    lane = lax.broadcasted_iota(jnp.int32, logits.shape, 1)
    lane_f = lane.astype(F32)
    is_group = (lane >= N_EXPERTS) & (lane < N_ROUTE)
    gl = jnp.where(is_group, logits, NEG_BIG)
    gmax = jnp.max(gl, axis=-1, keepdims=True)
    gsel = jnp.min(jnp.where(gl == gmax, lane_f, 1e9), axis=-1, keepdims=True) - N_EXPERTS
    p_group = 1.0 / jnp.sum(jnp.where(is_group, jnp.exp(logits - gmax), 0.0), axis=-1, keepdims=True)
    lo_lane = gsel * EXPERTS_PER_GROUP
    in_group = (lane_f >= lo_lane) & (lane_f < lo_lane + EXPERTS_PER_GROUP)
    el = jnp.where(in_group, logits, NEG_BIG)
    v1 = jnp.max(el, axis=-1, keepdims=True)
    i1 = jnp.min(jnp.where(el == v1, lane_f, 1e9), axis=-1, keepdims=True)
    el2 = jnp.where(lane_f == i1, NEG_BIG, el)
    v2 = jnp.max(el2, axis=-1, keepdims=True)
    i2 = jnp.min(jnp.where(el2 == v2, lane_f, 1e9), axis=-1, keepdims=True)
    e21 = jnp.exp(v2 - v1)
    w1 = p_group / (1.0 + e21)
    w2 = w1 * e21
    gate = jnp.where(lane_f == i1, w1, 0.0) + jnp.where(lane_f == i2, w2, 0.0)
    xg_ref[:, D_MODEL:] = jnp.where(lane == GROUP_LANE, gsel, gate)


def _out_stage(y_att, y_ret, proj, b_gate, x2d, w_att, w_ret, w_out, g_ffn, wr_hi, wr_lo, b_route):
    t = x2d.shape[0]
    ncb = D_MODEL // LANES

    def full(shape):
        return pl.BlockSpec(shape, lambda i: tuple(0 for _ in shape))

    return pl.pallas_call(
        _out_kernel,
        grid=(t // OUT_TM,),
        in_specs=[
            pl.BlockSpec((OUT_TM, ATT_W), lambda i: (i, 0)),
            pl.BlockSpec((OUT_TM, RET_V_W), lambda i: (i, 0)),
            pl.BlockSpec((ncb, OUT_TM, LANES), lambda i: (CB_GA // ncb, i, 0)),
            pl.BlockSpec((ncb, OUT_TM, LANES), lambda i: (CB_GB // ncb, i, 0)),
            full((1, 2 * D_MODEL)),
            pl.BlockSpec((OUT_TM, D_MODEL), lambda i: (i, 0)),
            full((ATT_W, D_MODEL)), full((RET_V_W, D_MODEL)), full((D_MODEL, D_MODEL)),
            full((1, D_MODEL)), full((D_MODEL, LANES)), full((D_MODEL, LANES)), full((1, LANES)),
        ],
        out_specs=[
            pl.BlockSpec((OUT_TM, D_MODEL), lambda i: (i, 0)),
            pl.BlockSpec((OUT_TM, XG_W), lambda i: (i, 0)),
        ],
        out_shape=[
            jax.ShapeDtypeStruct((t, D_MODEL), F32),
            jax.ShapeDtypeStruct((t, XG_W), F32),
        ],
        compiler_params=_cparams(("arbitrary",)),
        name="out_stage",
    )(y_att, y_ret, proj, proj, b_gate, x2d, w_att, w_ret, w_out, g_ffn, wr_hi, wr_lo, b_route)


MOE_TILE = 512
MOVE_TM = 1024
GROUP_FF = EXPERTS_PER_GROUP * EXPERT_FF


def _moe_plan(xg, t):
    i32 = jnp.int32
    g = xg[:, D_MODEL + GROUP_LANE].astype(i32)
    onehot = (g[:, None] == jnp.arange(N_GROUPS, dtype=i32)[None, :]).astype(i32)
    csum = jnp.cumsum(onehot, axis=0)
    rank = jnp.sum(onehot * csum, axis=1) - 1
    padded = (csum[-1] + MOE_TILE - 1) // MOE_TILE * MOE_TILE
    ends = jnp.cumsum(padded)
    pos = rank + jnp.sum(onehot * (ends - padded)[None, :], axis=1)
    tile_start = jnp.arange(t // MOE_TILE + N_GROUPS, dtype=i32) * MOE_TILE
    tile_group = jnp.minimum(jnp.sum((tile_start[:, None] >= ends[None, :]).astype(i32), axis=1),
                             N_GROUPS - 1)
    return pos.astype(i32), tile_group.astype(i32), (ends[-1:] // MOE_TILE).astype(i32)


def _dispatch_kernel(hi_ref, lo_ref, xg_ref, xs_init_ref, xs_ref, sem):
    del xs_init_ref
    base = pl.program_id(0) * MOVE_TM

    def send(i, carry):
        for u in range(SUBLANES):
            r = base + i * SUBLANES + u
            pltpu.make_async_copy(xg_ref.at[i, pl.ds(u, 1)],
                                  xs_ref.at[hi_ref[r], pl.ds(lo_ref[r], 1)], sem).start()
        return carry

    lax.fori_loop(0, MOVE_TM // SUBLANES, send, 0)
    pltpu.make_async_copy(xg_ref, xs_ref.at[pl.ds(0, MOVE_TM // SUBLANES)], sem).wait()


def _dispatch(pos_hi, pos_lo, xg, n_rows):
    t = xg.shape[0]
    xs = pl.pallas_call(
        _dispatch_kernel,
        grid_spec=pltpu.PrefetchScalarGridSpec(
            num_scalar_prefetch=2,
            grid=(t // MOVE_TM,),
            in_specs=[pl.BlockSpec((MOVE_TM // SUBLANES, SUBLANES, XG_W), lambda i, hi, lo: (i, 0, 0)),
                      pl.BlockSpec(memory_space=pl.ANY)],
            out_specs=pl.BlockSpec(memory_space=pl.ANY),
            scratch_shapes=[pltpu.SemaphoreType.DMA],
        ),
        out_shape=jax.ShapeDtypeStruct((n_rows // SUBLANES, SUBLANES, XG_W), F32),
        input_output_aliases={3: 0},
        compiler_params=_cparams(("arbitrary",)),
        name="moe_dispatch",
    )(pos_hi, pos_lo, xg.reshape(t // SUBLANES, SUBLANES, XG_W),
      jnp.zeros((n_rows // SUBLANES, SUBLANES, XG_W), F32))
    return xs.reshape(n_rows, XG_W)


def _experts_kernel(tg_ref, nused_ref, xs_ref, w1_ref, w3_ref, w2_ref, ys_ref):
    i = pl.program_id(0)

    @pl.when(i < nused_ref[0])
    def _():
        x = xs_ref[:, :D_MODEL].astype(BF16)
        gate = xs_ref[:, D_MODEL:]
        lane = lax.broadcasted_iota(jnp.int32, gate.shape, 1)
        first = tg_ref[i] * EXPERTS_PER_GROUP
        hidden = []
        for e in range(EXPERTS_PER_GROUP):
            a = jnp.dot(x, w1_ref[0, e], preferred_element_type=F32)
            b = jnp.dot(x, w3_ref[0, e], preferred_element_type=F32)
            g = jnp.sum(jnp.where(lane == first + e, gate, 0.0), axis=-1, keepdims=True)
            hidden.append((a * jax.nn.sigmoid(a) * b * g).astype(BF16))
        ys_ref[...] = jnp.dot(jnp.concatenate(hidden, axis=-1), w2_ref[0], preferred_element_type=F32)

    @pl.when(i >= nused_ref[0])
    def _():
        ys_ref[...] = jnp.zeros_like(ys_ref)


def _experts(tile_group, n_used, xs, w1g, w3g, w2g):
    n_rows = xs.shape[0]
    up = pl.BlockSpec((1, EXPERTS_PER_GROUP, D_MODEL, EXPERT_FF), lambda i, tg, nu: (tg[i], 0, 0, 0))
    return pl.pallas_call(
        _experts_kernel,
        grid_spec=pltpu.PrefetchScalarGridSpec(
            num_scalar_prefetch=2,
            grid=(n_rows // MOE_TILE,),
            in_specs=[
                pl.BlockSpec((MOE_TILE, XG_W), lambda i, tg, nu: (i, 0)),
                up, up,
                pl.BlockSpec((1, GROUP_FF, D_MODEL), lambda i, tg, nu: (tg[i], 0, 0)),
            ],
            out_specs=pl.BlockSpec((MOE_TILE, D_MODEL), lambda i, tg, nu: (i, 0)),
        ),
        out_shape=jax.ShapeDtypeStruct((n_rows, D_MODEL), F32),
        compiler_params=_cparams(("arbitrary",)),
        name="moe_experts",
    )(tile_group, n_used, xs, w1g, w3g, w2g)


def _combine_kernel(hi_ref, lo_ref, x1_ref, ys_ref, o_ref, buf, sem):
    base = pl.program_id(0) * MOVE_TM

    def fetch(i, carry):
        for u in range(SUBLANES):
            r = base + i * SUBLANES + u
            pltpu.make_async_copy(ys_ref.at[hi_ref[r], pl.ds(lo_ref[r], 1)],
                                  buf.at[i, pl.ds(u, 1)], sem).start()
        return carry

    lax.fori_loop(0, MOVE_TM // SUBLANES, fetch, 0)
    pltpu.make_async_copy(ys_ref.at[pl.ds(0, MOVE_TM // SUBLANES)], buf, sem).wait()
    o_ref[...] = x1_ref[...] + buf[...]


def _combine(pos_hi, pos_lo, x1, ys):
    t = x1.shape[0]
    tile = (MOVE_TM // SUBLANES, SUBLANES, D_MODEL)
    out = pl.pallas_call(
        _combine_kernel,
        grid_spec=pltpu.PrefetchScalarGridSpec(
            num_scalar_prefetch=2,
            grid=(t // MOVE_TM,),
            in_specs=[pl.BlockSpec(tile, lambda i, hi, lo: (i, 0, 0)),
                      pl.BlockSpec(memory_space=pl.ANY)],
            out_specs=pl.BlockSpec(tile, lambda i, hi, lo: (i, 0, 0)),
            scratch_shapes=[pltpu.VMEM(tile, F32), pltpu.SemaphoreType.DMA],
        ),
        out_shape=jax.ShapeDtypeStruct((t // SUBLANES, SUBLANES, D_MODEL), F32),
        compiler_params=_cparams(("arbitrary",)),
        name="moe_combine",
    )(pos_hi, pos_lo, x1.reshape(t // SUBLANES, SUBLANES, D_MODEL),
      ys.reshape(ys.shape[0] // SUBLANES, SUBLANES, D_MODEL))
    return out.reshape(t, D_MODEL)


def _moe(xg, x1, w1, w3, w2):
    t = xg.shape[0]
    w1g = w1.reshape(N_GROUPS, EXPERTS_PER_GROUP, D_MODEL, EXPERT_FF).astype(BF16)
    w3g = w3.reshape(N_GROUPS, EXPERTS_PER_GROUP, D_MODEL, EXPERT_FF).astype(BF16)
    w2g = w2.reshape(N_GROUPS, GROUP_FF, D_MODEL).astype(BF16)
    pos, tile_group, n_used = _moe_plan(xg, t)
    pos_hi, pos_lo = pos // SUBLANES, pos % SUBLANES
    xs = _dispatch(pos_hi, pos_lo, xg, t + N_GROUPS * MOE_TILE)
    ys = _experts(tile_group, n_used, xs, w1g, w3g, w2g)
    return _combine(pos_hi, pos_lo, x1, ys)


def _class_major(table):
    return table.reshape(ROWS_PER_CLASS, ATT_CLASSES, -1).transpose(1, 0, 2).reshape(SEQ, -1)


def _rope_tables_att():
    pos = jnp.arange(SEQ, dtype=F32)
    inv = ROPE_THETA ** (-jnp.arange(0, ATT_HEAD_DIM, 2, dtype=F32) / ATT_HEAD_DIM)
    ang = pos[:, None] * inv[None, :]
    cos, sin = jnp.cos(ang), jnp.sin(ang)
    cos_full = jnp.concatenate([cos, cos, cos, cos], axis=-1)
    sin_full = jnp.concatenate([-sin, -sin, sin, sin], axis=-1)
    return _class_major(cos_full), _class_major(sin_full)


def _pair_lanes(a):
    half = ATT_HEAD_DIM // 2
    lead = a.shape[:-1]
    a = a.reshape(lead + (-1, 2, 2, half))
    return jnp.swapaxes(a, -3, -2).reshape(lead + (-1,))


def _rope_tables_ret():
    pos = jnp.arange(SEQ, dtype=F32)
    inv = 1.0 / (ROPE_THETA ** jnp.linspace(0.0, 1.0, RET_QK_DIM // 2, dtype=F32))
    ang = pos[:, None] * inv[None, :]
    return jnp.cos(ang), jnp.sin(ang)


def _decay_tables():
    c = RET_CHUNK
    log_gamma = jnp.log(1.0 - jnp.exp2(-5.0 - jnp.arange(RET_HEADS, dtype=F32)))
    idx = jnp.arange(c, dtype=F32)
    diff = idx[:, None] - idx[None, :]
    decay = jnp.where(diff >= 0, jnp.exp(log_gamma[:, None, None] * jnp.maximum(diff, 0.0)), 0.0)
    zeta = jnp.exp(log_gamma[:, None] * (c - 1 - idx))
    xi = jnp.exp(log_gamma[:, None] * (idx + 1))
    gamma_c = jnp.exp(log_gamma * c)
    bc = lambda a: jnp.broadcast_to(a[:, :, None], (RET_HEADS, c, RET_QK_DIM)).astype(BF16)
    return gamma_c, decay, bc(xi), bc(zeta)


def kernel(x, g_norm_mix, w_in, b_merge_gate, g_q, g_k, w_branch_att, g_ret_norm, w_branch_ret,
           w_out, g_norm_ffn, w_router_group, b_router_group, w_router_expert, b_router_expert,
           w1, w3, w2):
    batch = x.shape[0]
    t = batch * SEQ
    cos_a, sin_a = _rope_tables_att()
    cos_r, sin_r = _rope_tables_ret()
    gamma_c, decay, xi, zeta = _decay_tables()
    xf = x.reshape(t, D_MODEL)
    for l in range(g_norm_mix.shape[0]):
        g_mix = g_norm_mix[l][None, :]
        proj_att = _inproj_att(xf, g_mix, _pair_lanes(w_in[l][:, :2 * ATT_W]).astype(BF16), w_in[l])
        proj = _inproj_rest(xf, g_mix, w_in[l], cos_r, sin_r)
        reps = LANES // ATT_HEAD_DIM
        gq = _pair_lanes(jnp.tile(g_q[l], reps))[None, :] * ATT_HEAD_DIM ** -0.5
        y_att = _attention(proj_att, cos_a, sin_a, gq, _pair_lanes(jnp.tile(g_k[l], reps))[None, :],
                           batch)
        y_ret = _retention(proj, gamma_c, decay, xi, zeta, g_ret_norm[l][:, None, :], batch)
        w_route = jnp.concatenate(
            [w_router_expert[l], w_router_group[l],
             jnp.zeros((D_MODEL, LANES - N_ROUTE), F32)], axis=-1)
        wr_hi, wr_lo = _split_bf16(w_route)
        b_route = jnp.concatenate(
            [b_router_expert[l], b_router_group[l], jnp.zeros((LANES - N_ROUTE,), F32)])[None, :]
        x1, xg = _out_stage(
            y_att, y_ret, proj, b_merge_gate[l][None, :], xf,
            w_branch_att[l].astype(BF16), w_branch_ret[l].astype(BF16), w_out[l].astype(BF16),
            g_norm_ffn[l][None, :], wr_hi, wr_lo, b_route)
        xf = _moe(xg, x1, w1[l], w3[l], w2[l])
    return xf.reshape(batch, SEQ, D_MODEL)
```

```python
import functools

import numpy as np

import jax
import jax.numpy as jnp
from jax import lax
from jax.experimental import pallas as pl
from jax.experimental.pallas import tpu as pltpu

F32 = jnp.float32
BF16 = jnp.bfloat16

D_MODEL = 1024
SEQ = 2048
ATT_HEADS = 16
ATT_HEAD_DIM = 64
ATT_W = ATT_HEADS * ATT_HEAD_DIM
ROPE_THETA = 10000.0
RET_HEADS = 4
RET_QK_DIM = 256
RET_V_DIM = 512
RET_QK_W = RET_HEADS * RET_QK_DIM
RET_V_W = RET_HEADS * RET_V_DIM
N_GROUPS = 4
EXPERTS_PER_GROUP = 8
N_EXPERTS = N_GROUPS * EXPERTS_PER_GROUP
EXPERT_FF = 256
EPS = 1e-6
ATT_IN_W = 3 * ATT_W
REST_IN_W = 2 * RET_QK_W + 2 * RET_V_W + 2 * D_MODEL

LANES = 128
SUBLANES = 8
CB_QA, CB_KA, CB_VA = 0, 8, 16
CB_QR, CB_KR, CB_VR, CB_GR, CB_GA, CB_GB = 0, 8, 16, 32, 48, 56

ATT_BLOCK = 128
ATT_CLASSES = 16
ROWS_PER_CLASS = SEQ // ATT_CLASSES
RET_CHUNK = 256
RET_UNROLL = 2
NEG_BIG = -1e30
VMEM_LIMIT = 48 * 1024 * 1024
IN_ATT_VMEM_LIMIT = 56 * 1024 * 1024


def _cparams(sem, vmem_limit=VMEM_LIMIT):
    return pltpu.CompilerParams(dimension_semantics=sem, vmem_limit_bytes=vmem_limit)


IN_TM = 1024
IN_TN = 1024
IN_NORM_ROWS = 256
REST_STEP_QK = CB_QR * LANES // (2 * IN_TN)
REST_STEP_GATE = CB_GR * LANES // (2 * IN_TN)
assert CB_KR * LANES == CB_QR * LANES + IN_TN and (CB_GA - CB_GR) * LANES == 2 * IN_TN


def _rmsnorm_rows(x, g_ref):
    ms = jnp.mean(x * x, axis=-1, keepdims=True)
    return x * lax.rsqrt(ms + EPS) * g_ref[...]


def _project(xn_sc, w_ref, o_ref, epilogue, first_block=0):
    xn = xn_sc[...]
    for c2 in range(IN_TN // 256):
        w = w_ref[:, c2 * 256:(c2 + 1) * 256].astype(BF16)
        acc = jnp.dot(xn, w, preferred_element_type=F32)
        lo, hi = epilogue(acc[:, :LANES], acc[:, LANES:])
        o_ref[first_block + 2 * c2] = lo.astype(BF16)
        o_ref[first_block + 2 * c2 + 1] = hi.astype(BF16)


def _inproj_att_kernel(x_ref, g_ref, wqk_ref, wv_ref, o_ref, xn_sc, xs):
    @pl.when(pl.program_id(1) == 0)
    def _():
        def norm_rows(ci, carry):
            rows = pl.ds(pl.multiple_of(ci * IN_NORM_ROWS, IN_NORM_ROWS), IN_NORM_ROWS)
            xn = _rmsnorm_rows(x_ref[rows, :], g_ref)
            for c in range(D_MODEL // LANES):
                xs[c, rows, :] = xn[:, c * LANES:(c + 1) * LANES]
            return carry

        lax.fori_loop(0, SEQ // IN_NORM_ROWS, norm_rows, 0)

        def gather_class(j, carry):
            dst = pl.ds(pl.multiple_of(j * ROWS_PER_CLASS, ROWS_PER_CLASS), ROWS_PER_CLASS)
            for c in range(D_MODEL // LANES):
                xn_sc[dst, c * LANES:(c + 1) * LANES] = xs[
                    c, pl.ds(j, ROWS_PER_CLASS, stride=ATT_CLASSES), :].astype(BF16)
            return carry

        lax.fori_loop(0, ATT_CLASSES, gather_class, 0)

    qk_tiles = 2 * ATT_W // IN_TN

    @pl.when(pl.program_id(1) < qk_tiles)
    def _():
        _project(xn_sc, wqk_ref, o_ref, lambda lo, hi: (lo, hi))

    @pl.when(pl.program_id(1) >= qk_tiles)
    def _():
        _project(xn_sc, wv_ref, o_ref, lambda lo, hi: (lo, hi))


def _inproj_rest_kernel(x_ref, g_ref, wa_ref, wb_ref, cos_ref, sin_ref, o_ref, xn_sc):
    j = pl.program_id(1)
    second = IN_TN // LANES

    @pl.when(j == 0)
    def _():
        xn_sc[...] = _rmsnorm_rows(x_ref[...], g_ref).astype(BF16)

    def rotate(scale):
        def epilogue(x1, x2):
            cos, sin = cos_ref[...], sin_ref[...]
            return (x1 * cos - x2 * sin) * scale, (x2 * cos + x1 * sin) * scale
        return epilogue

    def swish(lo, hi):
        return lo * jax.nn.sigmoid(lo), hi * jax.nn.sigmoid(hi)

    def both(first, last):
        _project(xn_sc, wa_ref, o_ref, first)
        _project(xn_sc, wb_ref, o_ref, last, second)

    @pl.when(j == REST_STEP_QK)
    def _():
        both(rotate(1.0), rotate(RET_QK_DIM ** -0.5))

    @pl.when(j == REST_STEP_GATE)
    def _():
        both(swish, swish)

    @pl.when((j != REST_STEP_QK) & (j != REST_STEP_GATE))
    def _():
        both(lambda lo, hi: (lo, hi), lambda lo, hi: (lo, hi))


def _inproj_att(x2d, g, w_qk_f32, w_in_f32):
    t = x2d.shape[0]
    width = ATT_IN_W
    qk_tiles = 2 * ATT_W // IN_TN
    return pl.pallas_call(
        _inproj_att_kernel,
        grid=(t // SEQ, width // IN_TN),
        in_specs=[
            pl.BlockSpec((SEQ, D_MODEL), lambda i, j: (i, 0)),
            pl.BlockSpec((1, D_MODEL), lambda i, j: (0, 0)),
            pl.BlockSpec((D_MODEL, IN_TN), lambda i, j: (0, jnp.minimum(j, qk_tiles - 1))),
            pl.BlockSpec((D_MODEL, IN_TN), lambda i, j: (0, qk_tiles), pipeline_mode=pl.Buffered(1)),
        ],
        out_specs=pl.BlockSpec((IN_TN // LANES, SEQ, LANES), lambda i, j: (j, i, 0)),
        out_shape=jax.ShapeDtypeStruct((width // LANES, t, LANES), BF16),
        scratch_shapes=[pltpu.VMEM((SEQ, D_MODEL), BF16),
                        pltpu.VMEM((D_MODEL // LANES, SEQ, LANES), F32)],
        compiler_params=_cparams(("arbitrary", "arbitrary"), vmem_limit=IN_ATT_VMEM_LIMIT),
        name="inproj_att",
    )(x2d, g, w_qk_f32, w_in_f32)


def _inproj_rest(x2d, g, w_in_f32, cos_r, sin_r):
    t = x2d.shape[0]
    tiles_per_seq = SEQ // IN_TM
    first_tile = ATT_IN_W // IN_TN
    table = pl.BlockSpec((IN_TM, LANES), lambda i, j: (i % tiles_per_seq, 0))
    return pl.pallas_call(
        _inproj_rest_kernel,
        grid=(t // IN_TM, REST_IN_W // (2 * IN_TN)),
        in_specs=[
            pl.BlockSpec((IN_TM, D_MODEL), lambda i, j: (i, 0)),
            pl.BlockSpec((1, D_MODEL), lambda i, j: (0, 0)),
            pl.BlockSpec((D_MODEL, IN_TN), lambda i, j: (0, first_tile + 2 * j)),
            pl.BlockSpec((D_MODEL, IN_TN), lambda i, j: (0, first_tile + 2 * j + 1)),
            table, table,
        ],
        out_specs=pl.BlockSpec((2 * IN_TN // LANES, IN_TM, LANES), lambda i, j: (j, i, 0)),
        out_shape=jax.ShapeDtypeStruct((REST_IN_W // LANES, t, LANES), BF16),
        scratch_shapes=[pltpu.VMEM((IN_TM, D_MODEL), BF16)],
        compiler_params=_cparams(("arbitrary", "arbitrary")),
        name="inproj_rest",
    )(x2d, g, w_in_f32, w_in_f32, cos_r, sin_r)


PREP_ROWS = 256
PREP_UNROLL = 4
SEG4 = ATT_BLOCK // 4
SEG1 = ATT_BLOCK // ATT_CLASSES


def _att_bias_tables():
    def tile(qpos, kpos):
        d = qpos[:, None] - kpos[None, :]
        one = np.where((d >= 0) & (d <= ATT_BLOCK), 0.0, NEG_BIG).astype(np.float32)
        return np.concatenate([one, one], axis=1)

    u = np.arange(ATT_BLOCK)
    q4 = 4 * (u % SEG4) + u // SEG4
    q1 = ATT_CLASSES * (u % SEG1) + u // SEG1
    w = np.arange(2 * ATT_BLOCK)
    k1 = ATT_CLASSES * (w % SEG1) + w // (2 * SEG1) + ATT_BLOCK * ((w // SEG1) % 2 - 1)
    return (tile(q1, k1), tile(q1, q1), tile(q4, np.concatenate([q4 - ATT_BLOCK, q4])), tile(q4, q4),
            tile(u, u))


def masked_pair(x, head0_mask):
    return jnp.where(head0_mask, x, 0.0).astype(BF16), jnp.where(head0_mask, 0.0, x).astype(BF16)


def _att_kernel(q_ref, k_ref, v_ref, cos_ref, sin_ref, gq_ref, gk_ref,
                b1_ref, b1f_ref, b4_ref, b4f_ref, b16_ref, o_ref,
                qj, kj, vj, qb, kb, vb, acc_sc, m_sc, l_sc, nat):
    lane = lax.broadcasted_iota(jnp.int32, (1, LANES), 1)
    half = ATT_HEAD_DIM // 2
    qk_head0 = (lane // half) % 2 == 0
    v_head0 = lane < ATT_HEAD_DIM
    qk_keep0, qk_keep1 = masked_pair(jnp.ones((1, LANES), F32), qk_head0)
    v_keep0, v_keep1 = masked_pair(jnp.ones((1, LANES), F32), v_head0)
    seg = jnp.where((lax.broadcasted_iota(jnp.int32, (LANES, LANES), 0) // half) % 2
                    == (lax.broadcasted_iota(jnp.int32, (LANES, LANES), 1) // half) % 2,
                    1.0 / ATT_HEAD_DIM, 0.0).astype(BF16)

    def prep(ci, carry):
        rows = pl.ds(pl.multiple_of(ci * PREP_ROWS, PREP_ROWS), PREP_ROWS)
        cos = cos_ref[rows, :]
        sin = sin_ref[rows, :]

        def norm_rope(src, g_ref):
            x = src[0, rows, :].astype(F32)
            ms = jnp.dot((x * x).astype(BF16), seg, preferred_element_type=F32)
            xn = x * lax.rsqrt(ms + EPS) * g_ref[...]
            return xn * cos + pltpu.roll(xn, ATT_HEAD_DIM, 1) * sin

        xq = norm_rope(q_ref, gq_ref)
        qj[rows, :] = xq
        qb[rows, :] = xq.astype(BF16)
        xk = norm_rope(k_ref, gk_ref)
        kj[rows, :] = xk
        xk = xk.astype(BF16)
        kb[0, rows, :] = xk * qk_keep0
        kb[1, rows, :] = xk * qk_keep1
        xv = v_ref[0, rows, :]
        vj[rows, :] = xv.astype(F32)
        vb[0, rows, :] = xv * v_keep0
        vb[1, rows, :] = xv * v_keep1
        return carry

    lax.fori_loop(0, SEQ // PREP_ROWS, prep, 0, unroll=PREP_UNROLL)

    nt_dims = (((1,), (1,)), ((), ()))

    def attend(q, k0, k1, v0, v1, bias):
        w = k0.shape[0]
        s = lax.dot_general(q, jnp.concatenate([k0, k1], axis=0), nt_dims,
                            preferred_element_type=F32) + bias
        m0 = jnp.max(s[:, :w], axis=-1, keepdims=True)
        m1 = jnp.max(s[:, w:], axis=-1, keepdims=True)
        e = jnp.concatenate([jnp.exp(s[:, :w] - m0), jnp.exp(s[:, w:] - m1)], axis=1).astype(BF16)
        v2 = jnp.concatenate([jnp.concatenate([v0, jnp.broadcast_to(v_keep0, (w, LANES))], axis=1),
                              jnp.concatenate([v1, jnp.broadcast_to(v_keep1, (w, LANES))], axis=1)], axis=0)
        r = jnp.dot(e, v2, preferred_element_type=F32)
        return r[:, :LANES], jnp.where(v_head0, m0, m1), r[:, LANES:]

    def gather(ref, pieces, lead=()):
        return jnp.concatenate([ref[lead + (rows, slice(None))] for rows in pieces], axis=0)


    def store(p, pieces, n, a, m, l):
        for idx, rows in enumerate(pieces):
            acc_sc[p, rows, :] = a[idx * n:(idx + 1) * n]
            m_sc[p, rows, :] = m[idx * n:(idx + 1) * n]
            l_sc[p, rows, :] = l[idx * n:(idx + 1) * n]

    for j in range(ATT_CLASSES):
        rows = [pl.ds(j * ATT_BLOCK, ATT_BLOCK)]
        a, m, l = attend(qb[rows[0], :], kb[0, rows[0], :], kb[1, rows[0], :],
                         vb[0, rows[0], :], vb[1, rows[0], :], b16_ref[...])
        store(1, rows, ATT_BLOCK, a, m, l)

    for c in range(4):
        def segs(n):
            return [pl.ds((4 * a + c) * ATT_BLOCK + SEG4 * n, SEG4) for a in range(4)]

        for n in range(SEQ // 4 // ATT_BLOCK):
            cur = segs(n)
            keys = cur if n == 0 else segs(n - 1) + cur
            bias = b4f_ref[...] if n == 0 else b4_ref[...]
            a, m, l = attend(gather(qb, cur), gather(kb, keys, (0,)), gather(kb, keys, (1,)),
                             gather(vb, keys, (0,)), gather(vb, keys, (1,)), bias)
            store(0, cur, SEG4, a, m, l)

    for n in range(SEQ // ATT_BLOCK):
        cur = [pl.ds(j * ATT_BLOCK + SEG1 * n, SEG1) for j in range(ATT_CLASSES)]
        if n == 0:
            keys, bias = cur, b1f_ref[...]
        else:
            keys = [pl.ds(j * ATT_BLOCK + SEG1 * n - SEG1, 2 * SEG1) for j in range(ATT_CLASSES)]
            bias = b1_ref[...]
        k0, k1 = masked_pair(gather(kj, keys), qk_head0)
        v0, v1 = masked_pair(gather(vj, keys), v_head0)
        a, m, l = attend(gather(qj, cur).astype(BF16), k0, k1, v0, v1, bias)
        stats = [(a, m, l)] + [(gather(acc_sc, cur, (p,)), gather(m_sc, cur, (p,)), gather(l_sc, cur, (p,)))
                               for p in range(2)]
        m_all = jnp.maximum(jnp.maximum(stats[0][1], stats[1][1]), stats[2][1])
        num = jnp.zeros((ATT_BLOCK, LANES), F32)
        den = jnp.zeros((ATT_BLOCK, LANES), F32)
        for ap, mp, lp in stats:
            w = jnp.exp(mp - m_all)
            num = num + w * ap
            den = den + w * lp
        merged = num / den
        for j in range(ATT_CLASSES):
            nat[pl.ds(ATT_BLOCK * n + j, SEG1, stride=ATT_CLASSES), :] = merged[j * SEG1:(j + 1) * SEG1]

    def emit(ci, carry):
        rows = pl.ds(pl.multiple_of(ci * PREP_ROWS, PREP_ROWS), PREP_ROWS)
        o_ref[rows, :] = nat[rows, :].astype(BF16)
        return carry

    lax.fori_loop(0, SEQ // PREP_ROWS, emit, 0)


def _attention(proj_att, cos_a, sin_a, gq, gk, batch):
    t = batch * SEQ
    hp = ATT_W // LANES

    def col(cb):
        return pl.BlockSpec((1, SEQ, LANES), lambda b, h: (cb + h, b, 0))

    def const(shape):
        return pl.BlockSpec(shape, lambda b, h: tuple(0 for _ in shape))

    biases = [jnp.asarray(b) for b in _att_bias_tables()]
    row_f32 = pltpu.VMEM((SEQ, LANES), F32)
    row_bf16 = pltpu.VMEM((SEQ, LANES), BF16)
    stat = pltpu.VMEM((2, SEQ, LANES), F32)
    return pl.pallas_call(
        _att_kernel,
        grid=(batch, hp),
        in_specs=[col(CB_QA), col(CB_KA), col(CB_VA),
                  const((SEQ, LANES)), const((SEQ, LANES)), const((1, LANES)), const((1, LANES))]
                 + [const(b.shape) for b in biases],
        out_specs=pl.BlockSpec((SEQ, LANES), lambda b, h: (b, h)),
        out_shape=jax.ShapeDtypeStruct((t, ATT_W), BF16),
        scratch_shapes=([row_f32] * 3 + [row_bf16] + [pltpu.VMEM((2, SEQ, LANES), BF16)] * 2 + [stat] * 3
                        + [row_f32]),
        compiler_params=_cparams(("arbitrary", "arbitrary")),
        name="dilated_attention",
    )(proj_att, proj_att, proj_att, cos_a, sin_a, gq, gk, *biases)


def _ret_kernel(gam_ref, q_ref, k_ref, v_ref, g_ref, decay_ref, xi_ref, zeta_ref,
                gn_ref, o_ref, state_sc):
    h = pl.program_id(1)
    gamma_c = gam_ref[h]
    state_sc[...] = jnp.zeros_like(state_sc)
    c = RET_CHUNK
    nt_dims = (((1,), (1,)), ((), ()))
    tn_dims = (((0,), (0,)), ((), ()))

    def cols(ref, rs, n):
        return jnp.concatenate([ref[i, rs, :] for i in range(n)], axis=-1)

    def chunk(n, carry):
        rs = pl.ds(pl.multiple_of(n * c, c), c)
        q = cols(q_ref, rs, RET_QK_DIM // LANES)
        k = cols(k_ref, rs, RET_QK_DIM // LANES)
        v = cols(v_ref, rs, RET_V_DIM // LANES)
        inner = lax.dot_general(q, k, nt_dims, preferred_element_type=F32) * decay_ref[0]
        y = jnp.dot(inner.astype(BF16), v, preferred_element_type=F32)
        state = state_sc[...]
        y = y + jnp.dot(q * xi_ref[0], state.astype(BF16), preferred_element_type=F32)
        state_sc[...] = state * gamma_c + lax.dot_general(k * zeta_ref[0], v, tn_dims,
                                                          preferred_element_type=F32)
        yn = y * lax.rsqrt(jnp.mean(y * y, axis=-1, keepdims=True) + EPS) * gn_ref[0]
        o_ref[rs, :] = (yn * cols(g_ref, rs, RET_V_DIM // LANES).astype(F32)).astype(BF16)
        return carry

    lax.fori_loop(0, SEQ // c, chunk, 0, unroll=RET_UNROLL)


def _retention(proj, gamma_c, decay, xi, zeta, g_ret, batch):
    t = batch * SEQ
    nq = RET_QK_DIM // LANES
    nv = RET_V_DIM // LANES

    def cols(cb, n):
        return pl.BlockSpec((n, SEQ, LANES), lambda b, h: (cb // n + h, b, 0))

    return pl.pallas_call(
        _ret_kernel,
        grid=(batch, RET_HEADS),
        in_specs=[
            pl.BlockSpec(memory_space=pltpu.SMEM),
            cols(CB_QR, nq), cols(CB_KR, nq), cols(CB_VR, nv), cols(CB_GR, nv),
            pl.BlockSpec((1, RET_CHUNK, RET_CHUNK), lambda b, h: (h, 0, 0)),
            pl.BlockSpec((1, RET_CHUNK, RET_QK_DIM), lambda b, h: (h, 0, 0)),
            pl.BlockSpec((1, RET_CHUNK, RET_QK_DIM), lambda b, h: (h, 0, 0)),
            pl.BlockSpec((1, 1, RET_V_DIM), lambda b, h: (h, 0, 0)),
        ],
        out_specs=pl.BlockSpec((SEQ, RET_V_DIM), lambda b, h: (b, h)),
        out_shape=jax.ShapeDtypeStruct((t, RET_V_W), BF16),
        scratch_shapes=[pltpu.VMEM((RET_QK_DIM, RET_V_DIM), F32)],
        compiler_params=_cparams(("arbitrary", "arbitrary")),
        name="retention",
    )(gamma_c, proj, proj, proj, proj, decay, xi, zeta, g_ret)


OUT_TM = 512
N_ROUTE = N_EXPERTS + N_GROUPS
GROUP_LANE = LANES - 1
XG_W = D_MODEL + LANES


def _split_bf16(x):
    hi = x.astype(BF16)
    lo = (x - hi.astype(F32)).astype(BF16)
    return hi, lo


def _out_kernel(ya_ref, yr_ref, ga_ref, gb_ref, bg_ref, x_ref, watt_ref, wret_ref, wout_ref,
                gffn_ref, wr_hi_ref, wr_lo_ref, br_ref, x1_ref, xg_ref):
    ya = jnp.dot(ya_ref[...], watt_ref[...], preferred_element_type=F32)
    yr = jnp.dot(yr_ref[...], wret_ref[...], preferred_element_type=F32)
    merged = []
    for cb in range(D_MODEL // LANES):
        cs = slice(cb * LANES, (cb + 1) * LANES)
        ga = jax.nn.sigmoid(ga_ref[cb].astype(F32) + bg_ref[:, cs])
        gb = jax.nn.sigmoid(gb_ref[cb].astype(F32) + bg_ref[:, D_MODEL + cb * LANES:D_MODEL + (cb + 1) * LANES])
        merged.append((ga * ya[:, cs] + gb * yr[:, cs]).astype(BF16))
    merged = jnp.concatenate(merged, axis=-1)
    x1 = x_ref[...] + jnp.dot(merged, wout_ref[...], preferred_element_type=F32)
    x1_ref[...] = x1
    xn = x1 * lax.rsqrt(jnp.mean(x1 * x1, axis=-1, keepdims=True) + EPS) * gffn_ref[...]
    xg_ref[:, :D_MODEL] = xn

    xh, xl = _split_bf16(xn)
    logits = (jnp.dot(xh, wr_hi_ref[...], preferred_element_type=F32)
              + jnp.dot(xl, wr_hi_ref[...], preferred_element_type=F32)
              + jnp.dot(xh, wr_lo_ref[...], preferred_element_type=F32)) + br_ref[...]
    lane = lax.broadcasted_iota(jnp.int32, logits.shape, 1)
    lane_f = lane.astype(F32)
    is_group = (lane >= N_EXPERTS) & (lane < N_ROUTE)
    gl = jnp.where(is_group, logits, NEG_BIG)
    gmax = jnp.max(gl, axis=-1, keepdims=True)
    gsel = jnp.min(jnp.where(gl == gmax, lane_f, 1e9), axis=-1, keepdims=True) - N_EXPERTS
    p_group = 1.0 / jnp.sum(jnp.where(is_group, jnp.exp(logits - gmax), 0.0), axis=-1, keepdims=True)
    lo_lane = gsel * EXPERTS_PER_GROUP
    in_group = (lane_f >= lo_lane) & (lane_f < lo_lane + EXPERTS_PER_GROUP)
    el = jnp.where(in_group, logits, NEG_BIG)
    v1 = jnp.max(el, axis=-1, keepdims=True)
    i1 = jnp.min(jnp.where(el == v1, lane_f, 1e9), axis=-1, keepdims=True)
    el2 = jnp.where(lane_f == i1, NEG_BIG, el)
    v2 = jnp.max(el2, axis=-1, keepdims=True)
    i2 = jnp.min(jnp.where(el2 == v2, lane_f, 1e9), axis=-1, keepdims=True)
    e21 = jnp.exp(v2 - v1)
    w1 = p_group / (1.0 + e21)
    w2 = w1 * e21
    gate = jnp.where(lane_f == i1, w1, 0.0) + jnp.where(lane_f == i2, w2, 0.0)
    xg_ref[:, D_MODEL:] = jnp.where(lane == GROUP_LANE, gsel, gate)


def _out_stage(y_att, y_ret, proj, b_gate, x2d, w_att, w_ret, w_out, g_ffn, wr_hi, wr_lo, b_route):
    t = x2d.shape[0]
    ncb = D_MODEL // LANES

    def full(shape):
        return pl.BlockSpec(shape, lambda i: tuple(0 for _ in shape))

    return pl.pallas_call(
        _out_kernel,
        grid=(t // OUT_TM,),
        in_specs=[
            pl.BlockSpec((OUT_TM, ATT_W), lambda i: (i, 0)),
            pl.BlockSpec((OUT_TM, RET_V_W), lambda i: (i, 0)),
            pl.BlockSpec((ncb, OUT_TM, LANES), lambda i: (CB_GA // ncb, i, 0)),
            pl.BlockSpec((ncb, OUT_TM, LANES), lambda i: (CB_GB // ncb, i, 0)),
            full((1, 2 * D_MODEL)),
            pl.BlockSpec((OUT_TM, D_MODEL), lambda i: (i, 0)),
            full((ATT_W, D_MODEL)), full((RET_V_W, D_MODEL)), full((D_MODEL, D_MODEL)),
            full((1, D_MODEL)), full((D_MODEL, LANES)), full((D_MODEL, LANES)), full((1, LANES)),
        ],
        out_specs=[
            pl.BlockSpec((OUT_TM, D_MODEL), lambda i: (i, 0)),
            pl.BlockSpec((OUT_TM, XG_W), lambda i: (i, 0)),
        ],
        out_shape=[
            jax.ShapeDtypeStruct((t, D_MODEL), F32),
            jax.ShapeDtypeStruct((t, XG_W), F32),
        ],
        compiler_params=_cparams(("arbitrary",)),
        name="out_stage",
    )(y_att, y_ret, proj, proj, b_gate, x2d, w_att, w_ret, w_out, g_ffn, wr_hi, wr_lo, b_route)


MOE_TILE = 512
MOVE_TM = 1024
GROUP_FF = EXPERTS_PER_GROUP * EXPERT_FF


def _moe_plan(xg, t):
    i32 = jnp.int32
    g = xg[:, D_MODEL + GROUP_LANE].astype(i32)
    onehot = (g[:, None] == jnp.arange(N_GROUPS, dtype=i32)[None, :]).astype(i32)
    csum = jnp.cumsum(onehot, axis=0)
    rank = jnp.sum(onehot * csum, axis=1) - 1
    padded = (csum[-1] + MOE_TILE - 1) // MOE_TILE * MOE_TILE
    ends = jnp.cumsum(padded)
    pos = rank + jnp.sum(onehot * (ends - padded)[None, :], axis=1)
    tile_start = jnp.arange(t // MOE_TILE + N_GROUPS, dtype=i32) * MOE_TILE
    tile_group = jnp.minimum(jnp.sum((tile_start[:, None] >= ends[None, :]).astype(i32), axis=1),
                             N_GROUPS - 1)
    tails = jnp.maximum(ends - MOE_TILE, 0)
    return pos.astype(i32), tile_group.astype(i32), (ends[-1:] // MOE_TILE).astype(i32), tails.astype(i32)


def _dispatch_kernel(hi_ref, lo_ref, tail_ref, xg_ref, xs_ref, zeros, sem, zero_sem):
    base = pl.program_id(0) * MOVE_TM

    @pl.when(pl.program_id(0) == 0)
    def _():
        zeros[...] = jnp.zeros_like(zeros)
        tile = MOE_TILE // SUBLANES
        spare = [xs_ref.shape[0] - (g + 1) * tile for g in range(N_GROUPS)]
        for start in [tail_ref[g] for g in range(N_GROUPS)] + spare:
            fill = pltpu.make_async_copy(zeros, xs_ref.at[pl.ds(start, tile)], zero_sem)
            fill.start()
            fill.wait()

    def send(i, carry):
        for u in range(SUBLANES):
            r = base + i * SUBLANES + u
            pltpu.make_async_copy(xg_ref.at[i, pl.ds(u, 1)],
                                  xs_ref.at[hi_ref[r], pl.ds(lo_ref[r], 1)], sem).start()
        return carry

    lax.fori_loop(0, MOVE_TM // SUBLANES, send, 0)
    pltpu.make_async_copy(xg_ref, xs_ref.at[pl.ds(0, MOVE_TM // SUBLANES)], sem).wait()


def _dispatch(pos_hi, pos_lo, tail_hi, xg, n_rows):
    t = xg.shape[0]
    xs = pl.pallas_call(
        _dispatch_kernel,
        grid_spec=pltpu.PrefetchScalarGridSpec(
            num_scalar_prefetch=3,
            grid=(t // MOVE_TM,),
            in_specs=[pl.BlockSpec((MOVE_TM // SUBLANES, SUBLANES, XG_W), lambda i, hi, lo, tl: (i, 0, 0))],
            out_specs=pl.BlockSpec(memory_space=pl.ANY),
            scratch_shapes=[pltpu.VMEM((MOE_TILE // SUBLANES, SUBLANES, XG_W), F32),
                            pltpu.SemaphoreType.DMA, pltpu.SemaphoreType.DMA],
        ),
        out_shape=jax.ShapeDtypeStruct((n_rows // SUBLANES, SUBLANES, XG_W), F32),
        compiler_params=_cparams(("arbitrary",)),
        name="moe_dispatch",
    )(pos_hi, pos_lo, tail_hi, xg.reshape(t // SUBLANES, SUBLANES, XG_W))
    return xs.reshape(n_rows, XG_W)


def _experts_kernel(tg_ref, nused_ref, xs_ref, w1_ref, w3_ref, w2_ref, ys_ref):
    i = pl.program_id(0)

    @pl.when(i < nused_ref[0])
    def _():
        x = xs_ref[:, :D_MODEL].astype(BF16)
        gate = xs_ref[:, D_MODEL:]
        lane = lax.broadcasted_iota(jnp.int32, gate.shape, 1)
        first = tg_ref[i] * EXPERTS_PER_GROUP
        hidden = []
        for e in range(EXPERTS_PER_GROUP):
            a = jnp.dot(x, w1_ref[0, e], preferred_element_type=F32)
            b = jnp.dot(x, w3_ref[0, e], preferred_element_type=F32)
            g = jnp.sum(jnp.where(lane == first + e, gate, 0.0), axis=-1, keepdims=True)
            hidden.append((a * jax.nn.sigmoid(a) * b * g).astype(BF16))
        ys_ref[...] = jnp.dot(jnp.concatenate(hidden, axis=-1), w2_ref[0], preferred_element_type=F32)

    @pl.when(i >= nused_ref[0])
    def _():
        ys_ref[...] = jnp.zeros_like(ys_ref)


def _experts(tile_group, n_used, xs, w1g, w3g, w2g):
    n_rows = xs.shape[0]
    up = pl.BlockSpec((1, EXPERTS_PER_GROUP, D_MODEL, EXPERT_FF), lambda i, tg, nu: (tg[i], 0, 0, 0))
    return pl.pallas_call(
        _experts_kernel,
        grid_spec=pltpu.PrefetchScalarGridSpec(
            num_scalar_prefetch=2,
            grid=(n_rows // MOE_TILE,),
            in_specs=[
                pl.BlockSpec((MOE_TILE, XG_W), lambda i, tg, nu: (i, 0)),
                up, up,
                pl.BlockSpec((1, GROUP_FF, D_MODEL), lambda i, tg, nu: (tg[i], 0, 0)),
            ],
            out_specs=pl.BlockSpec((MOE_TILE, D_MODEL), lambda i, tg, nu: (i, 0)),
        ),
        out_shape=jax.ShapeDtypeStruct((n_rows, D_MODEL), F32),
        compiler_params=_cparams(("arbitrary",)),
        name="moe_experts",
    )(tile_group, n_used, xs, w1g, w3g, w2g)


def _combine_kernel(hi_ref, lo_ref, x1_ref, ys_ref, o_ref, buf, sem):
    base = pl.program_id(0) * MOVE_TM

    def fetch(i, carry):
        for u in range(SUBLANES):
            r = base + i * SUBLANES + u
            pltpu.make_async_copy(ys_ref.at[hi_ref[r], pl.ds(lo_ref[r], 1)],
                                  buf.at[i, pl.ds(u, 1)], sem).start()
        return carry

    lax.fori_loop(0, MOVE_TM // SUBLANES, fetch, 0)
    pltpu.make_async_copy(ys_ref.at[pl.ds(0, MOVE_TM // SUBLANES)], buf, sem).wait()
    o_ref[...] = x1_ref[...] + buf[...]


def _combine(pos_hi, pos_lo, x1, ys):
    t = x1.shape[0]
    tile = (MOVE_TM // SUBLANES, SUBLANES, D_MODEL)
    out = pl.pallas_call(
        _combine_kernel,
        grid_spec=pltpu.PrefetchScalarGridSpec(
            num_scalar_prefetch=2,
            grid=(t // MOVE_TM,),
            in_specs=[pl.BlockSpec(tile, lambda i, hi, lo: (i, 0, 0)),
                      pl.BlockSpec(memory_space=pl.ANY)],
            out_specs=pl.BlockSpec(tile, lambda i, hi, lo: (i, 0, 0)),
            scratch_shapes=[pltpu.VMEM(tile, F32), pltpu.SemaphoreType.DMA],
        ),
        out_shape=jax.ShapeDtypeStruct((t // SUBLANES, SUBLANES, D_MODEL), F32),
        compiler_params=_cparams(("arbitrary",)),
        name="moe_combine",
    )(pos_hi, pos_lo, x1.reshape(t // SUBLANES, SUBLANES, D_MODEL),
      ys.reshape(ys.shape[0] // SUBLANES, SUBLANES, D_MODEL))
    return out.reshape(t, D_MODEL)


def _moe(xg, x1, w1, w3, w2):
    t = xg.shape[0]
    w1g = w1.reshape(N_GROUPS, EXPERTS_PER_GROUP, D_MODEL, EXPERT_FF).astype(BF16)
    w3g = w3.reshape(N_GROUPS, EXPERTS_PER_GROUP, D_MODEL, EXPERT_FF).astype(BF16)
    w2g = w2.reshape(N_GROUPS, GROUP_FF, D_MODEL).astype(BF16)
    pos, tile_group, n_used, tails = _moe_plan(xg, t)
    pos_hi, pos_lo = pos // SUBLANES, pos % SUBLANES
    xs = _dispatch(pos_hi, pos_lo, tails // SUBLANES, xg, t + N_GROUPS * MOE_TILE)
    ys = _experts(tile_group, n_used, xs, w1g, w3g, w2g)
    return _combine(pos_hi, pos_lo, x1, ys)


def _class_major(table):
    return table.reshape(ROWS_PER_CLASS, ATT_CLASSES, -1).transpose(1, 0, 2).reshape(SEQ, -1)


def _rope_tables_att():
    pos = jnp.arange(SEQ, dtype=F32)
    inv = ROPE_THETA ** (-jnp.arange(0, ATT_HEAD_DIM, 2, dtype=F32) / ATT_HEAD_DIM)
    ang = pos[:, None] * inv[None, :]
    cos, sin = jnp.cos(ang), jnp.sin(ang)
    cos_full = jnp.concatenate([cos, cos, cos, cos], axis=-1)
    sin_full = jnp.concatenate([-sin, -sin, sin, sin], axis=-1)
    return _class_major(cos_full), _class_major(sin_full)


def _pair_lanes(a):
    half = ATT_HEAD_DIM // 2
    lead = a.shape[:-1]
    a = a.reshape(lead + (-1, 2, 2, half))
    return jnp.swapaxes(a, -3, -2).reshape(lead + (-1,))


def _rope_tables_ret():
    pos = jnp.arange(SEQ, dtype=F32)
    inv = 1.0 / (ROPE_THETA ** jnp.linspace(0.0, 1.0, RET_QK_DIM // 2, dtype=F32))
    ang = pos[:, None] * inv[None, :]
    return jnp.cos(ang), jnp.sin(ang)


def _decay_tables():
    c = RET_CHUNK
    log_gamma = jnp.log(1.0 - jnp.exp2(-5.0 - jnp.arange(RET_HEADS, dtype=F32)))
    idx = jnp.arange(c, dtype=F32)
    diff = idx[:, None] - idx[None, :]
    decay = jnp.where(diff >= 0, jnp.exp(log_gamma[:, None, None] * jnp.maximum(diff, 0.0)), 0.0)
    zeta = jnp.exp(log_gamma[:, None] * (c - 1 - idx))
    xi = jnp.exp(log_gamma[:, None] * (idx + 1))
    gamma_c = jnp.exp(log_gamma * c)
    bc = lambda a: jnp.broadcast_to(a[:, :, None], (RET_HEADS, c, RET_QK_DIM)).astype(BF16)
    return gamma_c, decay, bc(xi), bc(zeta)


def kernel(x, g_norm_mix, w_in, b_merge_gate, g_q, g_k, w_branch_att, g_ret_norm, w_branch_ret,
           w_out, g_norm_ffn, w_router_group, b_router_group, w_router_expert, b_router_expert,
           w1, w3, w2):
    batch = x.shape[0]
    t = batch * SEQ
    cos_a, sin_a = _rope_tables_att()
    cos_r, sin_r = _rope_tables_ret()
    gamma_c, decay, xi, zeta = _decay_tables()
    xf = x.reshape(t, D_MODEL)
    for l in range(g_norm_mix.shape[0]):
        g_mix = g_norm_mix[l][None, :]
        proj_att = _inproj_att(xf, g_mix, _pair_lanes(w_in[l][:, :2 * ATT_W]).astype(BF16), w_in[l])
        proj = _inproj_rest(xf, g_mix, w_in[l], cos_r, sin_r)
        reps = LANES // ATT_HEAD_DIM
        gq = _pair_lanes(jnp.tile(g_q[l], reps))[None, :] * ATT_HEAD_DIM ** -0.5
        y_att = _attention(proj_att, cos_a, sin_a, gq, _pair_lanes(jnp.tile(g_k[l], reps))[None, :],
                           batch)
        y_ret = _retention(proj, gamma_c, decay, xi, zeta, g_ret_norm[l][:, None, :], batch)
        w_route = jnp.concatenate(
            [w_router_expert[l], w_router_group[l],
             jnp.zeros((D_MODEL, LANES - N_ROUTE), F32)], axis=-1)
        wr_hi, wr_lo = _split_bf16(w_route)
        b_route = jnp.concatenate(
            [b_router_expert[l], b_router_group[l], jnp.zeros((LANES - N_ROUTE,), F32)])[None, :]
        x1, xg = _out_stage(
            y_att, y_ret, proj, b_merge_gate[l][None, :], xf,
            w_branch_att[l].astype(BF16), w_branch_ret[l].astype(BF16), w_out[l].astype(BF16),
            g_norm_ffn[l][None, :], wr_hi, wr_lo, b_route)
        xf = _moe(xg, x1, w1[l], w3[l], w2[l])
    return xf.reshape(batch, SEQ, D_MODEL)
```

```python
import functools

import numpy as np

import jax
import jax.numpy as jnp
from jax import lax
from jax.experimental import pallas as pl
from jax.experimental.pallas import tpu as pltpu

F32 = jnp.float32
BF16 = jnp.bfloat16

D_MODEL = 1024
SEQ = 2048
ATT_HEADS = 16
ATT_HEAD_DIM = 64
ATT_W = ATT_HEADS * ATT_HEAD_DIM
ROPE_THETA = 10000.0
RET_HEADS = 4
RET_QK_DIM = 256
RET_V_DIM = 512
RET_QK_W = RET_HEADS * RET_QK_DIM
RET_V_W = RET_HEADS * RET_V_DIM
N_GROUPS = 4
EXPERTS_PER_GROUP = 8
N_EXPERTS = N_GROUPS * EXPERTS_PER_GROUP
EXPERT_FF = 256
EPS = 1e-6
ATT_IN_W = 3 * ATT_W
REST_IN_W = 2 * RET_QK_W + 2 * RET_V_W + 2 * D_MODEL

LANES = 128
SUBLANES = 8
CB_QA, CB_KA, CB_VA = 0, 8, 16
CB_QR, CB_KR, CB_VR, CB_GR, CB_GA, CB_GB = 0, 8, 16, 32, 48, 56

ATT_BLOCK = 128
ATT_CLASSES = 16
ROWS_PER_CLASS = SEQ // ATT_CLASSES
RET_CHUNK = 256
RET_UNROLL = 2
NEG_BIG = -1e30
VMEM_LIMIT = 48 * 1024 * 1024
IN_ATT_VMEM_LIMIT = 56 * 1024 * 1024


def _cparams(sem, vmem_limit=VMEM_LIMIT):
    return pltpu.CompilerParams(dimension_semantics=sem, vmem_limit_bytes=vmem_limit)


IN_TN = 1024
IN_NORM_ROWS = 256
REST_TILE_QR, REST_TILE_KR = CB_QR * LANES // IN_TN, CB_KR * LANES // IN_TN
REST_TILE_GR, REST_TILE_GA = CB_GR * LANES // IN_TN, CB_GA * LANES // IN_TN


def _rmsnorm_rows(x, g_ref):
    ms = jnp.mean(x * x, axis=-1, keepdims=True)
    return x * lax.rsqrt(ms + EPS) * g_ref[...]


def _project(xn_sc, w_ref, o_ref, epilogue, first_block=0):
    xn = xn_sc[...]
    for c2 in range(IN_TN // 256):
        w = w_ref[:, c2 * 256:(c2 + 1) * 256].astype(BF16)
        acc = jnp.dot(xn, w, preferred_element_type=F32)
        lo, hi = epilogue(acc[:, :LANES], acc[:, LANES:])
        o_ref[first_block + 2 * c2] = lo.astype(BF16)
        o_ref[first_block + 2 * c2 + 1] = hi.astype(BF16)


def _inproj_att_kernel(x_ref, g_ref, wqk_ref, wv_ref, o_ref, xn_sc, xs):
    @pl.when(pl.program_id(1) == 0)
    def _():
        def norm_rows(ci, carry):
            rows = pl.ds(pl.multiple_of(ci * IN_NORM_ROWS, IN_NORM_ROWS), IN_NORM_ROWS)
            xn = _rmsnorm_rows(x_ref[rows, :], g_ref)
            for c in range(D_MODEL // LANES):
                xs[c, rows, :] = xn[:, c * LANES:(c + 1) * LANES]
            return carry

        lax.fori_loop(0, SEQ // IN_NORM_ROWS, norm_rows, 0)

        def gather_class(j, carry):
            dst = pl.ds(pl.multiple_of(j * ROWS_PER_CLASS, ROWS_PER_CLASS), ROWS_PER_CLASS)
            for c in range(D_MODEL // LANES):
                xn_sc[dst, c * LANES:(c + 1) * LANES] = xs[
                    c, pl.ds(j, ROWS_PER_CLASS, stride=ATT_CLASSES), :].astype(BF16)
            return carry

        lax.fori_loop(0, ATT_CLASSES, gather_class, 0)

    qk_tiles = 2 * ATT_W // IN_TN

    @pl.when(pl.program_id(1) < qk_tiles)
    def _():
        _project(xn_sc, wqk_ref, o_ref, lambda lo, hi: (lo, hi))

    @pl.when(pl.program_id(1) >= qk_tiles)
    def _():
        _project(xn_sc, wv_ref, o_ref, lambda lo, hi: (lo, hi))


def _inproj_rest_kernel(x_ref, g_ref, w_ref, cos_ref, sin_ref, o_ref, xn_sc):
    j = pl.program_id(1)

    @pl.when(j == 0)
    def _():
        def norm_rows(ci, carry):
            rows = pl.ds(pl.multiple_of(ci * IN_NORM_ROWS, IN_NORM_ROWS), IN_NORM_ROWS)
            xn_sc[rows, :] = _rmsnorm_rows(x_ref[rows, :], g_ref).astype(BF16)
            return carry

        lax.fori_loop(0, SEQ // IN_NORM_ROWS, norm_rows, 0)

    def rotate(scale):
        def epilogue(x1, x2):
            cos, sin = cos_ref[...], sin_ref[...]
            return (x1 * cos - x2 * sin) * scale, (x2 * cos + x1 * sin) * scale
        return epilogue

    def swish(lo, hi):
        return lo * jax.nn.sigmoid(lo), hi * jax.nn.sigmoid(hi)

    is_gate = (j >= REST_TILE_GR) & (j < REST_TILE_GA)

    @pl.when(j == REST_TILE_QR)
    def _():
        _project(xn_sc, w_ref, o_ref, rotate(1.0))

    @pl.when(j == REST_TILE_KR)
    def _():
        _project(xn_sc, w_ref, o_ref, rotate(RET_QK_DIM ** -0.5))

    @pl.when(is_gate)
    def _():
        _project(xn_sc, w_ref, o_ref, swish)

    @pl.when((j > REST_TILE_KR) & jnp.logical_not(is_gate))
    def _():
        _project(xn_sc, w_ref, o_ref, lambda lo, hi: (lo, hi))


def _inproj_att(x2d, g, w_qk_f32, w_in_f32):
    t = x2d.shape[0]
    width = ATT_IN_W
    qk_tiles = 2 * ATT_W // IN_TN
    return pl.pallas_call(
        _inproj_att_kernel,
        grid=(t // SEQ, width // IN_TN),
        in_specs=[
            pl.BlockSpec((SEQ, D_MODEL), lambda i, j: (i, 0)),
            pl.BlockSpec((1, D_MODEL), lambda i, j: (0, 0)),
            pl.BlockSpec((D_MODEL, IN_TN), lambda i, j: (0, jnp.minimum(j, qk_tiles - 1))),
            pl.BlockSpec((D_MODEL, IN_TN), lambda i, j: (0, qk_tiles), pipeline_mode=pl.Buffered(1)),
        ],
        out_specs=pl.BlockSpec((IN_TN // LANES, SEQ, LANES), lambda i, j: (j, i, 0)),
        out_shape=jax.ShapeDtypeStruct((width // LANES, t, LANES), BF16),
        scratch_shapes=[pltpu.VMEM((SEQ, D_MODEL), BF16),
                        pltpu.VMEM((D_MODEL // LANES, SEQ, LANES), F32)],
        compiler_params=_cparams(("arbitrary", "arbitrary"), vmem_limit=IN_ATT_VMEM_LIMIT),
        name="inproj_att",
    )(x2d, g, w_qk_f32, w_in_f32)


def _inproj_rest(x2d, g, w_in_f32, cos_r, sin_r):
    t = x2d.shape[0]
    first_tile = ATT_IN_W // IN_TN
    table = pl.BlockSpec((SEQ, LANES), lambda i, j: (0, 0))
    return pl.pallas_call(
        _inproj_rest_kernel,
        grid=(t // SEQ, REST_IN_W // IN_TN),
        in_specs=[
            pl.BlockSpec((SEQ, D_MODEL), lambda i, j: (i, 0)),
            pl.BlockSpec((1, D_MODEL), lambda i, j: (0, 0)),
            pl.BlockSpec((D_MODEL, IN_TN), lambda i, j: (0, first_tile + j)),
            table, table,
        ],
        out_specs=pl.BlockSpec((IN_TN // LANES, SEQ, LANES), lambda i, j: (j, i, 0)),
        out_shape=jax.ShapeDtypeStruct((REST_IN_W // LANES, t, LANES), BF16),
        scratch_shapes=[pltpu.VMEM((SEQ, D_MODEL), BF16)],
        compiler_params=_cparams(("arbitrary", "arbitrary")),
        name="inproj_rest",
    )(x2d, g, w_in_f32, cos_r, sin_r)


PREP_ROWS = 256
PREP_UNROLL = 4
SEG4 = ATT_BLOCK // 4
SEG1 = ATT_BLOCK // ATT_CLASSES


def _att_bias_tables():
    def tile(qpos, kpos):
        d = qpos[:, None] - kpos[None, :]
        one = np.where((d >= 0) & (d <= ATT_BLOCK), 0.0, NEG_BIG).astype(np.float32)
        return np.concatenate([one, one], axis=1)

    u = np.arange(ATT_BLOCK)
    q4 = 4 * (u % SEG4) + u // SEG4
    q1 = ATT_CLASSES * (u % SEG1) + u // SEG1
    w = np.arange(2 * ATT_BLOCK)
    k1 = ATT_CLASSES * (w % SEG1) + w // (2 * SEG1) + ATT_BLOCK * ((w // SEG1) % 2 - 1)
    return (tile(q1, k1), tile(q1, q1), tile(q4, np.concatenate([q4 - ATT_BLOCK, q4])), tile(q4, q4),
            tile(u, u))


def masked_pair(x, head0_mask):
    return jnp.where(head0_mask, x, 0.0).astype(BF16), jnp.where(head0_mask, 0.0, x).astype(BF16)


def _att_kernel(q_ref, k_ref, v_ref, cos_ref, sin_ref, gq_ref, gk_ref,
                b1_ref, b1f_ref, b4_ref, b4f_ref, b16_ref, o_ref,
                qj, kj, vj, qb, kb, vb, acc_sc, m_sc, l_sc, nat):
    lane = lax.broadcasted_iota(jnp.int32, (1, LANES), 1)
    half = ATT_HEAD_DIM // 2
    qk_head0 = (lane // half) % 2 == 0
    v_head0 = lane < ATT_HEAD_DIM
    qk_keep0, qk_keep1 = masked_pair(jnp.ones((1, LANES), F32), qk_head0)
    v_keep0, v_keep1 = masked_pair(jnp.ones((1, LANES), F32), v_head0)
    seg = jnp.where((lax.broadcasted_iota(jnp.int32, (LANES, LANES), 0) // half) % 2
                    == (lax.broadcasted_iota(jnp.int32, (LANES, LANES), 1) // half) % 2,
                    1.0 / ATT_HEAD_DIM, 0.0).astype(BF16)

    def prep(ci, carry):
        rows = pl.ds(pl.multiple_of(ci * PREP_ROWS, PREP_ROWS), PREP_ROWS)
        cos = cos_ref[rows, :]
        sin = sin_ref[rows, :]

        def norm_rope(src, g_ref):
            x = src[0, rows, :].astype(F32)
            ms = jnp.dot((x * x).astype(BF16), seg, preferred_element_type=F32)
            xn = x * lax.rsqrt(ms + EPS) * g_ref[...]
            return xn * cos + pltpu.roll(xn, ATT_HEAD_DIM, 1) * sin

        xq = norm_rope(q_ref, gq_ref)
        qj[rows, :] = xq
        qb[rows, :] = xq.astype(BF16)
        xk = norm_rope(k_ref, gk_ref)
        kj[rows, :] = xk
        xk = xk.astype(BF16)
        kb[0, rows, :] = xk * qk_keep0
        kb[1, rows, :] = xk * qk_keep1
        xv = v_ref[0, rows, :]
        vj[rows, :] = xv.astype(F32)
        vb[0, rows, :] = xv * v_keep0
        vb[1, rows, :] = xv * v_keep1
        return carry

    lax.fori_loop(0, SEQ // PREP_ROWS, prep, 0, unroll=PREP_UNROLL)

    nt_dims = (((1,), (1,)), ((), ()))

    def attend(q, k0, k1, v0, v1, bias):
        w = k0.shape[0]
        s = lax.dot_general(q, jnp.concatenate([k0, k1], axis=0), nt_dims,
                            preferred_element_type=F32) + bias
        m0 = jnp.max(s[:, :w], axis=-1, keepdims=True)
        m1 = jnp.max(s[:, w:], axis=-1, keepdims=True)
        e = jnp.concatenate([jnp.exp(s[:, :w] - m0), jnp.exp(s[:, w:] - m1)], axis=1).astype(BF16)
        v2 = jnp.concatenate([jnp.concatenate([v0, jnp.broadcast_to(v_keep0, (w, LANES))], axis=1),
                              jnp.concatenate([v1, jnp.broadcast_to(v_keep1, (w, LANES))], axis=1)], axis=0)
        r = jnp.dot(e, v2, preferred_element_type=F32)
        return r[:, :LANES], jnp.where(v_head0, m0, m1), r[:, LANES:]

    def gather(ref, pieces, lead=()):
        return jnp.concatenate([ref[lead + (rows, slice(None))] for rows in pieces], axis=0)


    def store(p, pieces, n, a, m, l):
        for idx, rows in enumerate(pieces):
            acc_sc[p, rows, :] = a[idx * n:(idx + 1) * n]
            m_sc[p, rows, :] = m[idx * n:(idx + 1) * n]
            l_sc[p, rows, :] = l[idx * n:(idx + 1) * n]

    for j in range(ATT_CLASSES):
        rows = [pl.ds(j * ATT_BLOCK, ATT_BLOCK)]
        a, m, l = attend(qb[rows[0], :], kb[0, rows[0], :], kb[1, rows[0], :],
                         vb[0, rows[0], :], vb[1, rows[0], :], b16_ref[...])
        store(1, rows, ATT_BLOCK, a, m, l)

    for c in range(4):
        def segs(n):
            return [pl.ds((4 * a + c) * ATT_BLOCK + SEG4 * n, SEG4) for a in range(4)]

        for n in range(SEQ // 4 // ATT_BLOCK):
            cur = segs(n)
            keys = cur if n == 0 else segs(n - 1) + cur
            bias = b4f_ref[...] if n == 0 else b4_ref[...]
            a, m, l = attend(gather(qb, cur), gather(kb, keys, (0,)), gather(kb, keys, (1,)),
                             gather(vb, keys, (0,)), gather(vb, keys, (1,)), bias)
            store(0, cur, SEG4, a, m, l)

    for n in range(SEQ // ATT_BLOCK):
        cur = [pl.ds(j * ATT_BLOCK + SEG1 * n, SEG1) for j in range(ATT_CLASSES)]
        if n == 0:
            keys, bias = cur, b1f_ref[...]
        else:
            keys = [pl.ds(j * ATT_BLOCK + SEG1 * n - SEG1, 2 * SEG1) for j in range(ATT_CLASSES)]
            bias = b1_ref[...]
        k0, k1 = masked_pair(gather(kj, keys), qk_head0)
        v0, v1 = masked_pair(gather(vj, keys), v_head0)
        a, m, l = attend(gather(qj, cur).astype(BF16), k0, k1, v0, v1, bias)
        stats = [(a, m, l)] + [(gather(acc_sc, cur, (p,)), gather(m_sc, cur, (p,)), gather(l_sc, cur, (p,)))
                               for p in range(2)]
        m_all = jnp.maximum(jnp.maximum(stats[0][1], stats[1][1]), stats[2][1])
        num = jnp.zeros((ATT_BLOCK, LANES), F32)
        den = jnp.zeros((ATT_BLOCK, LANES), F32)
        for ap, mp, lp in stats:
            w = jnp.exp(mp - m_all)
            num = num + w * ap
            den = den + w * lp
        merged = num / den
        for j in range(ATT_CLASSES):
            nat[pl.ds(ATT_BLOCK * n + j, SEG1, stride=ATT_CLASSES), :] = merged[j * SEG1:(j + 1) * SEG1]

    def emit(ci, carry):
        rows = pl.ds(pl.multiple_of(ci * PREP_ROWS, PREP_ROWS), PREP_ROWS)
        o_ref[rows, :] = nat[rows, :].astype(BF16)
        return carry

    lax.fori_loop(0, SEQ // PREP_ROWS, emit, 0)


def _attention(proj_att, cos_a, sin_a, gq, gk, batch):
    t = batch * SEQ
    hp = ATT_W // LANES

    def col(cb):
        return pl.BlockSpec((1, SEQ, LANES), lambda b, h: (cb + h, b, 0))

    def const(shape):
        return pl.BlockSpec(shape, lambda b, h: tuple(0 for _ in shape))

    biases = [jnp.asarray(b) for b in _att_bias_tables()]
    row_f32 = pltpu.VMEM((SEQ, LANES), F32)
    row_bf16 = pltpu.VMEM((SEQ, LANES), BF16)
    stat = pltpu.VMEM((2, SEQ, LANES), F32)
    return pl.pallas_call(
        _att_kernel,
        grid=(batch, hp),
        in_specs=[col(CB_QA), col(CB_KA), col(CB_VA),
                  const((SEQ, LANES)), const((SEQ, LANES)), const((1, LANES)), const((1, LANES))]
                 + [const(b.shape) for b in biases],
        out_specs=pl.BlockSpec((SEQ, LANES), lambda b, h: (b, h)),
        out_shape=jax.ShapeDtypeStruct((t, ATT_W), BF16),
        scratch_shapes=([row_f32] * 3 + [row_bf16] + [pltpu.VMEM((2, SEQ, LANES), BF16)] * 2 + [stat] * 3
                        + [row_f32]),
        compiler_params=_cparams(("arbitrary", "arbitrary")),
        name="dilated_attention",
    )(proj_att, proj_att, proj_att, cos_a, sin_a, gq, gk, *biases)


def _ret_kernel(gam_ref, q_ref, k_ref, v_ref, g_ref, decay_ref, xi_ref, zeta_ref,
                gn_ref, o_ref, state_sc):
    h = pl.program_id(1)
    gamma_c = gam_ref[h]
    state_sc[...] = jnp.zeros_like(state_sc)
    c = RET_CHUNK
    nt_dims = (((1,), (1,)), ((), ()))
    tn_dims = (((0,), (0,)), ((), ()))

    def cols(ref, rs, n):
        return jnp.concatenate([ref[i, rs, :] for i in range(n)], axis=-1)

    def chunk(n, carry):
        rs = pl.ds(pl.multiple_of(n * c, c), c)
        q = cols(q_ref, rs, RET_QK_DIM // LANES)
        k = cols(k_ref, rs, RET_QK_DIM // LANES)
        v = cols(v_ref, rs, RET_V_DIM // LANES)
        inner = lax.dot_general(q, k, nt_dims, preferred_element_type=F32) * decay_ref[0]
        y = jnp.dot(inner.astype(BF16), v, preferred_element_type=F32)
        state = state_sc[...]
        y = y + jnp.dot(q * xi_ref[0], state.astype(BF16), preferred_element_type=F32)
        state_sc[...] = state * gamma_c + lax.dot_general(k * zeta_ref[0], v, tn_dims,
                                                          preferred_element_type=F32)
        yn = y * lax.rsqrt(jnp.mean(y * y, axis=-1, keepdims=True) + EPS) * gn_ref[0]
        o_ref[rs, :] = (yn * cols(g_ref, rs, RET_V_DIM // LANES).astype(F32)).astype(BF16)
        return carry

    lax.fori_loop(0, SEQ // c, chunk, 0, unroll=RET_UNROLL)


def _retention(proj, gamma_c, decay, xi, zeta, g_ret, batch):
    t = batch * SEQ
    nq = RET_QK_DIM // LANES
    nv = RET_V_DIM // LANES

    def cols(cb, n):
        return pl.BlockSpec((n, SEQ, LANES), lambda b, h: (cb // n + h, b, 0))

    return pl.pallas_call(
        _ret_kernel,
        grid=(batch, RET_HEADS),
        in_specs=[
            pl.BlockSpec(memory_space=pltpu.SMEM),
            cols(CB_QR, nq), cols(CB_KR, nq), cols(CB_VR, nv), cols(CB_GR, nv),
            pl.BlockSpec((1, RET_CHUNK, RET_CHUNK), lambda b, h: (h, 0, 0)),
            pl.BlockSpec((1, RET_CHUNK, RET_QK_DIM), lambda b, h: (h, 0, 0)),
            pl.BlockSpec((1, RET_CHUNK, RET_QK_DIM), lambda b, h: (h, 0, 0)),
            pl.BlockSpec((1, 1, RET_V_DIM), lambda b, h: (h, 0, 0)),
        ],
        out_specs=pl.BlockSpec((SEQ, RET_V_DIM), lambda b, h: (b, h)),
        out_shape=jax.ShapeDtypeStruct((t, RET_V_W), BF16),
        scratch_shapes=[pltpu.VMEM((RET_QK_DIM, RET_V_DIM), F32)],
        compiler_params=_cparams(("arbitrary", "arbitrary")),
        name="retention",
    )(gamma_c, proj, proj, proj, proj, decay, xi, zeta, g_ret)


OUT_TM = 512
N_ROUTE = N_EXPERTS + N_GROUPS
GROUP_LANE = LANES - 1
XG_W = D_MODEL + LANES


def _split_bf16(x):
    hi = x.astype(BF16)
    lo = (x - hi.astype(F32)).astype(BF16)
    return hi, lo


def _out_kernel(ya_ref, yr_ref, ga_ref, gb_ref, bg_ref, x_ref, watt_ref, wret_ref, wout_ref,
                gffn_ref, wr_hi_ref, wr_lo_ref, br_ref, x1_ref, xg_ref):
    ya = jnp.dot(ya_ref[...], watt_ref[...], preferred_element_type=F32)
    yr = jnp.dot(yr_ref[...], wret_ref[...], preferred_element_type=F32)
    merged = []
    for cb in range(D_MODEL // LANES):
        cs = slice(cb * LANES, (cb + 1) * LANES)
        ga = jax.nn.sigmoid(ga_ref[cb].astype(F32) + bg_ref[:, cs])
        gb = jax.nn.sigmoid(gb_ref[cb].astype(F32) + bg_ref[:, D_MODEL + cb * LANES:D_MODEL + (cb + 1) * LANES])
        merged.append((ga * ya[:, cs] + gb * yr[:, cs]).astype(BF16))
    merged = jnp.concatenate(merged, axis=-1)
    x1 = x_ref[...] + jnp.dot(merged, wout_ref[...], preferred_element_type=F32)
    x1_ref[...] = x1
    xn = x1 * lax.rsqrt(jnp.mean(x1 * x1, axis=-1, keepdims=True) + EPS) * gffn_ref[...]
    xg_ref[:, :D_MODEL] = xn

    xh, xl = _split_bf16(xn)
    logits = (jnp.dot(xh, wr_hi_ref[...], preferred_element_type=F32)
              + jnp.dot(xl, wr_hi_ref[...], preferred_element_type=F32)
              + jnp.dot(xh, wr_lo_ref[...], preferred_element_type=F32)) + br_ref[...]
    lane = lax.broadcasted_iota(jnp.int32, logits.shape, 1)
    lane_f = lane.astype(F32)
    is_group = (lane >= N_EXPERTS) & (lane < N_ROUTE)
    gl = jnp.where(is_group, logits, NEG_BIG)
    gmax = jnp.max(gl, axis=-1, keepdims=True)
    gsel = jnp.min(jnp.where(gl == gmax, lane_f, 1e9), axis=-1, keepdims=True) - N_EXPERTS
    p_group = 1.0 / jnp.sum(jnp.where(is_group, jnp.exp(logits - gmax), 0.0), axis=-1, keepdims=True)
    lo_lane = gsel * EXPERTS_PER_GROUP
    in_group = (lane_f >= lo_lane) & (lane_f < lo_lane + EXPERTS_PER_GROUP)
    el = jnp.where(in_group, logits, NEG_BIG)
    v1 = jnp.max(el, axis=-1, keepdims=True)
    i1 = jnp.min(jnp.where(el == v1, lane_f, 1e9), axis=-1, keepdims=True)
    el2 = jnp.where(lane_f == i1, NEG_BIG, el)
    v2 = jnp.max(el2, axis=-1, keepdims=True)
    i2 = jnp.min(jnp.where(el2 == v2, lane_f, 1e9), axis=-1, keepdims=True)
    e21 = jnp.exp(v2 - v1)
    w1 = p_group / (1.0 + e21)
    w2 = w1 * e21
    gate = jnp.where(lane_f == i1, w1, 0.0) + jnp.where(lane_f == i2, w2, 0.0)
    xg_ref[:, D_MODEL:] = jnp.where(lane == GROUP_LANE, gsel, gate)


def _out_stage(y_att, y_ret, proj, b_gate, x2d, w_att, w_ret, w_out, g_ffn, wr_hi, wr_lo, b_route):
    t = x2d.shape[0]
    ncb = D_MODEL // LANES

    def full(shape):
        return pl.BlockSpec(shape, lambda i: tuple(0 for _ in shape))

    return pl.pallas_call(
        _out_kernel,
        grid=(t // OUT_TM,),
        in_specs=[
            pl.BlockSpec((OUT_TM, ATT_W), lambda i: (i, 0)),
            pl.BlockSpec((OUT_TM, RET_V_W), lambda i: (i, 0)),
            pl.BlockSpec((ncb, OUT_TM, LANES), lambda i: (CB_GA // ncb, i, 0)),
            pl.BlockSpec((ncb, OUT_TM, LANES), lambda i: (CB_GB // ncb, i, 0)),
            full((1, 2 * D_MODEL)),
            pl.BlockSpec((OUT_TM, D_MODEL), lambda i: (i, 0)),
            full((ATT_W, D_MODEL)), full((RET_V_W, D_MODEL)), full((D_MODEL, D_MODEL)),
            full((1, D_MODEL)), full((D_MODEL, LANES)), full((D_MODEL, LANES)), full((1, LANES)),
        ],
        out_specs=[
            pl.BlockSpec((OUT_TM, D_MODEL), lambda i: (i, 0)),
            pl.BlockSpec((OUT_TM, XG_W), lambda i: (i, 0)),
        ],
        out_shape=[
            jax.ShapeDtypeStruct((t, D_MODEL), F32),
            jax.ShapeDtypeStruct((t, XG_W), F32),
        ],
        compiler_params=_cparams(("arbitrary",)),
        name="out_stage",
    )(y_att, y_ret, proj, proj, b_gate, x2d, w_att, w_ret, w_out, g_ffn, wr_hi, wr_lo, b_route)


MOE_TILE = 512
MOVE_TM = 1024
GROUP_FF = EXPERTS_PER_GROUP * EXPERT_FF


def _moe_plan(xg, t):
    i32 = jnp.int32
    g = xg[:, D_MODEL + GROUP_LANE].astype(i32)
    onehot = (g[:, None] == jnp.arange(N_GROUPS, dtype=i32)[None, :]).astype(i32)
    csum = jnp.cumsum(onehot, axis=0)
    rank = jnp.sum(onehot * csum, axis=1) - 1
    padded = (csum[-1] + MOE_TILE - 1) // MOE_TILE * MOE_TILE
    ends = jnp.cumsum(padded)
    pos = rank + jnp.sum(onehot * (ends - padded)[None, :], axis=1)
    tile_start = jnp.arange(t // MOE_TILE + N_GROUPS, dtype=i32) * MOE_TILE
    tile_group = jnp.minimum(jnp.sum((tile_start[:, None] >= ends[None, :]).astype(i32), axis=1),
                             N_GROUPS - 1)
    tails = jnp.maximum(ends - MOE_TILE, 0)
    return pos.astype(i32), tile_group.astype(i32), (ends[-1:] // MOE_TILE).astype(i32), tails.astype(i32)


def _dispatch_kernel(hi_ref, lo_ref, tail_ref, xg_ref, xs_ref, zeros, sem, zero_sem):
    base = pl.program_id(0) * MOVE_TM

    @pl.when(pl.program_id(0) == 0)
    def _():
        zeros[...] = jnp.zeros_like(zeros)
        tile = MOE_TILE // SUBLANES
        spare = [xs_ref.shape[0] - (g + 1) * tile for g in range(N_GROUPS)]
        for start in [tail_ref[g] for g in range(N_GROUPS)] + spare:
            fill = pltpu.make_async_copy(zeros, xs_ref.at[pl.ds(start, tile)], zero_sem)
            fill.start()
            fill.wait()

    def send(i, carry):
        for u in range(SUBLANES):
            r = base + i * SUBLANES + u
            pltpu.make_async_copy(xg_ref.at[i, pl.ds(u, 1)],
                                  xs_ref.at[hi_ref[r], pl.ds(lo_ref[r], 1)], sem).start()
        return carry

    lax.fori_loop(0, MOVE_TM // SUBLANES, send, 0)
    pltpu.make_async_copy(xg_ref, xs_ref.at[pl.ds(0, MOVE_TM // SUBLANES)], sem).wait()


def _dispatch(pos_hi, pos_lo, tail_hi, xg, n_rows):
    t = xg.shape[0]
    xs = pl.pallas_call(
        _dispatch_kernel,
        grid_spec=pltpu.PrefetchScalarGridSpec(
            num_scalar_prefetch=3,
            grid=(t // MOVE_TM,),
            in_specs=[pl.BlockSpec((MOVE_TM // SUBLANES, SUBLANES, XG_W), lambda i, hi, lo, tl: (i, 0, 0))],
            out_specs=pl.BlockSpec(memory_space=pl.ANY),
            scratch_shapes=[pltpu.VMEM((MOE_TILE // SUBLANES, SUBLANES, XG_W), F32),
                            pltpu.SemaphoreType.DMA, pltpu.SemaphoreType.DMA],
        ),
        out_shape=jax.ShapeDtypeStruct((n_rows // SUBLANES, SUBLANES, XG_W), F32),
        compiler_params=_cparams(("arbitrary",)),
        name="moe_dispatch",
    )(pos_hi, pos_lo, tail_hi, xg.reshape(t // SUBLANES, SUBLANES, XG_W))
    return xs.reshape(n_rows, XG_W)


def _experts_kernel(tg_ref, nused_ref, xs_ref, w1_ref, w3_ref, w2_ref, ys_ref):
    i = pl.program_id(0)

    @pl.when(i < nused_ref[0])
    def _():
        x = xs_ref[:, :D_MODEL].astype(BF16)
        gate = xs_ref[:, D_MODEL:]
        lane = lax.broadcasted_iota(jnp.int32, gate.shape, 1)
        first = tg_ref[i] * EXPERTS_PER_GROUP
        hidden = []
        for e in range(EXPERTS_PER_GROUP):
            a = jnp.dot(x, w1_ref[0, e], preferred_element_type=F32)
            b = jnp.dot(x, w3_ref[0, e], preferred_element_type=F32)
            g = jnp.sum(jnp.where(lane == first + e, gate, 0.0), axis=-1, keepdims=True)
            hidden.append((a * jax.nn.sigmoid(a) * b * g).astype(BF16))
        ys_ref[...] = jnp.dot(jnp.concatenate(hidden, axis=-1), w2_ref[0], preferred_element_type=F32)

    @pl.when(i >= nused_ref[0])
    def _():
        ys_ref[...] = jnp.zeros_like(ys_ref)


def _experts(tile_group, n_used, xs, w1g, w3g, w2g):
    n_rows = xs.shape[0]
    up = pl.BlockSpec((1, EXPERTS_PER_GROUP, D_MODEL, EXPERT_FF), lambda i, tg, nu: (tg[i], 0, 0, 0))
    return pl.pallas_call(
        _experts_kernel,
        grid_spec=pltpu.PrefetchScalarGridSpec(
            num_scalar_prefetch=2,
            grid=(n_rows // MOE_TILE,),
            in_specs=[
                pl.BlockSpec((MOE_TILE, XG_W), lambda i, tg, nu: (i, 0)),
                up, up,
                pl.BlockSpec((1, GROUP_FF, D_MODEL), lambda i, tg, nu: (tg[i], 0, 0)),
            ],
            out_specs=pl.BlockSpec((MOE_TILE, D_MODEL), lambda i, tg, nu: (i, 0)),
        ),
        out_shape=jax.ShapeDtypeStruct((n_rows, D_MODEL), F32),
        compiler_params=_cparams(("arbitrary",)),
        name="moe_experts",
    )(tile_group, n_used, xs, w1g, w3g, w2g)


def _combine_kernel(hi_ref, lo_ref, x1_ref, ys_ref, o_ref, buf, sem):
    base = pl.program_id(0) * MOVE_TM

    def fetch(i, carry):
        for u in range(SUBLANES):
            r = base + i * SUBLANES + u
            pltpu.make_async_copy(ys_ref.at[hi_ref[r], pl.ds(lo_ref[r], 1)],
                                  buf.at[i, pl.ds(u, 1)], sem).start()
        return carry

    lax.fori_loop(0, MOVE_TM // SUBLANES, fetch, 0)
    pltpu.make_async_copy(ys_ref.at[pl.ds(0, MOVE_TM // SUBLANES)], buf, sem).wait()
    o_ref[...] = x1_ref[...] + buf[...]


def _combine(pos_hi, pos_lo, x1, ys):
    t = x1.shape[0]
    tile = (MOVE_TM // SUBLANES, SUBLANES, D_MODEL)
    out = pl.pallas_call(
        _combine_kernel,
        grid_spec=pltpu.PrefetchScalarGridSpec(
            num_scalar_prefetch=2,
            grid=(t // MOVE_TM,),
            in_specs=[pl.BlockSpec(tile, lambda i, hi, lo: (i, 0, 0)),
                      pl.BlockSpec(memory_space=pl.ANY)],
            out_specs=pl.BlockSpec(tile, lambda i, hi, lo: (i, 0, 0)),
            scratch_shapes=[pltpu.VMEM(tile, F32), pltpu.SemaphoreType.DMA],
        ),
        out_shape=jax.ShapeDtypeStruct((t // SUBLANES, SUBLANES, D_MODEL), F32),
        compiler_params=_cparams(("arbitrary",)),
        name="moe_combine",
    )(pos_hi, pos_lo, x1.reshape(t // SUBLANES, SUBLANES, D_MODEL),
      ys.reshape(ys.shape[0] // SUBLANES, SUBLANES, D_MODEL))
    return out.reshape(t, D_MODEL)


def _moe(xg, x1, w1, w3, w2):
    t = xg.shape[0]
    w1g = w1.reshape(N_GROUPS, EXPERTS_PER_GROUP, D_MODEL, EXPERT_FF).astype(BF16)
    w3g = w3.reshape(N_GROUPS, EXPERTS_PER_GROUP, D_MODEL, EXPERT_FF).astype(BF16)
    w2g = w2.reshape(N_GROUPS, GROUP_FF, D_MODEL).astype(BF16)
    pos, tile_group, n_used, tails = _moe_plan(xg, t)
    pos_hi, pos_lo = pos // SUBLANES, pos % SUBLANES
    xs = _dispatch(pos_hi, pos_lo, tails // SUBLANES, xg, t + N_GROUPS * MOE_TILE)
    ys = _experts(tile_group, n_used, xs, w1g, w3g, w2g)
    return _combine(pos_hi, pos_lo, x1, ys)


def _class_major(table):
    return table.reshape(ROWS_PER_CLASS, ATT_CLASSES, -1).transpose(1, 0, 2).reshape(SEQ, -1)


def _rope_tables_att():
    pos = jnp.arange(SEQ, dtype=F32)
    inv = ROPE_THETA ** (-jnp.arange(0, ATT_HEAD_DIM, 2, dtype=F32) / ATT_HEAD_DIM)
    ang = pos[:, None] * inv[None, :]
    cos, sin = jnp.cos(ang), jnp.sin(ang)
    cos_full = jnp.concatenate([cos, cos, cos, cos], axis=-1)
    sin_full = jnp.concatenate([-sin, -sin, sin, sin], axis=-1)
    return _class_major(cos_full), _class_major(sin_full)


def _pair_lanes(a):
    half = ATT_HEAD_DIM // 2
    lead = a.shape[:-1]
    a = a.reshape(lead + (-1, 2, 2, half))
    return jnp.swapaxes(a, -3, -2).reshape(lead + (-1,))


def _rope_tables_ret():
    pos = jnp.arange(SEQ, dtype=F32)
    inv = 1.0 / (ROPE_THETA ** jnp.linspace(0.0, 1.0, RET_QK_DIM // 2, dtype=F32))
    ang = pos[:, None] * inv[None, :]
    return jnp.cos(ang), jnp.sin(ang)


def _decay_tables():
    c = RET_CHUNK
    log_gamma = jnp.log(1.0 - jnp.exp2(-5.0 - jnp.arange(RET_HEADS, dtype=F32)))
    idx = jnp.arange(c, dtype=F32)
    diff = idx[:, None] - idx[None, :]
    decay = jnp.where(diff >= 0, jnp.exp(log_gamma[:, None, None] * jnp.maximum(diff, 0.0)), 0.0)
    zeta = jnp.exp(log_gamma[:, None] * (c - 1 - idx))
    xi = jnp.exp(log_gamma[:, None] * (idx + 1))
    gamma_c = jnp.exp(log_gamma * c)
    bc = lambda a: jnp.broadcast_to(a[:, :, None], (RET_HEADS, c, RET_QK_DIM)).astype(BF16)
    return gamma_c, decay, bc(xi), bc(zeta)


def kernel(x, g_norm_mix, w_in, b_merge_gate, g_q, g_k, w_branch_att, g_ret_norm, w_branch_ret,
           w_out, g_norm_ffn, w_router_group, b_router_group, w_router_expert, b_router_expert,
           w1, w3, w2):
    batch = x.shape[0]
    t = batch * SEQ
    cos_a, sin_a = _rope_tables_att()
    cos_r, sin_r = _rope_tables_ret()
    gamma_c, decay, xi, zeta = _decay_tables()
    xf = x.reshape(t, D_MODEL)
    for l in range(g_norm_mix.shape[0]):
        g_mix = g_norm_mix[l][None, :]
        proj_att = _inproj_att(xf, g_mix, _pair_lanes(w_in[l][:, :2 * ATT_W]).astype(BF16), w_in[l])
        proj = _inproj_rest(xf, g_mix, w_in[l], cos_r, sin_r)
        reps = LANES // ATT_HEAD_DIM
        gq = _pair_lanes(jnp.tile(g_q[l], reps))[None, :] * ATT_HEAD_DIM ** -0.5
        y_att = _attention(proj_att, cos_a, sin_a, gq, _pair_lanes(jnp.tile(g_k[l], reps))[None, :],
                           batch)
        y_ret = _retention(proj, gamma_c, decay, xi, zeta, g_ret_norm[l][:, None, :], batch)
        w_route = jnp.concatenate(
            [w_router_expert[l], w_router_group[l],
             jnp.zeros((D_MODEL, LANES - N_ROUTE), F32)], axis=-1)
        wr_hi, wr_lo = _split_bf16(w_route)
        b_route = jnp.concatenate(
            [b_router_expert[l], b_router_group[l], jnp.zeros((LANES - N_ROUTE,), F32)])[None, :]
        x1, xg = _out_stage(
            y_att, y_ret, proj, b_merge_gate[l][None, :], xf,
            w_branch_att[l].astype(BF16), w_branch_ret[l].astype(BF16), w_out[l].astype(BF16),
            g_norm_ffn[l][None, :], wr_hi, wr_lo, b_route)
        xf = _moe(xg, x1, w1[l], w3[l], w2[l])
    return xf.reshape(batch, SEQ, D_MODEL)
```

```python
import functools

import numpy as np

import jax
import jax.numpy as jnp
from jax import lax
from jax.experimental import pallas as pl
from jax.experimental.pallas import tpu as pltpu

F32 = jnp.float32
BF16 = jnp.bfloat16

D_MODEL = 1024
SEQ = 2048
ATT_HEADS = 16
ATT_HEAD_DIM = 64
ATT_W = ATT_HEADS * ATT_HEAD_DIM
ROPE_THETA = 10000.0
RET_HEADS = 4
RET_QK_DIM = 256
RET_V_DIM = 512
RET_QK_W = RET_HEADS * RET_QK_DIM
RET_V_W = RET_HEADS * RET_V_DIM
N_GROUPS = 4
EXPERTS_PER_GROUP = 8
N_EXPERTS = N_GROUPS * EXPERTS_PER_GROUP
EXPERT_FF = 256
EPS = 1e-6
ATT_IN_W = 3 * ATT_W
REST_IN_W = 2 * RET_QK_W + 2 * RET_V_W + 2 * D_MODEL

LANES = 128
SUBLANES = 8
CB_QA, CB_KA, CB_VA = 0, 8, 16
CB_QR, CB_KR, CB_VR, CB_GR, CB_GA, CB_GB = 0, 8, 16, 32, 48, 56

ATT_BLOCK = 128
ATT_CLASSES = 16
ROWS_PER_CLASS = SEQ // ATT_CLASSES
RET_CHUNK = 256
RET_UNROLL = 2
NEG_BIG = -1e30
VMEM_LIMIT = 48 * 1024 * 1024
IN_ATT_VMEM_LIMIT = 56 * 1024 * 1024


def _cparams(sem, vmem_limit=VMEM_LIMIT):
    return pltpu.CompilerParams(dimension_semantics=sem, vmem_limit_bytes=vmem_limit)


IN_TN = 1024
IN_NORM_ROWS = 256
REST_TILE_QR, REST_TILE_KR = CB_QR * LANES // IN_TN, CB_KR * LANES // IN_TN
REST_TILE_GR, REST_TILE_GA = CB_GR * LANES // IN_TN, CB_GA * LANES // IN_TN


def _sigmoid(x):
    return 0.5 * jnp.tanh(0.5 * x) + 0.5


def _rmsnorm_rows(x, g_ref):
    ms = jnp.mean(x * x, axis=-1, keepdims=True)
    return x * lax.rsqrt(ms + EPS) * g_ref[...]


def _project(xn_sc, w_ref, o_ref, epilogue, first_block=0):
    xn = xn_sc[...]
    for c2 in range(IN_TN // 256):
        w = w_ref[:, c2 * 256:(c2 + 1) * 256].astype(BF16)
        acc = jnp.dot(xn, w, preferred_element_type=F32)
        lo, hi = epilogue(acc[:, :LANES], acc[:, LANES:])
        o_ref[first_block + 2 * c2] = lo.astype(BF16)
        o_ref[first_block + 2 * c2 + 1] = hi.astype(BF16)


def _inproj_att_kernel(x_ref, g_ref, wqk_ref, wv_ref, o_ref, xn_sc, xs):
    @pl.when(pl.program_id(1) == 0)
    def _():
        def norm_rows(ci, carry):
            rows = pl.ds(pl.multiple_of(ci * IN_NORM_ROWS, IN_NORM_ROWS), IN_NORM_ROWS)
            xn = _rmsnorm_rows(x_ref[rows, :], g_ref)
            for c in range(D_MODEL // LANES):
                xs[c, rows, :] = xn[:, c * LANES:(c + 1) * LANES]
            return carry

        lax.fori_loop(0, SEQ // IN_NORM_ROWS, norm_rows, 0)

        def gather_class(j, carry):
            dst = pl.ds(pl.multiple_of(j * ROWS_PER_CLASS, ROWS_PER_CLASS), ROWS_PER_CLASS)
            for c in range(D_MODEL // LANES):
                xn_sc[dst, c * LANES:(c + 1) * LANES] = xs[
                    c, pl.ds(j, ROWS_PER_CLASS, stride=ATT_CLASSES), :].astype(BF16)
            return carry

        lax.fori_loop(0, ATT_CLASSES, gather_class, 0)

    qk_tiles = 2 * ATT_W // IN_TN

    @pl.when(pl.program_id(1) < qk_tiles)
    def _():
        _project(xn_sc, wqk_ref, o_ref, lambda lo, hi: (lo, hi))

    @pl.when(pl.program_id(1) >= qk_tiles)
    def _():
        _project(xn_sc, wv_ref, o_ref, lambda lo, hi: (lo, hi))


def _inproj_rest_kernel(x_ref, g_ref, w_ref, cos_ref, sin_ref, o_ref, xn_sc):
    j = pl.program_id(1)

    @pl.when(j == 0)
    def _():
        def norm_rows(ci, carry):
            rows = pl.ds(pl.multiple_of(ci * IN_NORM_ROWS, IN_NORM_ROWS), IN_NORM_ROWS)
            xn_sc[rows, :] = _rmsnorm_rows(x_ref[rows, :], g_ref).astype(BF16)
            return carry

        lax.fori_loop(0, SEQ // IN_NORM_ROWS, norm_rows, 0)

    def rotate(scale):
        def epilogue(x1, x2):
            cos, sin = cos_ref[...], sin_ref[...]
            return (x1 * cos - x2 * sin) * scale, (x2 * cos + x1 * sin) * scale
        return epilogue

    def swish(lo, hi):
        return lo * _sigmoid(lo), hi * _sigmoid(hi)

    is_gate = (j >= REST_TILE_GR) & (j < REST_TILE_GA)

    @pl.when(j == REST_TILE_QR)
    def _():
        _project(xn_sc, w_ref, o_ref, rotate(1.0))

    @pl.when(j == REST_TILE_KR)
    def _():
        _project(xn_sc, w_ref, o_ref, rotate(RET_QK_DIM ** -0.5))

    @pl.when(is_gate)
    def _():
        _project(xn_sc, w_ref, o_ref, swish)

    @pl.when((j > REST_TILE_KR) & jnp.logical_not(is_gate))
    def _():
        _project(xn_sc, w_ref, o_ref, lambda lo, hi: (lo, hi))


def _inproj_att(x2d, g, w_qk_f32, w_in_f32):
    t = x2d.shape[0]
    width = ATT_IN_W
    qk_tiles = 2 * ATT_W // IN_TN
    return pl.pallas_call(
        _inproj_att_kernel,
        grid=(t // SEQ, width // IN_TN),
        in_specs=[
            pl.BlockSpec((SEQ, D_MODEL), lambda i, j: (i, 0)),
            pl.BlockSpec((1, D_MODEL), lambda i, j: (0, 0)),
            pl.BlockSpec((D_MODEL, IN_TN), lambda i, j: (0, jnp.minimum(j, qk_tiles - 1))),
            pl.BlockSpec((D_MODEL, IN_TN), lambda i, j: (0, qk_tiles), pipeline_mode=pl.Buffered(1)),
        ],
        out_specs=pl.BlockSpec((IN_TN // LANES, SEQ, LANES), lambda i, j: (j, i, 0)),
        out_shape=jax.ShapeDtypeStruct((width // LANES, t, LANES), BF16),
        scratch_shapes=[pltpu.VMEM((SEQ, D_MODEL), BF16),
                        pltpu.VMEM((D_MODEL // LANES, SEQ, LANES), F32)],
        compiler_params=_cparams(("arbitrary", "arbitrary"), vmem_limit=IN_ATT_VMEM_LIMIT),
        name="inproj_att",
    )(x2d, g, w_qk_f32, w_in_f32)


def _inproj_rest(x2d, g, w_in_f32, cos_r, sin_r):
    t = x2d.shape[0]
    first_tile = ATT_IN_W // IN_TN
    table = pl.BlockSpec((SEQ, LANES), lambda i, j: (0, 0))
    return pl.pallas_call(
        _inproj_rest_kernel,
        grid=(t // SEQ, REST_IN_W // IN_TN),
        in_specs=[
            pl.BlockSpec((SEQ, D_MODEL), lambda i, j: (i, 0)),
            pl.BlockSpec((1, D_MODEL), lambda i, j: (0, 0)),
            pl.BlockSpec((D_MODEL, IN_TN), lambda i, j: (0, first_tile + j)),
            table, table,
        ],
        out_specs=pl.BlockSpec((IN_TN // LANES, SEQ, LANES), lambda i, j: (j, i, 0)),
        out_shape=jax.ShapeDtypeStruct((REST_IN_W // LANES, t, LANES), BF16),
        scratch_shapes=[pltpu.VMEM((SEQ, D_MODEL), BF16)],
        compiler_params=_cparams(("arbitrary", "arbitrary")),
        name="inproj_rest",
    )(x2d, g, w_in_f32, cos_r, sin_r)


PREP_ROWS = 256
PREP_UNROLL = 4
SEG4 = ATT_BLOCK // 4
SEG1 = ATT_BLOCK // ATT_CLASSES


def _att_bias_tables():
    def tile(qpos, kpos):
        d = qpos[:, None] - kpos[None, :]
        one = np.where((d >= 0) & (d <= ATT_BLOCK), 0.0, NEG_BIG).astype(np.float32)
        return np.concatenate([one, one], axis=1)

    u = np.arange(ATT_BLOCK)
    q4 = 4 * (u % SEG4) + u // SEG4
    q1 = ATT_CLASSES * (u % SEG1) + u // SEG1
    w = np.arange(2 * ATT_BLOCK)
    k1 = ATT_CLASSES * (w % SEG1) + w // (2 * SEG1) + ATT_BLOCK * ((w // SEG1) % 2 - 1)
    return (tile(q1, k1), tile(q1, q1), tile(q4, np.concatenate([q4 - ATT_BLOCK, q4])), tile(q4, q4),
            tile(u, u))


def masked_pair(x, head0_mask):
    return jnp.where(head0_mask, x, 0.0).astype(BF16), jnp.where(head0_mask, 0.0, x).astype(BF16)


def _att_kernel(q_ref, k_ref, v_ref, cos_ref, sin_ref, gq_ref, gk_ref,
                b1_ref, b1f_ref, b4_ref, b4f_ref, b16_ref, o_ref,
                qj, kj, vj, qb, kb, vb, acc_sc, m_sc, l_sc, nat):
    lane = lax.broadcasted_iota(jnp.int32, (1, LANES), 1)
    half = ATT_HEAD_DIM // 2
    qk_head0 = (lane // half) % 2 == 0
    v_head0 = lane < ATT_HEAD_DIM
    qk_keep0, qk_keep1 = masked_pair(jnp.ones((1, LANES), F32), qk_head0)
    v_keep0, v_keep1 = masked_pair(jnp.ones((1, LANES), F32), v_head0)
    seg = jnp.where((lax.broadcasted_iota(jnp.int32, (LANES, LANES), 0) // half) % 2
                    == (lax.broadcasted_iota(jnp.int32, (LANES, LANES), 1) // half) % 2,
                    1.0 / ATT_HEAD_DIM, 0.0).astype(BF16)

    def prep(ci, carry):
        rows = pl.ds(pl.multiple_of(ci * PREP_ROWS, PREP_ROWS), PREP_ROWS)
        cos = cos_ref[rows, :]
        sin = sin_ref[rows, :]

        def norm_rope(src, g_ref):
            x = src[0, rows, :].astype(F32)
            ms = jnp.dot((x * x).astype(BF16), seg, preferred_element_type=F32)
            xn = x * lax.rsqrt(ms + EPS) * g_ref[...]
            return xn * cos + pltpu.roll(xn, ATT_HEAD_DIM, 1) * sin

        xq = norm_rope(q_ref, gq_ref)
        qj[rows, :] = xq
        qb[rows, :] = xq.astype(BF16)
        xk = norm_rope(k_ref, gk_ref)
        kj[rows, :] = xk
        xk = xk.astype(BF16)
        kb[0, rows, :] = xk * qk_keep0
        kb[1, rows, :] = xk * qk_keep1
        xv = v_ref[0, rows, :]
        vj[rows, :] = xv.astype(F32)
        vb[0, rows, :] = xv * v_keep0
        vb[1, rows, :] = xv * v_keep1
        return carry

    lax.fori_loop(0, SEQ // PREP_ROWS, prep, 0, unroll=PREP_UNROLL)

    nt_dims = (((1,), (1,)), ((), ()))

    def attend(q, k0, k1, v0, v1, bias):
        w = k0.shape[0]
        s = lax.dot_general(q, jnp.concatenate([k0, k1], axis=0), nt_dims,
                            preferred_element_type=F32) + bias
        m0 = jnp.max(s[:, :w], axis=-1, keepdims=True)
        m1 = jnp.max(s[:, w:], axis=-1, keepdims=True)
        e = jnp.concatenate([jnp.exp(s[:, :w] - m0), jnp.exp(s[:, w:] - m1)], axis=1).astype(BF16)
        v2 = jnp.concatenate([jnp.concatenate([v0, jnp.broadcast_to(v_keep0, (w, LANES))], axis=1),
                              jnp.concatenate([v1, jnp.broadcast_to(v_keep1, (w, LANES))], axis=1)], axis=0)
        r = jnp.dot(e, v2, preferred_element_type=F32)
        return r[:, :LANES], jnp.where(v_head0, m0, m1), r[:, LANES:]

    def gather(ref, pieces, lead=()):
        return jnp.concatenate([ref[lead + (rows, slice(None))] for rows in pieces], axis=0)


    def store(p, pieces, n, a, m, l):
        for idx, rows in enumerate(pieces):
            acc_sc[p, rows, :] = a[idx * n:(idx + 1) * n]
            m_sc[p, rows, :] = m[idx * n:(idx + 1) * n]
            l_sc[p, rows, :] = l[idx * n:(idx + 1) * n]

    for j in range(ATT_CLASSES):
        rows = [pl.ds(j * ATT_BLOCK, ATT_BLOCK)]
        a, m, l = attend(qb[rows[0], :], kb[0, rows[0], :], kb[1, rows[0], :],
                         vb[0, rows[0], :], vb[1, rows[0], :], b16_ref[...])
        store(1, rows, ATT_BLOCK, a, m, l)

    for c in range(4):
        def segs(n):
            return [pl.ds((4 * a + c) * ATT_BLOCK + SEG4 * n, SEG4) for a in range(4)]

        for n in range(SEQ // 4 // ATT_BLOCK):
            cur = segs(n)
            keys = cur if n == 0 else segs(n - 1) + cur
            bias = b4f_ref[...] if n == 0 else b4_ref[...]
            a, m, l = attend(gather(qb, cur), gather(kb, keys, (0,)), gather(kb, keys, (1,)),
                             gather(vb, keys, (0,)), gather(vb, keys, (1,)), bias)
            store(0, cur, SEG4, a, m, l)

    for n in range(SEQ // ATT_BLOCK):
        cur = [pl.ds(j * ATT_BLOCK + SEG1 * n, SEG1) for j in range(ATT_CLASSES)]
        if n == 0:
            keys, bias = cur, b1f_ref[...]
        else:
            keys = [pl.ds(j * ATT_BLOCK + SEG1 * n - SEG1, 2 * SEG1) for j in range(ATT_CLASSES)]
            bias = b1_ref[...]
        k0, k1 = masked_pair(gather(kj, keys), qk_head0)
        v0, v1 = masked_pair(gather(vj, keys), v_head0)
        a, m, l = attend(gather(qj, cur).astype(BF16), k0, k1, v0, v1, bias)
        stats = [(a, m, l)] + [(gather(acc_sc, cur, (p,)), gather(m_sc, cur, (p,)), gather(l_sc, cur, (p,)))
                               for p in range(2)]
        m_all = jnp.maximum(jnp.maximum(stats[0][1], stats[1][1]), stats[2][1])
        num = jnp.zeros((ATT_BLOCK, LANES), F32)
        den = jnp.zeros((ATT_BLOCK, LANES), F32)
        for ap, mp, lp in stats:
            w = jnp.exp(mp - m_all)
            num = num + w * ap
            den = den + w * lp
        merged = num / den
        for j in range(ATT_CLASSES):
            nat[pl.ds(ATT_BLOCK * n + j, SEG1, stride=ATT_CLASSES), :] = merged[j * SEG1:(j + 1) * SEG1]

    def emit(ci, carry):
        rows = pl.ds(pl.multiple_of(ci * PREP_ROWS, PREP_ROWS), PREP_ROWS)
        o_ref[rows, :] = nat[rows, :].astype(BF16)
        return carry

    lax.fori_loop(0, SEQ // PREP_ROWS, emit, 0)


def _attention(proj_att, cos_a, sin_a, gq, gk, batch):
    t = batch * SEQ
    hp = ATT_W // LANES

    def col(cb):
        return pl.BlockSpec((1, SEQ, LANES), lambda b, h: (cb + h, b, 0))

    def const(shape):
        return pl.BlockSpec(shape, lambda b, h: tuple(0 for _ in shape))

    biases = [jnp.asarray(b) for b in _att_bias_tables()]
    row_f32 = pltpu.VMEM((SEQ, LANES), F32)
    row_bf16 = pltpu.VMEM((SEQ, LANES), BF16)
    stat = pltpu.VMEM((2, SEQ, LANES), F32)
    return pl.pallas_call(
        _att_kernel,
        grid=(batch, hp),
        in_specs=[col(CB_QA), col(CB_KA), col(CB_VA),
                  const((SEQ, LANES)), const((SEQ, LANES)), const((1, LANES)), const((1, LANES))]
                 + [const(b.shape) for b in biases],
        out_specs=pl.BlockSpec((SEQ, LANES), lambda b, h: (b, h)),
        out_shape=jax.ShapeDtypeStruct((t, ATT_W), BF16),
        scratch_shapes=([row_f32] * 3 + [row_bf16] + [pltpu.VMEM((2, SEQ, LANES), BF16)] * 2 + [stat] * 3
                        + [row_f32]),
        compiler_params=_cparams(("arbitrary", "arbitrary")),
        name="dilated_attention",
    )(proj_att, proj_att, proj_att, cos_a, sin_a, gq, gk, *biases)


def _ret_kernel(gam_ref, q_ref, k_ref, v_ref, g_ref, decay_ref, xi_ref, zeta_ref,
                gn_ref, o_ref, state_sc):
    h = pl.program_id(1)
    gamma_c = gam_ref[h]
    state_sc[...] = jnp.zeros_like(state_sc)
    c = RET_CHUNK
    nt_dims = (((1,), (1,)), ((), ()))
    tn_dims = (((0,), (0,)), ((), ()))

    def cols(ref, rs, n):
        return jnp.concatenate([ref[i, rs, :] for i in range(n)], axis=-1)

    def chunk(n, carry):
        rs = pl.ds(pl.multiple_of(n * c, c), c)
        q = cols(q_ref, rs, RET_QK_DIM // LANES)
        k = cols(k_ref, rs, RET_QK_DIM // LANES)
        v = cols(v_ref, rs, RET_V_DIM // LANES)
        inner = lax.dot_general(q, k, nt_dims, preferred_element_type=F32) * decay_ref[0]
        y = jnp.dot(inner.astype(BF16), v, preferred_element_type=F32)
        state = state_sc[...]
        y = y + jnp.dot(q * xi_ref[0], state.astype(BF16), preferred_element_type=F32)
        state_sc[...] = state * gamma_c + lax.dot_general(k * zeta_ref[0], v, tn_dims,
                                                          preferred_element_type=F32)
        yn = y * lax.rsqrt(jnp.mean(y * y, axis=-1, keepdims=True) + EPS) * gn_ref[0]
        o_ref[rs, :] = (yn * cols(g_ref, rs, RET_V_DIM // LANES).astype(F32)).astype(BF16)
        return carry

    lax.fori_loop(0, SEQ // c, chunk, 0, unroll=RET_UNROLL)


def _retention(proj, gamma_c, decay, xi, zeta, g_ret, batch):
    t = batch * SEQ
    nq = RET_QK_DIM // LANES
    nv = RET_V_DIM // LANES

    def cols(cb, n):
        return pl.BlockSpec((n, SEQ, LANES), lambda b, h: (cb // n + h, b, 0))

    return pl.pallas_call(
        _ret_kernel,
        grid=(batch, RET_HEADS),
        in_specs=[
            pl.BlockSpec(memory_space=pltpu.SMEM),
            cols(CB_QR, nq), cols(CB_KR, nq), cols(CB_VR, nv), cols(CB_GR, nv),
            pl.BlockSpec((1, RET_CHUNK, RET_CHUNK), lambda b, h: (h, 0, 0)),
            pl.BlockSpec((1, RET_CHUNK, RET_QK_DIM), lambda b, h: (h, 0, 0)),
            pl.BlockSpec((1, RET_CHUNK, RET_QK_DIM), lambda b, h: (h, 0, 0)),
            pl.BlockSpec((1, 1, RET_V_DIM), lambda b, h: (h, 0, 0)),
        ],
        out_specs=pl.BlockSpec((SEQ, RET_V_DIM), lambda b, h: (b, h)),
        out_shape=jax.ShapeDtypeStruct((t, RET_V_W), BF16),
        scratch_shapes=[pltpu.VMEM((RET_QK_DIM, RET_V_DIM), F32)],
        compiler_params=_cparams(("arbitrary", "arbitrary")),
        name="retention",
    )(gamma_c, proj, proj, proj, proj, decay, xi, zeta, g_ret)


OUT_TM = 512
N_ROUTE = N_EXPERTS + N_GROUPS
GROUP_LANE = LANES - 1
XG_W = D_MODEL + LANES


def _split_bf16(x):
    hi = x.astype(BF16)
    lo = (x - hi.astype(F32)).astype(BF16)
    return hi, lo


def _out_kernel(ya_ref, yr_ref, ga_ref, gb_ref, bg_ref, x_ref, watt_ref, wret_ref, wout_ref,
                gffn_ref, wr_hi_ref, wr_lo_ref, br_ref, x1_ref, xg_ref):
    ya = jnp.dot(ya_ref[...], watt_ref[...], preferred_element_type=F32)
    yr = jnp.dot(yr_ref[...], wret_ref[...], preferred_element_type=F32)
    merged = []
    for cb in range(D_MODEL // LANES):
        cs = slice(cb * LANES, (cb + 1) * LANES)
        ga = _sigmoid(ga_ref[cb].astype(F32) + bg_ref[:, cs])
        gb = _sigmoid(gb_ref[cb].astype(F32) + bg_ref[:, D_MODEL + cb * LANES:D_MODEL + (cb + 1) * LANES])
        merged.append((ga * ya[:, cs] + gb * yr[:, cs]).astype(BF16))
    merged = jnp.concatenate(merged, axis=-1)
    x1 = x_ref[...] + jnp.dot(merged, wout_ref[...], preferred_element_type=F32)
    x1_ref[...] = x1
    xn = x1 * lax.rsqrt(jnp.mean(x1 * x1, axis=-1, keepdims=True) + EPS) * gffn_ref[...]
    xg_ref[:, :D_MODEL] = xn

    xh, xl = _split_bf16(xn)
    logits = (jnp.dot(xh, wr_hi_ref[...], preferred_element_type=F32)
              + jnp.dot(xl, wr_hi_ref[...], preferred_element_type=F32)
              + jnp.dot(xh, wr_lo_ref[...], preferred_element_type=F32)) + br_ref[...]
    lane = lax.broadcasted_iota(jnp.int32, logits.shape, 1)
    lane_f = lane.astype(F32)
    is_group = (lane >= N_EXPERTS) & (lane < N_ROUTE)
    gl = jnp.where(is_group, logits, NEG_BIG)
    gmax = jnp.max(gl, axis=-1, keepdims=True)
    gsel = jnp.min(jnp.where(gl == gmax, lane_f, 1e9), axis=-1, keepdims=True) - N_EXPERTS
    p_group = 1.0 / jnp.sum(jnp.where(is_group, jnp.exp(logits - gmax), 0.0), axis=-1, keepdims=True)
    lo_lane = gsel * EXPERTS_PER_GROUP
    in_group = (lane_f >= lo_lane) & (lane_f < lo_lane + EXPERTS_PER_GROUP)
    el = jnp.where(in_group, logits, NEG_BIG)
    v1 = jnp.max(el, axis=-1, keepdims=True)
    i1 = jnp.min(jnp.where(el == v1, lane_f, 1e9), axis=-1, keepdims=True)
    el2 = jnp.where(lane_f == i1, NEG_BIG, el)
    v2 = jnp.max(el2, axis=-1, keepdims=True)
    i2 = jnp.min(jnp.where(el2 == v2, lane_f, 1e9), axis=-1, keepdims=True)
    e21 = jnp.exp(v2 - v1)
    w1 = p_group / (1.0 + e21)
    w2 = w1 * e21
    gate = jnp.where(lane_f == i1, w1, 0.0) + jnp.where(lane_f == i2, w2, 0.0)
    xg_ref[:, D_MODEL:] = jnp.where(lane == GROUP_LANE, gsel, gate)


def _out_stage(y_att, y_ret, proj, b_gate, x2d, w_att, w_ret, w_out, g_ffn, wr_hi, wr_lo, b_route):
    t = x2d.shape[0]
    ncb = D_MODEL // LANES

    def full(shape):
        return pl.BlockSpec(shape, lambda i: tuple(0 for _ in shape))

    return pl.pallas_call(
        _out_kernel,
        grid=(t // OUT_TM,),
        in_specs=[
            pl.BlockSpec((OUT_TM, ATT_W), lambda i: (i, 0)),
            pl.BlockSpec((OUT_TM, RET_V_W), lambda i: (i, 0)),
            pl.BlockSpec((ncb, OUT_TM, LANES), lambda i: (CB_GA // ncb, i, 0)),
            pl.BlockSpec((ncb, OUT_TM, LANES), lambda i: (CB_GB // ncb, i, 0)),
            full((1, 2 * D_MODEL)),
            pl.BlockSpec((OUT_TM, D_MODEL), lambda i: (i, 0)),
            full((ATT_W, D_MODEL)), full((RET_V_W, D_MODEL)), full((D_MODEL, D_MODEL)),
            full((1, D_MODEL)), full((D_MODEL, LANES)), full((D_MODEL, LANES)), full((1, LANES)),
        ],
        out_specs=[
            pl.BlockSpec((OUT_TM, D_MODEL), lambda i: (i, 0)),
            pl.BlockSpec((OUT_TM, XG_W), lambda i: (i, 0)),
        ],
        out_shape=[
            jax.ShapeDtypeStruct((t, D_MODEL), F32),
            jax.ShapeDtypeStruct((t, XG_W), F32),
        ],
        compiler_params=_cparams(("arbitrary",)),
        name="out_stage",
    )(y_att, y_ret, proj, proj, b_gate, x2d, w_att, w_ret, w_out, g_ffn, wr_hi, wr_lo, b_route)


MOE_TILE = 512
MOVE_TM = 1024
GROUP_FF = EXPERTS_PER_GROUP * EXPERT_FF


def _moe_plan(xg, t):
    i32 = jnp.int32
    g = xg[:, D_MODEL + GROUP_LANE].astype(i32)
    onehot = (g[:, None] == jnp.arange(N_GROUPS, dtype=i32)[None, :]).astype(i32)
    csum = jnp.cumsum(onehot, axis=0)
    rank = jnp.sum(onehot * csum, axis=1) - 1
    padded = (csum[-1] + MOE_TILE - 1) // MOE_TILE * MOE_TILE
    ends = jnp.cumsum(padded)
    pos = rank + jnp.sum(onehot * (ends - padded)[None, :], axis=1)
    tile_start = jnp.arange(t // MOE_TILE + N_GROUPS, dtype=i32) * MOE_TILE
    tile_group = jnp.minimum(jnp.sum((tile_start[:, None] >= ends[None, :]).astype(i32), axis=1),
                             N_GROUPS - 1)
    tails = jnp.maximum(ends - MOE_TILE, 0)
    return pos.astype(i32), tile_group.astype(i32), (ends[-1:] // MOE_TILE).astype(i32), tails.astype(i32)


def _dispatch_kernel(hi_ref, lo_ref, tail_ref, xg_ref, xs_ref, zeros, sem, zero_sem):
    base = pl.program_id(0) * MOVE_TM

    @pl.when(pl.program_id(0) == 0)
    def _():
        zeros[...] = jnp.zeros_like(zeros)
        tile = MOE_TILE // SUBLANES
        def fill(start):
            return pltpu.make_async_copy(zeros, xs_ref.at[pl.ds(start, tile)], zero_sem)

        spare = [fill(xs_ref.shape[0] - (g + 1) * tile) for g in range(N_GROUPS)]
        for f in spare:
            f.start()
        for f in spare:
            f.wait()
        for g in range(N_GROUPS):
            f = fill(tail_ref[g])
            f.start()
            f.wait()

    def send(i, carry):
        for u in range(SUBLANES):
            r = base + i * SUBLANES + u
            pltpu.make_async_copy(xg_ref.at[i, pl.ds(u, 1)],
                                  xs_ref.at[hi_ref[r], pl.ds(lo_ref[r], 1)], sem).start()
        return carry

    lax.fori_loop(0, MOVE_TM // SUBLANES, send, 0)
    pltpu.make_async_copy(xg_ref, xs_ref.at[pl.ds(0, MOVE_TM // SUBLANES)], sem).wait()


def _dispatch(pos_hi, pos_lo, tail_hi, xg, n_rows):
    t = xg.shape[0]
    xs = pl.pallas_call(
        _dispatch_kernel,
        grid_spec=pltpu.PrefetchScalarGridSpec(
            num_scalar_prefetch=3,
            grid=(t // MOVE_TM,),
            in_specs=[pl.BlockSpec((MOVE_TM // SUBLANES, SUBLANES, XG_W), lambda i, hi, lo, tl: (i, 0, 0))],
            out_specs=pl.BlockSpec(memory_space=pl.ANY),
            scratch_shapes=[pltpu.VMEM((MOE_TILE // SUBLANES, SUBLANES, XG_W), F32),
                            pltpu.SemaphoreType.DMA, pltpu.SemaphoreType.DMA],
        ),
        out_shape=jax.ShapeDtypeStruct((n_rows // SUBLANES, SUBLANES, XG_W), F32),
        compiler_params=_cparams(("arbitrary",)),
        name="moe_dispatch",
    )(pos_hi, pos_lo, tail_hi, xg.reshape(t // SUBLANES, SUBLANES, XG_W))
    return xs.reshape(n_rows, XG_W)


def _experts_kernel(tg_ref, nused_ref, xs_ref, w1_ref, w3_ref, w2_ref, ys_ref):
    i = pl.program_id(0)

    @pl.when(i < nused_ref[0])
    def _():
        x = xs_ref[:, :D_MODEL].astype(BF16)
        gate = xs_ref[:, D_MODEL:]
        lane = lax.broadcasted_iota(jnp.int32, gate.shape, 1)
        first = tg_ref[i] * EXPERTS_PER_GROUP
        hidden = []
        for e in range(EXPERTS_PER_GROUP):
            a = jnp.dot(x, w1_ref[0, e], preferred_element_type=F32)
            b = jnp.dot(x, w3_ref[0, e], preferred_element_type=F32)
            g = jnp.sum(jnp.where(lane == first + e, gate, 0.0), axis=-1, keepdims=True)
            hidden.append((a * _sigmoid(a) * b * g).astype(BF16))
        ys_ref[...] = jnp.dot(jnp.concatenate(hidden, axis=-1), w2_ref[0], preferred_element_type=F32)

    @pl.when(i >= nused_ref[0])
    def _():
        ys_ref[...] = jnp.zeros_like(ys_ref)


def _experts(tile_group, n_used, xs, w1g, w3g, w2g):
    n_rows = xs.shape[0]
    up = pl.BlockSpec((1, EXPERTS_PER_GROUP, D_MODEL, EXPERT_FF), lambda i, tg, nu: (tg[i], 0, 0, 0))
    return pl.pallas_call(
        _experts_kernel,
        grid_spec=pltpu.PrefetchScalarGridSpec(
            num_scalar_prefetch=2,
            grid=(n_rows // MOE_TILE,),
            in_specs=[
                pl.BlockSpec((MOE_TILE, XG_W), lambda i, tg, nu: (i, 0)),
                up, up,
                pl.BlockSpec((1, GROUP_FF, D_MODEL), lambda i, tg, nu: (tg[i], 0, 0)),
            ],
            out_specs=pl.BlockSpec((MOE_TILE, D_MODEL), lambda i, tg, nu: (i, 0)),
        ),
        out_shape=jax.ShapeDtypeStruct((n_rows, D_MODEL), F32),
        compiler_params=_cparams(("arbitrary",)),
        name="moe_experts",
    )(tile_group, n_used, xs, w1g, w3g, w2g)


def _combine_kernel(hi_ref, lo_ref, x1_ref, ys_ref, o_ref, buf, sem):
    step = pl.program_id(0)
    slot = step % 2

    def issue(for_step, into):
        base = for_step * MOVE_TM

        def fetch(i, carry):
            for u in range(SUBLANES):
                r = base + i * SUBLANES + u
                pltpu.make_async_copy(ys_ref.at[hi_ref[r], pl.ds(lo_ref[r], 1)],
                                      buf.at[into, i, pl.ds(u, 1)], sem.at[into]).start()
            return carry

        lax.fori_loop(0, MOVE_TM // SUBLANES, fetch, 0)

    @pl.when(step == 0)
    def _():
        issue(step, slot)

    @pl.when(step + 1 < pl.num_programs(0))
    def _():
        issue(step + 1, 1 - slot)

    pltpu.make_async_copy(ys_ref.at[pl.ds(0, MOVE_TM // SUBLANES)], buf.at[slot], sem.at[slot]).wait()
    o_ref[...] = x1_ref[...] + buf[slot]


def _combine(pos_hi, pos_lo, x1, ys):
    t = x1.shape[0]
    tile = (MOVE_TM // SUBLANES, SUBLANES, D_MODEL)
    out = pl.pallas_call(
        _combine_kernel,
        grid_spec=pltpu.PrefetchScalarGridSpec(
            num_scalar_prefetch=2,
            grid=(t // MOVE_TM,),
            in_specs=[pl.BlockSpec(tile, lambda i, hi, lo: (i, 0, 0)),
                      pl.BlockSpec(memory_space=pl.ANY)],
            out_specs=pl.BlockSpec(tile, lambda i, hi, lo: (i, 0, 0)),
            scratch_shapes=[pltpu.VMEM((2,) + tile, F32), pltpu.SemaphoreType.DMA((2,))],
        ),
        out_shape=jax.ShapeDtypeStruct((t // SUBLANES, SUBLANES, D_MODEL), F32),
        compiler_params=_cparams(("arbitrary",)),
        name="moe_combine",
    )(pos_hi, pos_lo, x1.reshape(t // SUBLANES, SUBLANES, D_MODEL),
      ys.reshape(ys.shape[0] // SUBLANES, SUBLANES, D_MODEL))
    return out.reshape(t, D_MODEL)


def _moe(xg, x1, w1, w3, w2):
    t = xg.shape[0]
    w1g = w1.reshape(N_GROUPS, EXPERTS_PER_GROUP, D_MODEL, EXPERT_FF).astype(BF16)
    w3g = w3.reshape(N_GROUPS, EXPERTS_PER_GROUP, D_MODEL, EXPERT_FF).astype(BF16)
    w2g = w2.reshape(N_GROUPS, GROUP_FF, D_MODEL).astype(BF16)
    pos, tile_group, n_used, tails = _moe_plan(xg, t)
    pos_hi, pos_lo = pos // SUBLANES, pos % SUBLANES
    xs = _dispatch(pos_hi, pos_lo, tails // SUBLANES, xg, t + N_GROUPS * MOE_TILE)
    ys = _experts(tile_group, n_used, xs, w1g, w3g, w2g)
    return _combine(pos_hi, pos_lo, x1, ys)


def _class_major(table):
    return table.reshape(ROWS_PER_CLASS, ATT_CLASSES, -1).transpose(1, 0, 2).reshape(SEQ, -1)


def _rope_tables_att():
    pos = jnp.arange(SEQ, dtype=F32)
    inv = ROPE_THETA ** (-jnp.arange(0, ATT_HEAD_DIM, 2, dtype=F32) / ATT_HEAD_DIM)
    ang = pos[:, None] * inv[None, :]
    cos, sin = jnp.cos(ang), jnp.sin(ang)
    cos_full = jnp.concatenate([cos, cos, cos, cos], axis=-1)
    sin_full = jnp.concatenate([-sin, -sin, sin, sin], axis=-1)
    return _class_major(cos_full), _class_major(sin_full)


def _pair_lanes(a):
    half = ATT_HEAD_DIM // 2
    lead = a.shape[:-1]
    a = a.reshape(lead + (-1, 2, 2, half))
    return jnp.swapaxes(a, -3, -2).reshape(lead + (-1,))


def _rope_tables_ret():
    pos = jnp.arange(SEQ, dtype=F32)
    inv = 1.0 / (ROPE_THETA ** jnp.linspace(0.0, 1.0, RET_QK_DIM // 2, dtype=F32))
    ang = pos[:, None] * inv[None, :]
    return jnp.cos(ang), jnp.sin(ang)


def _decay_tables():
    c = RET_CHUNK
    log_gamma = jnp.log(1.0 - jnp.exp2(-5.0 - jnp.arange(RET_HEADS, dtype=F32)))
    idx = jnp.arange(c, dtype=F32)
    diff = idx[:, None] - idx[None, :]
    decay = jnp.where(diff >= 0, jnp.exp(log_gamma[:, None, None] * jnp.maximum(diff, 0.0)), 0.0)
    zeta = jnp.exp(log_gamma[:, None] * (c - 1 - idx))
    xi = jnp.exp(log_gamma[:, None] * (idx + 1))
    gamma_c = jnp.exp(log_gamma * c)
    bc = lambda a: jnp.broadcast_to(a[:, :, None], (RET_HEADS, c, RET_QK_DIM)).astype(BF16)
    return gamma_c, decay, bc(xi), bc(zeta)


def kernel(x, g_norm_mix, w_in, b_merge_gate, g_q, g_k, w_branch_att, g_ret_norm, w_branch_ret,
           w_out, g_norm_ffn, w_router_group, b_router_group, w_router_expert, b_router_expert,
           w1, w3, w2):
    batch = x.shape[0]
    t = batch * SEQ
    cos_a, sin_a = _rope_tables_att()
    cos_r, sin_r = _rope_tables_ret()
    gamma_c, decay, xi, zeta = _decay_tables()
    xf = x.reshape(t, D_MODEL)
    for l in range(g_norm_mix.shape[0]):
        g_mix = g_norm_mix[l][None, :]
        proj_att = _inproj_att(xf, g_mix, _pair_lanes(w_in[l][:, :2 * ATT_W]).astype(BF16), w_in[l])
        proj = _inproj_rest(xf, g_mix, w_in[l], cos_r, sin_r)
        reps = LANES // ATT_HEAD_DIM
        gq = _pair_lanes(jnp.tile(g_q[l], reps))[None, :] * ATT_HEAD_DIM ** -0.5
        y_att = _attention(proj_att, cos_a, sin_a, gq, _pair_lanes(jnp.tile(g_k[l], reps))[None, :],
                           batch)
        y_ret = _retention(proj, gamma_c, decay, xi, zeta, g_ret_norm[l][:, None, :], batch)
        w_route = jnp.concatenate(
            [w_router_expert[l], w_router_group[l],
             jnp.zeros((D_MODEL, LANES - N_ROUTE), F32)], axis=-1)
        wr_hi, wr_lo = _split_bf16(w_route)
        b_route = jnp.concatenate(
            [b_router_expert[l], b_router_group[l], jnp.zeros((LANES - N_ROUTE,), F32)])[None, :]
        x1, xg = _out_stage(
            y_att, y_ret, proj, b_merge_gate[l][None, :], xf,
            w_branch_att[l].astype(BF16), w_branch_ret[l].astype(BF16), w_out[l].astype(BF16),
            g_norm_ffn[l][None, :], wr_hi, wr_lo, b_route)
        xf = _moe(xg, x1, w1[l], w3[l], w2[l])
    return xf.reshape(batch, SEQ, D_MODEL)
```

```python
import functools

import numpy as np

import jax
import jax.numpy as jnp
from jax import lax
from jax.experimental import pallas as pl
from jax.experimental.pallas import tpu as pltpu

F32 = jnp.float32
BF16 = jnp.bfloat16

D_MODEL = 1024
SEQ = 2048
ATT_HEADS = 16
ATT_HEAD_DIM = 64
ATT_W = ATT_HEADS * ATT_HEAD_DIM
ROPE_THETA = 10000.0
RET_HEADS = 4
RET_QK_DIM = 256
RET_V_DIM = 512
RET_QK_W = RET_HEADS * RET_QK_DIM
RET_V_W = RET_HEADS * RET_V_DIM
N_GROUPS = 4
EXPERTS_PER_GROUP = 8
N_EXPERTS = N_GROUPS * EXPERTS_PER_GROUP
EXPERT_FF = 256
EPS = 1e-6
ATT_IN_W = 3 * ATT_W
REST_IN_W = 2 * RET_QK_W + 2 * RET_V_W + 2 * D_MODEL

LANES = 128
SUBLANES = 8
CB_QA, CB_KA, CB_VA = 0, 8, 16
CB_QR, CB_KR, CB_VR, CB_GR, CB_GA, CB_GB = 0, 8, 16, 32, 48, 56

ATT_BLOCK = 128
ATT_CLASSES = 16
ROWS_PER_CLASS = SEQ // ATT_CLASSES
RET_CHUNK = 256
RET_UNROLL = 2
NEG_BIG = -1e30
VMEM_LIMIT = 48 * 1024 * 1024
IN_ATT_VMEM_LIMIT = 56 * 1024 * 1024


def _cparams(sem, vmem_limit=VMEM_LIMIT):
    return pltpu.CompilerParams(dimension_semantics=sem, vmem_limit_bytes=vmem_limit)


IN_TN = 1024
IN_NORM_ROWS = 256
REST_TILE_QR, REST_TILE_KR = CB_QR * LANES // IN_TN, CB_KR * LANES // IN_TN
REST_TILE_GR, REST_TILE_GA = CB_GR * LANES // IN_TN, CB_GA * LANES // IN_TN


def _sigmoid(x):
    return 0.5 * jnp.tanh(0.5 * x) + 0.5


def _rmsnorm_rows(x, g_ref):
    ms = jnp.mean(x * x, axis=-1, keepdims=True)
    return x * lax.rsqrt(ms + EPS) * g_ref[...]


def _project(xn_sc, w_ref, o_ref, epilogue, first_block=0):
    xn = xn_sc[...]
    for c2 in range(IN_TN // 256):
        w = w_ref[:, c2 * 256:(c2 + 1) * 256].astype(BF16)
        acc = jnp.dot(xn, w, preferred_element_type=F32)
        lo, hi = epilogue(acc[:, :LANES], acc[:, LANES:])
        o_ref[first_block + 2 * c2] = lo.astype(BF16)
        o_ref[first_block + 2 * c2 + 1] = hi.astype(BF16)


def _inproj_att_kernel(x_ref, g_ref, wqk_ref, wv_ref, o_ref, xn_sc, xs):
    @pl.when(pl.program_id(1) == 0)
    def _():
        def norm_rows(ci, carry):
            rows = pl.ds(pl.multiple_of(ci * IN_NORM_ROWS, IN_NORM_ROWS), IN_NORM_ROWS)
            xn = _rmsnorm_rows(x_ref[rows, :], g_ref)
            for c in range(D_MODEL // LANES):
                xs[c, rows, :] = xn[:, c * LANES:(c + 1) * LANES]
            return carry

        lax.fori_loop(0, SEQ // IN_NORM_ROWS, norm_rows, 0)

        def gather_class(j, carry):
            dst = pl.ds(pl.multiple_of(j * ROWS_PER_CLASS, ROWS_PER_CLASS), ROWS_PER_CLASS)
            for c in range(D_MODEL // LANES):
                xn_sc[dst, c * LANES:(c + 1) * LANES] = xs[
                    c, pl.ds(j, ROWS_PER_CLASS, stride=ATT_CLASSES), :].astype(BF16)
            return carry

        lax.fori_loop(0, ATT_CLASSES, gather_class, 0)

    qk_tiles = 2 * ATT_W // IN_TN

    @pl.when(pl.program_id(1) < qk_tiles)
    def _():
        _project(xn_sc, wqk_ref, o_ref, lambda lo, hi: (lo, hi))

    @pl.when(pl.program_id(1) >= qk_tiles)
    def _():
        _project(xn_sc, wv_ref, o_ref, lambda lo, hi: (lo, hi))


def _inproj_rest_kernel(x_ref, g_ref, w_ref, cos_ref, sin_ref, o_ref, xn_sc):
    j = pl.program_id(1)

    @pl.when(j == 0)
    def _():
        def norm_rows(ci, carry):
            rows = pl.ds(pl.multiple_of(ci * IN_NORM_ROWS, IN_NORM_ROWS), IN_NORM_ROWS)
            xn_sc[rows, :] = _rmsnorm_rows(x_ref[rows, :], g_ref).astype(BF16)
            return carry

        lax.fori_loop(0, SEQ // IN_NORM_ROWS, norm_rows, 0)

    def rotate(scale):
        def epilogue(x1, x2):
            cos, sin = cos_ref[...], sin_ref[...]
            return (x1 * cos - x2 * sin) * scale, (x2 * cos + x1 * sin) * scale
        return epilogue

    def swish(lo, hi):
        return lo * _sigmoid(lo), hi * _sigmoid(hi)

    is_gate = (j >= REST_TILE_GR) & (j < REST_TILE_GA)

    @pl.when(j == REST_TILE_QR)
    def _():
        _project(xn_sc, w_ref, o_ref, rotate(1.0))

    @pl.when(j == REST_TILE_KR)
    def _():
        _project(xn_sc, w_ref, o_ref, rotate(RET_QK_DIM ** -0.5))

    @pl.when(is_gate)
    def _():
        _project(xn_sc, w_ref, o_ref, swish)

    @pl.when((j > REST_TILE_KR) & jnp.logical_not(is_gate))
    def _():
        _project(xn_sc, w_ref, o_ref, lambda lo, hi: (lo, hi))


def _inproj_att(x2d, g, w_qk_f32, w_in_f32):
    t = x2d.shape[0]
    width = ATT_IN_W
    qk_tiles = 2 * ATT_W // IN_TN
    return pl.pallas_call(
        _inproj_att_kernel,
        grid=(t // SEQ, width // IN_TN),
        in_specs=[
            pl.BlockSpec((SEQ, D_MODEL), lambda i, j: (i, 0)),
            pl.BlockSpec((1, D_MODEL), lambda i, j: (0, 0)),
            pl.BlockSpec((D_MODEL, IN_TN), lambda i, j: (0, jnp.minimum(j, qk_tiles - 1))),
            pl.BlockSpec((D_MODEL, IN_TN), lambda i, j: (0, qk_tiles), pipeline_mode=pl.Buffered(1)),
        ],
        out_specs=pl.BlockSpec((IN_TN // LANES, SEQ, LANES), lambda i, j: (j, i, 0)),
        out_shape=jax.ShapeDtypeStruct((width // LANES, t, LANES), BF16),
        scratch_shapes=[pltpu.VMEM((SEQ, D_MODEL), BF16),
                        pltpu.VMEM((D_MODEL // LANES, SEQ, LANES), F32)],
        compiler_params=_cparams(("arbitrary", "arbitrary"), vmem_limit=IN_ATT_VMEM_LIMIT),
        name="inproj_att",
    )(x2d, g, w_qk_f32, w_in_f32)


def _inproj_rest(x2d, g, w_in_f32, cos_r, sin_r):
    t = x2d.shape[0]
    first_tile = ATT_IN_W // IN_TN
    table = pl.BlockSpec((SEQ, LANES), lambda i, j: (0, 0))
    return pl.pallas_call(
        _inproj_rest_kernel,
        grid=(t // SEQ, REST_IN_W // IN_TN),
        in_specs=[
            pl.BlockSpec((SEQ, D_MODEL), lambda i, j: (i, 0)),
            pl.BlockSpec((1, D_MODEL), lambda i, j: (0, 0)),
            pl.BlockSpec((D_MODEL, IN_TN), lambda i, j: (0, first_tile + j)),
            table, table,
        ],
        out_specs=pl.BlockSpec((IN_TN // LANES, SEQ, LANES), lambda i, j: (j, i, 0)),
        out_shape=jax.ShapeDtypeStruct((REST_IN_W // LANES, t, LANES), BF16),
        scratch_shapes=[pltpu.VMEM((SEQ, D_MODEL), BF16)],
        compiler_params=_cparams(("arbitrary", "arbitrary")),
        name="inproj_rest",
    )(x2d, g, w_in_f32, cos_r, sin_r)


PREP_ROWS = 256
PREP_UNROLL = 4
SEG4 = ATT_BLOCK // 4
SEG1 = ATT_BLOCK // ATT_CLASSES


def _att_bias_tables():
    def tile(qpos, kpos):
        d = qpos[:, None] - kpos[None, :]
        one = np.where((d >= 0) & (d <= ATT_BLOCK), 0.0, NEG_BIG).astype(np.float32)
        return np.concatenate([one, one], axis=1)

    u = np.arange(ATT_BLOCK)
    q4 = 4 * (u % SEG4) + u // SEG4
    q1 = ATT_CLASSES * (u % SEG1) + u // SEG1
    w = np.arange(2 * ATT_BLOCK)
    k1 = ATT_CLASSES * (w % SEG1) + w // (2 * SEG1) + ATT_BLOCK * ((w // SEG1) % 2 - 1)
    return (tile(q1, k1), tile(q1, q1), tile(q4, np.concatenate([q4 - ATT_BLOCK, q4])), tile(q4, q4),
            tile(u, u))


def masked_pair(x, head0_mask):
    return jnp.where(head0_mask, x, 0.0).astype(BF16), jnp.where(head0_mask, 0.0, x).astype(BF16)


def _att_kernel(q_ref, k_ref, v_ref, cos_ref, sin_ref, gq_ref, gk_ref,
                b1_ref, b1f_ref, b4_ref, b4f_ref, b16_ref, o_ref,
                qj, kj, vj, qb, kb, vb, acc_sc, m_sc, l_sc, nat):
    lane = lax.broadcasted_iota(jnp.int32, (1, LANES), 1)
    half = ATT_HEAD_DIM // 2
    qk_head0 = (lane // half) % 2 == 0
    v_head0 = lane < ATT_HEAD_DIM
    qk_keep0, qk_keep1 = masked_pair(jnp.ones((1, LANES), F32), qk_head0)
    v_keep0, v_keep1 = masked_pair(jnp.ones((1, LANES), F32), v_head0)
    seg = jnp.where((lax.broadcasted_iota(jnp.int32, (LANES, LANES), 0) // half) % 2
                    == (lax.broadcasted_iota(jnp.int32, (LANES, LANES), 1) // half) % 2,
                    1.0 / ATT_HEAD_DIM, 0.0).astype(BF16)

    def prep(ci, carry):
        rows = pl.ds(pl.multiple_of(ci * PREP_ROWS, PREP_ROWS), PREP_ROWS)
        cos = cos_ref[rows, :]
        sin = sin_ref[rows, :]

        def norm_rope(src, g_ref):
            x = src[0, rows, :].astype(F32)
            ms = jnp.dot((x * x).astype(BF16), seg, preferred_element_type=F32)
            xn = x * lax.rsqrt(ms + EPS) * g_ref[...]
            return xn * cos + pltpu.roll(xn, ATT_HEAD_DIM, 1) * sin

        xq = norm_rope(q_ref, gq_ref)
        qj[rows, :] = xq
        qb[rows, :] = xq.astype(BF16)
        xk = norm_rope(k_ref, gk_ref)
        kj[rows, :] = xk
        xk = xk.astype(BF16)
        kb[0, rows, :] = xk * qk_keep0
        kb[1, rows, :] = xk * qk_keep1
        xv = v_ref[0, rows, :]
        vj[rows, :] = xv.astype(F32)
        vb[0, rows, :] = xv * v_keep0
        vb[1, rows, :] = xv * v_keep1
        return carry

    lax.fori_loop(0, SEQ // PREP_ROWS, prep, 0, unroll=PREP_UNROLL)

    nt_dims = (((1,), (1,)), ((), ()))

    def attend(q, k0, k1, v0, v1, bias):
        w = k0.shape[0]
        s = lax.dot_general(q, jnp.concatenate([k0, k1], axis=0), nt_dims,
                            preferred_element_type=F32) + bias
        m0 = jnp.max(s[:, :w], axis=-1, keepdims=True)
        m1 = jnp.max(s[:, w:], axis=-1, keepdims=True)
        e = jnp.concatenate([jnp.exp(s[:, :w] - m0), jnp.exp(s[:, w:] - m1)], axis=1).astype(BF16)
        v2 = jnp.concatenate([jnp.concatenate([v0, jnp.broadcast_to(v_keep0, (w, LANES))], axis=1),
                              jnp.concatenate([v1, jnp.broadcast_to(v_keep1, (w, LANES))], axis=1)], axis=0)
        r = jnp.dot(e, v2, preferred_element_type=F32)
        return r[:, :LANES], jnp.where(v_head0, m0, m1), r[:, LANES:]

    def gather(ref, pieces, lead=()):
        return jnp.concatenate([ref[lead + (rows, slice(None))] for rows in pieces], axis=0)


    def store(p, pieces, n, a, m, l):
        for idx, rows in enumerate(pieces):
            acc_sc[p, rows, :] = a[idx * n:(idx + 1) * n]
            m_sc[p, rows, :] = m[idx * n:(idx + 1) * n]
            l_sc[p, rows, :] = l[idx * n:(idx + 1) * n]

    for j in range(ATT_CLASSES):
        rows = [pl.ds(j * ATT_BLOCK, ATT_BLOCK)]
        a, m, l = attend(qb[rows[0], :], kb[0, rows[0], :], kb[1, rows[0], :],
                         vb[0, rows[0], :], vb[1, rows[0], :], b16_ref[...])
        store(1, rows, ATT_BLOCK, a, m, l)

    for c in range(4):
        def segs(n):
            return [pl.ds((4 * a + c) * ATT_BLOCK + SEG4 * n, SEG4) for a in range(4)]

        for n in range(SEQ // 4 // ATT_BLOCK):
            cur = segs(n)
            keys = cur if n == 0 else segs(n - 1) + cur
            bias = b4f_ref[...] if n == 0 else b4_ref[...]
            a, m, l = attend(gather(qb, cur), gather(kb, keys, (0,)), gather(kb, keys, (1,)),
                             gather(vb, keys, (0,)), gather(vb, keys, (1,)), bias)
            store(0, cur, SEG4, a, m, l)

    for n in range(SEQ // ATT_BLOCK):
        cur = [pl.ds(j * ATT_BLOCK + SEG1 * n, SEG1) for j in range(ATT_CLASSES)]
        if n == 0:
            keys, bias = cur, b1f_ref[...]
        else:
            keys = [pl.ds(j * ATT_BLOCK + SEG1 * n - SEG1, 2 * SEG1) for j in range(ATT_CLASSES)]
            bias = b1_ref[...]
        k0, k1 = masked_pair(gather(kj, keys), qk_head0)
        v0, v1 = masked_pair(gather(vj, keys), v_head0)
        a, m, l = attend(gather(qj, cur).astype(BF16), k0, k1, v0, v1, bias)
        stats = [(a, m, l)] + [(gather(acc_sc, cur, (p,)), gather(m_sc, cur, (p,)), gather(l_sc, cur, (p,)))
                               for p in range(2)]
        m_all = jnp.maximum(jnp.maximum(stats[0][1], stats[1][1]), stats[2][1])
        num = jnp.zeros((ATT_BLOCK, LANES), F32)
        den = jnp.zeros((ATT_BLOCK, LANES), F32)
        for ap, mp, lp in stats:
            w = jnp.exp(mp - m_all)
            num = num + w * ap
            den = den + w * lp
        merged = num / den
        for j in range(ATT_CLASSES):
            nat[pl.ds(ATT_BLOCK * n + j, SEG1, stride=ATT_CLASSES), :] = merged[j * SEG1:(j + 1) * SEG1]

    def emit(ci, carry):
        rows = pl.ds(pl.multiple_of(ci * PREP_ROWS, PREP_ROWS), PREP_ROWS)
        o_ref[rows, :] = nat[rows, :].astype(BF16)
        return carry

    lax.fori_loop(0, SEQ // PREP_ROWS, emit, 0)


def _attention(proj_att, cos_a, sin_a, gq, gk, batch):
    t = batch * SEQ
    hp = ATT_W // LANES

    def col(cb):
        return pl.BlockSpec((1, SEQ, LANES), lambda b, h: (cb + h, b, 0))

    def const(shape):
        return pl.BlockSpec(shape, lambda b, h: tuple(0 for _ in shape))

    biases = [jnp.asarray(b) for b in _att_bias_tables()]
    row_f32 = pltpu.VMEM((SEQ, LANES), F32)
    row_bf16 = pltpu.VMEM((SEQ, LANES), BF16)
    stat = pltpu.VMEM((2, SEQ, LANES), F32)
    return pl.pallas_call(
        _att_kernel,
        grid=(batch, hp),
        in_specs=[col(CB_QA), col(CB_KA), col(CB_VA),
                  const((SEQ, LANES)), const((SEQ, LANES)), const((1, LANES)), const((1, LANES))]
                 + [const(b.shape) for b in biases],
        out_specs=pl.BlockSpec((SEQ, LANES), lambda b, h: (b, h)),
        out_shape=jax.ShapeDtypeStruct((t, ATT_W), BF16),
        scratch_shapes=([row_f32] * 3 + [row_bf16] + [pltpu.VMEM((2, SEQ, LANES), BF16)] * 2 + [stat] * 3
                        + [row_f32]),
        compiler_params=_cparams(("arbitrary", "arbitrary")),
        name="dilated_attention",
    )(proj_att, proj_att, proj_att, cos_a, sin_a, gq, gk, *biases)


def _ret_kernel(gam_ref, q_ref, k_ref, v_ref, g_ref, decay_ref, xi_ref, zeta_ref,
                gn_ref, o_ref, state_sc):
    h = pl.program_id(1)
    gamma_c = gam_ref[h]
    state_sc[...] = jnp.zeros_like(state_sc)
    c = RET_CHUNK
    nt_dims = (((1,), (1,)), ((), ()))
    tn_dims = (((0,), (0,)), ((), ()))

    def cols(ref, rs, n):
        return jnp.concatenate([ref[i, rs, :] for i in range(n)], axis=-1)

    def chunk(n, carry):
        rs = pl.ds(pl.multiple_of(n * c, c), c)
        q = cols(q_ref, rs, RET_QK_DIM // LANES)
        k = cols(k_ref, rs, RET_QK_DIM // LANES)
        v = cols(v_ref, rs, RET_V_DIM // LANES)
        inner = lax.dot_general(q, k, nt_dims, preferred_element_type=F32) * decay_ref[0]
        y = jnp.dot(inner.astype(BF16), v, preferred_element_type=F32)
        state = state_sc[...]
        y = y + jnp.dot(q * xi_ref[0], state.astype(BF16), preferred_element_type=F32)
        state_sc[...] = state * gamma_c + lax.dot_general(k * zeta_ref[0], v, tn_dims,
                                                          preferred_element_type=F32)
        yn = y * lax.rsqrt(jnp.mean(y * y, axis=-1, keepdims=True) + EPS) * gn_ref[0]
        o_ref[rs, :] = (yn * cols(g_ref, rs, RET_V_DIM // LANES).astype(F32)).astype(BF16)
        return carry

    lax.fori_loop(0, SEQ // c, chunk, 0, unroll=RET_UNROLL)


def _retention(proj, gamma_c, decay, xi, zeta, g_ret, batch):
    t = batch * SEQ
    nq = RET_QK_DIM // LANES
    nv = RET_V_DIM // LANES

    def cols(cb, n):
        return pl.BlockSpec((n, SEQ, LANES), lambda b, h: (cb // n + h, b, 0))

    return pl.pallas_call(
        _ret_kernel,
        grid=(batch, RET_HEADS),
        in_specs=[
            pl.BlockSpec(memory_space=pltpu.SMEM),
            cols(CB_QR, nq), cols(CB_KR, nq), cols(CB_VR, nv), cols(CB_GR, nv),
            pl.BlockSpec((1, RET_CHUNK, RET_CHUNK), lambda b, h: (h, 0, 0)),
            pl.BlockSpec((1, RET_CHUNK, RET_QK_DIM), lambda b, h: (h, 0, 0)),
            pl.BlockSpec((1, RET_CHUNK, RET_QK_DIM), lambda b, h: (h, 0, 0)),
            pl.BlockSpec((1, 1, RET_V_DIM), lambda b, h: (h, 0, 0)),
        ],
        out_specs=pl.BlockSpec((SEQ, RET_V_DIM), lambda b, h: (b, h)),
        out_shape=jax.ShapeDtypeStruct((t, RET_V_W), BF16),
        scratch_shapes=[pltpu.VMEM((RET_QK_DIM, RET_V_DIM), F32)],
        compiler_params=_cparams(("arbitrary", "arbitrary")),
        name="retention",
    )(gamma_c, proj, proj, proj, proj, decay, xi, zeta, g_ret)


OUT_TM = 512
N_ROUTE = N_EXPERTS + N_GROUPS
GROUP_LANE = LANES - 1
XG_W = D_MODEL + LANES


def _split_bf16(x):
    hi = x.astype(BF16)
    lo = (x - hi.astype(F32)).astype(BF16)
    return hi, lo


def _out_kernel(ya_ref, yr_ref, ga_ref, gb_ref, bg_ref, x_ref, watt_ref, wret_ref, wout_ref,
                gffn_ref, wr_hi_ref, wr_lo_ref, br_ref, x1_ref, xg_ref):
    ya = jnp.dot(ya_ref[...], watt_ref[...], preferred_element_type=F32)
    yr = jnp.dot(yr_ref[...], wret_ref[...], preferred_element_type=F32)
    merged = []
    for cb in range(D_MODEL // LANES):
        cs = slice(cb * LANES, (cb + 1) * LANES)
        ga = _sigmoid(ga_ref[cb].astype(F32) + bg_ref[:, cs])
        gb = _sigmoid(gb_ref[cb].astype(F32) + bg_ref[:, D_MODEL + cb * LANES:D_MODEL + (cb + 1) * LANES])
        merged.append((ga * ya[:, cs] + gb * yr[:, cs]).astype(BF16))
    merged = jnp.concatenate(merged, axis=-1)
    x1 = x_ref[...] + jnp.dot(merged, wout_ref[...], preferred_element_type=F32)
    x1_ref[...] = x1
    xn = x1 * lax.rsqrt(jnp.mean(x1 * x1, axis=-1, keepdims=True) + EPS) * gffn_ref[...]
    xg_ref[:, :D_MODEL] = xn

    xh, xl = _split_bf16(xn)
    logits = (jnp.dot(xh, wr_hi_ref[...], preferred_element_type=F32)
              + jnp.dot(xl, wr_hi_ref[...], preferred_element_type=F32)
              + jnp.dot(xh, wr_lo_ref[...], preferred_element_type=F32)) + br_ref[...]
    lane = lax.broadcasted_iota(jnp.int32, logits.shape, 1)
    lane_f = lane.astype(F32)
    is_group = (lane >= N_EXPERTS) & (lane < N_ROUTE)
    gl = jnp.where(is_group, logits, NEG_BIG)
    gmax = jnp.max(gl, axis=-1, keepdims=True)
    gsel = jnp.min(jnp.where(gl == gmax, lane_f, 1e9), axis=-1, keepdims=True) - N_EXPERTS
    p_group = 1.0 / jnp.sum(jnp.where(is_group, jnp.exp(logits - gmax), 0.0), axis=-1, keepdims=True)
    lo_lane = gsel * EXPERTS_PER_GROUP
    in_group = (lane_f >= lo_lane) & (lane_f < lo_lane + EXPERTS_PER_GROUP)
    el = jnp.where(in_group, logits, NEG_BIG)
    v1 = jnp.max(el, axis=-1, keepdims=True)
    i1 = jnp.min(jnp.where(el == v1, lane_f, 1e9), axis=-1, keepdims=True)
    el2 = jnp.where(lane_f == i1, NEG_BIG, el)
    v2 = jnp.max(el2, axis=-1, keepdims=True)
    i2 = jnp.min(jnp.where(el2 == v2, lane_f, 1e9), axis=-1, keepdims=True)
    e21 = jnp.exp(v2 - v1)
    w1 = p_group / (1.0 + e21)
    w2 = w1 * e21
    gate = jnp.where(lane_f == i1, w1, 0.0) + jnp.where(lane_f == i2, w2, 0.0)
    xg_ref[:, D_MODEL:] = jnp.where(lane == GROUP_LANE, gsel, gate)


def _out_stage(y_att, y_ret, proj, b_gate, x2d, w_att, w_ret, w_out, g_ffn, wr_hi, wr_lo, b_route):
    t = x2d.shape[0]
    ncb = D_MODEL // LANES

    def full(shape):
        return pl.BlockSpec(shape, lambda i: tuple(0 for _ in shape))

    return pl.pallas_call(
        _out_kernel,
        grid=(t // OUT_TM,),
        in_specs=[
            pl.BlockSpec((OUT_TM, ATT_W), lambda i: (i, 0)),
            pl.BlockSpec((OUT_TM, RET_V_W), lambda i: (i, 0)),
            pl.BlockSpec((ncb, OUT_TM, LANES), lambda i: (CB_GA // ncb, i, 0)),
            pl.BlockSpec((ncb, OUT_TM, LANES), lambda i: (CB_GB // ncb, i, 0)),
            full((1, 2 * D_MODEL)),
            pl.BlockSpec((OUT_TM, D_MODEL), lambda i: (i, 0)),
            full((ATT_W, D_MODEL)), full((RET_V_W, D_MODEL)), full((D_MODEL, D_MODEL)),
            full((1, D_MODEL)), full((D_MODEL, LANES)), full((D_MODEL, LANES)), full((1, LANES)),
        ],
        out_specs=[
            pl.BlockSpec((OUT_TM, D_MODEL), lambda i: (i, 0)),
            pl.BlockSpec((OUT_TM, XG_W), lambda i: (i, 0)),
        ],
        out_shape=[
            jax.ShapeDtypeStruct((t, D_MODEL), F32),
            jax.ShapeDtypeStruct((t, XG_W), F32),
        ],
        compiler_params=_cparams(("arbitrary",)),
        name="out_stage",
    )(y_att, y_ret, proj, proj, b_gate, x2d, w_att, w_ret, w_out, g_ffn, wr_hi, wr_lo, b_route)


MOE_TILE = 512
MOVE_TM = 1024
GROUP_FF = EXPERTS_PER_GROUP * EXPERT_FF


def _moe_plan(xg, t):
    i32 = jnp.int32
    g = xg[:, D_MODEL + GROUP_LANE].astype(i32)
    onehot = (g[:, None] == jnp.arange(N_GROUPS, dtype=i32)[None, :]).astype(i32)
    csum = jnp.cumsum(onehot, axis=0)
    rank = jnp.sum(onehot * csum, axis=1) - 1
    padded = (csum[-1] + MOE_TILE - 1) // MOE_TILE * MOE_TILE
    ends = jnp.cumsum(padded)
    pos = rank + jnp.sum(onehot * (ends - padded)[None, :], axis=1)
    tile_start = jnp.arange(t // MOE_TILE + N_GROUPS, dtype=i32) * MOE_TILE
    tile_group = jnp.minimum(jnp.sum((tile_start[:, None] >= ends[None, :]).astype(i32), axis=1),
                             N_GROUPS - 1)
    tails = jnp.maximum(ends - MOE_TILE, 0)
    return pos.astype(i32), tile_group.astype(i32), (ends[-1:] // MOE_TILE).astype(i32), tails.astype(i32)


def _dispatch_kernel(hi_ref, lo_ref, tail_ref, xg_ref, xs_ref, zeros, stage, sem, zero_sem):
    step = pl.program_id(0)
    slot = step % 2
    base = step * MOVE_TM

    @pl.when(pl.program_id(0) == 0)
    def _():
        zeros[...] = jnp.zeros_like(zeros)
        tile = MOE_TILE // SUBLANES
        def fill(start):
            return pltpu.make_async_copy(zeros, xs_ref.at[pl.ds(start, tile)], zero_sem)

        spare = [fill(xs_ref.shape[0] - (g + 1) * tile) for g in range(N_GROUPS)]
        for f in spare:
            f.start()
        for f in spare:
            f.wait()
        for g in range(N_GROUPS):
            f = fill(tail_ref[g])
            f.start()
            f.wait()

    def wait_rows(s):
        pltpu.make_async_copy(stage.at[s], xs_ref.at[pl.ds(0, MOVE_TM // SUBLANES)], sem.at[s]).wait()

    @pl.when(step >= 2)
    def _():
        wait_rows(slot)

    stage[slot] = xg_ref[...]

    def send(i, carry):
        for u in range(SUBLANES):
            r = base + i * SUBLANES + u
            pltpu.make_async_copy(stage.at[slot, i, pl.ds(u, 1)],
                                  xs_ref.at[hi_ref[r], pl.ds(lo_ref[r], 1)], sem.at[slot]).start()
        return carry

    lax.fori_loop(0, MOVE_TM // SUBLANES, send, 0)

    @pl.when(step == pl.num_programs(0) - 1)
    def _():
        wait_rows(1 - slot)
        wait_rows(slot)


def _dispatch(pos_hi, pos_lo, tail_hi, xg, n_rows):
    t = xg.shape[0]
    xs = pl.pallas_call(
        _dispatch_kernel,
        grid_spec=pltpu.PrefetchScalarGridSpec(
            num_scalar_prefetch=3,
            grid=(t // MOVE_TM,),
            in_specs=[pl.BlockSpec((MOVE_TM // SUBLANES, SUBLANES, XG_W), lambda i, hi, lo, tl: (i, 0, 0))],
            out_specs=pl.BlockSpec(memory_space=pl.ANY),
            scratch_shapes=[pltpu.VMEM((MOE_TILE // SUBLANES, SUBLANES, XG_W), F32),
                            pltpu.VMEM((2, MOVE_TM // SUBLANES, SUBLANES, XG_W), F32),
                            pltpu.SemaphoreType.DMA((2,)), pltpu.SemaphoreType.DMA],
        ),
        out_shape=jax.ShapeDtypeStruct((n_rows // SUBLANES, SUBLANES, XG_W), F32),
        compiler_params=_cparams(("arbitrary",)),
        name="moe_dispatch",
    )(pos_hi, pos_lo, tail_hi, xg.reshape(t // SUBLANES, SUBLANES, XG_W))
    return xs.reshape(n_rows, XG_W)


def _experts_kernel(tg_ref, nused_ref, xs_ref, w1_ref, w3_ref, w2_ref, ys_ref):
    i = pl.program_id(0)

    @pl.when(i < nused_ref[0])
    def _():
        x = xs_ref[:, :D_MODEL].astype(BF16)
        gate = xs_ref[:, D_MODEL:]
        lane = lax.broadcasted_iota(jnp.int32, gate.shape, 1)
        first = tg_ref[i] * EXPERTS_PER_GROUP
        hidden = []
        for e in range(EXPERTS_PER_GROUP):
            a = jnp.dot(x, w1_ref[0, e], preferred_element_type=F32)
            b = jnp.dot(x, w3_ref[0, e], preferred_element_type=F32)
            g = jnp.sum(jnp.where(lane == first + e, gate, 0.0), axis=-1, keepdims=True)
            hidden.append((a * _sigmoid(a) * b * g).astype(BF16))
        ys_ref[...] = jnp.dot(jnp.concatenate(hidden, axis=-1), w2_ref[0], preferred_element_type=F32)

    @pl.when(i >= nused_ref[0])
    def _():
        ys_ref[...] = jnp.zeros_like(ys_ref)


def _experts(tile_group, n_used, xs, w1g, w3g, w2g):
    n_rows = xs.shape[0]
    up = pl.BlockSpec((1, EXPERTS_PER_GROUP, D_MODEL, EXPERT_FF), lambda i, tg, nu: (tg[i], 0, 0, 0))
    return pl.pallas_call(
        _experts_kernel,
        grid_spec=pltpu.PrefetchScalarGridSpec(
            num_scalar_prefetch=2,
            grid=(n_rows // MOE_TILE,),
            in_specs=[
                pl.BlockSpec((MOE_TILE, XG_W), lambda i, tg, nu: (i, 0)),
                up, up,
                pl.BlockSpec((1, GROUP_FF, D_MODEL), lambda i, tg, nu: (tg[i], 0, 0)),
            ],
            out_specs=pl.BlockSpec((MOE_TILE, D_MODEL), lambda i, tg, nu: (i, 0)),
        ),
        out_shape=jax.ShapeDtypeStruct((n_rows, D_MODEL), F32),
        compiler_params=_cparams(("arbitrary",)),
        name="moe_experts",
    )(tile_group, n_used, xs, w1g, w3g, w2g)


def _combine_kernel(hi_ref, lo_ref, x1_ref, ys_ref, o_ref, buf, sem):
    step = pl.program_id(0)
    slot = step % 2

    def issue(for_step, into):
        base = for_step * MOVE_TM

        def fetch(i, carry):
            for u in range(SUBLANES):
                r = base + i * SUBLANES + u
                pltpu.make_async_copy(ys_ref.at[hi_ref[r], pl.ds(lo_ref[r], 1)],
                                      buf.at[into, i, pl.ds(u, 1)], sem.at[into]).start()
            return carry

        lax.fori_loop(0, MOVE_TM // SUBLANES, fetch, 0)

    @pl.when(step == 0)
    def _():
        issue(step, slot)

    @pl.when(step + 1 < pl.num_programs(0))
    def _():
        issue(step + 1, 1 - slot)

    pltpu.make_async_copy(ys_ref.at[pl.ds(0, MOVE_TM // SUBLANES)], buf.at[slot], sem.at[slot]).wait()
    o_ref[...] = x1_ref[...] + buf[slot]


def _combine(pos_hi, pos_lo, x1, ys):
    t = x1.shape[0]
    tile = (MOVE_TM // SUBLANES, SUBLANES, D_MODEL)
    out = pl.pallas_call(
        _combine_kernel,
        grid_spec=pltpu.PrefetchScalarGridSpec(
            num_scalar_prefetch=2,
            grid=(t // MOVE_TM,),
            in_specs=[pl.BlockSpec(tile, lambda i, hi, lo: (i, 0, 0)),
                      pl.BlockSpec(memory_space=pl.ANY)],
            out_specs=pl.BlockSpec(tile, lambda i, hi, lo: (i, 0, 0)),
            scratch_shapes=[pltpu.VMEM((2,) + tile, F32), pltpu.SemaphoreType.DMA((2,))],
        ),
        out_shape=jax.ShapeDtypeStruct((t // SUBLANES, SUBLANES, D_MODEL), F32),
        compiler_params=_cparams(("arbitrary",)),
        name="moe_combine",
    )(pos_hi, pos_lo, x1.reshape(t // SUBLANES, SUBLANES, D_MODEL),
      ys.reshape(ys.shape[0] // SUBLANES, SUBLANES, D_MODEL))
    return out.reshape(t, D_MODEL)


def _moe(xg, x1, w1, w3, w2):
    t = xg.shape[0]
    w1g = w1.reshape(N_GROUPS, EXPERTS_PER_GROUP, D_MODEL, EXPERT_FF).astype(BF16)
    w3g = w3.reshape(N_GROUPS, EXPERTS_PER_GROUP, D_MODEL, EXPERT_FF).astype(BF16)
    w2g = w2.reshape(N_GROUPS, GROUP_FF, D_MODEL).astype(BF16)
    pos, tile_group, n_used, tails = _moe_plan(xg, t)
    pos_hi, pos_lo = pos // SUBLANES, pos % SUBLANES
    xs = _dispatch(pos_hi, pos_lo, tails // SUBLANES, xg, t + N_GROUPS * MOE_TILE)
    ys = _experts(tile_group, n_used, xs, w1g, w3g, w2g)
    return _combine(pos_hi, pos_lo, x1, ys)


def _class_major(table):
    return table.reshape(ROWS_PER_CLASS, ATT_CLASSES, -1).transpose(1, 0, 2).reshape(SEQ, -1)


def _rope_tables_att():
    pos = jnp.arange(SEQ, dtype=F32)
    inv = ROPE_THETA ** (-jnp.arange(0, ATT_HEAD_DIM, 2, dtype=F32) / ATT_HEAD_DIM)
    ang = pos[:, None] * inv[None, :]
    cos, sin = jnp.cos(ang), jnp.sin(ang)
    cos_full = jnp.concatenate([cos, cos, cos, cos], axis=-1)
    sin_full = jnp.concatenate([-sin, -sin, sin, sin], axis=-1)
    return _class_major(cos_full), _class_major(sin_full)


def _pair_lanes(a):
    half = ATT_HEAD_DIM // 2
    lead = a.shape[:-1]
    a = a.reshape(lead + (-1, 2, 2, half))
    return jnp.swapaxes(a, -3, -2).reshape(lead + (-1,))


def _rope_tables_ret():
    pos = jnp.arange(SEQ, dtype=F32)
    inv = 1.0 / (ROPE_THETA ** jnp.linspace(0.0, 1.0, RET_QK_DIM // 2, dtype=F32))
    ang = pos[:, None] * inv[None, :]
    return jnp.cos(ang), jnp.sin(ang)


def _decay_tables():
    c = RET_CHUNK
    log_gamma = jnp.log(1.0 - jnp.exp2(-5.0 - jnp.arange(RET_HEADS, dtype=F32)))
    idx = jnp.arange(c, dtype=F32)
    diff = idx[:, None] - idx[None, :]
    decay = jnp.where(diff >= 0, jnp.exp(log_gamma[:, None, None] * jnp.maximum(diff, 0.0)), 0.0)
    zeta = jnp.exp(log_gamma[:, None] * (c - 1 - idx))
    xi = jnp.exp(log_gamma[:, None] * (idx + 1))
    gamma_c = jnp.exp(log_gamma * c)
    bc = lambda a: jnp.broadcast_to(a[:, :, None], (RET_HEADS, c, RET_QK_DIM)).astype(BF16)
    return gamma_c, decay, bc(xi), bc(zeta)


def kernel(x, g_norm_mix, w_in, b_merge_gate, g_q, g_k, w_branch_att, g_ret_norm, w_branch_ret,
           w_out, g_norm_ffn, w_router_group, b_router_group, w_router_expert, b_router_expert,
           w1, w3, w2):
    batch = x.shape[0]
    t = batch * SEQ
    cos_a, sin_a = _rope_tables_att()
    cos_r, sin_r = _rope_tables_ret()
    gamma_c, decay, xi, zeta = _decay_tables()
    xf = x.reshape(t, D_MODEL)
    for l in range(g_norm_mix.shape[0]):
        g_mix = g_norm_mix[l][None, :]
        proj_att = _inproj_att(xf, g_mix, _pair_lanes(w_in[l][:, :2 * ATT_W]).astype(BF16), w_in[l])
        proj = _inproj_rest(xf, g_mix, w_in[l], cos_r, sin_r)
        reps = LANES // ATT_HEAD_DIM
        gq = _pair_lanes(jnp.tile(g_q[l], reps))[None, :] * ATT_HEAD_DIM ** -0.5
        y_att = _attention(proj_att, cos_a, sin_a, gq, _pair_lanes(jnp.tile(g_k[l], reps))[None, :],
                           batch)
        y_ret = _retention(proj, gamma_c, decay, xi, zeta, g_ret_norm[l][:, None, :], batch)
        w_route = jnp.concatenate(
            [w_router_expert[l], w_router_group[l],
             jnp.zeros((D_MODEL, LANES - N_ROUTE), F32)], axis=-1)
        wr_hi, wr_lo = _split_bf16(w_route)
        b_route = jnp.concatenate(
            [b_router_expert[l], b_router_group[l], jnp.zeros((LANES - N_ROUTE,), F32)])[None, :]
        x1, xg = _out_stage(
            y_att, y_ret, proj, b_merge_gate[l][None, :], xf,
            w_branch_att[l].astype(BF16), w_branch_ret[l].astype(BF16), w_out[l].astype(BF16),
            g_norm_ffn[l][None, :], wr_hi, wr_lo, b_route)
        xf = _moe(xg, x1, w1[l], w3[l], w2[l])
    return xf.reshape(batch, SEQ, D_MODEL)
```

```python
import functools

import numpy as np

import jax
import jax.numpy as jnp
from jax import lax
from jax.experimental import pallas as pl
from jax.experimental.pallas import tpu as pltpu

F32 = jnp.float32
BF16 = jnp.bfloat16

D_MODEL = 1024
SEQ = 2048
ATT_HEADS = 16
ATT_HEAD_DIM = 64
ATT_W = ATT_HEADS * ATT_HEAD_DIM
ROPE_THETA = 10000.0
RET_HEADS = 4
RET_QK_DIM = 256
RET_V_DIM = 512
RET_QK_W = RET_HEADS * RET_QK_DIM
RET_V_W = RET_HEADS * RET_V_DIM
N_GROUPS = 4
EXPERTS_PER_GROUP = 8
N_EXPERTS = N_GROUPS * EXPERTS_PER_GROUP
EXPERT_FF = 256
EPS = 1e-6
ATT_IN_W = 3 * ATT_W
REST_IN_W = 2 * RET_QK_W + 2 * RET_V_W + 2 * D_MODEL

LANES = 128
SUBLANES = 8
CB_QA, CB_KA, CB_VA = 0, 8, 16
CB_QR, CB_KR, CB_VR, CB_GR, CB_GA, CB_GB = 0, 8, 16, 32, 48, 56

ATT_BLOCK = 128
ATT_CLASSES = 16
ROWS_PER_CLASS = SEQ // ATT_CLASSES
RET_CHUNK = 256
RET_UNROLL = 2
NEG_BIG = -1e30
VMEM_LIMIT = 48 * 1024 * 1024
IN_ATT_VMEM_LIMIT = 56 * 1024 * 1024


def _cparams(sem, vmem_limit=VMEM_LIMIT):
    return pltpu.CompilerParams(dimension_semantics=sem, vmem_limit_bytes=vmem_limit)


IN_TN = 1024
IN_NORM_ROWS = 256
REST_TILE_QR, REST_TILE_KR = CB_QR * LANES // IN_TN, CB_KR * LANES // IN_TN
REST_TILE_GR, REST_TILE_GA = CB_GR * LANES // IN_TN, CB_GA * LANES // IN_TN


def _sigmoid(x):
    return 0.5 * jnp.tanh(0.5 * x) + 0.5


def _rmsnorm_rows(x, g_ref):
    ms = jnp.mean(x * x, axis=-1, keepdims=True)
    return x * lax.rsqrt(ms + EPS) * g_ref[...]


def _project(xn_sc, w_ref, o_ref, epilogue, first_block=0):
    xn = xn_sc[...]
    for c2 in range(IN_TN // 256):
        w = w_ref[:, c2 * 256:(c2 + 1) * 256].astype(BF16)
        acc = jnp.dot(xn, w, preferred_element_type=F32)
        lo, hi = epilogue(acc[:, :LANES], acc[:, LANES:])
        o_ref[first_block + 2 * c2] = lo.astype(BF16)
        o_ref[first_block + 2 * c2 + 1] = hi.astype(BF16)


def _inproj_att_kernel(x_ref, g_ref, wqk_ref, wv_ref, o_ref, xn_sc, xs):
    @pl.when(pl.program_id(1) == 0)
    def _():
        def norm_rows(ci, carry):
            rows = pl.ds(pl.multiple_of(ci * IN_NORM_ROWS, IN_NORM_ROWS), IN_NORM_ROWS)
            xn = _rmsnorm_rows(x_ref[rows, :], g_ref)
            for c in range(D_MODEL // LANES):
                xs[c, rows, :] = xn[:, c * LANES:(c + 1) * LANES]
            return carry

        lax.fori_loop(0, SEQ // IN_NORM_ROWS, norm_rows, 0)

        def gather_class(j, carry):
            dst = pl.ds(pl.multiple_of(j * ROWS_PER_CLASS, ROWS_PER_CLASS), ROWS_PER_CLASS)
            for c in range(D_MODEL // LANES):
                xn_sc[dst, c * LANES:(c + 1) * LANES] = xs[
                    c, pl.ds(j, ROWS_PER_CLASS, stride=ATT_CLASSES), :].astype(BF16)
            return carry

        lax.fori_loop(0, ATT_CLASSES, gather_class, 0)

    qk_tiles = 2 * ATT_W // IN_TN

    @pl.when(pl.program_id(1) < qk_tiles)
    def _():
        _project(xn_sc, wqk_ref, o_ref, lambda lo, hi: (lo, hi))

    @pl.when(pl.program_id(1) >= qk_tiles)
    def _():
        _project(xn_sc, wv_ref, o_ref, lambda lo, hi: (lo, hi))


def _inproj_rest_kernel(x_ref, g_ref, w_ref, cos_ref, sin_ref, o_ref, xn_sc):
    j = pl.program_id(1)

    @pl.when(j == 0)
    def _():
        def norm_rows(ci, carry):
            rows = pl.ds(pl.multiple_of(ci * IN_NORM_ROWS, IN_NORM_ROWS), IN_NORM_ROWS)
            xn_sc[rows, :] = _rmsnorm_rows(x_ref[rows, :], g_ref).astype(BF16)
            return carry

        lax.fori_loop(0, SEQ // IN_NORM_ROWS, norm_rows, 0)

    def rotate(scale):
        def epilogue(x1, x2):
            cos, sin = cos_ref[...], sin_ref[...]
            return (x1 * cos - x2 * sin) * scale, (x2 * cos + x1 * sin) * scale
        return epilogue

    def swish(lo, hi):
        return lo * _sigmoid(lo), hi * _sigmoid(hi)

    is_gate = (j >= REST_TILE_GR) & (j < REST_TILE_GA)

    @pl.when(j == REST_TILE_QR)
    def _():
        _project(xn_sc, w_ref, o_ref, rotate(1.0))

    @pl.when(j == REST_TILE_KR)
    def _():
        _project(xn_sc, w_ref, o_ref, rotate(RET_QK_DIM ** -0.5))

    @pl.when(is_gate)
    def _():
        _project(xn_sc, w_ref, o_ref, swish)

    @pl.when((j > REST_TILE_KR) & jnp.logical_not(is_gate))
    def _():
        _project(xn_sc, w_ref, o_ref, lambda lo, hi: (lo, hi))


def _inproj_att(x2d, g, w_qk_f32, w_in_f32):
    t = x2d.shape[0]
    width = ATT_IN_W
    qk_tiles = 2 * ATT_W // IN_TN
    return pl.pallas_call(
        _inproj_att_kernel,
        grid=(t // SEQ, width // IN_TN),
        in_specs=[
            pl.BlockSpec((SEQ, D_MODEL), lambda i, j: (i, 0)),
            pl.BlockSpec((1, D_MODEL), lambda i, j: (0, 0)),
            pl.BlockSpec((D_MODEL, IN_TN), lambda i, j: (0, jnp.minimum(j, qk_tiles - 1))),
            pl.BlockSpec((D_MODEL, IN_TN), lambda i, j: (0, qk_tiles), pipeline_mode=pl.Buffered(1)),
        ],
        out_specs=pl.BlockSpec((IN_TN // LANES, SEQ, LANES), lambda i, j: (j, i, 0)),
        out_shape=jax.ShapeDtypeStruct((width // LANES, t, LANES), BF16),
        scratch_shapes=[pltpu.VMEM((SEQ, D_MODEL), BF16),
                        pltpu.VMEM((D_MODEL // LANES, SEQ, LANES), F32)],
        compiler_params=_cparams(("arbitrary", "arbitrary"), vmem_limit=IN_ATT_VMEM_LIMIT),
        name="inproj_att",
    )(x2d, g, w_qk_f32, w_in_f32)


def _inproj_rest(x2d, g, w_in_f32, cos_r, sin_r):
    t = x2d.shape[0]
    first_tile = ATT_IN_W // IN_TN
    table = pl.BlockSpec((SEQ, LANES), lambda i, j: (0, 0))
    return pl.pallas_call(
        _inproj_rest_kernel,
        grid=(t // SEQ, REST_IN_W // IN_TN),
        in_specs=[
            pl.BlockSpec((SEQ, D_MODEL), lambda i, j: (i, 0)),
            pl.BlockSpec((1, D_MODEL), lambda i, j: (0, 0)),
            pl.BlockSpec((D_MODEL, IN_TN), lambda i, j: (0, first_tile + j)),
            table, table,
        ],
        out_specs=pl.BlockSpec((IN_TN // LANES, SEQ, LANES), lambda i, j: (j, i, 0)),
        out_shape=jax.ShapeDtypeStruct((REST_IN_W // LANES, t, LANES), BF16),
        scratch_shapes=[pltpu.VMEM((SEQ, D_MODEL), BF16)],
        compiler_params=_cparams(("arbitrary", "arbitrary")),
        name="inproj_rest",
    )(x2d, g, w_in_f32, cos_r, sin_r)


PREP_ROWS = 256
PREP_UNROLL = 4
SEG4 = ATT_BLOCK // 4
SEG1 = ATT_BLOCK // ATT_CLASSES


def _att_bias_tables():
    def tile(qpos, kpos):
        d = qpos[:, None] - kpos[None, :]
        one = np.where((d >= 0) & (d <= ATT_BLOCK), 0.0, NEG_BIG).astype(np.float32)
        return np.concatenate([one, one], axis=1)

    u = np.arange(ATT_BLOCK)
    q4 = 4 * (u % SEG4) + u // SEG4
    q1 = ATT_CLASSES * (u % SEG1) + u // SEG1
    w = np.arange(2 * ATT_BLOCK)
    k1 = ATT_CLASSES * (w % SEG1) + w // (2 * SEG1) + ATT_BLOCK * ((w // SEG1) % 2 - 1)
    return (tile(q1, k1), tile(q1, q1), tile(q4, np.concatenate([q4 - ATT_BLOCK, q4])), tile(q4, q4),
            tile(u, u))


def masked_pair(x, head0_mask):
    return jnp.where(head0_mask, x, 0.0).astype(BF16), jnp.where(head0_mask, 0.0, x).astype(BF16)


def _att_kernel(q_ref, k_ref, v_ref, cos_ref, sin_ref, gq_ref, gk_ref,
                b1_ref, b1f_ref, b4_ref, b4f_ref, b16_ref, o_ref,
                qj, kj, vj, qb, kb, vb, acc_sc, m_sc, l_sc, nat):
    lane = lax.broadcasted_iota(jnp.int32, (1, LANES), 1)
    half = ATT_HEAD_DIM // 2
    qk_head0 = (lane // half) % 2 == 0
    v_head0 = lane < ATT_HEAD_DIM
    qk_keep0, qk_keep1 = masked_pair(jnp.ones((1, LANES), F32), qk_head0)
    v_keep0, v_keep1 = masked_pair(jnp.ones((1, LANES), F32), v_head0)
    seg = jnp.where((lax.broadcasted_iota(jnp.int32, (LANES, LANES), 0) // half) % 2
                    == (lax.broadcasted_iota(jnp.int32, (LANES, LANES), 1) // half) % 2,
                    1.0 / ATT_HEAD_DIM, 0.0).astype(BF16)

    def prep(ci, carry):
        rows = pl.ds(pl.multiple_of(ci * PREP_ROWS, PREP_ROWS), PREP_ROWS)
        cos = cos_ref[rows, :]
        sin = sin_ref[rows, :]

        def norm_rope(src, g_ref):
            x = src[0, rows, :].astype(F32)
            ms = jnp.dot((x * x).astype(BF16), seg, preferred_element_type=F32)
            xn = x * lax.rsqrt(ms + EPS) * g_ref[...]
            return xn * cos + pltpu.roll(xn, ATT_HEAD_DIM, 1) * sin

        xq = norm_rope(q_ref, gq_ref)
        qj[rows, :] = xq
        qb[rows, :] = xq.astype(BF16)
        xk = norm_rope(k_ref, gk_ref)
        kj[rows, :] = xk
        xk = xk.astype(BF16)
        kb[0, rows, :] = xk * qk_keep0
        kb[1, rows, :] = xk * qk_keep1
        xv = v_ref[0, rows, :]
        vj[rows, :] = xv.astype(F32)
        vb[0, rows, :] = xv * v_keep0
        vb[1, rows, :] = xv * v_keep1
        return carry

    lax.fori_loop(0, SEQ // PREP_ROWS, prep, 0, unroll=PREP_UNROLL)

    nt_dims = (((1,), (1,)), ((), ()))

    def attend(q, k0, k1, v0, v1, bias):
        w = k0.shape[0]
        s = lax.dot_general(q, jnp.concatenate([k0, k1], axis=0), nt_dims,
                            preferred_element_type=F32) + bias
        m0 = jnp.max(s[:, :w], axis=-1, keepdims=True)
        m1 = jnp.max(s[:, w:], axis=-1, keepdims=True)
        e = jnp.concatenate([jnp.exp2(s[:, :w] - m0), jnp.exp2(s[:, w:] - m1)], axis=1).astype(BF16)
        v2 = jnp.concatenate([jnp.concatenate([v0, jnp.broadcast_to(v_keep0, (w, LANES))], axis=1),
                              jnp.concatenate([v1, jnp.broadcast_to(v_keep1, (w, LANES))], axis=1)], axis=0)
        r = jnp.dot(e, v2, preferred_element_type=F32)
        return r[:, :LANES], jnp.where(v_head0, m0, m1), r[:, LANES:]

    def gather(ref, pieces, lead=()):
        return jnp.concatenate([ref[lead + (rows, slice(None))] for rows in pieces], axis=0)


    def store(p, pieces, n, a, m, l):
        for idx, rows in enumerate(pieces):
            acc_sc[p, rows, :] = a[idx * n:(idx + 1) * n]
            m_sc[p, rows, :] = m[idx * n:(idx + 1) * n]
            l_sc[p, rows, :] = l[idx * n:(idx + 1) * n]

    for j in range(ATT_CLASSES):
        rows = [pl.ds(j * ATT_BLOCK, ATT_BLOCK)]
        a, m, l = attend(qb[rows[0], :], kb[0, rows[0], :], kb[1, rows[0], :],
                         vb[0, rows[0], :], vb[1, rows[0], :], b16_ref[...])
        store(1, rows, ATT_BLOCK, a, m, l)

    for c in range(4):
        def segs(n):
            return [pl.ds((4 * a + c) * ATT_BLOCK + SEG4 * n, SEG4) for a in range(4)]

        for n in range(SEQ // 4 // ATT_BLOCK):
            cur = segs(n)
            keys = cur if n == 0 else segs(n - 1) + cur
            bias = b4f_ref[...] if n == 0 else b4_ref[...]
            a, m, l = attend(gather(qb, cur), gather(kb, keys, (0,)), gather(kb, keys, (1,)),
                             gather(vb, keys, (0,)), gather(vb, keys, (1,)), bias)
            store(0, cur, SEG4, a, m, l)

    for n in range(SEQ // ATT_BLOCK):
        cur = [pl.ds(j * ATT_BLOCK + SEG1 * n, SEG1) for j in range(ATT_CLASSES)]
        if n == 0:
            keys, bias = cur, b1f_ref[...]
        else:
            keys = [pl.ds(j * ATT_BLOCK + SEG1 * n - SEG1, 2 * SEG1) for j in range(ATT_CLASSES)]
            bias = b1_ref[...]
        k0, k1 = masked_pair(gather(kj, keys), qk_head0)
        v0, v1 = masked_pair(gather(vj, keys), v_head0)
        a, m, l = attend(gather(qj, cur).astype(BF16), k0, k1, v0, v1, bias)
        stats = [(a, m, l)] + [(gather(acc_sc, cur, (p,)), gather(m_sc, cur, (p,)), gather(l_sc, cur, (p,)))
                               for p in range(2)]
        m_all = jnp.maximum(jnp.maximum(stats[0][1], stats[1][1]), stats[2][1])
        num = jnp.zeros((ATT_BLOCK, LANES), F32)
        den = jnp.zeros((ATT_BLOCK, LANES), F32)
        for ap, mp, lp in stats:
            w = jnp.exp2(mp - m_all)
            num = num + w * ap
            den = den + w * lp
        merged = num / den
        for j in range(ATT_CLASSES):
            nat[pl.ds(ATT_BLOCK * n + j, SEG1, stride=ATT_CLASSES), :] = merged[j * SEG1:(j + 1) * SEG1]

    def emit(ci, carry):
        rows = pl.ds(pl.multiple_of(ci * PREP_ROWS, PREP_ROWS), PREP_ROWS)
        o_ref[rows, :] = nat[rows, :].astype(BF16)
        return carry

    lax.fori_loop(0, SEQ // PREP_ROWS, emit, 0)


def _attention(proj_att, cos_a, sin_a, gq, gk, batch):
    t = batch * SEQ
    hp = ATT_W // LANES

    def col(cb):
        return pl.BlockSpec((1, SEQ, LANES), lambda b, h: (cb + h, b, 0))

    def const(shape):
        return pl.BlockSpec(shape, lambda b, h: tuple(0 for _ in shape))

    biases = [jnp.asarray(b) for b in _att_bias_tables()]
    row_f32 = pltpu.VMEM((SEQ, LANES), F32)
    row_bf16 = pltpu.VMEM((SEQ, LANES), BF16)
    stat = pltpu.VMEM((2, SEQ, LANES), F32)
    return pl.pallas_call(
        _att_kernel,
        grid=(batch, hp),
        in_specs=[col(CB_QA), col(CB_KA), col(CB_VA),
                  const((SEQ, LANES)), const((SEQ, LANES)), const((1, LANES)), const((1, LANES))]
                 + [const(b.shape) for b in biases],
        out_specs=pl.BlockSpec((SEQ, LANES), lambda b, h: (b, h)),
        out_shape=jax.ShapeDtypeStruct((t, ATT_W), BF16),
        scratch_shapes=([row_f32] * 3 + [row_bf16] + [pltpu.VMEM((2, SEQ, LANES), BF16)] * 2 + [stat] * 3
                        + [row_f32]),
        compiler_params=_cparams(("arbitrary", "arbitrary")),
        name="dilated_attention",
    )(proj_att, proj_att, proj_att, cos_a, sin_a, gq, gk, *biases)


def _ret_kernel(gam_ref, q_ref, k_ref, v_ref, g_ref, decay_ref, xi_ref, zeta_ref,
                gn_ref, o_ref, state_sc):
    h = pl.program_id(1)
    gamma_c = gam_ref[h]
    state_sc[...] = jnp.zeros_like(state_sc)
    c = RET_CHUNK
    nt_dims = (((1,), (1,)), ((), ()))
    tn_dims = (((0,), (0,)), ((), ()))

    def cols(ref, rs, n):
        return jnp.concatenate([ref[i, rs, :] for i in range(n)], axis=-1)

    def chunk(n, carry):
        rs = pl.ds(pl.multiple_of(n * c, c), c)
        q = cols(q_ref, rs, RET_QK_DIM // LANES)
        k = cols(k_ref, rs, RET_QK_DIM // LANES)
        v = cols(v_ref, rs, RET_V_DIM // LANES)
        inner = lax.dot_general(q, k, nt_dims, preferred_element_type=F32) * decay_ref[0]
        y = jnp.dot(inner.astype(BF16), v, preferred_element_type=F32)
        state = state_sc[...]
        y = y + jnp.dot(q * xi_ref[0], state.astype(BF16), preferred_element_type=F32)
        state_sc[...] = state * gamma_c + lax.dot_general(k * zeta_ref[0], v, tn_dims,
                                                          preferred_element_type=F32)
        yn = y * lax.rsqrt(jnp.mean(y * y, axis=-1, keepdims=True) + EPS) * gn_ref[0]
        o_ref[rs, :] = (yn * cols(g_ref, rs, RET_V_DIM // LANES).astype(F32)).astype(BF16)
        return carry

    lax.fori_loop(0, SEQ // c, chunk, 0, unroll=RET_UNROLL)


def _retention(proj, gamma_c, decay, xi, zeta, g_ret, batch):
    t = batch * SEQ
    nq = RET_QK_DIM // LANES
    nv = RET_V_DIM // LANES

    def cols(cb, n):
        return pl.BlockSpec((n, SEQ, LANES), lambda b, h: (cb // n + h, b, 0))

    return pl.pallas_call(
        _ret_kernel,
        grid=(batch, RET_HEADS),
        in_specs=[
            pl.BlockSpec(memory_space=pltpu.SMEM),
            cols(CB_QR, nq), cols(CB_KR, nq), cols(CB_VR, nv), cols(CB_GR, nv),
            pl.BlockSpec((1, RET_CHUNK, RET_CHUNK), lambda b, h: (h, 0, 0)),
            pl.BlockSpec((1, RET_CHUNK, RET_QK_DIM), lambda b, h: (h, 0, 0)),
            pl.BlockSpec((1, RET_CHUNK, RET_QK_DIM), lambda b, h: (h, 0, 0)),
            pl.BlockSpec((1, 1, RET_V_DIM), lambda b, h: (h, 0, 0)),
        ],
        out_specs=pl.BlockSpec((SEQ, RET_V_DIM), lambda b, h: (b, h)),
        out_shape=jax.ShapeDtypeStruct((t, RET_V_W), BF16),
        scratch_shapes=[pltpu.VMEM((RET_QK_DIM, RET_V_DIM), F32)],
        compiler_params=_cparams(("arbitrary", "arbitrary")),
        name="retention",
    )(gamma_c, proj, proj, proj, proj, decay, xi, zeta, g_ret)


OUT_TM = 512
N_ROUTE = N_EXPERTS + N_GROUPS
GROUP_LANE = LANES - 1
XG_W = D_MODEL + LANES


def _split_bf16(x):
    hi = x.astype(BF16)
    lo = (x - hi.astype(F32)).astype(BF16)
    return hi, lo


def _out_kernel(ya_ref, yr_ref, ga_ref, gb_ref, bg_ref, x_ref, watt_ref, wret_ref, wout_ref,
                gffn_ref, wr_hi_ref, wr_lo_ref, br_ref, x1_ref, xg_ref):
    ya = jnp.dot(ya_ref[...], watt_ref[...], preferred_element_type=F32)
    yr = jnp.dot(yr_ref[...], wret_ref[...], preferred_element_type=F32)
    merged = []
    for cb in range(D_MODEL // LANES):
        cs = slice(cb * LANES, (cb + 1) * LANES)
        ga = _sigmoid(ga_ref[cb].astype(F32) + bg_ref[:, cs])
        gb = _sigmoid(gb_ref[cb].astype(F32) + bg_ref[:, D_MODEL + cb * LANES:D_MODEL + (cb + 1) * LANES])
        merged.append((ga * ya[:, cs] + gb * yr[:, cs]).astype(BF16))
    merged = jnp.concatenate(merged, axis=-1)
    x1 = x_ref[...] + jnp.dot(merged, wout_ref[...], preferred_element_type=F32)
    x1_ref[...] = x1
    xn = x1 * lax.rsqrt(jnp.mean(x1 * x1, axis=-1, keepdims=True) + EPS) * gffn_ref[...]
    xg_ref[:, :D_MODEL] = xn

    xh, xl = _split_bf16(xn)
    logits = (jnp.dot(xh, wr_hi_ref[...], preferred_element_type=F32)
              + jnp.dot(xl, wr_hi_ref[...], preferred_element_type=F32)
              + jnp.dot(xh, wr_lo_ref[...], preferred_element_type=F32)) + br_ref[...]
    lane = lax.broadcasted_iota(jnp.int32, logits.shape, 1)
    lane_f = lane.astype(F32)
    is_group = (lane >= N_EXPERTS) & (lane < N_ROUTE)
    gl = jnp.where(is_group, logits, NEG_BIG)
    gmax = jnp.max(gl, axis=-1, keepdims=True)
    gsel = jnp.min(jnp.where(gl == gmax, lane_f, 1e9), axis=-1, keepdims=True) - N_EXPERTS
    p_group = 1.0 / jnp.sum(jnp.where(is_group, jnp.exp(logits - gmax), 0.0), axis=-1, keepdims=True)
    lo_lane = gsel * EXPERTS_PER_GROUP
    in_group = (lane_f >= lo_lane) & (lane_f < lo_lane + EXPERTS_PER_GROUP)
    el = jnp.where(in_group, logits, NEG_BIG)
    v1 = jnp.max(el, axis=-1, keepdims=True)
    i1 = jnp.min(jnp.where(el == v1, lane_f, 1e9), axis=-1, keepdims=True)
    el2 = jnp.where(lane_f == i1, NEG_BIG, el)
    v2 = jnp.max(el2, axis=-1, keepdims=True)
    i2 = jnp.min(jnp.where(el2 == v2, lane_f, 1e9), axis=-1, keepdims=True)
    e21 = jnp.exp(v2 - v1)
    w1 = p_group / (1.0 + e21)
    w2 = w1 * e21
    gate = jnp.where(lane_f == i1, w1, 0.0) + jnp.where(lane_f == i2, w2, 0.0)
    xg_ref[:, D_MODEL:] = jnp.where(lane == GROUP_LANE, gsel, gate)


def _out_stage(y_att, y_ret, proj, b_gate, x2d, w_att, w_ret, w_out, g_ffn, wr_hi, wr_lo, b_route):
    t = x2d.shape[0]
    ncb = D_MODEL // LANES

    def full(shape):
        return pl.BlockSpec(shape, lambda i: tuple(0 for _ in shape))

    return pl.pallas_call(
        _out_kernel,
        grid=(t // OUT_TM,),
        in_specs=[
            pl.BlockSpec((OUT_TM, ATT_W), lambda i: (i, 0)),
            pl.BlockSpec((OUT_TM, RET_V_W), lambda i: (i, 0)),
            pl.BlockSpec((ncb, OUT_TM, LANES), lambda i: (CB_GA // ncb, i, 0)),
            pl.BlockSpec((ncb, OUT_TM, LANES), lambda i: (CB_GB // ncb, i, 0)),
            full((1, 2 * D_MODEL)),
            pl.BlockSpec((OUT_TM, D_MODEL), lambda i: (i, 0)),
            full((ATT_W, D_MODEL)), full((RET_V_W, D_MODEL)), full((D_MODEL, D_MODEL)),
            full((1, D_MODEL)), full((D_MODEL, LANES)), full((D_MODEL, LANES)), full((1, LANES)),
        ],
        out_specs=[
            pl.BlockSpec((OUT_TM, D_MODEL), lambda i: (i, 0)),
            pl.BlockSpec((OUT_TM, XG_W), lambda i: (i, 0)),
        ],
        out_shape=[
            jax.ShapeDtypeStruct((t, D_MODEL), F32),
            jax.ShapeDtypeStruct((t, XG_W), F32),
        ],
        compiler_params=_cparams(("arbitrary",)),
        name="out_stage",
    )(y_att, y_ret, proj, proj, b_gate, x2d, w_att, w_ret, w_out, g_ffn, wr_hi, wr_lo, b_route)


MOE_TILE = 512
MOVE_TM = 1024
GROUP_FF = EXPERTS_PER_GROUP * EXPERT_FF


def _moe_plan(xg, t):
    i32 = jnp.int32
    g = xg[:, D_MODEL + GROUP_LANE].astype(i32)
    onehot = (g[:, None] == jnp.arange(N_GROUPS, dtype=i32)[None, :]).astype(i32)
    csum = jnp.cumsum(onehot, axis=0)
    rank = jnp.sum(onehot * csum, axis=1) - 1
    padded = (csum[-1] + MOE_TILE - 1) // MOE_TILE * MOE_TILE
    ends = jnp.cumsum(padded)
    pos = rank + jnp.sum(onehot * (ends - padded)[None, :], axis=1)
    tile_start = jnp.arange(t // MOE_TILE + N_GROUPS, dtype=i32) * MOE_TILE
    tile_group = jnp.minimum(jnp.sum((tile_start[:, None] >= ends[None, :]).astype(i32), axis=1),
                             N_GROUPS - 1)
    tails = jnp.maximum(ends - MOE_TILE, 0)
    return pos.astype(i32), tile_group.astype(i32), (ends[-1:] // MOE_TILE).astype(i32), tails.astype(i32)


def _dispatch_kernel(hi_ref, lo_ref, tail_ref, xg_ref, xs_ref, zeros, sem, zero_sem):
    base = pl.program_id(0) * MOVE_TM

    @pl.when(pl.program_id(0) == 0)
    def _():
        zeros[...] = jnp.zeros_like(zeros)
        tile = MOE_TILE // SUBLANES
        def fill(start):
            return pltpu.make_async_copy(zeros, xs_ref.at[pl.ds(start, tile)], zero_sem)

        spare = [fill(xs_ref.shape[0] - (g + 1) * tile) for g in range(N_GROUPS)]
        for f in spare:
            f.start()
        for f in spare:
            f.wait()
        for g in range(N_GROUPS):
            f = fill(tail_ref[g])
            f.start()
            f.wait()

    def send(i, carry):
        for u in range(SUBLANES):
            r = base + i * SUBLANES + u
            pltpu.make_async_copy(xg_ref.at[i, pl.ds(u, 1)],
                                  xs_ref.at[hi_ref[r], pl.ds(lo_ref[r], 1)], sem).start()
        return carry

    lax.fori_loop(0, MOVE_TM // SUBLANES, send, 0)
    pltpu.make_async_copy(xg_ref, xs_ref.at[pl.ds(0, MOVE_TM // SUBLANES)], sem).wait()


def _dispatch(pos_hi, pos_lo, tail_hi, xg, n_rows):
    t = xg.shape[0]
    xs = pl.pallas_call(
        _dispatch_kernel,
        grid_spec=pltpu.PrefetchScalarGridSpec(
            num_scalar_prefetch=3,
            grid=(t // MOVE_TM,),
            in_specs=[pl.BlockSpec((MOVE_TM // SUBLANES, SUBLANES, XG_W), lambda i, hi, lo, tl: (i, 0, 0))],
            out_specs=pl.BlockSpec(memory_space=pl.ANY),
            scratch_shapes=[pltpu.VMEM((MOE_TILE // SUBLANES, SUBLANES, XG_W), F32),
                            pltpu.SemaphoreType.DMA, pltpu.SemaphoreType.DMA],
        ),
        out_shape=jax.ShapeDtypeStruct((n_rows // SUBLANES, SUBLANES, XG_W), F32),
        compiler_params=_cparams(("arbitrary",)),
        name="moe_dispatch",
    )(pos_hi, pos_lo, tail_hi, xg.reshape(t // SUBLANES, SUBLANES, XG_W))
    return xs.reshape(n_rows, XG_W)


def _experts_kernel(tg_ref, nused_ref, xs_ref, w1_ref, w3_ref, w2_ref, ys_ref):
    i = pl.program_id(0)

    @pl.when(i < nused_ref[0])
    def _():
        x = xs_ref[:, :D_MODEL].astype(BF16)
        gate = xs_ref[:, D_MODEL:]
        lane = lax.broadcasted_iota(jnp.int32, gate.shape, 1)
        first = tg_ref[i] * EXPERTS_PER_GROUP
        hidden = []
        for e in range(EXPERTS_PER_GROUP):
            a = jnp.dot(x, w1_ref[0, e], preferred_element_type=F32)
            b = jnp.dot(x, w3_ref[0, e], preferred_element_type=F32)
            g = jnp.sum(jnp.where(lane == first + e, gate, 0.0), axis=-1, keepdims=True)
            hidden.append((a * _sigmoid(a) * b * g).astype(BF16))
        ys_ref[...] = jnp.dot(jnp.concatenate(hidden, axis=-1), w2_ref[0], preferred_element_type=F32)

    @pl.when(i >= nused_ref[0])
    def _():
        ys_ref[...] = jnp.zeros_like(ys_ref)


def _experts(tile_group, n_used, xs, w1g, w3g, w2g):
    n_rows = xs.shape[0]
    up = pl.BlockSpec((1, EXPERTS_PER_GROUP, D_MODEL, EXPERT_FF), lambda i, tg, nu: (tg[i], 0, 0, 0))
    return pl.pallas_call(
        _experts_kernel,
        grid_spec=pltpu.PrefetchScalarGridSpec(
            num_scalar_prefetch=2,
            grid=(n_rows // MOE_TILE,),
            in_specs=[
                pl.BlockSpec((MOE_TILE, XG_W), lambda i, tg, nu: (i, 0)),
                up, up,
                pl.BlockSpec((1, GROUP_FF, D_MODEL), lambda i, tg, nu: (tg[i], 0, 0)),
            ],
            out_specs=pl.BlockSpec((MOE_TILE, D_MODEL), lambda i, tg, nu: (i, 0)),
        ),
        out_shape=jax.ShapeDtypeStruct((n_rows, D_MODEL), F32),
        compiler_params=_cparams(("arbitrary",)),
        name="moe_experts",
    )(tile_group, n_used, xs, w1g, w3g, w2g)


def _combine_kernel(hi_ref, lo_ref, x1_ref, ys_ref, o_ref, buf, sem):
    step = pl.program_id(0)
    slot = step % 2

    def issue(for_step, into):
        base = for_step * MOVE_TM

        def fetch(i, carry):
            for u in range(SUBLANES):
                r = base + i * SUBLANES + u
                pltpu.make_async_copy(ys_ref.at[hi_ref[r], pl.ds(lo_ref[r], 1)],
                                      buf.at[into, i, pl.ds(u, 1)], sem.at[into]).start()
            return carry

        lax.fori_loop(0, MOVE_TM // SUBLANES, fetch, 0)

    @pl.when(step == 0)
    def _():
        issue(step, slot)

    @pl.when(step + 1 < pl.num_programs(0))
    def _():
        issue(step + 1, 1 - slot)

    pltpu.make_async_copy(ys_ref.at[pl.ds(0, MOVE_TM // SUBLANES)], buf.at[slot], sem.at[slot]).wait()
    o_ref[...] = x1_ref[...] + buf[slot]


def _combine(pos_hi, pos_lo, x1, ys):
    t = x1.shape[0]
    tile = (MOVE_TM // SUBLANES, SUBLANES, D_MODEL)
    out = pl.pallas_call(
        _combine_kernel,
        grid_spec=pltpu.PrefetchScalarGridSpec(
            num_scalar_prefetch=2,
            grid=(t // MOVE_TM,),
            in_specs=[pl.BlockSpec(tile, lambda i, hi, lo: (i, 0, 0)),
                      pl.BlockSpec(memory_space=pl.ANY)],
            out_specs=pl.BlockSpec(tile, lambda i, hi, lo: (i, 0, 0)),
            scratch_shapes=[pltpu.VMEM((2,) + tile, F32), pltpu.SemaphoreType.DMA((2,))],
        ),
        out_shape=jax.ShapeDtypeStruct((t // SUBLANES, SUBLANES, D_MODEL), F32),
        compiler_params=_cparams(("arbitrary",)),
        name="moe_combine",
    )(pos_hi, pos_lo, x1.reshape(t // SUBLANES, SUBLANES, D_MODEL),
      ys.reshape(ys.shape[0] // SUBLANES, SUBLANES, D_MODEL))
    return out.reshape(t, D_MODEL)


def _moe(xg, x1, w1, w3, w2):
    t = xg.shape[0]
    w1g = w1.reshape(N_GROUPS, EXPERTS_PER_GROUP, D_MODEL, EXPERT_FF).astype(BF16)
    w3g = w3.reshape(N_GROUPS, EXPERTS_PER_GROUP, D_MODEL, EXPERT_FF).astype(BF16)
    w2g = w2.reshape(N_GROUPS, GROUP_FF, D_MODEL).astype(BF16)
    pos, tile_group, n_used, tails = _moe_plan(xg, t)
    pos_hi, pos_lo = pos // SUBLANES, pos % SUBLANES
    xs = _dispatch(pos_hi, pos_lo, tails // SUBLANES, xg, t + N_GROUPS * MOE_TILE)
    ys = _experts(tile_group, n_used, xs, w1g, w3g, w2g)
    return _combine(pos_hi, pos_lo, x1, ys)


def _class_major(table):
    return table.reshape(ROWS_PER_CLASS, ATT_CLASSES, -1).transpose(1, 0, 2).reshape(SEQ, -1)


def _rope_tables_att():
    pos = jnp.arange(SEQ, dtype=F32)
    inv = ROPE_THETA ** (-jnp.arange(0, ATT_HEAD_DIM, 2, dtype=F32) / ATT_HEAD_DIM)
    ang = pos[:, None] * inv[None, :]
    cos, sin = jnp.cos(ang), jnp.sin(ang)
    cos_full = jnp.concatenate([cos, cos, cos, cos], axis=-1)
    sin_full = jnp.concatenate([-sin, -sin, sin, sin], axis=-1)
    return _class_major(cos_full), _class_major(sin_full)


def _pair_lanes(a):
    half = ATT_HEAD_DIM // 2
    lead = a.shape[:-1]
    a = a.reshape(lead + (-1, 2, 2, half))
    return jnp.swapaxes(a, -3, -2).reshape(lead + (-1,))


def _rope_tables_ret():
    pos = jnp.arange(SEQ, dtype=F32)
    inv = 1.0 / (ROPE_THETA ** jnp.linspace(0.0, 1.0, RET_QK_DIM // 2, dtype=F32))
    ang = pos[:, None] * inv[None, :]
    return jnp.cos(ang), jnp.sin(ang)


def _decay_tables():
    c = RET_CHUNK
    log_gamma = jnp.log(1.0 - jnp.exp2(-5.0 - jnp.arange(RET_HEADS, dtype=F32)))
    idx = jnp.arange(c, dtype=F32)
    diff = idx[:, None] - idx[None, :]
    decay = jnp.where(diff >= 0, jnp.exp(log_gamma[:, None, None] * jnp.maximum(diff, 0.0)), 0.0)
    zeta = jnp.exp(log_gamma[:, None] * (c - 1 - idx))
    xi = jnp.exp(log_gamma[:, None] * (idx + 1))
    gamma_c = jnp.exp(log_gamma * c)
    bc = lambda a: jnp.broadcast_to(a[:, :, None], (RET_HEADS, c, RET_QK_DIM)).astype(BF16)
    return gamma_c, decay, bc(xi), bc(zeta)


def kernel(x, g_norm_mix, w_in, b_merge_gate, g_q, g_k, w_branch_att, g_ret_norm, w_branch_ret,
           w_out, g_norm_ffn, w_router_group, b_router_group, w_router_expert, b_router_expert,
           w1, w3, w2):
    batch = x.shape[0]
    t = batch * SEQ
    cos_a, sin_a = _rope_tables_att()
    cos_r, sin_r = _rope_tables_ret()
    gamma_c, decay, xi, zeta = _decay_tables()
    xf = x.reshape(t, D_MODEL)
    for l in range(g_norm_mix.shape[0]):
        g_mix = g_norm_mix[l][None, :]
        proj_att = _inproj_att(xf, g_mix, _pair_lanes(w_in[l][:, :2 * ATT_W]).astype(BF16), w_in[l])
        proj = _inproj_rest(xf, g_mix, w_in[l], cos_r, sin_r)
        reps = LANES // ATT_HEAD_DIM
        gq = _pair_lanes(jnp.tile(g_q[l], reps))[None, :] * (ATT_HEAD_DIM ** -0.5 * np.log2(np.e))
        y_att = _attention(proj_att, cos_a, sin_a, gq, _pair_lanes(jnp.tile(g_k[l], reps))[None, :],
                           batch)
        y_ret = _retention(proj, gamma_c, decay, xi, zeta, g_ret_norm[l][:, None, :], batch)
        w_route = jnp.concatenate(
            [w_router_expert[l], w_router_group[l],
             jnp.zeros((D_MODEL, LANES - N_ROUTE), F32)], axis=-1)
        wr_hi, wr_lo = _split_bf16(w_route)
        b_route = jnp.concatenate(
            [b_router_expert[l], b_router_group[l], jnp.zeros((LANES - N_ROUTE,), F32)])[None, :]
        x1, xg = _out_stage(
            y_att, y_ret, proj, b_merge_gate[l][None, :], xf,
            w_branch_att[l].astype(BF16), w_branch_ret[l].astype(BF16), w_out[l].astype(BF16),
            g_norm_ffn[l][None, :], wr_hi, wr_lo, b_route)
        xf = _moe(xg, x1, w1[l], w3[l], w2[l])
    return xf.reshape(batch, SEQ, D_MODEL)
```

```python
import functools

import numpy as np

import jax
import jax.numpy as jnp
from jax import lax
from jax.experimental import pallas as pl
from jax.experimental.pallas import tpu as pltpu

F32 = jnp.float32
BF16 = jnp.bfloat16

D_MODEL = 1024
SEQ = 2048
ATT_HEADS = 16
ATT_HEAD_DIM = 64
ATT_W = ATT_HEADS * ATT_HEAD_DIM
ROPE_THETA = 10000.0
RET_HEADS = 4
RET_QK_DIM = 256
RET_V_DIM = 512
RET_QK_W = RET_HEADS * RET_QK_DIM
RET_V_W = RET_HEADS * RET_V_DIM
N_GROUPS = 4
EXPERTS_PER_GROUP = 8
N_EXPERTS = N_GROUPS * EXPERTS_PER_GROUP
EXPERT_FF = 256
EPS = 1e-6
ATT_IN_W = 3 * ATT_W
REST_IN_W = 2 * RET_QK_W + 2 * RET_V_W + 2 * D_MODEL

LANES = 128
SUBLANES = 8
CB_QA, CB_KA, CB_VA = 0, 8, 16
CB_QR, CB_KR, CB_VR, CB_GR, CB_GA, CB_GB = 0, 8, 16, 32, 48, 56

ATT_BLOCK = 128
ATT_CLASSES = 16
ROWS_PER_CLASS = SEQ // ATT_CLASSES
RET_CHUNK = 256
RET_UNROLL = 2
NEG_BIG = -1e30
VMEM_LIMIT = 48 * 1024 * 1024
IN_ATT_VMEM_LIMIT = 56 * 1024 * 1024


def _cparams(sem, vmem_limit=VMEM_LIMIT):
    return pltpu.CompilerParams(dimension_semantics=sem, vmem_limit_bytes=vmem_limit)


IN_TN = 1024
IN_NORM_ROWS = 256
REST_TILE_QR, REST_TILE_KR = CB_QR * LANES // IN_TN, CB_KR * LANES // IN_TN
REST_TILE_GR, REST_TILE_GA = CB_GR * LANES // IN_TN, CB_GA * LANES // IN_TN


def _swish(x):
    h = 0.5 * x
    return h * (jnp.tanh(h) + 1.0)


def _sigmoid(x):
    return 0.5 * jnp.tanh(0.5 * x) + 0.5


def _rmsnorm_rows(x, g_ref):
    ms = jnp.mean(x * x, axis=-1, keepdims=True)
    return x * lax.rsqrt(ms + EPS) * g_ref[...]


def _project(xn_sc, w_ref, o_ref, epilogue, first_block=0):
    xn = xn_sc[...]
    for c2 in range(IN_TN // 256):
        w = w_ref[:, c2 * 256:(c2 + 1) * 256].astype(BF16)
        acc = jnp.dot(xn, w, preferred_element_type=F32)
        lo, hi = epilogue(acc[:, :LANES], acc[:, LANES:])
        o_ref[first_block + 2 * c2] = lo.astype(BF16)
        o_ref[first_block + 2 * c2 + 1] = hi.astype(BF16)


def _inproj_att_kernel(x_ref, g_ref, wqk_ref, wv_ref, o_ref, xn_sc, xs):
    @pl.when(pl.program_id(1) == 0)
    def _():
        def norm_rows(ci, carry):
            rows = pl.ds(pl.multiple_of(ci * IN_NORM_ROWS, IN_NORM_ROWS), IN_NORM_ROWS)
            xn = _rmsnorm_rows(x_ref[rows, :], g_ref)
            for c in range(D_MODEL // LANES):
                xs[c, rows, :] = xn[:, c * LANES:(c + 1) * LANES]
            return carry

        lax.fori_loop(0, SEQ // IN_NORM_ROWS, norm_rows, 0)

        def gather_class(j, carry):
            dst = pl.ds(pl.multiple_of(j * ROWS_PER_CLASS, ROWS_PER_CLASS), ROWS_PER_CLASS)
            for c in range(D_MODEL // LANES):
                xn_sc[dst, c * LANES:(c + 1) * LANES] = xs[
                    c, pl.ds(j, ROWS_PER_CLASS, stride=ATT_CLASSES), :].astype(BF16)
            return carry

        lax.fori_loop(0, ATT_CLASSES, gather_class, 0)

    qk_tiles = 2 * ATT_W // IN_TN

    @pl.when(pl.program_id(1) < qk_tiles)
    def _():
        _project(xn_sc, wqk_ref, o_ref, lambda lo, hi: (lo, hi))

    @pl.when(pl.program_id(1) >= qk_tiles)
    def _():
        _project(xn_sc, wv_ref, o_ref, lambda lo, hi: (lo, hi))


def _inproj_rest_kernel(x_ref, g_ref, w_ref, cos_ref, sin_ref, o_ref, xn_sc):
    j = pl.program_id(1)

    @pl.when(j == 0)
    def _():
        def norm_rows(ci, carry):
            rows = pl.ds(pl.multiple_of(ci * IN_NORM_ROWS, IN_NORM_ROWS), IN_NORM_ROWS)
            xn_sc[rows, :] = _rmsnorm_rows(x_ref[rows, :], g_ref).astype(BF16)
            return carry

        lax.fori_loop(0, SEQ // IN_NORM_ROWS, norm_rows, 0)

    def rotate(scale):
        def epilogue(x1, x2):
            cos, sin = cos_ref[...], sin_ref[...]
            return (x1 * cos - x2 * sin) * scale, (x2 * cos + x1 * sin) * scale
        return epilogue

    def swish(lo, hi):
        return _swish(lo), _swish(hi)

    is_gate = (j >= REST_TILE_GR) & (j < REST_TILE_GA)

    @pl.when(j == REST_TILE_QR)
    def _():
        _project(xn_sc, w_ref, o_ref, rotate(1.0))

    @pl.when(j == REST_TILE_KR)
    def _():
        _project(xn_sc, w_ref, o_ref, rotate(RET_QK_DIM ** -0.5))

    @pl.when(is_gate)
    def _():
        _project(xn_sc, w_ref, o_ref, swish)

    @pl.when((j > REST_TILE_KR) & jnp.logical_not(is_gate))
    def _():
        _project(xn_sc, w_ref, o_ref, lambda lo, hi: (lo, hi))


def _inproj_att(x2d, g, w_qk_f32, w_in_f32):
    t = x2d.shape[0]
    width = ATT_IN_W
    qk_tiles = 2 * ATT_W // IN_TN
    return pl.pallas_call(
        _inproj_att_kernel,
        grid=(t // SEQ, width // IN_TN),
        in_specs=[
            pl.BlockSpec((SEQ, D_MODEL), lambda i, j: (i, 0)),
            pl.BlockSpec((1, D_MODEL), lambda i, j: (0, 0)),
            pl.BlockSpec((D_MODEL, IN_TN), lambda i, j: (0, jnp.minimum(j, qk_tiles - 1))),
            pl.BlockSpec((D_MODEL, IN_TN), lambda i, j: (0, qk_tiles), pipeline_mode=pl.Buffered(1)),
        ],
        out_specs=pl.BlockSpec((IN_TN // LANES, SEQ, LANES), lambda i, j: (j, i, 0)),
        out_shape=jax.ShapeDtypeStruct((width // LANES, t, LANES), BF16),
        scratch_shapes=[pltpu.VMEM((SEQ, D_MODEL), BF16),
                        pltpu.VMEM((D_MODEL // LANES, SEQ, LANES), F32)],
        compiler_params=_cparams(("arbitrary", "arbitrary"), vmem_limit=IN_ATT_VMEM_LIMIT),
        name="inproj_att",
    )(x2d, g, w_qk_f32, w_in_f32)


def _inproj_rest(x2d, g, w_in_f32, cos_r, sin_r):
    t = x2d.shape[0]
    first_tile = ATT_IN_W // IN_TN
    table = pl.BlockSpec((SEQ, LANES), lambda i, j: (0, 0))
    return pl.pallas_call(
        _inproj_rest_kernel,
        grid=(t // SEQ, REST_IN_W // IN_TN),
        in_specs=[
            pl.BlockSpec((SEQ, D_MODEL), lambda i, j: (i, 0)),
            pl.BlockSpec((1, D_MODEL), lambda i, j: (0, 0)),
            pl.BlockSpec((D_MODEL, IN_TN), lambda i, j: (0, first_tile + j)),
            table, table,
        ],
        out_specs=pl.BlockSpec((IN_TN // LANES, SEQ, LANES), lambda i, j: (j, i, 0)),
        out_shape=jax.ShapeDtypeStruct((REST_IN_W // LANES, t, LANES), BF16),
        scratch_shapes=[pltpu.VMEM((SEQ, D_MODEL), BF16)],
        compiler_params=_cparams(("arbitrary", "arbitrary")),
        name="inproj_rest",
    )(x2d, g, w_in_f32, cos_r, sin_r)


PREP_ROWS = 256
PREP_UNROLL = 4
SEG4 = ATT_BLOCK // 4
SEG1 = ATT_BLOCK // ATT_CLASSES


def _att_bias_tables():
    def tile(qpos, kpos):
        d = qpos[:, None] - kpos[None, :]
        one = np.where((d >= 0) & (d <= ATT_BLOCK), 0.0, NEG_BIG).astype(np.float32)
        return np.concatenate([one, one], axis=1)

    u = np.arange(ATT_BLOCK)
    q4 = 4 * (u % SEG4) + u // SEG4
    q1 = ATT_CLASSES * (u % SEG1) + u // SEG1
    w = np.arange(2 * ATT_BLOCK)
    k1 = ATT_CLASSES * (w % SEG1) + w // (2 * SEG1) + ATT_BLOCK * ((w // SEG1) % 2 - 1)
    return (tile(q1, k1), tile(q1, q1), tile(q4, np.concatenate([q4 - ATT_BLOCK, q4])), tile(q4, q4),
            tile(u, u))


def masked_pair(x, head0_mask):
    return jnp.where(head0_mask, x, 0.0).astype(BF16), jnp.where(head0_mask, 0.0, x).astype(BF16)


def _att_kernel(q_ref, k_ref, v_ref, cos_ref, sin_ref, gq_ref, gk_ref,
                b1_ref, b1f_ref, b4_ref, b4f_ref, b16_ref, o_ref,
                qj, kj, vj, qb, kb, vb, acc_sc, m_sc, l_sc, nat):
    lane = lax.broadcasted_iota(jnp.int32, (1, LANES), 1)
    half = ATT_HEAD_DIM // 2
    qk_head0 = (lane // half) % 2 == 0
    v_head0 = lane < ATT_HEAD_DIM
    qk_keep0, qk_keep1 = masked_pair(jnp.ones((1, LANES), F32), qk_head0)
    v_keep0, v_keep1 = masked_pair(jnp.ones((1, LANES), F32), v_head0)
    seg = jnp.where((lax.broadcasted_iota(jnp.int32, (LANES, LANES), 0) // half) % 2
                    == (lax.broadcasted_iota(jnp.int32, (LANES, LANES), 1) // half) % 2,
                    1.0 / ATT_HEAD_DIM, 0.0).astype(BF16)

    def prep(ci, carry):
        rows = pl.ds(pl.multiple_of(ci * PREP_ROWS, PREP_ROWS), PREP_ROWS)
        cos = cos_ref[rows, :]
        sin = sin_ref[rows, :]

        def norm_rope(src, g_ref):
            x = src[0, rows, :].astype(F32)
            ms = jnp.dot((x * x).astype(BF16), seg, preferred_element_type=F32)
            xn = x * lax.rsqrt(ms + EPS) * g_ref[...]
            return xn * cos + pltpu.roll(xn, ATT_HEAD_DIM, 1) * sin

        xq = norm_rope(q_ref, gq_ref)
        qj[rows, :] = xq
        qb[rows, :] = xq.astype(BF16)
        xk = norm_rope(k_ref, gk_ref)
        kj[rows, :] = xk
        xk = xk.astype(BF16)
        kb[0, rows, :] = xk * qk_keep0
        kb[1, rows, :] = xk * qk_keep1
        xv = v_ref[0, rows, :]
        vj[rows, :] = xv.astype(F32)
        vb[0, rows, :] = xv * v_keep0
        vb[1, rows, :] = xv * v_keep1
        return carry

    lax.fori_loop(0, SEQ // PREP_ROWS, prep, 0, unroll=PREP_UNROLL)

    nt_dims = (((1,), (1,)), ((), ()))

    def attend(q, k0, k1, v0, v1, bias):
        w = k0.shape[0]
        s = lax.dot_general(q, jnp.concatenate([k0, k1], axis=0), nt_dims,
                            preferred_element_type=F32) + bias
        m0 = jnp.max(s[:, :w], axis=-1, keepdims=True)
        m1 = jnp.max(s[:, w:], axis=-1, keepdims=True)
        e = jnp.concatenate([jnp.exp2(s[:, :w] - m0), jnp.exp2(s[:, w:] - m1)], axis=1).astype(BF16)
        v2 = jnp.concatenate([jnp.concatenate([v0, jnp.broadcast_to(v_keep0, (w, LANES))], axis=1),
                              jnp.concatenate([v1, jnp.broadcast_to(v_keep1, (w, LANES))], axis=1)], axis=0)
        r = jnp.dot(e, v2, preferred_element_type=F32)
        return r[:, :LANES], jnp.where(v_head0, m0, m1), r[:, LANES:]

    def gather(ref, pieces, lead=()):
        return jnp.concatenate([ref[lead + (rows, slice(None))] for rows in pieces], axis=0)


    def store(p, pieces, n, a, m, l):
        for idx, rows in enumerate(pieces):
            acc_sc[p, rows, :] = a[idx * n:(idx + 1) * n]
            m_sc[p, rows, :] = m[idx * n:(idx + 1) * n]
            l_sc[p, rows, :] = l[idx * n:(idx + 1) * n]

    for j in range(ATT_CLASSES):
        rows = [pl.ds(j * ATT_BLOCK, ATT_BLOCK)]
        a, m, l = attend(qb[rows[0], :], kb[0, rows[0], :], kb[1, rows[0], :],
                         vb[0, rows[0], :], vb[1, rows[0], :], b16_ref[...])
        store(1, rows, ATT_BLOCK, a, m, l)

    for c in range(4):
        def segs(n):
            return [pl.ds((4 * a + c) * ATT_BLOCK + SEG4 * n, SEG4) for a in range(4)]

        for n in range(SEQ // 4 // ATT_BLOCK):
            cur = segs(n)
            keys = cur if n == 0 else segs(n - 1) + cur
            bias = b4f_ref[...] if n == 0 else b4_ref[...]
            a, m, l = attend(gather(qb, cur), gather(kb, keys, (0,)), gather(kb, keys, (1,)),
                             gather(vb, keys, (0,)), gather(vb, keys, (1,)), bias)
            store(0, cur, SEG4, a, m, l)

    for n in range(SEQ // ATT_BLOCK):
        cur = [pl.ds(j * ATT_BLOCK + SEG1 * n, SEG1) for j in range(ATT_CLASSES)]
        if n == 0:
            keys, bias = cur, b1f_ref[...]
        else:
            keys = [pl.ds(j * ATT_BLOCK + SEG1 * n - SEG1, 2 * SEG1) for j in range(ATT_CLASSES)]
            bias = b1_ref[...]
        k0, k1 = masked_pair(gather(kj, keys), qk_head0)
        v0, v1 = masked_pair(gather(vj, keys), v_head0)
        a, m, l = attend(gather(qj, cur).astype(BF16), k0, k1, v0, v1, bias)
        stats = [(a, m, l)] + [(gather(acc_sc, cur, (p,)), gather(m_sc, cur, (p,)), gather(l_sc, cur, (p,)))
                               for p in range(2)]
        m_all = jnp.maximum(jnp.maximum(stats[0][1], stats[1][1]), stats[2][1])
        num = jnp.zeros((ATT_BLOCK, LANES), F32)
        den = jnp.zeros((ATT_BLOCK, LANES), F32)
        for ap, mp, lp in stats:
            w = jnp.exp2(mp - m_all)
            num = num + w * ap
            den = den + w * lp
        merged = num / den
        for j in range(ATT_CLASSES):
            nat[pl.ds(ATT_BLOCK * n + j, SEG1, stride=ATT_CLASSES), :] = merged[j * SEG1:(j + 1) * SEG1]

    def emit(ci, carry):
        rows = pl.ds(pl.multiple_of(ci * PREP_ROWS, PREP_ROWS), PREP_ROWS)
        o_ref[rows, :] = nat[rows, :].astype(BF16)
        return carry

    lax.fori_loop(0, SEQ // PREP_ROWS, emit, 0)


def _attention(proj_att, cos_a, sin_a, gq, gk, batch):
    t = batch * SEQ
    hp = ATT_W // LANES

    def col(cb):
        return pl.BlockSpec((1, SEQ, LANES), lambda b, h: (cb + h, b, 0))

    def const(shape):
        return pl.BlockSpec(shape, lambda b, h: tuple(0 for _ in shape))

    biases = [jnp.asarray(b) for b in _att_bias_tables()]
    row_f32 = pltpu.VMEM((SEQ, LANES), F32)
    row_bf16 = pltpu.VMEM((SEQ, LANES), BF16)
    stat = pltpu.VMEM((2, SEQ, LANES), F32)
    return pl.pallas_call(
        _att_kernel,
        grid=(batch, hp),
        in_specs=[col(CB_QA), col(CB_KA), col(CB_VA),
                  const((SEQ, LANES)), const((SEQ, LANES)), const((1, LANES)), const((1, LANES))]
                 + [const(b.shape) for b in biases],
        out_specs=pl.BlockSpec((SEQ, LANES), lambda b, h: (b, h)),
        out_shape=jax.ShapeDtypeStruct((t, ATT_W), BF16),
        scratch_shapes=([row_f32] * 3 + [row_bf16] + [pltpu.VMEM((2, SEQ, LANES), BF16)] * 2 + [stat] * 3
                        + [row_f32]),
        compiler_params=_cparams(("arbitrary", "arbitrary")),
        name="dilated_attention",
    )(proj_att, proj_att, proj_att, cos_a, sin_a, gq, gk, *biases)


def _ret_kernel(gam_ref, q_ref, k_ref, v_ref, g_ref, decay_ref, xi_ref, zeta_ref,
                gn_ref, o_ref, state_sc):
    h = pl.program_id(1)
    gamma_c = gam_ref[h]
    state_sc[...] = jnp.zeros_like(state_sc)
    c = RET_CHUNK
    nt_dims = (((1,), (1,)), ((), ()))
    tn_dims = (((0,), (0,)), ((), ()))

    def cols(ref, rs, n):
        return jnp.concatenate([ref[i, rs, :] for i in range(n)], axis=-1)

    def chunk(n, carry):
        rs = pl.ds(pl.multiple_of(n * c, c), c)
        q = cols(q_ref, rs, RET_QK_DIM // LANES)
        k = cols(k_ref, rs, RET_QK_DIM // LANES)
        v = cols(v_ref, rs, RET_V_DIM // LANES)
        inner = lax.dot_general(q, k, nt_dims, preferred_element_type=F32) * decay_ref[0]
        y = jnp.dot(inner.astype(BF16), v, preferred_element_type=F32)
        state = state_sc[...]
        y = y + jnp.dot(q * xi_ref[0], state.astype(BF16), preferred_element_type=F32)
        state_sc[...] = state * gamma_c + lax.dot_general(k * zeta_ref[0], v, tn_dims,
                                                          preferred_element_type=F32)
        yn = y * lax.rsqrt(jnp.mean(y * y, axis=-1, keepdims=True) + EPS) * gn_ref[0]
        o_ref[rs, :] = (yn * cols(g_ref, rs, RET_V_DIM // LANES).astype(F32)).astype(BF16)
        return carry

    lax.fori_loop(0, SEQ // c, chunk, 0, unroll=RET_UNROLL)


def _retention(proj, gamma_c, decay, xi, zeta, g_ret, batch):
    t = batch * SEQ
    nq = RET_QK_DIM // LANES
    nv = RET_V_DIM // LANES

    def cols(cb, n):
        return pl.BlockSpec((n, SEQ, LANES), lambda b, h: (cb // n + h, b, 0))

    return pl.pallas_call(
        _ret_kernel,
        grid=(batch, RET_HEADS),
        in_specs=[
            pl.BlockSpec(memory_space=pltpu.SMEM),
            cols(CB_QR, nq), cols(CB_KR, nq), cols(CB_VR, nv), cols(CB_GR, nv),
            pl.BlockSpec((1, RET_CHUNK, RET_CHUNK), lambda b, h: (h, 0, 0)),
            pl.BlockSpec((1, RET_CHUNK, RET_QK_DIM), lambda b, h: (h, 0, 0)),
            pl.BlockSpec((1, RET_CHUNK, RET_QK_DIM), lambda b, h: (h, 0, 0)),
            pl.BlockSpec((1, 1, RET_V_DIM), lambda b, h: (h, 0, 0)),
        ],
        out_specs=pl.BlockSpec((SEQ, RET_V_DIM), lambda b, h: (b, h)),
        out_shape=jax.ShapeDtypeStruct((t, RET_V_W), BF16),
        scratch_shapes=[pltpu.VMEM((RET_QK_DIM, RET_V_DIM), F32)],
        compiler_params=_cparams(("arbitrary", "arbitrary")),
        name="retention",
    )(gamma_c, proj, proj, proj, proj, decay, xi, zeta, g_ret)


OUT_TM = 512
N_ROUTE = N_EXPERTS + N_GROUPS
GROUP_LANE = LANES - 1
XG_W = D_MODEL + LANES


def _split_bf16(x):
    hi = x.astype(BF16)
    lo = (x - hi.astype(F32)).astype(BF16)
    return hi, lo


def _out_kernel(ya_ref, yr_ref, ga_ref, gb_ref, bg_ref, x_ref, watt_ref, wret_ref, wout_ref,
                gffn_ref, wr_hi_ref, wr_lo_ref, br_ref, x1_ref, xg_ref):
    ya = jnp.dot(ya_ref[...], watt_ref[...], preferred_element_type=F32)
    yr = jnp.dot(yr_ref[...], wret_ref[...], preferred_element_type=F32)
    merged = []
    for cb in range(D_MODEL // LANES):
        cs = slice(cb * LANES, (cb + 1) * LANES)
        ga = _sigmoid(ga_ref[cb].astype(F32) + bg_ref[:, cs])
        gb = _sigmoid(gb_ref[cb].astype(F32) + bg_ref[:, D_MODEL + cb * LANES:D_MODEL + (cb + 1) * LANES])
        merged.append((ga * ya[:, cs] + gb * yr[:, cs]).astype(BF16))
    merged = jnp.concatenate(merged, axis=-1)
    x1 = x_ref[...] + jnp.dot(merged, wout_ref[...], preferred_element_type=F32)
    x1_ref[...] = x1
    xn = x1 * lax.rsqrt(jnp.mean(x1 * x1, axis=-1, keepdims=True) + EPS) * gffn_ref[...]
    xg_ref[:, :D_MODEL] = xn

    xh, xl = _split_bf16(xn)
    logits = (jnp.dot(xh, wr_hi_ref[...], preferred_element_type=F32)
              + jnp.dot(xl, wr_hi_ref[...], preferred_element_type=F32)
              + jnp.dot(xh, wr_lo_ref[...], preferred_element_type=F32)) + br_ref[...]
    lane = lax.broadcasted_iota(jnp.int32, logits.shape, 1)
    lane_f = lane.astype(F32)
    is_group = (lane >= N_EXPERTS) & (lane < N_ROUTE)
    gl = jnp.where(is_group, logits, NEG_BIG)
    gmax = jnp.max(gl, axis=-1, keepdims=True)
    gsel = jnp.min(jnp.where(gl == gmax, lane_f, 1e9), axis=-1, keepdims=True) - N_EXPERTS
    p_group = 1.0 / jnp.sum(jnp.where(is_group, jnp.exp(logits - gmax), 0.0), axis=-1, keepdims=True)
    lo_lane = gsel * EXPERTS_PER_GROUP
    in_group = (lane_f >= lo_lane) & (lane_f < lo_lane + EXPERTS_PER_GROUP)
    el = jnp.where(in_group, logits, NEG_BIG)
    v1 = jnp.max(el, axis=-1, keepdims=True)
    i1 = jnp.min(jnp.where(el == v1, lane_f, 1e9), axis=-1, keepdims=True)
    el2 = jnp.where(lane_f == i1, NEG_BIG, el)
    v2 = jnp.max(el2, axis=-1, keepdims=True)
    i2 = jnp.min(jnp.where(el2 == v2, lane_f, 1e9), axis=-1, keepdims=True)
    e21 = jnp.exp(v2 - v1)
    w1 = p_group / (1.0 + e21)
    w2 = w1 * e21
    gate = jnp.where(lane_f == i1, w1, 0.0) + jnp.where(lane_f == i2, w2, 0.0)
    xg_ref[:, D_MODEL:] = jnp.where(lane == GROUP_LANE, gsel, gate)


def _out_stage(y_att, y_ret, proj, b_gate, x2d, w_att, w_ret, w_out, g_ffn, wr_hi, wr_lo, b_route):
    t = x2d.shape[0]
    ncb = D_MODEL // LANES

    def full(shape):
        return pl.BlockSpec(shape, lambda i: tuple(0 for _ in shape))

    return pl.pallas_call(
        _out_kernel,
        grid=(t // OUT_TM,),
        in_specs=[
            pl.BlockSpec((OUT_TM, ATT_W), lambda i: (i, 0)),
            pl.BlockSpec((OUT_TM, RET_V_W), lambda i: (i, 0)),
            pl.BlockSpec((ncb, OUT_TM, LANES), lambda i: (CB_GA // ncb, i, 0)),
            pl.BlockSpec((ncb, OUT_TM, LANES), lambda i: (CB_GB // ncb, i, 0)),
            full((1, 2 * D_MODEL)),
            pl.BlockSpec((OUT_TM, D_MODEL), lambda i: (i, 0)),
            full((ATT_W, D_MODEL)), full((RET_V_W, D_MODEL)), full((D_MODEL, D_MODEL)),
            full((1, D_MODEL)), full((D_MODEL, LANES)), full((D_MODEL, LANES)), full((1, LANES)),
        ],
        out_specs=[
            pl.BlockSpec((OUT_TM, D_MODEL), lambda i: (i, 0)),
            pl.BlockSpec((OUT_TM, XG_W), lambda i: (i, 0)),
        ],
        out_shape=[
            jax.ShapeDtypeStruct((t, D_MODEL), F32),
            jax.ShapeDtypeStruct((t, XG_W), F32),
        ],
        compiler_params=_cparams(("arbitrary",)),
        name="out_stage",
    )(y_att, y_ret, proj, proj, b_gate, x2d, w_att, w_ret, w_out, g_ffn, wr_hi, wr_lo, b_route)


MOE_TILE = 512
MOVE_TM = 1024
GROUP_FF = EXPERTS_PER_GROUP * EXPERT_FF


def _moe_plan(xg, t):
    i32 = jnp.int32
    g = xg[:, D_MODEL + GROUP_LANE].astype(i32)
    onehot = (g[:, None] == jnp.arange(N_GROUPS, dtype=i32)[None, :]).astype(i32)
    csum = jnp.cumsum(onehot, axis=0)
    rank = jnp.sum(onehot * csum, axis=1) - 1
    padded = (csum[-1] + MOE_TILE - 1) // MOE_TILE * MOE_TILE
    ends = jnp.cumsum(padded)
    pos = rank + jnp.sum(onehot * (ends - padded)[None, :], axis=1)
    tile_start = jnp.arange(t // MOE_TILE + N_GROUPS, dtype=i32) * MOE_TILE
    tile_group = jnp.minimum(jnp.sum((tile_start[:, None] >= ends[None, :]).astype(i32), axis=1),
                             N_GROUPS - 1)
    tails = jnp.maximum(ends - MOE_TILE, 0)
    return pos.astype(i32), tile_group.astype(i32), (ends[-1:] // MOE_TILE).astype(i32), tails.astype(i32)


def _dispatch_kernel(hi_ref, lo_ref, tail_ref, xg_ref, xs_ref, zeros, sem, zero_sem):
    base = pl.program_id(0) * MOVE_TM

    @pl.when(pl.program_id(0) == 0)
    def _():
        zeros[...] = jnp.zeros_like(zeros)
        tile = MOE_TILE // SUBLANES
        def fill(start):
            return pltpu.make_async_copy(zeros, xs_ref.at[pl.ds(start, tile)], zero_sem)

        spare = [fill(xs_ref.shape[0] - (g + 1) * tile) for g in range(N_GROUPS)]
        for f in spare:
            f.start()
        for f in spare:
            f.wait()
        for g in range(N_GROUPS):
            f = fill(tail_ref[g])
            f.start()
            f.wait()

    def send(i, carry):
        for u in range(SUBLANES):
            r = base + i * SUBLANES + u
            pltpu.make_async_copy(xg_ref.at[i, pl.ds(u, 1)],
                                  xs_ref.at[hi_ref[r], pl.ds(lo_ref[r], 1)], sem).start()
        return carry

    lax.fori_loop(0, MOVE_TM // SUBLANES, send, 0)
    pltpu.make_async_copy(xg_ref, xs_ref.at[pl.ds(0, MOVE_TM // SUBLANES)], sem).wait()


def _dispatch(pos_hi, pos_lo, tail_hi, xg, n_rows):
    t = xg.shape[0]
    xs = pl.pallas_call(
        _dispatch_kernel,
        grid_spec=pltpu.PrefetchScalarGridSpec(
            num_scalar_prefetch=3,
            grid=(t // MOVE_TM,),
            in_specs=[pl.BlockSpec((MOVE_TM // SUBLANES, SUBLANES, XG_W), lambda i, hi, lo, tl: (i, 0, 0))],
            out_specs=pl.BlockSpec(memory_space=pl.ANY),
            scratch_shapes=[pltpu.VMEM((MOE_TILE // SUBLANES, SUBLANES, XG_W), F32),
                            pltpu.SemaphoreType.DMA, pltpu.SemaphoreType.DMA],
        ),
        out_shape=jax.ShapeDtypeStruct((n_rows // SUBLANES, SUBLANES, XG_W), F32),
        compiler_params=_cparams(("arbitrary",)),
        name="moe_dispatch",
    )(pos_hi, pos_lo, tail_hi, xg.reshape(t // SUBLANES, SUBLANES, XG_W))
    return xs.reshape(n_rows, XG_W)


def _experts_kernel(tg_ref, nused_ref, xs_ref, w1_ref, w3_ref, w2_ref, ys_ref):
    i = pl.program_id(0)

    @pl.when(i < nused_ref[0])
    def _():
        x = xs_ref[:, :D_MODEL].astype(BF16)
        gate = xs_ref[:, D_MODEL:]
        lane = lax.broadcasted_iota(jnp.int32, gate.shape, 1)
        first = tg_ref[i] * EXPERTS_PER_GROUP
        hidden = []
        for e in range(EXPERTS_PER_GROUP):
            a = jnp.dot(x, w1_ref[0, e], preferred_element_type=F32)
            b = jnp.dot(x, w3_ref[0, e], preferred_element_type=F32)
            g = jnp.sum(jnp.where(lane == first + e, gate, 0.0), axis=-1, keepdims=True)
            hidden.append((_swish(a) * b * g).astype(BF16))
        ys_ref[...] = jnp.dot(jnp.concatenate(hidden, axis=-1), w2_ref[0], preferred_element_type=F32)

    @pl.when(i >= nused_ref[0])
    def _():
        ys_ref[...] = jnp.zeros_like(ys_ref)


def _experts(tile_group, n_used, xs, w1g, w3g, w2g):
    n_rows = xs.shape[0]
    up = pl.BlockSpec((1, EXPERTS_PER_GROUP, D_MODEL, EXPERT_FF), lambda i, tg, nu: (tg[i], 0, 0, 0))
    return pl.pallas_call(
        _experts_kernel,
        grid_spec=pltpu.PrefetchScalarGridSpec(
            num_scalar_prefetch=2,
            grid=(n_rows // MOE_TILE,),
            in_specs=[
                pl.BlockSpec((MOE_TILE, XG_W), lambda i, tg, nu: (i, 0)),
                up, up,
                pl.BlockSpec((1, GROUP_FF, D_MODEL), lambda i, tg, nu: (tg[i], 0, 0)),
            ],
            out_specs=pl.BlockSpec((MOE_TILE, D_MODEL), lambda i, tg, nu: (i, 0)),
        ),
        out_shape=jax.ShapeDtypeStruct((n_rows, D_MODEL), F32),
        compiler_params=_cparams(("arbitrary",)),
        name="moe_experts",
    )(tile_group, n_used, xs, w1g, w3g, w2g)


def _combine_kernel(hi_ref, lo_ref, x1_ref, ys_ref, o_ref, buf, sem):
    step = pl.program_id(0)
    slot = step % 2

    def issue(for_step, into):
        base = for_step * MOVE_TM

        def fetch(i, carry):
            for u in range(SUBLANES):
                r = base + i * SUBLANES + u
                pltpu.make_async_copy(ys_ref.at[hi_ref[r], pl.ds(lo_ref[r], 1)],
                                      buf.at[into, i, pl.ds(u, 1)], sem.at[into]).start()
            return carry

        lax.fori_loop(0, MOVE_TM // SUBLANES, fetch, 0)

    @pl.when(step == 0)
    def _():
        issue(step, slot)

    @pl.when(step + 1 < pl.num_programs(0))
    def _():
        issue(step + 1, 1 - slot)

    pltpu.make_async_copy(ys_ref.at[pl.ds(0, MOVE_TM // SUBLANES)], buf.at[slot], sem.at[slot]).wait()
    o_ref[...] = x1_ref[...] + buf[slot]


def _combine(pos_hi, pos_lo, x1, ys):
    t = x1.shape[0]
    tile = (MOVE_TM // SUBLANES, SUBLANES, D_MODEL)
    out = pl.pallas_call(
        _combine_kernel,
        grid_spec=pltpu.PrefetchScalarGridSpec(
            num_scalar_prefetch=2,
            grid=(t // MOVE_TM,),
            in_specs=[pl.BlockSpec(tile, lambda i, hi, lo: (i, 0, 0)),
                      pl.BlockSpec(memory_space=pl.ANY)],
            out_specs=pl.BlockSpec(tile, lambda i, hi, lo: (i, 0, 0)),
            scratch_shapes=[pltpu.VMEM((2,) + tile, F32), pltpu.SemaphoreType.DMA((2,))],
        ),
        out_shape=jax.ShapeDtypeStruct((t // SUBLANES, SUBLANES, D_MODEL), F32),
        compiler_params=_cparams(("arbitrary",)),
        name="moe_combine",
    )(pos_hi, pos_lo, x1.reshape(t // SUBLANES, SUBLANES, D_MODEL),
      ys.reshape(ys.shape[0] // SUBLANES, SUBLANES, D_MODEL))
    return out.reshape(t, D_MODEL)


def _moe(xg, x1, w1, w3, w2):
    t = xg.shape[0]
    w1g = w1.reshape(N_GROUPS, EXPERTS_PER_GROUP, D_MODEL, EXPERT_FF).astype(BF16)
    w3g = w3.reshape(N_GROUPS, EXPERTS_PER_GROUP, D_MODEL, EXPERT_FF).astype(BF16)
    w2g = w2.reshape(N_GROUPS, GROUP_FF, D_MODEL).astype(BF16)
    pos, tile_group, n_used, tails = _moe_plan(xg, t)
    pos_hi, pos_lo = pos // SUBLANES, pos % SUBLANES
    xs = _dispatch(pos_hi, pos_lo, tails // SUBLANES, xg, t + N_GROUPS * MOE_TILE)
    ys = _experts(tile_group, n_used, xs, w1g, w3g, w2g)
    return _combine(pos_hi, pos_lo, x1, ys)


def _class_major(table):
    return table.reshape(ROWS_PER_CLASS, ATT_CLASSES, -1).transpose(1, 0, 2).reshape(SEQ, -1)


def _rope_tables_att():
    pos = jnp.arange(SEQ, dtype=F32)
    inv = ROPE_THETA ** (-jnp.arange(0, ATT_HEAD_DIM, 2, dtype=F32) / ATT_HEAD_DIM)
    ang = pos[:, None] * inv[None, :]
    cos, sin = jnp.cos(ang), jnp.sin(ang)
    cos_full = jnp.concatenate([cos, cos, cos, cos], axis=-1)
    sin_full = jnp.concatenate([-sin, -sin, sin, sin], axis=-1)
    return _class_major(cos_full), _class_major(sin_full)


def _pair_lanes(a):
    half = ATT_HEAD_DIM // 2
    lead = a.shape[:-1]
    a = a.reshape(lead + (-1, 2, 2, half))
    return jnp.swapaxes(a, -3, -2).reshape(lead + (-1,))


def _rope_tables_ret():
    pos = jnp.arange(SEQ, dtype=F32)
    inv = 1.0 / (ROPE_THETA ** jnp.linspace(0.0, 1.0, RET_QK_DIM // 2, dtype=F32))
    ang = pos[:, None] * inv[None, :]
    return jnp.cos(ang), jnp.sin(ang)


def _decay_tables():
    c = RET_CHUNK
    log_gamma = jnp.log(1.0 - jnp.exp2(-5.0 - jnp.arange(RET_HEADS, dtype=F32)))
    idx = jnp.arange(c, dtype=F32)
    diff = idx[:, None] - idx[None, :]
    decay = jnp.where(diff >= 0, jnp.exp(log_gamma[:, None, None] * jnp.maximum(diff, 0.0)), 0.0)
    zeta = jnp.exp(log_gamma[:, None] * (c - 1 - idx))
    xi = jnp.exp(log_gamma[:, None] * (idx + 1))
    gamma_c = jnp.exp(log_gamma * c)
    bc = lambda a: jnp.broadcast_to(a[:, :, None], (RET_HEADS, c, RET_QK_DIM)).astype(BF16)
    return gamma_c, decay, bc(xi), bc(zeta)


def kernel(x, g_norm_mix, w_in, b_merge_gate, g_q, g_k, w_branch_att, g_ret_norm, w_branch_ret,
           w_out, g_norm_ffn, w_router_group, b_router_group, w_router_expert, b_router_expert,
           w1, w3, w2):
    batch = x.shape[0]
    t = batch * SEQ
    cos_a, sin_a = _rope_tables_att()
    cos_r, sin_r = _rope_tables_ret()
    gamma_c, decay, xi, zeta = _decay_tables()
    xf = x.reshape(t, D_MODEL)
    for l in range(g_norm_mix.shape[0]):
        g_mix = g_norm_mix[l][None, :]
        proj_att = _inproj_att(xf, g_mix, _pair_lanes(w_in[l][:, :2 * ATT_W]).astype(BF16), w_in[l])
        proj = _inproj_rest(xf, g_mix, w_in[l], cos_r, sin_r)
        reps = LANES // ATT_HEAD_DIM
        gq = _pair_lanes(jnp.tile(g_q[l], reps))[None, :] * (ATT_HEAD_DIM ** -0.5 * np.log2(np.e))
        y_att = _attention(proj_att, cos_a, sin_a, gq, _pair_lanes(jnp.tile(g_k[l], reps))[None, :],
                           batch)
        y_ret = _retention(proj, gamma_c, decay, xi, zeta, g_ret_norm[l][:, None, :], batch)
        w_route = jnp.concatenate(
            [w_router_expert[l], w_router_group[l],
             jnp.zeros((D_MODEL, LANES - N_ROUTE), F32)], axis=-1)
        wr_hi, wr_lo = _split_bf16(w_route)
        b_route = jnp.concatenate(
            [b_router_expert[l], b_router_group[l], jnp.zeros((LANES - N_ROUTE,), F32)])[None, :]
        x1, xg = _out_stage(
            y_att, y_ret, proj, b_merge_gate[l][None, :], xf,
            w_branch_att[l].astype(BF16), w_branch_ret[l].astype(BF16), w_out[l].astype(BF16),
            g_norm_ffn[l][None, :], wr_hi, wr_lo, b_route)
        xf = _moe(xg, x1, w1[l], w3[l], w2[l])
    return xf.reshape(batch, SEQ, D_MODEL)
```

```python
import numpy as np

import jax
import jax.numpy as jnp
from jax import lax
from jax.experimental import pallas as pl
from jax.experimental.pallas import tpu as pltpu

F32 = jnp.float32
BF16 = jnp.bfloat16

D_MODEL = 1024
SEQ = 2048
ATT_HEADS = 16
ATT_HEAD_DIM = 64
ATT_W = ATT_HEADS * ATT_HEAD_DIM
ROPE_THETA = 10000.0
RET_HEADS = 4
RET_QK_DIM = 256
RET_V_DIM = 512
RET_QK_W = RET_HEADS * RET_QK_DIM
RET_V_W = RET_HEADS * RET_V_DIM
N_GROUPS = 4
EXPERTS_PER_GROUP = 8
N_EXPERTS = N_GROUPS * EXPERTS_PER_GROUP
EXPERT_FF = 256
EPS = 1e-6
ATT_IN_W = 3 * ATT_W
REST_IN_W = 2 * RET_QK_W + 2 * RET_V_W + 2 * D_MODEL

LANES = 128
SUBLANES = 8
MXU_WIDTH = 256
CB_QA, CB_KA, CB_VA = 0, 8, 16
CB_QR, CB_KR, CB_VR, CB_GR, CB_GA, CB_GB = 0, 8, 16, 32, 48, 56

ATT_BLOCK = 128
ATT_CLASSES = 16
ROWS_PER_CLASS = SEQ // ATT_CLASSES
RET_CHUNK = 256
RET_UNROLL = 2
NEG_BIG = -1e30
VMEM_LIMIT = 48 * 1024 * 1024
IN_ATT_VMEM_LIMIT = 56 * 1024 * 1024


def _cparams(sem, vmem_limit=VMEM_LIMIT):
    return pltpu.CompilerParams(dimension_semantics=sem, vmem_limit_bytes=vmem_limit)


IN_TN = 1024
IN_NORM_ROWS = 256
REST_TILE_QR, REST_TILE_KR = CB_QR * LANES // IN_TN, CB_KR * LANES // IN_TN
REST_TILE_GR, REST_TILE_GA = CB_GR * LANES // IN_TN, CB_GA * LANES // IN_TN


def _sigmoid(x):
    return 0.5 * jnp.tanh(0.5 * x) + 0.5


def _rmsnorm_rows(x, g_ref):
    ms = jnp.mean(x * x, axis=-1, keepdims=True)
    return x * lax.rsqrt(ms + EPS) * g_ref[...]


def _project(xn_sc, w_ref, o_ref, epilogue, first_block=0):
    xn = xn_sc[...]
    for c2 in range(IN_TN // MXU_WIDTH):
        w = w_ref[:, c2 * MXU_WIDTH:(c2 + 1) * MXU_WIDTH].astype(BF16)
        acc = jnp.dot(xn, w, preferred_element_type=F32)
        lo, hi = epilogue(acc[:, :LANES], acc[:, LANES:])
        o_ref[first_block + 2 * c2] = lo.astype(BF16)
        o_ref[first_block + 2 * c2 + 1] = hi.astype(BF16)


def _inproj_att_kernel(x_ref, g_ref, wqk_ref, wv_ref, o_ref, xn_sc, xs):
    @pl.when(pl.program_id(1) == 0)
    def _():
        def norm_rows(ci, carry):
            rows = pl.ds(pl.multiple_of(ci * IN_NORM_ROWS, IN_NORM_ROWS), IN_NORM_ROWS)
            xn = _rmsnorm_rows(x_ref[rows, :], g_ref)
            for c in range(D_MODEL // LANES):
                xs[c, rows, :] = xn[:, c * LANES:(c + 1) * LANES]
            return carry

        lax.fori_loop(0, SEQ // IN_NORM_ROWS, norm_rows, 0)

        def gather_class(j, carry):
            dst = pl.ds(pl.multiple_of(j * ROWS_PER_CLASS, ROWS_PER_CLASS), ROWS_PER_CLASS)
            for c in range(D_MODEL // LANES):
                xn_sc[dst, c * LANES:(c + 1) * LANES] = xs[
                    c, pl.ds(j, ROWS_PER_CLASS, stride=ATT_CLASSES), :].astype(BF16)
            return carry

        lax.fori_loop(0, ATT_CLASSES, gather_class, 0)

    qk_tiles = 2 * ATT_W // IN_TN

    @pl.when(pl.program_id(1) < qk_tiles)
    def _():
        _project(xn_sc, wqk_ref, o_ref, lambda lo, hi: (lo, hi))

    @pl.when(pl.program_id(1) >= qk_tiles)
    def _():
        _project(xn_sc, wv_ref, o_ref, lambda lo, hi: (lo, hi))


def _inproj_rest_kernel(x_ref, g_ref, w_ref, cos_ref, sin_ref, o_ref, xn_sc):
    j = pl.program_id(1)

    @pl.when(j == 0)
    def _():
        def norm_rows(ci, carry):
            rows = pl.ds(pl.multiple_of(ci * IN_NORM_ROWS, IN_NORM_ROWS), IN_NORM_ROWS)
            xn_sc[rows, :] = _rmsnorm_rows(x_ref[rows, :], g_ref).astype(BF16)
            return carry

        lax.fori_loop(0, SEQ // IN_NORM_ROWS, norm_rows, 0)

    def rotate(scale):
        def epilogue(x1, x2):
            cos, sin = cos_ref[...], sin_ref[...]
            return (x1 * cos - x2 * sin) * scale, (x2 * cos + x1 * sin) * scale
        return epilogue

    def swish(lo, hi):
        return lo * _sigmoid(lo), hi * _sigmoid(hi)

    is_gate = (j >= REST_TILE_GR) & (j < REST_TILE_GA)

    @pl.when(j == REST_TILE_QR)
    def _():
        _project(xn_sc, w_ref, o_ref, rotate(1.0))

    @pl.when(j == REST_TILE_KR)
    def _():
        _project(xn_sc, w_ref, o_ref, rotate(RET_QK_DIM ** -0.5))

    @pl.when(is_gate)
    def _():
        _project(xn_sc, w_ref, o_ref, swish)

    @pl.when((j > REST_TILE_KR) & jnp.logical_not(is_gate))
    def _():
        _project(xn_sc, w_ref, o_ref, lambda lo, hi: (lo, hi))


def _inproj_att(x2d, g, w_qk_f32, w_in_f32):
    t = x2d.shape[0]
    width = ATT_IN_W
    qk_tiles = 2 * ATT_W // IN_TN
    return pl.pallas_call(
        _inproj_att_kernel,
        grid=(t // SEQ, width // IN_TN),
        in_specs=[
            pl.BlockSpec((SEQ, D_MODEL), lambda i, j: (i, 0)),
            pl.BlockSpec((1, D_MODEL), lambda i, j: (0, 0)),
            pl.BlockSpec((D_MODEL, IN_TN), lambda i, j: (0, jnp.minimum(j, qk_tiles - 1))),
            pl.BlockSpec((D_MODEL, IN_TN), lambda i, j: (0, qk_tiles), pipeline_mode=pl.Buffered(1)),
        ],
        out_specs=pl.BlockSpec((IN_TN // LANES, SEQ, LANES), lambda i, j: (j, i, 0)),
        out_shape=jax.ShapeDtypeStruct((width // LANES, t, LANES), BF16),
        scratch_shapes=[pltpu.VMEM((SEQ, D_MODEL), BF16),
                        pltpu.VMEM((D_MODEL // LANES, SEQ, LANES), F32)],
        compiler_params=_cparams(("arbitrary", "arbitrary"), vmem_limit=IN_ATT_VMEM_LIMIT),
        name="inproj_att",
    )(x2d, g, w_qk_f32, w_in_f32)


def _inproj_rest(x2d, g, w_in_f32, cos_r, sin_r):
    t = x2d.shape[0]
    first_tile = ATT_IN_W // IN_TN
    table = pl.BlockSpec((SEQ, LANES), lambda i, j: (0, 0))
    return pl.pallas_call(
        _inproj_rest_kernel,
        grid=(t // SEQ, REST_IN_W // IN_TN),
        in_specs=[
            pl.BlockSpec((SEQ, D_MODEL), lambda i, j: (i, 0)),
            pl.BlockSpec((1, D_MODEL), lambda i, j: (0, 0)),
            pl.BlockSpec((D_MODEL, IN_TN), lambda i, j: (0, first_tile + j)),
            table, table,
        ],
        out_specs=pl.BlockSpec((IN_TN // LANES, SEQ, LANES), lambda i, j: (j, i, 0)),
        out_shape=jax.ShapeDtypeStruct((REST_IN_W // LANES, t, LANES), BF16),
        scratch_shapes=[pltpu.VMEM((SEQ, D_MODEL), BF16)],
        compiler_params=_cparams(("arbitrary", "arbitrary")),
        name="inproj_rest",
    )(x2d, g, w_in_f32, cos_r, sin_r)


PREP_ROWS = 256
PREP_UNROLL = 4
SEG4 = ATT_BLOCK // 4
SEG1 = ATT_BLOCK // ATT_CLASSES


def _att_bias_tables():
    def tile(qpos, kpos):
        d = qpos[:, None] - kpos[None, :]
        one = np.where((d >= 0) & (d <= ATT_BLOCK), 0.0, NEG_BIG).astype(np.float32)
        return np.concatenate([one, one], axis=1)

    u = np.arange(ATT_BLOCK)
    q4 = 4 * (u % SEG4) + u // SEG4
    q1 = ATT_CLASSES * (u % SEG1) + u // SEG1
    w = np.arange(2 * ATT_BLOCK)
    k1 = ATT_CLASSES * (w % SEG1) + w // (2 * SEG1) + ATT_BLOCK * ((w // SEG1) % 2 - 1)
    return (tile(q1, k1), tile(q1, q1), tile(q4, np.concatenate([q4 - ATT_BLOCK, q4])), tile(q4, q4),
            tile(u, u))


def masked_pair(x, head0_mask):
    return jnp.where(head0_mask, x, 0.0).astype(BF16), jnp.where(head0_mask, 0.0, x).astype(BF16)


def _att_kernel(q_ref, k_ref, v_ref, cos_ref, sin_ref, gq_ref, gk_ref,
                b1_ref, b1f_ref, b4_ref, b4f_ref, b16_ref, o_ref,
                qj, kj, vj, qb, kb, vb, acc_sc, m_sc, l_sc, nat):
    lane = lax.broadcasted_iota(jnp.int32, (1, LANES), 1)
    half = ATT_HEAD_DIM // 2
    qk_head0 = (lane // half) % 2 == 0
    v_head0 = lane < ATT_HEAD_DIM
    qk_keep0, qk_keep1 = masked_pair(jnp.ones((1, LANES), F32), qk_head0)
    v_keep0, v_keep1 = masked_pair(jnp.ones((1, LANES), F32), v_head0)
    seg = jnp.where((lax.broadcasted_iota(jnp.int32, (LANES, LANES), 0) // half) % 2
                    == (lax.broadcasted_iota(jnp.int32, (LANES, LANES), 1) // half) % 2,
                    1.0 / ATT_HEAD_DIM, 0.0).astype(BF16)

    def prep(ci, carry):
        rows = pl.ds(pl.multiple_of(ci * PREP_ROWS, PREP_ROWS), PREP_ROWS)
        cos = cos_ref[rows, :]
        sin = sin_ref[rows, :]

        def norm_rope(src, g_ref):
            x = src[0, rows, :].astype(F32)
            ms = jnp.dot((x * x).astype(BF16), seg, preferred_element_type=F32)
            xn = x * lax.rsqrt(ms + EPS) * g_ref[...]
            return xn * cos + pltpu.roll(xn, ATT_HEAD_DIM, 1) * sin

        xq = norm_rope(q_ref, gq_ref)
        qj[rows, :] = xq
        qb[rows, :] = xq.astype(BF16)
        xk = norm_rope(k_ref, gk_ref)
        kj[rows, :] = xk
        xk = xk.astype(BF16)
        kb[0, rows, :] = xk * qk_keep0
        kb[1, rows, :] = xk * qk_keep1
        xv = v_ref[0, rows, :]
        vj[rows, :] = xv.astype(F32)
        vb[0, rows, :] = xv * v_keep0
        vb[1, rows, :] = xv * v_keep1
        return carry

    lax.fori_loop(0, SEQ // PREP_ROWS, prep, 0, unroll=PREP_UNROLL)

    nt_dims = (((1,), (1,)), ((), ()))

    def attend(q, k0, k1, v0, v1, bias):
        w = k0.shape[0]
        s = lax.dot_general(q, jnp.concatenate([k0, k1], axis=0), nt_dims,
                            preferred_element_type=F32) + bias
        m0 = jnp.max(s[:, :w], axis=-1, keepdims=True)
        m1 = jnp.max(s[:, w:], axis=-1, keepdims=True)
        e = jnp.concatenate([jnp.exp2(s[:, :w] - m0), jnp.exp2(s[:, w:] - m1)], axis=1).astype(BF16)
        v2 = jnp.concatenate([jnp.concatenate([v0, jnp.broadcast_to(v_keep0, (w, LANES))], axis=1),
                              jnp.concatenate([v1, jnp.broadcast_to(v_keep1, (w, LANES))], axis=1)], axis=0)
        r = jnp.dot(e, v2, preferred_element_type=F32)
        return r[:, :LANES], jnp.where(v_head0, m0, m1), r[:, LANES:]

    def gather(ref, pieces, lead=()):
        return jnp.concatenate([ref[lead + (rows, slice(None))] for rows in pieces], axis=0)


    def store(p, pieces, n, a, m, l):
        for idx, rows in enumerate(pieces):
            acc_sc[p, rows, :] = a[idx * n:(idx + 1) * n]
            m_sc[p, rows, :] = m[idx * n:(idx + 1) * n]
            l_sc[p, rows, :] = l[idx * n:(idx + 1) * n]

    for j in range(ATT_CLASSES):
        rows = [pl.ds(j * ATT_BLOCK, ATT_BLOCK)]
        a, m, l = attend(qb[rows[0], :], kb[0, rows[0], :], kb[1, rows[0], :],
                         vb[0, rows[0], :], vb[1, rows[0], :], b16_ref[...])
        store(1, rows, ATT_BLOCK, a, m, l)

    for c in range(4):
        def segs(n):
            return [pl.ds((4 * a + c) * ATT_BLOCK + SEG4 * n, SEG4) for a in range(4)]

        for n in range(SEQ // 4 // ATT_BLOCK):
            cur = segs(n)
            keys = cur if n == 0 else segs(n - 1) + cur
            bias = b4f_ref[...] if n == 0 else b4_ref[...]
            a, m, l = attend(gather(qb, cur), gather(kb, keys, (0,)), gather(kb, keys, (1,)),
                             gather(vb, keys, (0,)), gather(vb, keys, (1,)), bias)
            store(0, cur, SEG4, a, m, l)

    for n in range(SEQ // ATT_BLOCK):
        cur = [pl.ds(j * ATT_BLOCK + SEG1 * n, SEG1) for j in range(ATT_CLASSES)]
        if n == 0:
            keys, bias = cur, b1f_ref[...]
        else:
            keys = [pl.ds(j * ATT_BLOCK + SEG1 * n - SEG1, 2 * SEG1) for j in range(ATT_CLASSES)]
            bias = b1_ref[...]
        k0, k1 = masked_pair(gather(kj, keys), qk_head0)
        v0, v1 = masked_pair(gather(vj, keys), v_head0)
        a, m, l = attend(gather(qj, cur).astype(BF16), k0, k1, v0, v1, bias)
        stats = [(a, m, l)] + [(gather(acc_sc, cur, (p,)), gather(m_sc, cur, (p,)), gather(l_sc, cur, (p,)))
                               for p in range(2)]
        m_all = jnp.maximum(jnp.maximum(stats[0][1], stats[1][1]), stats[2][1])
        num = jnp.zeros((ATT_BLOCK, LANES), F32)
        den = jnp.zeros((ATT_BLOCK, LANES), F32)
        for ap, mp, lp in stats:
            w = jnp.exp2(mp - m_all)
            num = num + w * ap
            den = den + w * lp
        merged = num / den
        for j in range(ATT_CLASSES):
            nat[pl.ds(ATT_BLOCK * n + j, SEG1, stride=ATT_CLASSES), :] = merged[j * SEG1:(j + 1) * SEG1]

    def emit(ci, carry):
        rows = pl.ds(pl.multiple_of(ci * PREP_ROWS, PREP_ROWS), PREP_ROWS)
        o_ref[rows, :] = nat[rows, :].astype(BF16)
        return carry

    lax.fori_loop(0, SEQ // PREP_ROWS, emit, 0)


def _attention(proj_att, cos_a, sin_a, gq, gk, batch):
    t = batch * SEQ
    hp = ATT_W // LANES

    def col(cb):
        return pl.BlockSpec((1, SEQ, LANES), lambda b, h: (cb + h, b, 0))

    def const(shape):
        return pl.BlockSpec(shape, lambda b, h: tuple(0 for _ in shape))

    biases = [jnp.asarray(b) for b in _att_bias_tables()]
    row_f32 = pltpu.VMEM((SEQ, LANES), F32)
    row_bf16 = pltpu.VMEM((SEQ, LANES), BF16)
    stat = pltpu.VMEM((2, SEQ, LANES), F32)
    return pl.pallas_call(
        _att_kernel,
        grid=(batch, hp),
        in_specs=[col(CB_QA), col(CB_KA), col(CB_VA),
                  const((SEQ, LANES)), const((SEQ, LANES)), const((1, LANES)), const((1, LANES))]
                 + [const(b.shape) for b in biases],
        out_specs=pl.BlockSpec((SEQ, LANES), lambda b, h: (b, h)),
        out_shape=jax.ShapeDtypeStruct((t, ATT_W), BF16),
        scratch_shapes=([row_f32] * 3 + [row_bf16] + [pltpu.VMEM((2, SEQ, LANES), BF16)] * 2 + [stat] * 3
                        + [row_f32]),
        compiler_params=_cparams(("arbitrary", "arbitrary")),
        name="dilated_attention",
    )(proj_att, proj_att, proj_att, cos_a, sin_a, gq, gk, *biases)


def _ret_kernel(gam_ref, q_ref, k_ref, v_ref, g_ref, decay_ref, xi_ref, zeta_ref,
                gn_ref, o_ref, state_sc):
    h = pl.program_id(1)
    gamma_c = gam_ref[h]
    state_sc[...] = jnp.zeros_like(state_sc)
    c = RET_CHUNK
    nt_dims = (((1,), (1,)), ((), ()))
    tn_dims = (((0,), (0,)), ((), ()))

    def cols(ref, rs, n):
        return jnp.concatenate([ref[i, rs, :] for i in range(n)], axis=-1)

    def chunk(n, carry):
        rs = pl.ds(pl.multiple_of(n * c, c), c)
        q = cols(q_ref, rs, RET_QK_DIM // LANES)
        k = cols(k_ref, rs, RET_QK_DIM // LANES)
        v = cols(v_ref, rs, RET_V_DIM // LANES)
        inner = lax.dot_general(q, k, nt_dims, preferred_element_type=F32) * decay_ref[0]
        y = jnp.dot(inner.astype(BF16), v, preferred_element_type=F32)
        state = state_sc[...]
        y = y + jnp.dot(q * xi_ref[0], state.astype(BF16), preferred_element_type=F32)
        state_sc[...] = state * gamma_c + lax.dot_general(k * zeta_ref[0], v, tn_dims,
                                                          preferred_element_type=F32)
        yn = y * lax.rsqrt(jnp.mean(y * y, axis=-1, keepdims=True) + EPS) * gn_ref[0]
        o_ref[rs, :] = (yn * cols(g_ref, rs, RET_V_DIM // LANES).astype(F32)).astype(BF16)
        return carry

    lax.fori_loop(0, SEQ // c, chunk, 0, unroll=RET_UNROLL)


def _retention(proj, gamma_c, decay, xi, zeta, g_ret, batch):
    t = batch * SEQ
    nq = RET_QK_DIM // LANES
    nv = RET_V_DIM // LANES

    def cols(cb, n):
        return pl.BlockSpec((n, SEQ, LANES), lambda b, h: (cb // n + h, b, 0))

    return pl.pallas_call(
        _ret_kernel,
        grid=(batch, RET_HEADS),
        in_specs=[
            pl.BlockSpec(memory_space=pltpu.SMEM),
            cols(CB_QR, nq), cols(CB_KR, nq), cols(CB_VR, nv), cols(CB_GR, nv),
            pl.BlockSpec((1, RET_CHUNK, RET_CHUNK), lambda b, h: (h, 0, 0)),
            pl.BlockSpec((1, RET_CHUNK, RET_QK_DIM), lambda b, h: (h, 0, 0)),
            pl.BlockSpec((1, RET_CHUNK, RET_QK_DIM), lambda b, h: (h, 0, 0)),
            pl.BlockSpec((1, 1, RET_V_DIM), lambda b, h: (h, 0, 0)),
        ],
        out_specs=pl.BlockSpec((SEQ, RET_V_DIM), lambda b, h: (b, h)),
        out_shape=jax.ShapeDtypeStruct((t, RET_V_W), BF16),
        scratch_shapes=[pltpu.VMEM((RET_QK_DIM, RET_V_DIM), F32)],
        compiler_params=_cparams(("arbitrary", "arbitrary")),
        name="retention",
    )(gamma_c, proj, proj, proj, proj, decay, xi, zeta, g_ret)


OUT_TM = 512
N_ROUTE = N_EXPERTS + N_GROUPS
GROUP_LANE = LANES - 1
XG_W = D_MODEL + LANES


def _split_bf16(x):
    hi = x.astype(BF16)
    lo = (x - hi.astype(F32)).astype(BF16)
    return hi, lo


def _out_kernel(ya_ref, yr_ref, ga_ref, gb_ref, bg_ref, x_ref, watt_ref, wret_ref, wout_ref,
                gffn_ref, wr_hi_ref, wr_lo_ref, br_ref, x1_ref, xg_ref):
    ya = jnp.dot(ya_ref[...], watt_ref[...], preferred_element_type=F32)
    yr = jnp.dot(yr_ref[...], wret_ref[...], preferred_element_type=F32)
    merged = []
    for cb in range(D_MODEL // LANES):
        cs = slice(cb * LANES, (cb + 1) * LANES)
        ga = _sigmoid(ga_ref[cb].astype(F32) + bg_ref[:, cs])
        gb = _sigmoid(gb_ref[cb].astype(F32) + bg_ref[:, D_MODEL + cb * LANES:D_MODEL + (cb + 1) * LANES])
        merged.append((ga * ya[:, cs] + gb * yr[:, cs]).astype(BF16))
    merged = jnp.concatenate(merged, axis=-1)
    x1 = x_ref[...] + jnp.dot(merged, wout_ref[...], preferred_element_type=F32)
    x1_ref[...] = x1
    xn = x1 * lax.rsqrt(jnp.mean(x1 * x1, axis=-1, keepdims=True) + EPS) * gffn_ref[...]
    xg_ref[:, :D_MODEL] = xn

    xh, xl = _split_bf16(xn)
    logits = (jnp.dot(xh, wr_hi_ref[...], preferred_element_type=F32)
              + jnp.dot(xl, wr_hi_ref[...], preferred_element_type=F32)
              + jnp.dot(xh, wr_lo_ref[...], preferred_element_type=F32)) + br_ref[...]
    lane = lax.broadcasted_iota(jnp.int32, logits.shape, 1)
    lane_f = lane.astype(F32)
    is_group = (lane >= N_EXPERTS) & (lane < N_ROUTE)
    gl = jnp.where(is_group, logits, NEG_BIG)
    gmax = jnp.max(gl, axis=-1, keepdims=True)
    gsel = jnp.min(jnp.where(gl == gmax, lane_f, 1e9), axis=-1, keepdims=True) - N_EXPERTS
    p_group = 1.0 / jnp.sum(jnp.where(is_group, jnp.exp(logits - gmax), 0.0), axis=-1, keepdims=True)
    lo_lane = gsel * EXPERTS_PER_GROUP
    in_group = (lane_f >= lo_lane) & (lane_f < lo_lane + EXPERTS_PER_GROUP)
    el = jnp.where(in_group, logits, NEG_BIG)
    v1 = jnp.max(el, axis=-1, keepdims=True)
    i1 = jnp.min(jnp.where(el == v1, lane_f, 1e9), axis=-1, keepdims=True)
    el2 = jnp.where(lane_f == i1, NEG_BIG, el)
    v2 = jnp.max(el2, axis=-1, keepdims=True)
    i2 = jnp.min(jnp.where(el2 == v2, lane_f, 1e9), axis=-1, keepdims=True)
    e21 = jnp.exp(v2 - v1)
    w1 = p_group / (1.0 + e21)
    w2 = w1 * e21
    gate = jnp.where(lane_f == i1, w1, 0.0) + jnp.where(lane_f == i2, w2, 0.0)
    xg_ref[:, D_MODEL:] = jnp.where(lane == GROUP_LANE, gsel, gate)


def _out_stage(y_att, y_ret, proj, b_gate, x2d, w_att, w_ret, w_out, g_ffn, wr_hi, wr_lo, b_route):
    t = x2d.shape[0]
    ncb = D_MODEL // LANES

    def full(shape):
        return pl.BlockSpec(shape, lambda i: tuple(0 for _ in shape))

    return pl.pallas_call(
        _out_kernel,
        grid=(t // OUT_TM,),
        in_specs=[
            pl.BlockSpec((OUT_TM, ATT_W), lambda i: (i, 0)),
            pl.BlockSpec((OUT_TM, RET_V_W), lambda i: (i, 0)),
            pl.BlockSpec((ncb, OUT_TM, LANES), lambda i: (CB_GA // ncb, i, 0)),
            pl.BlockSpec((ncb, OUT_TM, LANES), lambda i: (CB_GB // ncb, i, 0)),
            full((1, 2 * D_MODEL)),
            pl.BlockSpec((OUT_TM, D_MODEL), lambda i: (i, 0)),
            full((ATT_W, D_MODEL)), full((RET_V_W, D_MODEL)), full((D_MODEL, D_MODEL)),
            full((1, D_MODEL)), full((D_MODEL, LANES)), full((D_MODEL, LANES)), full((1, LANES)),
        ],
        out_specs=[
            pl.BlockSpec((OUT_TM, D_MODEL), lambda i: (i, 0)),
            pl.BlockSpec((OUT_TM, XG_W), lambda i: (i, 0)),
        ],
        out_shape=[
            jax.ShapeDtypeStruct((t, D_MODEL), F32),
            jax.ShapeDtypeStruct((t, XG_W), F32),
        ],
        compiler_params=_cparams(("arbitrary",)),
        name="out_stage",
    )(y_att, y_ret, proj, proj, b_gate, x2d, w_att, w_ret, w_out, g_ffn, wr_hi, wr_lo, b_route)


MOE_TILE = 512
MOVE_TM = 1024
GROUP_FF = EXPERTS_PER_GROUP * EXPERT_FF


def _moe_plan(xg, t):
    i32 = jnp.int32
    g = xg[:, D_MODEL + GROUP_LANE].astype(i32)
    onehot = (g[:, None] == jnp.arange(N_GROUPS, dtype=i32)[None, :]).astype(i32)
    csum = jnp.cumsum(onehot, axis=0)
    rank = jnp.sum(onehot * csum, axis=1) - 1
    padded = (csum[-1] + MOE_TILE - 1) // MOE_TILE * MOE_TILE
    ends = jnp.cumsum(padded)
    pos = rank + jnp.sum(onehot * (ends - padded)[None, :], axis=1)
    tile_start = jnp.arange(t // MOE_TILE + N_GROUPS, dtype=i32) * MOE_TILE
    tile_group = jnp.minimum(jnp.sum((tile_start[:, None] >= ends[None, :]).astype(i32), axis=1),
                             N_GROUPS - 1)
    tails = jnp.maximum(ends - MOE_TILE, 0)
    return pos.astype(i32), tile_group.astype(i32), (ends[-1:] // MOE_TILE).astype(i32), tails.astype(i32)


def _dispatch_kernel(hi_ref, lo_ref, tail_ref, xg_ref, xs_ref, zeros, sem, zero_sem):
    base = pl.program_id(0) * MOVE_TM

    @pl.when(pl.program_id(0) == 0)
    def _():
        zeros[...] = jnp.zeros_like(zeros)
        tile = MOE_TILE // SUBLANES
        def fill(start):
            return pltpu.make_async_copy(zeros, xs_ref.at[pl.ds(start, tile)], zero_sem)

        spare = [fill(xs_ref.shape[0] - (g + 1) * tile) for g in range(N_GROUPS)]
        for f in spare:
            f.start()
        for f in spare:
            f.wait()
        for g in range(N_GROUPS):
            f = fill(tail_ref[g])
            f.start()
            f.wait()

    def send(i, carry):
        for u in range(SUBLANES):
            r = base + i * SUBLANES + u
            pltpu.make_async_copy(xg_ref.at[i, pl.ds(u, 1)],
                                  xs_ref.at[hi_ref[r], pl.ds(lo_ref[r], 1)], sem).start()
        return carry

    lax.fori_loop(0, MOVE_TM // SUBLANES, send, 0)
    pltpu.make_async_copy(xg_ref, xs_ref.at[pl.ds(0, MOVE_TM // SUBLANES)], sem).wait()


def _dispatch(pos_hi, pos_lo, tail_hi, xg, n_rows):
    t = xg.shape[0]
    xs = pl.pallas_call(
        _dispatch_kernel,
        grid_spec=pltpu.PrefetchScalarGridSpec(
            num_scalar_prefetch=3,
            grid=(t // MOVE_TM,),
            in_specs=[pl.BlockSpec((MOVE_TM // SUBLANES, SUBLANES, XG_W), lambda i, hi, lo, tl: (i, 0, 0))],
            out_specs=pl.BlockSpec(memory_space=pl.ANY),
            scratch_shapes=[pltpu.VMEM((MOE_TILE // SUBLANES, SUBLANES, XG_W), F32),
                            pltpu.SemaphoreType.DMA, pltpu.SemaphoreType.DMA],
        ),
        out_shape=jax.ShapeDtypeStruct((n_rows // SUBLANES, SUBLANES, XG_W), F32),
        compiler_params=_cparams(("arbitrary",)),
        name="moe_dispatch",
    )(pos_hi, pos_lo, tail_hi, xg.reshape(t // SUBLANES, SUBLANES, XG_W))
    return xs.reshape(n_rows, XG_W)


def _experts_kernel(tg_ref, nused_ref, xs_ref, w1_ref, w3_ref, w2_ref, ys_ref):
    i = pl.program_id(0)

    @pl.when(i < nused_ref[0])
    def _():
        x = xs_ref[:, :D_MODEL].astype(BF16)
        gate = xs_ref[:, D_MODEL:]
        lane = lax.broadcasted_iota(jnp.int32, gate.shape, 1)
        first = tg_ref[i] * EXPERTS_PER_GROUP
        hidden = []
        for e in range(EXPERTS_PER_GROUP):
            a = jnp.dot(x, w1_ref[0, e], preferred_element_type=F32)
            b = jnp.dot(x, w3_ref[0, e], preferred_element_type=F32)
            g = jnp.sum(jnp.where(lane == first + e, gate, 0.0), axis=-1, keepdims=True)
            hidden.append((a * _sigmoid(a) * b * g).astype(BF16))
        ys_ref[...] = jnp.dot(jnp.concatenate(hidden, axis=-1), w2_ref[0], preferred_element_type=F32)

    @pl.when(i >= nused_ref[0])
    def _():
        ys_ref[...] = jnp.zeros_like(ys_ref)


def _experts(tile_group, n_used, xs, w1g, w3g, w2g):
    n_rows = xs.shape[0]
    up = pl.BlockSpec((1, EXPERTS_PER_GROUP, D_MODEL, EXPERT_FF), lambda i, tg, nu: (tg[i], 0, 0, 0))
    return pl.pallas_call(
        _experts_kernel,
        grid_spec=pltpu.PrefetchScalarGridSpec(
            num_scalar_prefetch=2,
            grid=(n_rows // MOE_TILE,),
            in_specs=[
                pl.BlockSpec((MOE_TILE, XG_W), lambda i, tg, nu: (i, 0)),
                up, up,
                pl.BlockSpec((1, GROUP_FF, D_MODEL), lambda i, tg, nu: (tg[i], 0, 0)),
            ],
            out_specs=pl.BlockSpec((MOE_TILE, D_MODEL), lambda i, tg, nu: (i, 0)),
        ),
        out_shape=jax.ShapeDtypeStruct((n_rows, D_MODEL), F32),
        compiler_params=_cparams(("arbitrary",)),
        name="moe_experts",
    )(tile_group, n_used, xs, w1g, w3g, w2g)


def _combine_kernel(hi_ref, lo_ref, x1_ref, ys_ref, o_ref, buf, sem):
    step = pl.program_id(0)
    slot = step % 2

    def issue(for_step, into):
        base = for_step * MOVE_TM

        def fetch(i, carry):
            for u in range(SUBLANES):
                r = base + i * SUBLANES + u
                pltpu.make_async_copy(ys_ref.at[hi_ref[r], pl.ds(lo_ref[r], 1)],
                                      buf.at[into, i, pl.ds(u, 1)], sem.at[into]).start()
            return carry

        lax.fori_loop(0, MOVE_TM // SUBLANES, fetch, 0)

    @pl.when(step == 0)
    def _():
        issue(step, slot)

    @pl.when(step + 1 < pl.num_programs(0))
    def _():
        issue(step + 1, 1 - slot)

    pltpu.make_async_copy(ys_ref.at[pl.ds(0, MOVE_TM // SUBLANES)], buf.at[slot], sem.at[slot]).wait()
    o_ref[...] = x1_ref[...] + buf[slot]


def _combine(pos_hi, pos_lo, x1, ys):
    t = x1.shape[0]
    tile = (MOVE_TM // SUBLANES, SUBLANES, D_MODEL)
    out = pl.pallas_call(
        _combine_kernel,
        grid_spec=pltpu.PrefetchScalarGridSpec(
            num_scalar_prefetch=2,
            grid=(t // MOVE_TM,),
            in_specs=[pl.BlockSpec(tile, lambda i, hi, lo: (i, 0, 0)),
                      pl.BlockSpec(memory_space=pl.ANY)],
            out_specs=pl.BlockSpec(tile, lambda i, hi, lo: (i, 0, 0)),
            scratch_shapes=[pltpu.VMEM((2,) + tile, F32), pltpu.SemaphoreType.DMA((2,))],
        ),
        out_shape=jax.ShapeDtypeStruct((t // SUBLANES, SUBLANES, D_MODEL), F32),
        compiler_params=_cparams(("arbitrary",)),
        name="moe_combine",
    )(pos_hi, pos_lo, x1.reshape(t // SUBLANES, SUBLANES, D_MODEL),
      ys.reshape(ys.shape[0] // SUBLANES, SUBLANES, D_MODEL))
    return out.reshape(t, D_MODEL)


def _moe(xg, x1, w1, w3, w2):
    t = xg.shape[0]
    w1g = w1.reshape(N_GROUPS, EXPERTS_PER_GROUP, D_MODEL, EXPERT_FF).astype(BF16)
    w3g = w3.reshape(N_GROUPS, EXPERTS_PER_GROUP, D_MODEL, EXPERT_FF).astype(BF16)
    w2g = w2.reshape(N_GROUPS, GROUP_FF, D_MODEL).astype(BF16)
    pos, tile_group, n_used, tails = _moe_plan(xg, t)
    pos_hi, pos_lo = pos // SUBLANES, pos % SUBLANES
    xs = _dispatch(pos_hi, pos_lo, tails // SUBLANES, xg, t + N_GROUPS * MOE_TILE)
    ys = _experts(tile_group, n_used, xs, w1g, w3g, w2g)
    return _combine(pos_hi, pos_lo, x1, ys)


def _class_major(table):
    return table.reshape(ROWS_PER_CLASS, ATT_CLASSES, -1).transpose(1, 0, 2).reshape(SEQ, -1)


def _rope_tables_att():
    pos = jnp.arange(SEQ, dtype=F32)
    inv = ROPE_THETA ** (-jnp.arange(0, ATT_HEAD_DIM, 2, dtype=F32) / ATT_HEAD_DIM)
    ang = pos[:, None] * inv[None, :]
    cos, sin = jnp.cos(ang), jnp.sin(ang)
    cos_full = jnp.concatenate([cos, cos, cos, cos], axis=-1)
    sin_full = jnp.concatenate([-sin, -sin, sin, sin], axis=-1)
    return _class_major(cos_full), _class_major(sin_full)


def _pair_lanes(a):
    half = ATT_HEAD_DIM // 2
    lead = a.shape[:-1]
    a = a.reshape(lead + (-1, 2, 2, half))
    return jnp.swapaxes(a, -3, -2).reshape(lead + (-1,))


def _rope_tables_ret():
    pos = jnp.arange(SEQ, dtype=F32)
    inv = 1.0 / (ROPE_THETA ** jnp.linspace(0.0, 1.0, RET_QK_DIM // 2, dtype=F32))
    ang = pos[:, None] * inv[None, :]
    return jnp.cos(ang), jnp.sin(ang)


def _decay_tables():
    c = RET_CHUNK
    log_gamma = jnp.log(1.0 - jnp.exp2(-5.0 - jnp.arange(RET_HEADS, dtype=F32)))
    idx = jnp.arange(c, dtype=F32)
    diff = idx[:, None] - idx[None, :]
    decay = jnp.where(diff >= 0, jnp.exp(log_gamma[:, None, None] * jnp.maximum(diff, 0.0)), 0.0)
    zeta = jnp.exp(log_gamma[:, None] * (c - 1 - idx))
    xi = jnp.exp(log_gamma[:, None] * (idx + 1))
    gamma_c = jnp.exp(log_gamma * c)
    bc = lambda a: jnp.broadcast_to(a[:, :, None], (RET_HEADS, c, RET_QK_DIM)).astype(BF16)
    return gamma_c, decay, bc(xi), bc(zeta)


def kernel(x, g_norm_mix, w_in, b_merge_gate, g_q, g_k, w_branch_att, g_ret_norm, w_branch_ret,
           w_out, g_norm_ffn, w_router_group, b_router_group, w_router_expert, b_router_expert,
           w1, w3, w2):
    batch = x.shape[0]
    t = batch * SEQ
    cos_a, sin_a = _rope_tables_att()
    cos_r, sin_r = _rope_tables_ret()
    gamma_c, decay, xi, zeta = _decay_tables()
    xf = x.reshape(t, D_MODEL)
    for l in range(g_norm_mix.shape[0]):
        g_mix = g_norm_mix[l][None, :]
        proj_att = _inproj_att(xf, g_mix, _pair_lanes(w_in[l][:, :2 * ATT_W]).astype(BF16), w_in[l])
        proj = _inproj_rest(xf, g_mix, w_in[l], cos_r, sin_r)
        reps = LANES // ATT_HEAD_DIM
        gq = _pair_lanes(jnp.tile(g_q[l], reps))[None, :] * (ATT_HEAD_DIM ** -0.5 * np.log2(np.e))
        y_att = _attention(proj_att, cos_a, sin_a, gq, _pair_lanes(jnp.tile(g_k[l], reps))[None, :],
                           batch)
        y_ret = _retention(proj, gamma_c, decay, xi, zeta, g_ret_norm[l][:, None, :], batch)
        w_route = jnp.concatenate(
            [w_router_expert[l], w_router_group[l],
             jnp.zeros((D_MODEL, LANES - N_ROUTE), F32)], axis=-1)
        wr_hi, wr_lo = _split_bf16(w_route)
        b_route = jnp.concatenate(
            [b_router_expert[l], b_router_group[l], jnp.zeros((LANES - N_ROUTE,), F32)])[None, :]
        x1, xg = _out_stage(
            y_att, y_ret, proj, b_merge_gate[l][None, :], xf,
            w_branch_att[l].astype(BF16), w_branch_ret[l].astype(BF16), w_out[l].astype(BF16),
            g_norm_ffn[l][None, :], wr_hi, wr_lo, b_route)
        xf = _moe(xg, x1, w1[l], w3[l], w2[l])
    return xf.reshape(batch, SEQ, D_MODEL)
```

```python
import numpy as np

import jax
import jax.numpy as jnp
from jax import lax
from jax.experimental import pallas as pl
from jax.experimental.pallas import tpu as pltpu

F32 = jnp.float32
BF16 = jnp.bfloat16

D_MODEL = 1024
SEQ = 2048
ATT_HEADS = 16
ATT_HEAD_DIM = 64
ATT_W = ATT_HEADS * ATT_HEAD_DIM
ROPE_THETA = 10000.0
RET_HEADS = 4
RET_QK_DIM = 256
RET_V_DIM = 512
RET_QK_W = RET_HEADS * RET_QK_DIM
RET_V_W = RET_HEADS * RET_V_DIM
N_GROUPS = 4
EXPERTS_PER_GROUP = 8
N_EXPERTS = N_GROUPS * EXPERTS_PER_GROUP
EXPERT_FF = 256
EPS = 1e-6
ATT_IN_W = 3 * ATT_W
REST_IN_W = 2 * RET_QK_W + 2 * RET_V_W + 2 * D_MODEL

LANES = 128
SUBLANES = 8
MXU_WIDTH = 256
CB_QA, CB_KA, CB_VA = 0, 8, 16
CB_QR, CB_KR, CB_VR, CB_GR, CB_GA, CB_GB = 0, 8, 16, 32, 48, 56

ATT_BLOCK = 128
ATT_CLASSES = 16
ROWS_PER_CLASS = SEQ // ATT_CLASSES
RET_CHUNK = 256
RET_UNROLL = 2
NEG_BIG = -1e30
VMEM_LIMIT = 48 * 1024 * 1024
IN_ATT_VMEM_LIMIT = 56 * 1024 * 1024


def _cparams(sem, vmem_limit=VMEM_LIMIT):
    return pltpu.CompilerParams(dimension_semantics=sem, vmem_limit_bytes=vmem_limit)


IN_TN = 1024
IN_NORM_ROWS = 256
REST_TILE_QR, REST_TILE_KR = CB_QR * LANES // IN_TN, CB_KR * LANES // IN_TN
REST_TILE_GR, REST_TILE_GA = CB_GR * LANES // IN_TN, CB_GA * LANES // IN_TN


def _sigmoid(x):
    return 0.5 * jnp.tanh(0.5 * x) + 0.5


def _rmsnorm_rows(x, g_ref):
    ms = jnp.mean(x * x, axis=-1, keepdims=True)
    return x * lax.rsqrt(ms + EPS) * g_ref[...]


def _project(xn_sc, w_ref, o_ref, epilogue, first_block=0):
    xn = xn_sc[...]
    for c2 in range(IN_TN // MXU_WIDTH):
        w = w_ref[:, c2 * MXU_WIDTH:(c2 + 1) * MXU_WIDTH].astype(BF16)
        acc = jnp.dot(xn, w, preferred_element_type=F32)
        lo, hi = epilogue(acc[:, :LANES], acc[:, LANES:])
        o_ref[first_block + 2 * c2] = lo.astype(BF16)
        o_ref[first_block + 2 * c2 + 1] = hi.astype(BF16)


def _inproj_att_kernel(x_ref, g_ref, wqk_ref, wv_ref, o_ref, xn_sc, xs):
    @pl.when(pl.program_id(1) == 0)
    def _():
        def norm_rows(ci, carry):
            rows = pl.ds(pl.multiple_of(ci * IN_NORM_ROWS, IN_NORM_ROWS), IN_NORM_ROWS)
            xn = _rmsnorm_rows(x_ref[rows, :], g_ref)
            for c in range(D_MODEL // LANES):
                xs[c, rows, :] = xn[:, c * LANES:(c + 1) * LANES]
            return carry

        lax.fori_loop(0, SEQ // IN_NORM_ROWS, norm_rows, 0)

        def gather_class(j, carry):
            dst = pl.ds(pl.multiple_of(j * ROWS_PER_CLASS, ROWS_PER_CLASS), ROWS_PER_CLASS)
            for c in range(D_MODEL // LANES):
                xn_sc[dst, c * LANES:(c + 1) * LANES] = xs[
                    c, pl.ds(j, ROWS_PER_CLASS, stride=ATT_CLASSES), :].astype(BF16)
            return carry

        lax.fori_loop(0, ATT_CLASSES, gather_class, 0)

    qk_tiles = 2 * ATT_W // IN_TN

    @pl.when(pl.program_id(1) < qk_tiles)
    def _():
        _project(xn_sc, wqk_ref, o_ref, lambda lo, hi: (lo, hi))

    @pl.when(pl.program_id(1) >= qk_tiles)
    def _():
        _project(xn_sc, wv_ref, o_ref, lambda lo, hi: (lo, hi))


def _inproj_rest_kernel(x_ref, g_ref, w_ref, cos_ref, sin_ref, o_ref, xn_sc):
    j = pl.program_id(1)

    @pl.when(j == 0)
    def _():
        def norm_rows(ci, carry):
            rows = pl.ds(pl.multiple_of(ci * IN_NORM_ROWS, IN_NORM_ROWS), IN_NORM_ROWS)
            xn_sc[rows, :] = _rmsnorm_rows(x_ref[rows, :], g_ref).astype(BF16)
            return carry

        lax.fori_loop(0, SEQ // IN_NORM_ROWS, norm_rows, 0)

    def rotate(scale):
        def epilogue(x1, x2):
            cos, sin = cos_ref[...], sin_ref[...]
            return (x1 * cos - x2 * sin) * scale, (x2 * cos + x1 * sin) * scale
        return epilogue

    def swish(lo, hi):
        return lo * _sigmoid(lo), hi * _sigmoid(hi)

    is_gate = (j >= REST_TILE_GR) & (j < REST_TILE_GA)

    @pl.when(j == REST_TILE_QR)
    def _():
        _project(xn_sc, w_ref, o_ref, rotate(1.0))

    @pl.when(j == REST_TILE_KR)
    def _():
        _project(xn_sc, w_ref, o_ref, rotate(RET_QK_DIM ** -0.5))

    @pl.when(is_gate)
    def _():
        _project(xn_sc, w_ref, o_ref, swish)

    @pl.when((j > REST_TILE_KR) & jnp.logical_not(is_gate))
    def _():
        _project(xn_sc, w_ref, o_ref, lambda lo, hi: (lo, hi))


def _inproj_att(x2d, g, w_qk_f32, w_in_f32):
    t = x2d.shape[0]
    width = ATT_IN_W
    qk_tiles = 2 * ATT_W // IN_TN
    return pl.pallas_call(
        _inproj_att_kernel,
        grid=(t // SEQ, width // IN_TN),
        in_specs=[
            pl.BlockSpec((SEQ, D_MODEL), lambda i, j: (i, 0)),
            pl.BlockSpec((1, D_MODEL), lambda i, j: (0, 0)),
            pl.BlockSpec((D_MODEL, IN_TN), lambda i, j: (0, jnp.minimum(j, qk_tiles - 1))),
            pl.BlockSpec((D_MODEL, IN_TN), lambda i, j: (0, qk_tiles), pipeline_mode=pl.Buffered(1)),
        ],
        out_specs=pl.BlockSpec((IN_TN // LANES, SEQ, LANES), lambda i, j: (j, i, 0)),
        out_shape=jax.ShapeDtypeStruct((width // LANES, t, LANES), BF16),
        scratch_shapes=[pltpu.VMEM((SEQ, D_MODEL), BF16),
                        pltpu.VMEM((D_MODEL // LANES, SEQ, LANES), F32)],
        compiler_params=_cparams(("arbitrary", "arbitrary"), vmem_limit=IN_ATT_VMEM_LIMIT),
        name="inproj_att",
    )(x2d, g, w_qk_f32, w_in_f32)


def _inproj_rest(x2d, g, w_in_f32, cos_r, sin_r):
    t = x2d.shape[0]
    first_tile = ATT_IN_W // IN_TN
    table = pl.BlockSpec((SEQ, LANES), lambda i, j: (0, 0))
    return pl.pallas_call(
        _inproj_rest_kernel,
        grid=(t // SEQ, REST_IN_W // IN_TN),
        in_specs=[
            pl.BlockSpec((SEQ, D_MODEL), lambda i, j: (i, 0)),
            pl.BlockSpec((1, D_MODEL), lambda i, j: (0, 0)),
            pl.BlockSpec((D_MODEL, IN_TN), lambda i, j: (0, first_tile + j)),
            table, table,
        ],
        out_specs=pl.BlockSpec((IN_TN // LANES, SEQ, LANES), lambda i, j: (j, i, 0)),
        out_shape=jax.ShapeDtypeStruct((REST_IN_W // LANES, t, LANES), BF16),
        scratch_shapes=[pltpu.VMEM((SEQ, D_MODEL), BF16)],
        compiler_params=_cparams(("arbitrary", "arbitrary")),
        name="inproj_rest",
    )(x2d, g, w_in_f32, cos_r, sin_r)


PREP_ROWS = 256
PREP_UNROLL = 4
SEG4 = ATT_BLOCK // 4
SEG1 = ATT_BLOCK // ATT_CLASSES


def _att_bias_tables():
    def tile(qpos, kpos):
        d = qpos[:, None] - kpos[None, :]
        one = np.where((d >= 0) & (d <= ATT_BLOCK), 0.0, NEG_BIG).astype(np.float32)
        return np.concatenate([one, one], axis=1)

    u = np.arange(ATT_BLOCK)
    q4 = 4 * (u % SEG4) + u // SEG4
    q1 = ATT_CLASSES * (u % SEG1) + u // SEG1
    w = np.arange(2 * ATT_BLOCK)
    k1 = ATT_CLASSES * (w % SEG1) + w // (2 * SEG1) + ATT_BLOCK * ((w // SEG1) % 2 - 1)
    return (tile(q1, k1), tile(q1, q1), tile(q4, np.concatenate([q4 - ATT_BLOCK, q4])), tile(q4, q4),
            tile(u, u))


def masked_pair(x, head0_mask):
    return jnp.where(head0_mask, x, 0.0).astype(BF16), jnp.where(head0_mask, 0.0, x).astype(BF16)


def _att_kernel(q_ref, k_ref, v_ref, cos_ref, sin_ref, gq_ref, gk_ref,
                b1_ref, b1f_ref, b4_ref, b4f_ref, b16_ref, o_ref,
                qj, kj, vj, qb, kb, vb, acc_sc, m_sc, l_sc, nat):
    lane = lax.broadcasted_iota(jnp.int32, (1, LANES), 1)
    half = ATT_HEAD_DIM // 2
    qk_head0 = (lane // half) % 2 == 0
    v_head0 = lane < ATT_HEAD_DIM
    qk_keep0, qk_keep1 = masked_pair(jnp.ones((1, LANES), F32), qk_head0)
    v_keep0, v_keep1 = masked_pair(jnp.ones((1, LANES), F32), v_head0)
    seg = jnp.where((lax.broadcasted_iota(jnp.int32, (LANES, LANES), 0) // half) % 2
                    == (lax.broadcasted_iota(jnp.int32, (LANES, LANES), 1) // half) % 2,
                    1.0 / ATT_HEAD_DIM, 0.0).astype(BF16)

    def prep(ci, carry):
        rows = pl.ds(pl.multiple_of(ci * PREP_ROWS, PREP_ROWS), PREP_ROWS)
        cos = cos_ref[rows, :]
        sin = sin_ref[rows, :]

        def norm_rope(src, g_ref):
            x = src[0, rows, :].astype(F32)
            ms = jnp.dot((x * x).astype(BF16), seg, preferred_element_type=F32)
            xn = x * lax.rsqrt(ms + EPS) * g_ref[...]
            return xn * cos + pltpu.roll(xn, ATT_HEAD_DIM, 1) * sin

        xq = norm_rope(q_ref, gq_ref)
        qj[rows, :] = xq
        qb[rows, :] = xq.astype(BF16)
        xk = norm_rope(k_ref, gk_ref)
        kj[rows, :] = xk
        xk = xk.astype(BF16)
        kb[0, rows, :] = xk * qk_keep0
        kb[1, rows, :] = xk * qk_keep1
        xv = v_ref[0, rows, :]
        vj[rows, :] = xv.astype(F32)
        vb[0, rows, :] = xv * v_keep0
        vb[1, rows, :] = xv * v_keep1
        return carry

    lax.fori_loop(0, SEQ // PREP_ROWS, prep, 0, unroll=PREP_UNROLL)

    nt_dims = (((1,), (1,)), ((), ()))

    def attend(q, k0, k1, v0, v1, bias):
        w = k0.shape[0]
        s = lax.dot_general(q, jnp.concatenate([k0, k1], axis=0), nt_dims,
                            preferred_element_type=F32) + bias
        m0 = jnp.max(s[:, :w], axis=-1, keepdims=True)
        m1 = jnp.max(s[:, w:], axis=-1, keepdims=True)
        e = jnp.concatenate([jnp.exp2(s[:, :w] - m0), jnp.exp2(s[:, w:] - m1)], axis=1).astype(BF16)
        v2 = jnp.concatenate([jnp.concatenate([v0, jnp.broadcast_to(v_keep0, (w, LANES))], axis=1),
                              jnp.concatenate([v1, jnp.broadcast_to(v_keep1, (w, LANES))], axis=1)], axis=0)
        r = jnp.dot(e, v2, preferred_element_type=F32)
        return r[:, :LANES], jnp.where(v_head0, m0, m1), r[:, LANES:]

    def gather(ref, pieces, lead=()):
        return jnp.concatenate([ref[lead + (rows, slice(None))] for rows in pieces], axis=0)


    def store(p, pieces, n, a, m, l):
        for idx, rows in enumerate(pieces):
            acc_sc[p, rows, :] = a[idx * n:(idx + 1) * n]
            m_sc[p, rows, :] = m[idx * n:(idx + 1) * n]
            l_sc[p, rows, :] = l[idx * n:(idx + 1) * n]

    for j in range(ATT_CLASSES):
        rows = [pl.ds(j * ATT_BLOCK, ATT_BLOCK)]
        a, m, l = attend(qb[rows[0], :], kb[0, rows[0], :], kb[1, rows[0], :],
                         vb[0, rows[0], :], vb[1, rows[0], :], b16_ref[...])
        store(1, rows, ATT_BLOCK, a, m, l)

    for c in range(4):
        def segs(n):
            return [pl.ds((4 * a + c) * ATT_BLOCK + SEG4 * n, SEG4) for a in range(4)]

        for n in range(SEQ // 4 // ATT_BLOCK):
            cur = segs(n)
            keys = cur if n == 0 else segs(n - 1) + cur
            bias = b4f_ref[...] if n == 0 else b4_ref[...]
            a, m, l = attend(gather(qb, cur), gather(kb, keys, (0,)), gather(kb, keys, (1,)),
                             gather(vb, keys, (0,)), gather(vb, keys, (1,)), bias)
            store(0, cur, SEG4, a, m, l)

    for n in range(SEQ // ATT_BLOCK):
        cur = [pl.ds(j * ATT_BLOCK + SEG1 * n, SEG1) for j in range(ATT_CLASSES)]
        if n == 0:
            keys, bias = cur, b1f_ref[...]
        else:
            keys = [pl.ds(j * ATT_BLOCK + SEG1 * n - SEG1, 2 * SEG1) for j in range(ATT_CLASSES)]
            bias = b1_ref[...]
        k0, k1 = masked_pair(gather(kj, keys), qk_head0)
        v0, v1 = masked_pair(gather(vj, keys), v_head0)
        a, m, l = attend(gather(qj, cur).astype(BF16), k0, k1, v0, v1, bias)
        stats = [(a, m, l)] + [(gather(acc_sc, cur, (p,)), gather(m_sc, cur, (p,)), gather(l_sc, cur, (p,)))
                               for p in range(2)]
        m_all = jnp.maximum(jnp.maximum(stats[0][1], stats[1][1]), stats[2][1])
        num = jnp.zeros((ATT_BLOCK, LANES), F32)
        den = jnp.zeros((ATT_BLOCK, LANES), F32)
        for ap, mp, lp in stats:
            w = jnp.exp2(mp - m_all)
            num = num + w * ap
            den = den + w * lp
        merged = num / den
        for j in range(ATT_CLASSES):
            nat[pl.ds(ATT_BLOCK * n + j, SEG1, stride=ATT_CLASSES), :] = merged[j * SEG1:(j + 1) * SEG1]

    def emit(ci, carry):
        rows = pl.ds(pl.multiple_of(ci * PREP_ROWS, PREP_ROWS), PREP_ROWS)
        o_ref[rows, :] = nat[rows, :].astype(BF16)
        return carry

    lax.fori_loop(0, SEQ // PREP_ROWS, emit, 0)


def _attention(proj_att, cos_a, sin_a, gq, gk, batch):
    t = batch * SEQ
    hp = ATT_W // LANES

    def col(cb):
        return pl.BlockSpec((1, SEQ, LANES), lambda b, h: (cb + h, b, 0))

    def const(shape):
        return pl.BlockSpec(shape, lambda b, h: tuple(0 for _ in shape))

    biases = [jnp.asarray(b) for b in _att_bias_tables()]
    row_f32 = pltpu.VMEM((SEQ, LANES), F32)
    row_bf16 = pltpu.VMEM((SEQ, LANES), BF16)
    stat = pltpu.VMEM((2, SEQ, LANES), F32)
    return pl.pallas_call(
        _att_kernel,
        grid=(batch, hp),
        in_specs=[col(CB_QA), col(CB_KA), col(CB_VA),
                  const((SEQ, LANES)), const((SEQ, LANES)), const((1, LANES)), const((1, LANES))]
                 + [const(b.shape) for b in biases],
        out_specs=pl.BlockSpec((SEQ, LANES), lambda b, h: (b, h)),
        out_shape=jax.ShapeDtypeStruct((t, ATT_W), BF16),
        scratch_shapes=([row_f32] * 3 + [row_bf16] + [pltpu.VMEM((2, SEQ, LANES), BF16)] * 2 + [stat] * 3
                        + [row_f32]),
        compiler_params=_cparams(("arbitrary", "arbitrary")),
        name="dilated_attention",
    )(proj_att, proj_att, proj_att, cos_a, sin_a, gq, gk, *biases)


def _ret_kernel(gam_ref, q_ref, k_ref, v_ref, g_ref, decay_ref, xi_ref, zeta_ref,
                gn_ref, o_ref, state_sc):
    h = pl.program_id(1)
    gamma_c = gam_ref[h]
    state_sc[...] = jnp.zeros_like(state_sc)
    c = RET_CHUNK
    nt_dims = (((1,), (1,)), ((), ()))
    tn_dims = (((0,), (0,)), ((), ()))

    def cols(ref, rs, n):
        return jnp.concatenate([ref[i, rs, :] for i in range(n)], axis=-1)

    def chunk(n, carry):
        rs = pl.ds(pl.multiple_of(n * c, c), c)
        q = cols(q_ref, rs, RET_QK_DIM // LANES)
        k = cols(k_ref, rs, RET_QK_DIM // LANES)
        v = cols(v_ref, rs, RET_V_DIM // LANES)
        inner = lax.dot_general(q, k, nt_dims, preferred_element_type=F32) * decay_ref[0]
        y = jnp.dot(inner.astype(BF16), v, preferred_element_type=F32)
        state = state_sc[...]
        y = y + jnp.dot(q * xi_ref[0], state.astype(BF16), preferred_element_type=F32)
        state_sc[...] = state * gamma_c + lax.dot_general(k * zeta_ref[0], v, tn_dims,
                                                          preferred_element_type=F32)
        yn = y * lax.rsqrt(jnp.mean(y * y, axis=-1, keepdims=True) + EPS) * gn_ref[0]
        o_ref[rs, :] = (yn * cols(g_ref, rs, RET_V_DIM // LANES).astype(F32)).astype(BF16)
        return carry

    lax.fori_loop(0, SEQ // c, chunk, 0, unroll=RET_UNROLL)


def _retention(proj, gamma_c, decay, xi, zeta, g_ret, batch):
    t = batch * SEQ
    nq = RET_QK_DIM // LANES
    nv = RET_V_DIM // LANES

    def cols(cb, n):
        return pl.BlockSpec((n, SEQ, LANES), lambda b, h: (cb // n + h, b, 0))

    return pl.pallas_call(
        _ret_kernel,
        grid=(batch, RET_HEADS),
        in_specs=[
            pl.BlockSpec(memory_space=pltpu.SMEM),
            cols(CB_QR, nq), cols(CB_KR, nq), cols(CB_VR, nv), cols(CB_GR, nv),
            pl.BlockSpec((1, RET_CHUNK, RET_CHUNK), lambda b, h: (h, 0, 0)),
            pl.BlockSpec((1, RET_CHUNK, RET_QK_DIM), lambda b, h: (h, 0, 0)),
            pl.BlockSpec((1, RET_CHUNK, RET_QK_DIM), lambda b, h: (h, 0, 0)),
            pl.BlockSpec((1, 1, RET_V_DIM), lambda b, h: (h, 0, 0)),
        ],
        out_specs=pl.BlockSpec((SEQ, RET_V_DIM), lambda b, h: (b, h)),
        out_shape=jax.ShapeDtypeStruct((t, RET_V_W), BF16),
        scratch_shapes=[pltpu.VMEM((RET_QK_DIM, RET_V_DIM), F32)],
        compiler_params=_cparams(("arbitrary", "arbitrary")),
        name="retention",
    )(gamma_c, proj, proj, proj, proj, decay, xi, zeta, g_ret)


OUT_TM = 512
N_ROUTE = N_EXPERTS + N_GROUPS
GROUP_LANE = LANES - 1
XG_W = D_MODEL + LANES


def _split_bf16(x):
    hi = x.astype(BF16)
    lo = (x - hi.astype(F32)).astype(BF16)
    return hi, lo


def _out_kernel(ya_ref, yr_ref, ga_ref, gb_ref, bg_ref, x_ref, watt_ref, wret_ref, wout_ref,
                gffn_ref, wr_hi_ref, wr_lo_ref, br_ref, x1_ref, xg_ref):
    ya = jnp.dot(ya_ref[...], watt_ref[...], preferred_element_type=F32)
    yr = jnp.dot(yr_ref[...], wret_ref[...], preferred_element_type=F32)
    merged = []
    for cb in range(D_MODEL // LANES):
        cs = slice(cb * LANES, (cb + 1) * LANES)
        ga = _sigmoid(ga_ref[cb].astype(F32) + bg_ref[:, cs])
        gb = _sigmoid(gb_ref[cb].astype(F32) + bg_ref[:, D_MODEL + cb * LANES:D_MODEL + (cb + 1) * LANES])
        merged.append((ga * ya[:, cs] + gb * yr[:, cs]).astype(BF16))
    merged = jnp.concatenate(merged, axis=-1)
    x1 = x_ref[...] + jnp.dot(merged, wout_ref[...], preferred_element_type=F32)
    x1_ref[...] = x1
    xn = x1 * lax.rsqrt(jnp.mean(x1 * x1, axis=-1, keepdims=True) + EPS) * gffn_ref[...]
    xg_ref[:, :D_MODEL] = xn

    xh, xl = _split_bf16(xn)
    logits = (jnp.dot(xh, wr_hi_ref[...], preferred_element_type=F32)
              + jnp.dot(xl, wr_hi_ref[...], preferred_element_type=F32)
              + jnp.dot(xh, wr_lo_ref[...], preferred_element_type=F32)) + br_ref[...]
    lane = lax.broadcasted_iota(jnp.int32, logits.shape, 1)
    lane_f = lane.astype(F32)
    is_group = (lane >= N_EXPERTS) & (lane < N_ROUTE)
    gl = jnp.where(is_group, logits, NEG_BIG)
    gmax = jnp.max(gl, axis=-1, keepdims=True)
    gsel = jnp.min(jnp.where(gl == gmax, lane_f, 1e9), axis=-1, keepdims=True) - N_EXPERTS
    p_group = 1.0 / jnp.sum(jnp.where(is_group, jnp.exp(logits - gmax), 0.0), axis=-1, keepdims=True)
    lo_lane = gsel * EXPERTS_PER_GROUP
    in_group = (lane_f >= lo_lane) & (lane_f < lo_lane + EXPERTS_PER_GROUP)
    el = jnp.where(in_group, logits, NEG_BIG)
    v1 = jnp.max(el, axis=-1, keepdims=True)
    i1 = jnp.min(jnp.where(el == v1, lane_f, 1e9), axis=-1, keepdims=True)
    el2 = jnp.where(lane_f == i1, NEG_BIG, el)
    v2 = jnp.max(el2, axis=-1, keepdims=True)
    i2 = jnp.min(jnp.where(el2 == v2, lane_f, 1e9), axis=-1, keepdims=True)
    e21 = jnp.exp(v2 - v1)
    w1 = p_group / (1.0 + e21)
    w2 = w1 * e21
    gate = jnp.where(lane_f == i1, w1, 0.0) + jnp.where(lane_f == i2, w2, 0.0)
    xg_ref[:, D_MODEL:] = jnp.where(lane == GROUP_LANE, gsel, gate)


def _out_stage(y_att, y_ret, proj, b_gate, x2d, w_att, w_ret, w_out, g_ffn, wr_hi, wr_lo, b_route):
    t = x2d.shape[0]
    ncb = D_MODEL // LANES

    def full(shape):
        return pl.BlockSpec(shape, lambda i: tuple(0 for _ in shape))

    return pl.pallas_call(
        _out_kernel,
        grid=(t // OUT_TM,),
        in_specs=[
            pl.BlockSpec((OUT_TM, ATT_W), lambda i: (i, 0)),
            pl.BlockSpec((OUT_TM, RET_V_W), lambda i: (i, 0)),
            pl.BlockSpec((ncb, OUT_TM, LANES), lambda i: (CB_GA // ncb, i, 0)),
            pl.BlockSpec((ncb, OUT_TM, LANES), lambda i: (CB_GB // ncb, i, 0)),
            full((1, 2 * D_MODEL)),
            pl.BlockSpec((OUT_TM, D_MODEL), lambda i: (i, 0)),
            full((ATT_W, D_MODEL)), full((RET_V_W, D_MODEL)), full((D_MODEL, D_MODEL)),
            full((1, D_MODEL)), full((D_MODEL, LANES)), full((D_MODEL, LANES)), full((1, LANES)),
        ],
        out_specs=[
            pl.BlockSpec((OUT_TM, D_MODEL), lambda i: (i, 0)),
            pl.BlockSpec((OUT_TM, XG_W), lambda i: (i, 0)),
        ],
        out_shape=[
            jax.ShapeDtypeStruct((t, D_MODEL), F32),
            jax.ShapeDtypeStruct((t, XG_W), F32),
        ],
        compiler_params=_cparams(("arbitrary",)),
        name="out_stage",
    )(y_att, y_ret, proj, proj, b_gate, x2d, w_att, w_ret, w_out, g_ffn, wr_hi, wr_lo, b_route)


MOE_TILE = 512
MOVE_TM = 1024
GROUP_FF = EXPERTS_PER_GROUP * EXPERT_FF


def _moe_plan(xg, t):
    i32 = jnp.int32
    g = xg[:, D_MODEL + GROUP_LANE].astype(i32)
    onehot = (g[:, None] == jnp.arange(N_GROUPS, dtype=i32)[None, :]).astype(i32)
    csum = jnp.cumsum(onehot, axis=0)
    rank = jnp.sum(onehot * csum, axis=1) - 1
    padded = (csum[-1] + MOE_TILE - 1) // MOE_TILE * MOE_TILE
    ends = jnp.cumsum(padded)
    pos = rank + jnp.sum(onehot * (ends - padded)[None, :], axis=1)
    tile_start = jnp.arange(t // MOE_TILE + N_GROUPS, dtype=i32) * MOE_TILE
    tile_group = jnp.minimum(jnp.sum((tile_start[:, None] >= ends[None, :]).astype(i32), axis=1),
                             N_GROUPS - 1)
    tails = jnp.maximum(ends - MOE_TILE, 0)
    return pos.astype(i32), tile_group.astype(i32), (ends[-1:] // MOE_TILE).astype(i32), tails.astype(i32)


def _dispatch_kernel(hi_ref, lo_ref, tail_ref, xg_ref, xs_ref, zeros, sem, zero_sem):
    base = pl.program_id(0) * MOVE_TM

    @pl.when(pl.program_id(0) == 0)
    def _():
        zeros[...] = jnp.zeros_like(zeros)
        tile = MOE_TILE // SUBLANES
        def fill(start):
            return pltpu.make_async_copy(zeros, xs_ref.at[pl.ds(start, tile)], zero_sem)

        spare = [fill(xs_ref.shape[0] - (g + 1) * tile) for g in range(N_GROUPS)]
        for f in spare:
            f.start()
        for f in spare:
            f.wait()
        for g in range(N_GROUPS):
            f = fill(tail_ref[g])
            f.start()
            f.wait()

    def send(i, carry):
        for u in range(SUBLANES):
            r = base + i * SUBLANES + u
            pltpu.make_async_copy(xg_ref.at[i, pl.ds(u, 1)],
                                  xs_ref.at[hi_ref[r], pl.ds(lo_ref[r], 1)], sem).start(priority=u % 2)
        return carry

    lax.fori_loop(0, MOVE_TM // SUBLANES, send, 0)
    pltpu.make_async_copy(xg_ref, xs_ref.at[pl.ds(0, MOVE_TM // SUBLANES)], sem).wait()


def _dispatch(pos_hi, pos_lo, tail_hi, xg, n_rows):
    t = xg.shape[0]
    xs = pl.pallas_call(
        _dispatch_kernel,
        grid_spec=pltpu.PrefetchScalarGridSpec(
            num_scalar_prefetch=3,
            grid=(t // MOVE_TM,),
            in_specs=[pl.BlockSpec((MOVE_TM // SUBLANES, SUBLANES, XG_W), lambda i, hi, lo, tl: (i, 0, 0))],
            out_specs=pl.BlockSpec(memory_space=pl.ANY),
            scratch_shapes=[pltpu.VMEM((MOE_TILE // SUBLANES, SUBLANES, XG_W), F32),
                            pltpu.SemaphoreType.DMA, pltpu.SemaphoreType.DMA],
        ),
        out_shape=jax.ShapeDtypeStruct((n_rows // SUBLANES, SUBLANES, XG_W), F32),
        compiler_params=_cparams(("arbitrary",)),
        name="moe_dispatch",
    )(pos_hi, pos_lo, tail_hi, xg.reshape(t // SUBLANES, SUBLANES, XG_W))
    return xs.reshape(n_rows, XG_W)


def _experts_kernel(tg_ref, nused_ref, xs_ref, w1_ref, w3_ref, w2_ref, ys_ref):
    i = pl.program_id(0)

    @pl.when(i < nused_ref[0])
    def _():
        x = xs_ref[:, :D_MODEL].astype(BF16)
        gate = xs_ref[:, D_MODEL:]
        lane = lax.broadcasted_iota(jnp.int32, gate.shape, 1)
        first = tg_ref[i] * EXPERTS_PER_GROUP
        hidden = []
        for e in range(EXPERTS_PER_GROUP):
            a = jnp.dot(x, w1_ref[0, e], preferred_element_type=F32)
            b = jnp.dot(x, w3_ref[0, e], preferred_element_type=F32)
            g = jnp.sum(jnp.where(lane == first + e, gate, 0.0), axis=-1, keepdims=True)
            hidden.append((a * _sigmoid(a) * b * g).astype(BF16))
        ys_ref[...] = jnp.dot(jnp.concatenate(hidden, axis=-1), w2_ref[0], preferred_element_type=F32)

    @pl.when(i >= nused_ref[0])
    def _():
        ys_ref[...] = jnp.zeros_like(ys_ref)


def _experts(tile_group, n_used, xs, w1g, w3g, w2g):
    n_rows = xs.shape[0]
    up = pl.BlockSpec((1, EXPERTS_PER_GROUP, D_MODEL, EXPERT_FF), lambda i, tg, nu: (tg[i], 0, 0, 0))
    return pl.pallas_call(
        _experts_kernel,
        grid_spec=pltpu.PrefetchScalarGridSpec(
            num_scalar_prefetch=2,
            grid=(n_rows // MOE_TILE,),
            in_specs=[
                pl.BlockSpec((MOE_TILE, XG_W), lambda i, tg, nu: (i, 0)),
                up, up,
                pl.BlockSpec((1, GROUP_FF, D_MODEL), lambda i, tg, nu: (tg[i], 0, 0)),
            ],
            out_specs=pl.BlockSpec((MOE_TILE, D_MODEL), lambda i, tg, nu: (i, 0)),
        ),
        out_shape=jax.ShapeDtypeStruct((n_rows, D_MODEL), F32),
        compiler_params=_cparams(("arbitrary",)),
        name="moe_experts",
    )(tile_group, n_used, xs, w1g, w3g, w2g)


def _combine_kernel(hi_ref, lo_ref, x1_ref, ys_ref, o_ref, buf, sem):
    step = pl.program_id(0)
    slot = step % 2

    def issue(for_step, into):
        base = for_step * MOVE_TM

        def fetch(i, carry):
            for u in range(SUBLANES):
                r = base + i * SUBLANES + u
                pltpu.make_async_copy(ys_ref.at[hi_ref[r], pl.ds(lo_ref[r], 1)],
                                      buf.at[into, i, pl.ds(u, 1)], sem.at[into]).start(priority=u % 2)
            return carry

        lax.fori_loop(0, MOVE_TM // SUBLANES, fetch, 0)

    @pl.when(step == 0)
    def _():
        issue(step, slot)

    @pl.when(step + 1 < pl.num_programs(0))
    def _():
        issue(step + 1, 1 - slot)

    pltpu.make_async_copy(ys_ref.at[pl.ds(0, MOVE_TM // SUBLANES)], buf.at[slot], sem.at[slot]).wait()
    o_ref[...] = x1_ref[...] + buf[slot]


def _combine(pos_hi, pos_lo, x1, ys):
    t = x1.shape[0]
    tile = (MOVE_TM // SUBLANES, SUBLANES, D_MODEL)
    out = pl.pallas_call(
        _combine_kernel,
        grid_spec=pltpu.PrefetchScalarGridSpec(
            num_scalar_prefetch=2,
            grid=(t // MOVE_TM,),
            in_specs=[pl.BlockSpec(tile, lambda i, hi, lo: (i, 0, 0)),
                      pl.BlockSpec(memory_space=pl.ANY)],
            out_specs=pl.BlockSpec(tile, lambda i, hi, lo: (i, 0, 0)),
            scratch_shapes=[pltpu.VMEM((2,) + tile, F32), pltpu.SemaphoreType.DMA((2,))],
        ),
        out_shape=jax.ShapeDtypeStruct((t // SUBLANES, SUBLANES, D_MODEL), F32),
        compiler_params=_cparams(("arbitrary",)),
        name="moe_combine",
    )(pos_hi, pos_lo, x1.reshape(t // SUBLANES, SUBLANES, D_MODEL),
      ys.reshape(ys.shape[0] // SUBLANES, SUBLANES, D_MODEL))
    return out.reshape(t, D_MODEL)


def _moe(xg, x1, w1, w3, w2):
    t = xg.shape[0]
    w1g = w1.reshape(N_GROUPS, EXPERTS_PER_GROUP, D_MODEL, EXPERT_FF).astype(BF16)
    w3g = w3.reshape(N_GROUPS, EXPERTS_PER_GROUP, D_MODEL, EXPERT_FF).astype(BF16)
    w2g = w2.reshape(N_GROUPS, GROUP_FF, D_MODEL).astype(BF16)
    pos, tile_group, n_used, tails = _moe_plan(xg, t)
    pos_hi, pos_lo = pos // SUBLANES, pos % SUBLANES
    xs = _dispatch(pos_hi, pos_lo, tails // SUBLANES, xg, t + N_GROUPS * MOE_TILE)
    ys = _experts(tile_group, n_used, xs, w1g, w3g, w2g)
    return _combine(pos_hi, pos_lo, x1, ys)


def _class_major(table):
    return table.reshape(ROWS_PER_CLASS, ATT_CLASSES, -1).transpose(1, 0, 2).reshape(SEQ, -1)


def _rope_tables_att():
    pos = jnp.arange(SEQ, dtype=F32)
    inv = ROPE_THETA ** (-jnp.arange(0, ATT_HEAD_DIM, 2, dtype=F32) / ATT_HEAD_DIM)
    ang = pos[:, None] * inv[None, :]
    cos, sin = jnp.cos(ang), jnp.sin(ang)
    cos_full = jnp.concatenate([cos, cos, cos, cos], axis=-1)
    sin_full = jnp.concatenate([-sin, -sin, sin, sin], axis=-1)
    return _class_major(cos_full), _class_major(sin_full)


def _pair_lanes(a):
    half = ATT_HEAD_DIM // 2
    lead = a.shape[:-1]
    a = a.reshape(lead + (-1, 2, 2, half))
    return jnp.swapaxes(a, -3, -2).reshape(lead + (-1,))


def _rope_tables_ret():
    pos = jnp.arange(SEQ, dtype=F32)
    inv = 1.0 / (ROPE_THETA ** jnp.linspace(0.0, 1.0, RET_QK_DIM // 2, dtype=F32))
    ang = pos[:, None] * inv[None, :]
    return jnp.cos(ang), jnp.sin(ang)


def _decay_tables():
    c = RET_CHUNK
    log_gamma = jnp.log(1.0 - jnp.exp2(-5.0 - jnp.arange(RET_HEADS, dtype=F32)))
    idx = jnp.arange(c, dtype=F32)
    diff = idx[:, None] - idx[None, :]
    decay = jnp.where(diff >= 0, jnp.exp(log_gamma[:, None, None] * jnp.maximum(diff, 0.0)), 0.0)
    zeta = jnp.exp(log_gamma[:, None] * (c - 1 - idx))
    xi = jnp.exp(log_gamma[:, None] * (idx + 1))
    gamma_c = jnp.exp(log_gamma * c)
    bc = lambda a: jnp.broadcast_to(a[:, :, None], (RET_HEADS, c, RET_QK_DIM)).astype(BF16)
    return gamma_c, decay, bc(xi), bc(zeta)


def kernel(x, g_norm_mix, w_in, b_merge_gate, g_q, g_k, w_branch_att, g_ret_norm, w_branch_ret,
           w_out, g_norm_ffn, w_router_group, b_router_group, w_router_expert, b_router_expert,
           w1, w3, w2):
    batch = x.shape[0]
    t = batch * SEQ
    cos_a, sin_a = _rope_tables_att()
    cos_r, sin_r = _rope_tables_ret()
    gamma_c, decay, xi, zeta = _decay_tables()
    xf = x.reshape(t, D_MODEL)
    for l in range(g_norm_mix.shape[0]):
        g_mix = g_norm_mix[l][None, :]
        proj_att = _inproj_att(xf, g_mix, _pair_lanes(w_in[l][:, :2 * ATT_W]).astype(BF16), w_in[l])
        proj = _inproj_rest(xf, g_mix, w_in[l], cos_r, sin_r)
        reps = LANES // ATT_HEAD_DIM
        gq = _pair_lanes(jnp.tile(g_q[l], reps))[None, :] * (ATT_HEAD_DIM ** -0.5 * np.log2(np.e))
        y_att = _attention(proj_att, cos_a, sin_a, gq, _pair_lanes(jnp.tile(g_k[l], reps))[None, :],
                           batch)
        y_ret = _retention(proj, gamma_c, decay, xi, zeta, g_ret_norm[l][:, None, :], batch)
        w_route = jnp.concatenate(
            [w_router_expert[l], w_router_group[l],
             jnp.zeros((D_MODEL, LANES - N_ROUTE), F32)], axis=-1)
        wr_hi, wr_lo = _split_bf16(w_route)
        b_route = jnp.concatenate(
            [b_router_expert[l], b_router_group[l], jnp.zeros((LANES - N_ROUTE,), F32)])[None, :]
        x1, xg = _out_stage(
            y_att, y_ret, proj, b_merge_gate[l][None, :], xf,
            w_branch_att[l].astype(BF16), w_branch_ret[l].astype(BF16), w_out[l].astype(BF16),
            g_norm_ffn[l][None, :], wr_hi, wr_lo, b_route)
        xf = _moe(xg, x1, w1[l], w3[l], w2[l])
    return xf.reshape(batch, SEQ, D_MODEL)
```
